```python
import jax, jax.numpy as jnp
from jax import lax
import numpy as np

D_MODEL = 2048
BATCH = 8
SEQ = 8192
DEPTH = 2

CHUNK = 64
N_MIXERS = 2
N_A_LAYERS = (DEPTH + 1) // 2
N_B_LAYERS = DEPTH // 2
EPS = 1e-6

D_FF = 5632

SGU_BLOCK = 128
SGU_WIDTH = 2 * D_MODEL
SGU_GROUPS = 8
SGU_GROUP_DIM = SGU_WIDTH // SGU_GROUPS

MLA_HEADS = 16
Q_LORA = 512
KV_LORA = 512
QK_NOPE = 128
QK_ROPE = 64
V_DIM = 128
QK_DIM = QK_NOPE + QK_ROPE
ROPE_THETA = 10000.0
Q_BLOCK = 128

kernel_name = "hybrid_sgu_mla_macaron_encoder"


def rmsnorm(x, g):
    xf = x.astype(jnp.float32)
    y = xf * lax.rsqrt(jnp.mean(xf * xf, axis=-1, keepdims=True) + EPS)
    return (y * g.astype(jnp.float32)).astype(x.dtype)


def layernorm(x, g, b):
    xf = x.astype(jnp.float32)
    mu = jnp.mean(xf, axis=-1, keepdims=True)
    var = jnp.mean(jnp.square(xf - mu), axis=-1, keepdims=True)
    y = (xf - mu) * lax.rsqrt(var + EPS)
    return (y * g.astype(jnp.float32) + b.astype(jnp.float32)).astype(x.dtype)


def swiglu(h, w_in, w_out):
    gate, up = jnp.split(h @ w_in, 2, axis=-1)
    return (jax.nn.silu(gate) * up) @ w_out


def rope(x, positions):
    half = x.shape[-1] // 2
    inv_freq = 1.0 / (ROPE_THETA ** (jnp.arange(half, dtype=jnp.float32) / half))
    ang = positions.astype(jnp.float32)[..., None] * inv_freq
    cos = jnp.cos(ang)[:, :, None, :]
    sin = jnp.sin(ang)[:, :, None, :]
    xf = x.astype(jnp.float32)
    x1, x2 = xf[..., :half], xf[..., half:]
    out = jnp.concatenate([x1 * cos - x2 * sin, x1 * sin + x2 * cos], axis=-1)
    return out.astype(x.dtype)


def sgu_mixer(h, w_in, v_gain, v_bias, w_spatial, b_spatial, w_out):
    B, S, _ = h.shape
    uv = jax.nn.gelu(h @ w_in)
    u, v = jnp.split(uv, 2, axis=-1)
    v = layernorm(v, v_gain, v_bias)
    nb = S // SGU_BLOCK
    v = v.reshape(B, nb, SGU_BLOCK, SGU_GROUPS, SGU_GROUP_DIM)
    pos_chunk = jnp.arange(SGU_BLOCK) // CHUNK
    mask = pos_chunk[:, None] >= pos_chunk[None, :]
    ws = jnp.where(mask[None], w_spatial, jnp.zeros_like(w_spatial))
    mixed = jnp.einsum('gij,bnjgc->bnigc', ws, v)
    mixed = mixed + b_spatial.T[None, None, :, :, None]
    gated = u * mixed.reshape(B, S, SGU_WIDTH)
    return gated @ w_out


def mla_mixer(h, positions, w_in, q_norm_g, w_q_up, kv_norm_g, w_kv_up, w_out):
    B, S, _ = h.shape
    proj = h @ w_in
    q_lat, kv_lat, k_rope = jnp.split(proj, [Q_LORA, Q_LORA + KV_LORA], axis=-1)
    q = (rmsnorm(q_lat, q_norm_g) @ w_q_up).reshape(B, S, MLA_HEADS, QK_DIM)
    q = jnp.concatenate([q[..., :QK_NOPE], rope(q[..., QK_NOPE:], positions)], axis=-1)
    q = q * (QK_DIM ** -0.5)
    k_rope = rope(k_rope[:, :, None, :], positions)
    kv = (rmsnorm(kv_lat, kv_norm_g) @ w_kv_up).reshape(B, S, MLA_HEADS, QK_NOPE + V_DIM)
    k_nope, v = kv[..., :QK_NOPE], kv[..., QK_NOPE:]
    k = jnp.concatenate(
        [k_nope, jnp.broadcast_to(k_rope, (B, S, MLA_HEADS, QK_ROPE))], axis=-1)

    nq = S // Q_BLOCK
    q_blocks = q.reshape(B, nq, Q_BLOCK, MLA_HEADS, QK_DIM).transpose(1, 0, 2, 3, 4)
    key_chunk = jnp.arange(S) // CHUNK

    def attend(args):
        qb, idx = args
        q_chunk = (idx * Q_BLOCK + jnp.arange(Q_BLOCK)) // CHUNK
        mask = key_chunk[None, :] <= q_chunk[:, None]
        s = jnp.einsum('bqhd,bkhd->bhqk', qb, k).astype(jnp.float32)
        s = jnp.where(mask[None, None], s, -jnp.inf)
        p = jax.nn.softmax(s, axis=-1).astype(v.dtype)
        return jnp.einsum('bhqk,bkhd->bqhd', p, v)

    o = lax.map(attend, (q_blocks, jnp.arange(nq)))
    o = o.transpose(1, 0, 2, 3, 4).reshape(B, S, MLA_HEADS * V_DIM)
    return o @ w_out


def _fwd_setup_inputs(seed: int = 0) -> dict:
    key = jax.random.key(seed)
    ks = jax.random.split(key, 24)

    def nrm(k, shape, scale):
        return jax.random.normal(k, shape, jnp.float32) * scale

    def gain(k, shape):
        return 1.0 + 0.02 * jax.random.normal(k, shape, jnp.float32)

    x = jax.random.normal(ks[0], (BATCH, SEQ, D_MODEL), jnp.float32)
    offset = jax.random.randint(ks[1], (BATCH, 1), 0, 4096, dtype=jnp.int32)
    positions = offset + jnp.arange(SEQ, dtype=jnp.int32)[None, :]
    return {
        "x": x,
        "positions": positions,
        "ln_ffn1": gain(ks[2], (DEPTH, D_MODEL)),
        "ffn1_w_in": nrm(ks[3], (DEPTH, D_MODEL, 2 * D_FF), D_MODEL ** -0.5),
        "ffn1_w_out": nrm(ks[4], (DEPTH, D_FF, D_MODEL), D_FF ** -0.5),
        "ln_mix": gain(ks[5], (DEPTH, D_MODEL)),
        "ln_ffn2": gain(ks[6], (DEPTH, D_MODEL)),
        "ffn2_w_in": nrm(ks[7], (DEPTH, D_MODEL, 2 * D_FF), D_MODEL ** -0.5),
        "ffn2_w_out": nrm(ks[8], (DEPTH, D_FF, D_MODEL), D_FF ** -0.5),
        "sgu_w_in": nrm(ks[9], (N_A_LAYERS, D_MODEL, 2 * SGU_WIDTH), D_MODEL ** -0.5),
        "sgu_v_gain": gain(ks[10], (N_A_LAYERS, SGU_WIDTH)),
        "sgu_v_bias": nrm(ks[11], (N_A_LAYERS, SGU_WIDTH), 0.02),
        "sgu_w_spatial": nrm(ks[12], (N_A_LAYERS, SGU_GROUPS, SGU_BLOCK, SGU_BLOCK), SGU_BLOCK ** -0.5),
        "sgu_b_spatial": gain(ks[13], (N_A_LAYERS, SGU_GROUPS, SGU_BLOCK)),
        "sgu_w_out": nrm(ks[14], (N_A_LAYERS, SGU_WIDTH, D_MODEL), SGU_WIDTH ** -0.5),
        "mla_w_in": nrm(ks[15], (N_B_LAYERS, D_MODEL, Q_LORA + KV_LORA + QK_ROPE), D_MODEL ** -0.5),
        "mla_q_norm": gain(ks[16], (N_B_LAYERS, Q_LORA)),
        "mla_w_q_up": nrm(ks[17], (N_B_LAYERS, Q_LORA, MLA_HEADS * QK_DIM), Q_LORA ** -0.5),
        "mla_kv_norm": gain(ks[18], (N_B_LAYERS, KV_LORA)),
        "mla_w_kv_up": nrm(ks[19], (N_B_LAYERS, KV_LORA, MLA_HEADS * (QK_NOPE + V_DIM)), KV_LORA ** -0.5),
        "mla_w_out": nrm(ks[20], (N_B_LAYERS, MLA_HEADS * V_DIM, D_MODEL), (MLA_HEADS * V_DIM) ** -0.5),
        "ln_final": gain(ks[21], (D_MODEL,)),
    }


def _fwd_reference(x, positions, ln_ffn1, ffn1_w_in, ffn1_w_out, ln_mix, ln_ffn2, ffn2_w_in, ffn2_w_out,
              sgu_w_in, sgu_v_gain, sgu_v_bias, sgu_w_spatial, sgu_b_spatial, sgu_w_out,
              mla_w_in, mla_q_norm, mla_w_q_up, mla_kv_norm, mla_w_kv_up, mla_w_out, ln_final):
    for i in range(DEPTH):
        x = x + 0.5 * swiglu(rmsnorm(x, ln_ffn1[i]), ffn1_w_in[i], ffn1_w_out[i])
        h = rmsnorm(x, ln_mix[i])
        j = i // N_MIXERS
        if i % N_MIXERS == 0:
            x = x + sgu_mixer(h, sgu_w_in[j], sgu_v_gain[j], sgu_v_bias[j],
                              sgu_w_spatial[j], sgu_b_spatial[j], sgu_w_out[j])
        else:
            x = x + mla_mixer(h, positions, mla_w_in[j], mla_q_norm[j], mla_w_q_up[j],
                              mla_kv_norm[j], mla_w_kv_up[j], mla_w_out[j])
        x = x + 0.5 * swiglu(rmsnorm(x, ln_ffn2[i]), ffn2_w_in[i], ffn2_w_out[i])
    return rmsnorm(x, ln_final)


import jax as _jax
import jax.numpy as _jnp

TWIN_FORMAT = 'train_step'
FWD_PARAMS = ['x', 'positions', 'ln_ffn1', 'ffn1_w_in', 'ffn1_w_out', 'ln_mix', 'ln_ffn2', 'ffn2_w_in', 'ffn2_w_out', 'sgu_w_in', 'sgu_v_gain', 'sgu_v_bias', 'sgu_w_spatial', 'sgu_b_spatial', 'sgu_w_out', 'mla_w_in', 'mla_q_norm', 'mla_w_q_up', 'mla_kv_norm', 'mla_w_kv_up', 'mla_w_out', 'ln_final']
TWIN_WEIGHTS = ['ln_ffn1', 'ffn1_w_in', 'ffn1_w_out', 'ln_mix', 'ln_ffn2', 'ffn2_w_in', 'ffn2_w_out', 'sgu_w_in', 'sgu_v_gain', 'sgu_v_bias', 'sgu_w_spatial', 'sgu_b_spatial', 'sgu_w_out', 'mla_w_in', 'mla_q_norm', 'mla_w_q_up', 'mla_kv_norm', 'mla_w_kv_up', 'mla_w_out', 'ln_final']
TWIN_DIFF_INPUT = 'x'
TWIN_INPUTS = ['x', 'positions', 'ln_ffn1', 'ffn1_w_in', 'ffn1_w_out', 'ln_mix', 'ln_ffn2', 'ffn2_w_in', 'ffn2_w_out', 'sgu_w_in', 'sgu_v_gain', 'sgu_v_bias', 'sgu_w_spatial', 'sgu_b_spatial', 'sgu_w_out', 'mla_w_in', 'mla_q_norm', 'mla_w_q_up', 'mla_kv_norm', 'mla_w_kv_up', 'mla_w_out', 'ln_final', 'loss_target', 'm_ln_ffn1', 'm_ffn1_w_in', 'm_ffn1_w_out', 'm_ln_mix', 'm_ln_ffn2', 'm_ffn2_w_in', 'm_ffn2_w_out', 'm_sgu_w_in', 'm_sgu_v_gain', 'm_sgu_v_bias', 'm_sgu_w_spatial', 'm_sgu_b_spatial', 'm_sgu_w_out', 'm_mla_w_in', 'm_mla_q_norm', 'm_mla_w_q_up', 'm_mla_kv_norm', 'm_mla_w_kv_up', 'm_mla_w_out', 'm_ln_final', 'v_ln_ffn1', 'v_ffn1_w_in', 'v_ffn1_w_out', 'v_ln_mix', 'v_ln_ffn2', 'v_ffn2_w_in', 'v_ffn2_w_out', 'v_sgu_w_in', 'v_sgu_v_gain', 'v_sgu_v_bias', 'v_sgu_w_spatial', 'v_sgu_b_spatial', 'v_sgu_w_out', 'v_mla_w_in', 'v_mla_q_norm', 'v_mla_w_q_up', 'v_mla_kv_norm', 'v_mla_w_kv_up', 'v_mla_w_out', 'v_ln_final']
TWIN_OUTPUTS = ['loss', 'grad_x', 'grad_ln_ffn1', 'grad_ffn1_w_in', 'grad_ffn1_w_out', 'grad_ln_mix', 'grad_ln_ffn2', 'grad_ffn2_w_in', 'grad_ffn2_w_out', 'grad_sgu_w_in', 'grad_sgu_v_gain', 'grad_sgu_v_bias', 'grad_sgu_w_spatial', 'grad_sgu_b_spatial', 'grad_sgu_w_out', 'grad_mla_w_in', 'grad_mla_q_norm', 'grad_mla_w_q_up', 'grad_mla_kv_norm', 'grad_mla_w_kv_up', 'grad_mla_w_out', 'grad_ln_final', 'delta_ln_ffn1', 'delta_ffn1_w_in', 'delta_ffn1_w_out', 'delta_ln_mix', 'delta_ln_ffn2', 'delta_ffn2_w_in', 'delta_ffn2_w_out', 'delta_sgu_w_in', 'delta_sgu_v_gain', 'delta_sgu_v_bias', 'delta_sgu_w_spatial', 'delta_sgu_b_spatial', 'delta_sgu_w_out', 'delta_mla_w_in', 'delta_mla_q_norm', 'delta_mla_w_q_up', 'delta_mla_kv_norm', 'delta_mla_w_kv_up', 'delta_mla_w_out', 'delta_ln_final', 'new_m_ln_ffn1', 'new_m_ffn1_w_in', 'new_m_ffn1_w_out', 'new_m_ln_mix', 'new_m_ln_ffn2', 'new_m_ffn2_w_in', 'new_m_ffn2_w_out', 'new_m_sgu_w_in', 'new_m_sgu_v_gain', 'new_m_sgu_v_bias', 'new_m_sgu_w_spatial', 'new_m_sgu_b_spatial', 'new_m_sgu_w_out', 'new_m_mla_w_in', 'new_m_mla_q_norm', 'new_m_mla_w_q_up', 'new_m_mla_kv_norm', 'new_m_mla_w_kv_up', 'new_m_mla_w_out', 'new_m_ln_final', 'new_v_ln_ffn1', 'new_v_ffn1_w_in', 'new_v_ffn1_w_out', 'new_v_ln_mix', 'new_v_ln_ffn2', 'new_v_ffn2_w_in', 'new_v_ffn2_w_out', 'new_v_sgu_w_in', 'new_v_sgu_v_gain', 'new_v_sgu_v_bias', 'new_v_sgu_w_spatial', 'new_v_sgu_b_spatial', 'new_v_sgu_w_out', 'new_v_mla_w_in', 'new_v_mla_q_norm', 'new_v_mla_w_q_up', 'new_v_mla_kv_norm', 'new_v_mla_w_kv_up', 'new_v_mla_w_out', 'new_v_ln_final']
TWIN_LEAF_KINDS = {'loss': 'loss', 'grad_x': 'grad_x', 'grad_ln_ffn1': 'grad_w', 'grad_ffn1_w_in': 'grad_w', 'grad_ffn1_w_out': 'grad_w', 'grad_ln_mix': 'grad_w', 'grad_ln_ffn2': 'grad_w', 'grad_ffn2_w_in': 'grad_w', 'grad_ffn2_w_out': 'grad_w', 'grad_sgu_w_in': 'grad_w', 'grad_sgu_v_gain': 'grad_w', 'grad_sgu_v_bias': 'grad_w', 'grad_sgu_w_spatial': 'grad_w', 'grad_sgu_b_spatial': 'grad_w', 'grad_sgu_w_out': 'grad_w', 'grad_mla_w_in': 'grad_w', 'grad_mla_q_norm': 'grad_w', 'grad_mla_w_q_up': 'grad_w', 'grad_mla_kv_norm': 'grad_w', 'grad_mla_w_kv_up': 'grad_w', 'grad_mla_w_out': 'grad_w', 'grad_ln_final': 'grad_w', 'delta_ln_ffn1': 'delta_w', 'delta_ffn1_w_in': 'delta_w', 'delta_ffn1_w_out': 'delta_w', 'delta_ln_mix': 'delta_w', 'delta_ln_ffn2': 'delta_w', 'delta_ffn2_w_in': 'delta_w', 'delta_ffn2_w_out': 'delta_w', 'delta_sgu_w_in': 'delta_w', 'delta_sgu_v_gain': 'delta_w', 'delta_sgu_v_bias': 'delta_w', 'delta_sgu_w_spatial': 'delta_w', 'delta_sgu_b_spatial': 'delta_w', 'delta_sgu_w_out': 'delta_w', 'delta_mla_w_in': 'delta_w', 'delta_mla_q_norm': 'delta_w', 'delta_mla_w_q_up': 'delta_w', 'delta_mla_kv_norm': 'delta_w', 'delta_mla_w_kv_up': 'delta_w', 'delta_mla_w_out': 'delta_w', 'delta_ln_final': 'delta_w', 'new_m_ln_ffn1': 'new_m', 'new_m_ffn1_w_in': 'new_m', 'new_m_ffn1_w_out': 'new_m', 'new_m_ln_mix': 'new_m', 'new_m_ln_ffn2': 'new_m', 'new_m_ffn2_w_in': 'new_m', 'new_m_ffn2_w_out': 'new_m', 'new_m_sgu_w_in': 'new_m', 'new_m_sgu_v_gain': 'new_m', 'new_m_sgu_v_bias': 'new_m', 'new_m_sgu_w_spatial': 'new_m', 'new_m_sgu_b_spatial': 'new_m', 'new_m_sgu_w_out': 'new_m', 'new_m_mla_w_in': 'new_m', 'new_m_mla_q_norm': 'new_m', 'new_m_mla_w_q_up': 'new_m', 'new_m_mla_kv_norm': 'new_m', 'new_m_mla_w_kv_up': 'new_m', 'new_m_mla_w_out': 'new_m', 'new_m_ln_final': 'new_m', 'new_v_ln_ffn1': 'new_v', 'new_v_ffn1_w_in': 'new_v', 'new_v_ffn1_w_out': 'new_v', 'new_v_ln_mix': 'new_v', 'new_v_ln_ffn2': 'new_v', 'new_v_ffn2_w_in': 'new_v', 'new_v_ffn2_w_out': 'new_v', 'new_v_sgu_w_in': 'new_v', 'new_v_sgu_v_gain': 'new_v', 'new_v_sgu_v_bias': 'new_v', 'new_v_sgu_w_spatial': 'new_v', 'new_v_sgu_b_spatial': 'new_v', 'new_v_sgu_w_out': 'new_v', 'new_v_mla_w_in': 'new_v', 'new_v_mla_q_norm': 'new_v', 'new_v_mla_w_q_up': 'new_v', 'new_v_mla_kv_norm': 'new_v', 'new_v_mla_w_kv_up': 'new_v', 'new_v_mla_w_out': 'new_v', 'new_v_ln_final': 'new_v'}


def _forward(args):
    return _fwd_reference(*[args[k] for k in FWD_PARAMS])


def _output_shape():
    def fwd():
        inp = _fwd_setup_inputs(0)
        return _fwd_reference(*[inp[k] for k in FWD_PARAMS])
    out = _jax.eval_shape(fwd)
    return out.shape, out.dtype

N_MICROBATCH = 1
ADAM_LR = 0.001
ADAM_B1 = 0.9
ADAM_B2 = 0.999
ADAM_EPS = 1e-08
ADAM_WD = 0.01
ADAM_STEP = 10
PER_EXAMPLE_BATCH_AXIS = {'x': 0, 'positions': 0, 'loss_target': 0}
SHARED_INPUTS = []
_WEIGHT_DTYPES = {'ln_ffn1': _jnp.float32, 'ffn1_w_in': _jnp.float32, 'ffn1_w_out': _jnp.float32, 'ln_mix': _jnp.float32, 'ln_ffn2': _jnp.float32, 'ffn2_w_in': _jnp.float32, 'ffn2_w_out': _jnp.float32, 'sgu_w_in': _jnp.float32, 'sgu_v_gain': _jnp.float32, 'sgu_v_bias': _jnp.float32, 'sgu_w_spatial': _jnp.float32, 'sgu_b_spatial': _jnp.float32, 'sgu_w_out': _jnp.float32, 'mla_w_in': _jnp.float32, 'mla_q_norm': _jnp.float32, 'mla_w_q_up': _jnp.float32, 'mla_kv_norm': _jnp.float32, 'mla_w_kv_up': _jnp.float32, 'mla_w_out': _jnp.float32, 'ln_final': _jnp.float32}
MOMENT_SCALE = {'ln_ffn1': 5.213827e-02, 'ffn1_w_in': 2.201190e-02, 'ffn1_w_out': 3.589465e-02, 'ln_mix': 8.184036e-02, 'ln_ffn2': 3.947480e-02, 'ffn2_w_in': 1.655470e-02, 'ffn2_w_out': 2.715124e-02, 'sgu_w_in': 5.443742e-02, 'sgu_v_gain': 3.905636e-02, 'sgu_v_bias': 4.054087e-02, 'sgu_w_spatial': 7.826726e-02, 'sgu_b_spatial': 9.238087e-02, 'sgu_w_out': 9.630223e-02, 'mla_w_in': 3.837131e-02, 'mla_q_norm': 2.456188e-02, 'mla_w_q_up': 9.983327e-03, 'mla_kv_norm': 6.332269e-02, 'mla_w_kv_up': 2.012154e-02, 'mla_w_out': 2.789537e-02, 'ln_final': 3.210114e+01}


def _to_microbatches(a, axis):
    t = _jnp.moveaxis(a, axis, 0)
    t = t.reshape((N_MICROBATCH, t.shape[0] // N_MICROBATCH) + t.shape[1:])
    return _jnp.moveaxis(t, 1, axis + 1)


def setup_inputs(seed: int = 0) -> dict:
    inp = _fwd_setup_inputs(seed)
    key = _jax.random.fold_in(_jax.random.key(seed), 7919)
    shape, _ = _output_shape()
    out = dict(inp)
    out["loss_target"] = _jax.random.normal(_jax.random.fold_in(key, 0), shape, _jnp.float32)
    for i, name in enumerate(TWIN_WEIGHTS):
        w = inp[name].astype(_jnp.float32)
        if MOMENT_SCALE is None:
            s = _jnp.sqrt(_jnp.mean(_jnp.square(w)) + 1e-30)
        else:
            s = MOMENT_SCALE[name]
        km, kv = _jax.random.split(_jax.random.fold_in(key, i + 1))
        out[name] = w
        out["m_" + name] = s * _jax.random.normal(km, w.shape, _jnp.float32)
        out["v_" + name] = (s * s) * _jax.random.uniform(kv, w.shape, _jnp.float32, 0.5, 1.5)
    if N_MICROBATCH > 1:
        for name, axis in PER_EXAMPLE_BATCH_AXIS.items():
            out[name] = _to_microbatches(out[name], axis)
    return {'x': out['x'], 'positions': out['positions'], 'ln_ffn1': out['ln_ffn1'], 'ffn1_w_in': out['ffn1_w_in'], 'ffn1_w_out': out['ffn1_w_out'], 'ln_mix': out['ln_mix'], 'ln_ffn2': out['ln_ffn2'], 'ffn2_w_in': out['ffn2_w_in'], 'ffn2_w_out': out['ffn2_w_out'], 'sgu_w_in': out['sgu_w_in'], 'sgu_v_gain': out['sgu_v_gain'], 'sgu_v_bias': out['sgu_v_bias'], 'sgu_w_spatial': out['sgu_w_spatial'], 'sgu_b_spatial': out['sgu_b_spatial'], 'sgu_w_out': out['sgu_w_out'], 'mla_w_in': out['mla_w_in'], 'mla_q_norm': out['mla_q_norm'], 'mla_w_q_up': out['mla_w_q_up'], 'mla_kv_norm': out['mla_kv_norm'], 'mla_w_kv_up': out['mla_w_kv_up'], 'mla_w_out': out['mla_w_out'], 'ln_final': out['ln_final'], 'loss_target': out['loss_target'], 'm_ln_ffn1': out['m_ln_ffn1'], 'm_ffn1_w_in': out['m_ffn1_w_in'], 'm_ffn1_w_out': out['m_ffn1_w_out'], 'm_ln_mix': out['m_ln_mix'], 'm_ln_ffn2': out['m_ln_ffn2'], 'm_ffn2_w_in': out['m_ffn2_w_in'], 'm_ffn2_w_out': out['m_ffn2_w_out'], 'm_sgu_w_in': out['m_sgu_w_in'], 'm_sgu_v_gain': out['m_sgu_v_gain'], 'm_sgu_v_bias': out['m_sgu_v_bias'], 'm_sgu_w_spatial': out['m_sgu_w_spatial'], 'm_sgu_b_spatial': out['m_sgu_b_spatial'], 'm_sgu_w_out': out['m_sgu_w_out'], 'm_mla_w_in': out['m_mla_w_in'], 'm_mla_q_norm': out['m_mla_q_norm'], 'm_mla_w_q_up': out['m_mla_w_q_up'], 'm_mla_kv_norm': out['m_mla_kv_norm'], 'm_mla_w_kv_up': out['m_mla_w_kv_up'], 'm_mla_w_out': out['m_mla_w_out'], 'm_ln_final': out['m_ln_final'], 'v_ln_ffn1': out['v_ln_ffn1'], 'v_ffn1_w_in': out['v_ffn1_w_in'], 'v_ffn1_w_out': out['v_ffn1_w_out'], 'v_ln_mix': out['v_ln_mix'], 'v_ln_ffn2': out['v_ln_ffn2'], 'v_ffn2_w_in': out['v_ffn2_w_in'], 'v_ffn2_w_out': out['v_ffn2_w_out'], 'v_sgu_w_in': out['v_sgu_w_in'], 'v_sgu_v_gain': out['v_sgu_v_gain'], 'v_sgu_v_bias': out['v_sgu_v_bias'], 'v_sgu_w_spatial': out['v_sgu_w_spatial'], 'v_sgu_b_spatial': out['v_sgu_b_spatial'], 'v_sgu_w_out': out['v_sgu_w_out'], 'v_mla_w_in': out['v_mla_w_in'], 'v_mla_q_norm': out['v_mla_q_norm'], 'v_mla_w_q_up': out['v_mla_w_q_up'], 'v_mla_kv_norm': out['v_mla_kv_norm'], 'v_mla_w_kv_up': out['v_mla_w_kv_up'], 'v_mla_w_out': out['v_mla_w_out'], 'v_ln_final': out['v_ln_final']}


def _loss(weights, diff, rest, loss_target):
    with _jax.named_scope("forward"):
        args = {**rest, TWIN_DIFF_INPUT: diff, **{k: w.astype(_WEIGHT_DTYPES[k]) for k, w in weights.items()}}
        y = _forward(args)
    with _jax.named_scope("loss_head"):
        err = _jnp.square(y.astype(_jnp.float32) - loss_target)
        return 0.5 * _jnp.sum(_jnp.mean(err, axis=-1)) if err.ndim else 0.5 * err


def _adamw(w, g, m, v):
    m = ADAM_B1 * m + (1.0 - ADAM_B1) * g
    v = ADAM_B2 * v + (1.0 - ADAM_B2) * _jnp.square(g)
    m_hat = m / (1.0 - ADAM_B1 ** ADAM_STEP)
    v_hat = v / (1.0 - ADAM_B2 ** ADAM_STEP)
    delta = -ADAM_LR * (m_hat / (_jnp.sqrt(v_hat) + ADAM_EPS) + ADAM_WD * w)
    return delta, m, v


def reference(x, positions, ln_ffn1, ffn1_w_in, ffn1_w_out, ln_mix, ln_ffn2, ffn2_w_in, ffn2_w_out, sgu_w_in, sgu_v_gain, sgu_v_bias, sgu_w_spatial, sgu_b_spatial, sgu_w_out, mla_w_in, mla_q_norm, mla_w_q_up, mla_kv_norm, mla_w_kv_up, mla_w_out, ln_final, loss_target, m_ln_ffn1, m_ffn1_w_in, m_ffn1_w_out, m_ln_mix, m_ln_ffn2, m_ffn2_w_in, m_ffn2_w_out, m_sgu_w_in, m_sgu_v_gain, m_sgu_v_bias, m_sgu_w_spatial, m_sgu_b_spatial, m_sgu_w_out, m_mla_w_in, m_mla_q_norm, m_mla_w_q_up, m_mla_kv_norm, m_mla_w_kv_up, m_mla_w_out, m_ln_final, v_ln_ffn1, v_ffn1_w_in, v_ffn1_w_out, v_ln_mix, v_ln_ffn2, v_ffn2_w_in, v_ffn2_w_out, v_sgu_w_in, v_sgu_v_gain, v_sgu_v_bias, v_sgu_w_spatial, v_sgu_b_spatial, v_sgu_w_out, v_mla_w_in, v_mla_q_norm, v_mla_w_q_up, v_mla_kv_norm, v_mla_w_kv_up, v_mla_w_out, v_ln_final):
    given = dict(x=x, positions=positions, ln_ffn1=ln_ffn1, ffn1_w_in=ffn1_w_in, ffn1_w_out=ffn1_w_out, ln_mix=ln_mix, ln_ffn2=ln_ffn2, ffn2_w_in=ffn2_w_in, ffn2_w_out=ffn2_w_out, sgu_w_in=sgu_w_in, sgu_v_gain=sgu_v_gain, sgu_v_bias=sgu_v_bias, sgu_w_spatial=sgu_w_spatial, sgu_b_spatial=sgu_b_spatial, sgu_w_out=sgu_w_out, mla_w_in=mla_w_in, mla_q_norm=mla_q_norm, mla_w_q_up=mla_w_q_up, mla_kv_norm=mla_kv_norm, mla_w_kv_up=mla_w_kv_up, mla_w_out=mla_w_out, ln_final=ln_final, loss_target=loss_target, m_ln_ffn1=m_ln_ffn1, m_ffn1_w_in=m_ffn1_w_in, m_ffn1_w_out=m_ffn1_w_out, m_ln_mix=m_ln_mix, m_ln_ffn2=m_ln_ffn2, m_ffn2_w_in=m_ffn2_w_in, m_ffn2_w_out=m_ffn2_w_out, m_sgu_w_in=m_sgu_w_in, m_sgu_v_gain=m_sgu_v_gain, m_sgu_v_bias=m_sgu_v_bias, m_sgu_w_spatial=m_sgu_w_spatial, m_sgu_b_spatial=m_sgu_b_spatial, m_sgu_w_out=m_sgu_w_out, m_mla_w_in=m_mla_w_in, m_mla_q_norm=m_mla_q_norm, m_mla_w_q_up=m_mla_w_q_up, m_mla_kv_norm=m_mla_kv_norm, m_mla_w_kv_up=m_mla_w_kv_up, m_mla_w_out=m_mla_w_out, m_ln_final=m_ln_final, v_ln_ffn1=v_ln_ffn1, v_ffn1_w_in=v_ffn1_w_in, v_ffn1_w_out=v_ffn1_w_out, v_ln_mix=v_ln_mix, v_ln_ffn2=v_ln_ffn2, v_ffn2_w_in=v_ffn2_w_in, v_ffn2_w_out=v_ffn2_w_out, v_sgu_w_in=v_sgu_w_in, v_sgu_v_gain=v_sgu_v_gain, v_sgu_v_bias=v_sgu_v_bias, v_sgu_w_spatial=v_sgu_w_spatial, v_sgu_b_spatial=v_sgu_b_spatial, v_sgu_w_out=v_sgu_w_out, v_mla_w_in=v_mla_w_in, v_mla_q_norm=v_mla_q_norm, v_mla_w_q_up=v_mla_w_q_up, v_mla_kv_norm=v_mla_kv_norm, v_mla_w_kv_up=v_mla_w_kv_up, v_mla_w_out=v_mla_w_out, v_ln_final=v_ln_final)
    weights = {n: given[n] for n in TWIN_WEIGHTS}
    shared = {n: given[n] for n in SHARED_INPUTS}
    per_example = {n: given[n] for n in ['x', 'positions']}
    grad_fn = _jax.value_and_grad(_loss, argnums=(0, 1))

    def one_microbatch(ex, loss_target):
        ex = dict(ex)
        diff = ex.pop(TWIN_DIFF_INPUT)
        return grad_fn(weights, diff, {**shared, **ex}, loss_target)

    if N_MICROBATCH == 1:
        loss, (grad_w, grad_x) = one_microbatch(per_example, given["loss_target"])
    else:
        def body(carry, xs):
            loss_sum, grad_sum = carry
            l_k, (gw_k, gx_k) = one_microbatch(xs[0], xs[1])
            with _jax.named_scope("update"):
                return (loss_sum + l_k, _jax.tree.map(_jnp.add, grad_sum, gw_k)), gx_k

        init = (_jnp.zeros((), _jnp.float32), _jax.tree.map(_jnp.zeros_like, weights))
        (loss, grad_w), grad_x = _jax.lax.scan(body, init, (per_example, given["loss_target"]))
    with _jax.named_scope("update"):
        delta_w, new_m, new_v = {}, {}, {}
        for n in TWIN_WEIGHTS:
            delta_w[n], new_m[n], new_v[n] = _adamw(weights[n], grad_w[n], given["m_" + n], given["v_" + n])
    return (loss, grad_x, *[grad_w[n] for n in TWIN_WEIGHTS], *[delta_w[n] for n in TWIN_WEIGHTS],
            *[new_m[n] for n in TWIN_WEIGHTS], *[new_v[n] for n in TWIN_WEIGHTS])
```

```python
import functools
import math

import jax
import jax.numpy as jnp
from jax import lax
from jax.experimental import pallas as pl
from jax.experimental.pallas import tpu as pltpu

F32 = jnp.float32
BF16 = jnp.bfloat16
MESH = pl.DeviceIdType.MESH

EPS = 1e-6
CHUNK = 64
SGU_BLOCK = 128
SGU_GROUPS = 8
QK_NOPE = 128
QK_ROPE = 64
V_DIM = 128
QK_DIM = QK_NOPE + QK_ROPE
HEAD_PAD = 256
ROPE_THETA = 10000.0
N_CHIPS = 4
N_DEV = 8

ADAM_LR = 0.001
ADAM_B1 = 0.9
ADAM_B2 = 0.999
ADAM_EPS = 1e-08
ADAM_WD = 0.01
ADAM_STEP = 10

LANE = 128
VMEM_LIMIT_BYTES = 56 * 1024 * 1024

_DIMS = {
    "nn": (((1,), (0,)), ((), ())),
    "nt": (((1,), (1,)), ((), ())),
    "tn": (((0,), (0,)), ((), ())),
}


def _tile(n, pref):
    t = (min(pref, n) // LANE) * LANE
    while t >= LANE:
        if n % t == 0:
            return t
        t -= LANE
    return n


def _params(sem):
    return pltpu.CompilerParams(dimension_semantics=sem, vmem_limit_bytes=VMEM_LIMIT_BYTES)


def _dot(a, b, mode):
    return lax.dot_general(a, b, _DIMS[mode], preferred_element_type=F32)


def _matmul(name, mode, grid, a, a_spec, b, b_spec, extras, out_shapes, out_specs, acc_shape, epilogue):
    nk = grid[2]
    n_ex = len(extras)
    n_out = len(out_shapes)

    def body(*refs):
        a_ref, b_ref = refs[0], refs[1]
        ex = refs[2:2 + n_ex]
        outs = refs[2 + n_ex:2 + n_ex + n_out]
        ids = (pl.program_id(0), pl.program_id(1))
        part = _dot(a_ref[...], b_ref[...], mode)
        if nk == 1:
            epilogue(part, ex, outs, ids)
        else:
            acc = refs[-1]
            k = pl.program_id(2)

            @pl.when(k == 0)
            def _():
                acc[...] = part

            @pl.when(k > 0)
            def _():
                acc[...] += part

            @pl.when(k == nk - 1)
            def _():
                epilogue(acc[...], ex, outs, ids)

    scratch = [pltpu.VMEM(acc_shape, F32)] if nk > 1 else []
    return pl.pallas_call(
        body,
        name=name,
        grid=grid,
        in_specs=[a_spec, b_spec] + [s for _, s in extras],
        out_specs=out_specs,
        out_shape=out_shapes,
        scratch_shapes=scratch,
        compiler_params=_params(("parallel", "parallel", "arbitrary")),
    )(a, b, *[e for e, _ in extras])


def _store(scale, dtype):
    def epilogue(acc, ex, outs, ids):
        v = acc if scale == 1.0 else acc * scale
        outs[0][...] = v.astype(dtype)

    return epilogue


def _mm_nn_full(name, a, b, out_dtype, tm_pref=1024, tn_pref=512):
    m, kd = a.shape
    n = b.shape[1]
    tm, tn = _tile(m, tm_pref), _tile(n, tn_pref)
    return _matmul(
        name, "nn", (m // tm, n // tn, 1),
        a, pl.BlockSpec((tm, kd), lambda i, j, k: (i, 0)),
        b, pl.BlockSpec((kd, tn), lambda i, j, k: (0, j)),
        [], [jax.ShapeDtypeStruct((m, n), out_dtype)], [pl.BlockSpec((tm, tn), lambda i, j, k: (i, j))],
        None, _store(1.0, out_dtype))[0]


def _mm_nt_full(name, a, b, out_dtype, scale=1.0, tm_pref=1024, tn_pref=512):
    m, kd = a.shape
    n = b.shape[0]
    tm, tn = _tile(m, tm_pref), _tile(n, tn_pref)
    return _matmul(
        name, "nt", (m // tm, n // tn, 1),
        a, pl.BlockSpec((tm, kd), lambda i, j, k: (i, 0)),
        b, pl.BlockSpec((tn, kd), lambda i, j, k: (j, 0)),
        [], [jax.ShapeDtypeStruct((m, n), out_dtype)], [pl.BlockSpec((tm, tn), lambda i, j, k: (i, j))],
        None, _store(scale, out_dtype))[0]


def _mm_nt_k(name, a, b, out_dtype, tk_pref=1024, tm_pref=1024, tn_pref=1024):
    m, kd = a.shape
    n = b.shape[0]
    tm, tn, tk = _tile(m, tm_pref), _tile(n, tn_pref), _tile(kd, tk_pref)
    return _matmul(
        name, "nt", (m // tm, n // tn, kd // tk),
        a, pl.BlockSpec((tm, tk), lambda i, j, k: (i, k)),
        b, pl.BlockSpec((tn, tk), lambda i, j, k: (j, k)),
        [], [jax.ShapeDtypeStruct((m, n), out_dtype)], [pl.BlockSpec((tm, tn), lambda i, j, k: (i, j))],
        (tm, tn), _store(1.0, out_dtype))[0]


def _mm_tn(name, a, b, scale=1.0, tm_pref=1024, tn_pref=1024, tk_pref=512):
    t, m = a.shape
    n = b.shape[1]
    tm, tn, tk = _tile(m, tm_pref), _tile(n, tn_pref), _tile(t, tk_pref)
    return _matmul(
        name, "tn", (m // tm, n // tn, t // tk),
        a, pl.BlockSpec((tk, tm), lambda i, j, k: (k, i)),
        b, pl.BlockSpec((tk, tn), lambda i, j, k: (k, j)),
        [], [jax.ShapeDtypeStruct((m, n), BF16)], [pl.BlockSpec((tm, tn), lambda i, j, k: (i, j))],
        (tm, tn), _store(scale, BF16))[0]


def _mm_residual(name, a, b, x, scale, tm_pref=1024, tn_pref=512, tk_pref=1408):
    m, kd = a.shape
    n = b.shape[1]
    tm, tn, tk = _tile(m, tm_pref), _tile(n, tn_pref), _tile(kd, tk_pref)

    def epilogue(acc, ex, outs, ids):
        outs[0][...] = ex[0][...] + scale * acc

    return _matmul(
        name, "nn", (m // tm, n // tn, kd // tk),
        a, pl.BlockSpec((tm, tk), lambda i, j, k: (i, k)),
        b, pl.BlockSpec((tk, tn), lambda i, j, k: (k, j)),
        [(x, pl.BlockSpec((tm, tn), lambda i, j, k: (i, j)))],
        [jax.ShapeDtypeStruct((m, n), F32)], [pl.BlockSpec((tm, tn), lambda i, j, k: (i, j))],
        (tm, tn), epilogue)[0]


def _rms_fwd(name, x, g):
    t, d = x.shape
    tm = _tile(t, 512)

    def body(x_ref, g_ref, h_ref):
        xv = x_ref[...]
        r = lax.rsqrt(jnp.mean(xv * xv, axis=-1, keepdims=True) + EPS)
        h_ref[...] = (xv * r * g_ref[...]).astype(BF16)

    return pl.pallas_call(
        body, name=name, grid=(t // tm,),
        in_specs=[pl.BlockSpec((tm, d), lambda i: (i, 0)), pl.BlockSpec((1, d), lambda i: (0, 0))],
        out_specs=pl.BlockSpec((tm, d), lambda i: (i, 0)),
        out_shape=jax.ShapeDtypeStruct((t, d), BF16),
        compiler_params=_params(("parallel",)),
    )(x, g.reshape(1, d))


def _rms_bwd_math(dh, xv, g):
    r = lax.rsqrt(jnp.mean(xv * xv, axis=-1, keepdims=True) + EPS)
    xhat = xv * r
    dxh = dh * g
    dx = r * (dxh - xhat * jnp.mean(dxh * xhat, axis=-1, keepdims=True))
    return dx, dh * xhat


def _rms_bwd(name, dh, x, g, dres):
    t, d = x.shape
    tm = _tile(t, 256)

    def body(dh_ref, x_ref, g_ref, dres_ref, dx_ref, dxb_ref, dg_ref):
        dx, dgt = _rms_bwd_math(dh_ref[...].astype(F32), x_ref[...], g_ref[...])
        dx = dres_ref[...] + dx
        dx_ref[...] = dx
        dxb_ref[...] = dx.astype(BF16)

        @pl.when(pl.program_id(0) == 0)
        def _():
            dg_ref[...] = jnp.zeros_like(dg_ref)

        dg_ref[...] += jnp.sum(dgt, axis=0, keepdims=True)

    row = pl.BlockSpec((tm, d), lambda i: (i, 0))
    vec = pl.BlockSpec((1, d), lambda i: (0, 0))
    return pl.pallas_call(
        body, name=name, grid=(t // tm,),
        in_specs=[row, row, vec, row],
        out_specs=[row, row, vec],
        out_shape=[jax.ShapeDtypeStruct((t, d), F32), jax.ShapeDtypeStruct((t, d), BF16),
                   jax.ShapeDtypeStruct((1, d), F32)],
        compiler_params=_params(("arbitrary",)),
    )(dh, x, g.reshape(1, d), dres)


def _loss_bwd(x, g, target):
    t, d = x.shape
    tm = _tile(t, 256)

    def body(x_ref, g_ref, tgt_ref, loss_ref, dx_ref, dxb_ref, dg_ref):
        xv = x_ref[...]
        gv = g_ref[...]
        r = lax.rsqrt(jnp.mean(xv * xv, axis=-1, keepdims=True) + EPS)
        err = xv * r * gv - tgt_ref[...]
        part = 0.5 * jnp.sum(jnp.mean(err * err, axis=-1, keepdims=True), axis=0, keepdims=True)
        dx, dgt = _rms_bwd_math(err * (1.0 / d), xv, gv)
        dx_ref[...] = dx
        dxb_ref[...] = dx.astype(BF16)

        @pl.when(pl.program_id(0) == 0)
        def _():
            dg_ref[...] = jnp.zeros_like(dg_ref)
            loss_ref[...] = jnp.zeros_like(loss_ref)

        dg_ref[...] += jnp.sum(dgt, axis=0, keepdims=True)
        loss_ref[...] += jnp.broadcast_to(part, loss_ref.shape)

    row = pl.BlockSpec((tm, d), lambda i: (i, 0))
    vec = pl.BlockSpec((1, d), lambda i: (0, 0))
    return pl.pallas_call(
        body, name="loss_bwd", grid=(t // tm,),
        in_specs=[row, vec, row],
        out_specs=[pl.BlockSpec((1, LANE), lambda i: (0, 0)), row, row, vec],
        out_shape=[jax.ShapeDtypeStruct((1, LANE), F32), jax.ShapeDtypeStruct((t, d), F32),
                   jax.ShapeDtypeStruct((t, d), BF16), jax.ShapeDtypeStruct((1, d), F32)],
        compiler_params=_params(("arbitrary",)),
    )(x, g.reshape(1, d), target)


def _sigmoid(x):
    return 1.0 / (1.0 + jnp.exp(-x))


def _ffn_up(name, h, w_in):
    t, d = h.shape
    fs = w_in.shape[2]
    f = 2 * fs
    tm, tn = _tile(t, 1024), _tile(fs, 256)
    per = fs // tn

    def body(h_ref, wg_ref, wu_ref, gu_ref, z_ref):
        hv = h_ref[...]
        gate = _dot(hv, wg_ref[...], "nn")
        up = _dot(hv, wu_ref[...], "nn")
        gu_ref[0] = gate.astype(BF16)
        gu_ref[1] = up.astype(BF16)
        z_ref[...] = (gate * _sigmoid(gate) * up).astype(BF16)

    return pl.pallas_call(
        body, name=name, grid=(t // tm, f // tn),
        in_specs=[pl.BlockSpec((tm, d), lambda i, j: (i, 0)),
                  pl.BlockSpec((None, d, tn), lambda i, j: (j // per, 0, j % per)),
                  pl.BlockSpec((None, d, tn), lambda i, j: (2 + j // per, 0, j % per))],
        out_specs=[pl.BlockSpec((2, tm, tn), lambda i, j: (0, i, j)),
                   pl.BlockSpec((tm, tn), lambda i, j: (i, j))],
        out_shape=[jax.ShapeDtypeStruct((2, t, f), BF16), jax.ShapeDtypeStruct((t, f), BF16)],
        compiler_params=_params(("parallel", "parallel")),
    )(h, w_in, w_in)


def _ffn_dact(name, dxb, w_out, gu):
    t, d = dxb.shape
    f = w_out.shape[0]
    tm, tn = _tile(t, 1024), _tile(f, 256)

    def epilogue(acc, ex, outs, ids):
        dz = 0.5 * acc
        gate = ex[0][0].astype(F32)
        up = ex[0][1].astype(F32)
        sg = _sigmoid(gate)
        outs[0][0] = (dz * up * (sg * (1.0 + gate * (1.0 - sg)))).astype(BF16)
        outs[0][1] = (dz * gate * sg).astype(BF16)

    blk = pl.BlockSpec((2, tm, tn), lambda i, j, k: (0, i, j))
    return _matmul(
        name, "nt", (t // tm, f // tn, 1),
        dxb, pl.BlockSpec((tm, d), lambda i, j, k: (i, 0)),
        w_out, pl.BlockSpec((tn, d), lambda i, j, k: (j, 0)),
        [(gu, blk)], [jax.ShapeDtypeStruct((2, t, f), BF16)], [blk], None, epilogue)[0]


def _grad_colsharded(name, h, da):
    t, d = h.shape
    w = da.shape[2]
    ws = w // 2
    tm, tn, tk = _tile(d, 1024), _tile(ws, 1408), _tile(t, 512)
    per = ws // tn
    return _matmul(
        name, "tn", (d // tm, (2 * w) // tn, t // tk),
        h, pl.BlockSpec((tk, tm), lambda i, j, k: (k, i)),
        da, pl.BlockSpec((None, tk, tn), lambda i, j, k: (j // (2 * per), k, j % (2 * per))),
        [], [jax.ShapeDtypeStruct((N_CHIPS, d, ws), BF16)],
        [pl.BlockSpec((None, tm, tn), lambda i, j, k: (j // per, i, j % per))],
        (tm, tn), _store(1.0, BF16))[0]


def _back_colsharded(name, da, w_g):
    _, t, w = da.shape
    d, ws = w_g.shape[1], w_g.shape[2]
    tm, tn, tk = _tile(t, 1024), _tile(d, 1024), _tile(ws, 1408)
    per = ws // tk
    return _matmul(
        name, "nt", (t // tm, d // tn, (2 * w) // tk),
        da, pl.BlockSpec((None, tm, tk), lambda i, j, k: (k // (2 * per), i, k % (2 * per))),
        w_g, pl.BlockSpec((None, tn, tk), lambda i, j, k: (k // per, j, k % per)),
        [], [jax.ShapeDtypeStruct((t, d), F32)], [pl.BlockSpec((tm, tn), lambda i, j, k: (i, j))],
        (tm, tn), _store(1.0, F32))[0]


def _ffn_fwd(tag, x, g, w_in, w_out):
    h = _rms_fwd(tag + "_norm", x, g)
    gu, z = _ffn_up(tag + "_up", h, w_in)
    y = _mm_residual(tag + "_down", z, w_out, x, 0.5)
    return y, (x, h, gu, z)


def _ffn_bwd(tag, saved, g, w_in, w_out, dx, dxb):
    x, h, gu, z = saved
    f = z.shape[1]
    d_w_out = _mm_tn(tag + "_dwout", z, dxb, scale=0.5, tm_pref=1408)
    da = _ffn_dact(tag + "_dact", dxb, w_out, gu)
    d_w_in = _grad_colsharded(tag + "_dwin", h, da)
    dh = _back_colsharded(tag + "_dh", da, w_in)
    dx, dxb, dg = _rms_bwd(tag + "_dnorm", dh, x, g, dx)
    return dx, dxb, dg, d_w_in, d_w_out.reshape(N_CHIPS, f // N_CHIPS, -1)


_GELU_K = math.sqrt(2.0 / math.pi)
_GELU_C = 0.044715


def _gelu(x):
    t = jnp.tanh(_GELU_K * (x + _GELU_C * x * x * x))
    return 0.5 * x * (1.0 + t), t


def _dgelu(x, t):
    return 0.5 * (1.0 + t) + 0.5 * x * (1.0 - t * t) * (_GELU_K * (1.0 + 3.0 * _GELU_C * x * x))


def _causal_block_mask():
    r = lax.broadcasted_iota(jnp.int32, (SGU_BLOCK, SGU_BLOCK), 0) // CHUNK
    c = lax.broadcasted_iota(jnp.int32, (SGU_BLOCK, SGU_BLOCK), 1) // CHUNK
    return r >= c


def _sgu_pre(name, h, w_in):
    t, d = h.shape
    ws = w_in.shape[2]
    w = 2 * ws
    tm, tn = _tile(t, 1024), _tile(ws, 512)
    per = ws // tn
    return _matmul(
        name, "nn", (t // tm, (2 * w) // tn, 1),
        h, pl.BlockSpec((tm, d), lambda i, j, k: (i, 0)),
        w_in, pl.BlockSpec((None, d, tn), lambda i, j, k: (j // per, 0, j % per)),
        [], [jax.ShapeDtypeStruct((2, t, w), BF16)],
        [pl.BlockSpec((None, tm, tn), lambda i, j, k: (j // (2 * per), i, j % (2 * per)))],
        None, _store(1.0, BF16))[0]


def _layernorm_stats(v):
    mu = jnp.mean(v, axis=-1, keepdims=True)
    vc = v - mu
    rstd = lax.rsqrt(jnp.mean(vc * vc, axis=-1, keepdims=True) + EPS)
    return vc * rstd, rstd


def _sgu_mid_fwd(pre, gain, bias, w_sp, b_sp_t):
    _, t, w = pre.shape
    gd = w // SGU_GROUPS

    def body(pre_ref, gain_ref, bias_ref, ws_ref, bt_ref, out_ref):
        mask = _causal_block_mask()
        u, _ = _gelu(pre_ref[0].astype(F32))
        v, _ = _gelu(pre_ref[1].astype(F32))
        vhat, _ = _layernorm_stats(v)
        vln = (vhat * gain_ref[...] + bias_ref[...]).astype(BF16)
        for gi in range(SGU_GROUPS):
            cols = slice(gi * gd, (gi + 1) * gd)
            wg = jnp.where(mask, ws_ref[gi], 0.0).astype(BF16)
            mixed = _dot(wg, vln[:, cols], "nn") + bt_ref[:, gi:gi + 1]
            out_ref[:, cols] = (u[:, cols] * mixed).astype(BF16)

    return pl.pallas_call(
        body, name="sgu_mid_fwd", grid=(t // SGU_BLOCK,),
        in_specs=[pl.BlockSpec((2, SGU_BLOCK, w), lambda n: (0, n, 0)),
                  pl.BlockSpec((1, w), lambda n: (0, 0)), pl.BlockSpec((1, w), lambda n: (0, 0)),
                  pl.BlockSpec((SGU_GROUPS, SGU_BLOCK, SGU_BLOCK), lambda n: (0, 0, 0)),
                  pl.BlockSpec((SGU_BLOCK, SGU_GROUPS), lambda n: (0, 0))],
        out_specs=pl.BlockSpec((SGU_BLOCK, w), lambda n: (n, 0)),
        out_shape=jax.ShapeDtypeStruct((t, w), BF16),
        compiler_params=_params(("parallel",)),
    )(pre, gain, bias, w_sp, b_sp_t)


def _sgu_mid_bwd(pre, dgated, gain, bias, w_sp, b_sp_t):
    _, t, w = pre.shape
    gd = w // SGU_GROUPS

    def body(pre_ref, dg_ref, gain_ref, bias_ref, ws_ref, bt_ref,
             dpre_ref, dgain_ref, dbias_ref, dws_ref, dbt_ref, dvln_s):
        @pl.when(pl.program_id(0) == 0)
        def _():
            dgain_ref[...] = jnp.zeros_like(dgain_ref)
            dbias_ref[...] = jnp.zeros_like(dbias_ref)
            dws_ref[...] = jnp.zeros_like(dws_ref)
            dbt_ref[...] = jnp.zeros_like(dbt_ref)

        mask = _causal_block_mask()
        pu = pre_ref[0].astype(F32)
        pv = pre_ref[1].astype(F32)
        u, tu = _gelu(pu)
        v, tv = _gelu(pv)
        vhat, rstd = _layernorm_stats(v)
        gain_v = gain_ref[...]
        vln = (vhat * gain_v + bias_ref[...]).astype(BF16)
        dgt = dg_ref[...].astype(F32)
        for gi in range(SGU_GROUPS):
            cols = slice(gi * gd, (gi + 1) * gd)
            wg = jnp.where(mask, ws_ref[gi], 0.0).astype(BF16)
            vg = vln[:, cols]
            mixed = _dot(wg, vg, "nn") + bt_ref[:, gi:gi + 1]
            dgg = dgt[:, cols]
            dmixed = dgg * u[:, cols]
            dmb = dmixed.astype(BF16)
            dpre_ref[0, :, cols] = (dgg * mixed * _dgelu(pu[:, cols], tu[:, cols])).astype(BF16)
            dbt_ref[:, gi:gi + 1] += jnp.sum(dmixed, axis=1, keepdims=True)
            dws_ref[gi] += jnp.where(mask, _dot(dmb, vg, "nt"), 0.0)
            dvln_s[:, cols] = _dot(wg, dmb, "tn")
        dvln = dvln_s[...]
        dgain_ref[...] += jnp.sum(dvln * vhat, axis=0, keepdims=True)
        dbias_ref[...] += jnp.sum(dvln, axis=0, keepdims=True)
        dvh = dvln * gain_v
        dv = rstd * (dvh - jnp.mean(dvh, axis=-1, keepdims=True)
                     - vhat * jnp.mean(dvh * vhat, axis=-1, keepdims=True))
        dpre_ref[1] = (dv * _dgelu(pv, tv)).astype(BF16)

    vec = pl.BlockSpec((1, w), lambda n: (0, 0))
    wsb = pl.BlockSpec((SGU_GROUPS, SGU_BLOCK, SGU_BLOCK), lambda n: (0, 0, 0))
    btb = pl.BlockSpec((SGU_BLOCK, SGU_GROUPS), lambda n: (0, 0))
    blk2 = pl.BlockSpec((2, SGU_BLOCK, w), lambda n: (0, n, 0))
    return pl.pallas_call(
        body, name="sgu_mid_bwd", grid=(t // SGU_BLOCK,),
        in_specs=[blk2, pl.BlockSpec((SGU_BLOCK, w), lambda n: (n, 0)), vec, vec, wsb, btb],
        out_specs=[blk2, vec, vec, wsb, btb],
        out_shape=[jax.ShapeDtypeStruct((2, t, w), BF16), jax.ShapeDtypeStruct((1, w), F32),
                   jax.ShapeDtypeStruct((1, w), F32),
                   jax.ShapeDtypeStruct((SGU_GROUPS, SGU_BLOCK, SGU_BLOCK), F32),
                   jax.ShapeDtypeStruct((SGU_BLOCK, SGU_GROUPS), F32)],
        scratch_shapes=[pltpu.VMEM((SGU_BLOCK, w), F32)],
        compiler_params=_params(("arbitrary",)),
    )(pre, dgated, gain, bias, w_sp, b_sp_t)


def _sgu_fwd(x, g, w_in, gain, bias, w_sp, b_sp, w_out):
    h = _rms_fwd("sgu_norm", x, g)
    pre = _sgu_pre("sgu_pre", h, w_in)
    gated = _sgu_mid_fwd(pre, gain, bias, w_sp, b_sp.T)
    y = _mm_residual("sgu_out", gated, w_out, x, 1.0, tk_pref=1024)
    return y, (x, h, pre, gated)


def _sgu_bwd(saved, g, w_in, gain, bias, w_sp, b_sp, w_out, dx, dxb):
    x, h, pre, gated = saved
    w = gated.shape[1]
    d_w_out = _mm_tn("sgu_dwout", gated, dxb)
    dgated = _mm_nt_full("sgu_dgated", dxb, w_out, BF16)
    dpre, dgain, dbias, dws, dbt = _sgu_mid_bwd(pre, dgated, gain, bias, w_sp, b_sp.T)
    d_w_in = _grad_colsharded("sgu_dwin", h, dpre)
    dh = _back_colsharded("sgu_dh", dpre, w_in)
    dx, dxb, dg = _rms_bwd("sgu_dnorm", dh, x, g, dx)
    small = dict(ln=dg, gain=dgain, bias=dbias, w_sp=dws, b_sp=dbt.T)
    return dx, dxb, small, d_w_in, d_w_out.reshape(N_CHIPS, w // N_CHIPS, -1)


def _rope_tables(positions):
    half = QK_ROPE // 2
    inv_freq = 1.0 / (ROPE_THETA ** (jnp.arange(half, dtype=F32) / half))
    ang = positions.astype(F32)[:, None] * inv_freq
    cos, sin = jnp.cos(ang), jnp.sin(ang)
    t = positions.shape[0]
    zeros = jnp.zeros((t, half), F32)
    rest = jnp.zeros((t, LANE - QK_ROPE), F32)
    c = jnp.concatenate([cos, cos, rest + 1.0], axis=1)
    s_up = jnp.concatenate([zeros, sin, rest], axis=1)
    s_dn = jnp.concatenate([-sin, zeros, rest], axis=1)
    return c, s_up, s_dn


def _rope_apply(x, c, s_up, s_dn):
    half = QK_ROPE // 2
    return x * c + pltpu.roll(x, half, 1) * s_up + pltpu.roll(x, LANE - half, 1) * s_dn


def _rope_apply_t(dy, c, s_up, s_dn):
    half = QK_ROPE // 2
    return dy * c - pltpu.roll(dy, LANE - half, 1) * s_dn - pltpu.roll(dy, half, 1) * s_up


def _mla_norm_fwd(proj, gq, gkv):
    t, p = proj.shape
    ql, kvl = gq.shape[1], gkv.shape[1]
    tm = _tile(t, 512)

    def body(p_ref, gq_ref, gkv_ref, qn_ref, kvn_ref):
        for lo, n, g_ref, o_ref in ((0, ql, gq_ref, qn_ref), (ql, kvl, gkv_ref, kvn_ref)):
            xv = p_ref[:, lo:lo + n]
            r = lax.rsqrt(jnp.mean(xv * xv, axis=-1, keepdims=True) + EPS)
            o_ref[...] = (xv * r * g_ref[...]).astype(BF16)

    return pl.pallas_call(
        body, name="mla_norm_fwd", grid=(t // tm,),
        in_specs=[pl.BlockSpec((tm, p), lambda i: (i, 0)), pl.BlockSpec((1, ql), lambda i: (0, 0)),
                  pl.BlockSpec((1, kvl), lambda i: (0, 0))],
        out_specs=[pl.BlockSpec((tm, ql), lambda i: (i, 0)), pl.BlockSpec((tm, kvl), lambda i: (i, 0))],
        out_shape=[jax.ShapeDtypeStruct((t, ql), BF16), jax.ShapeDtypeStruct((t, kvl), BF16)],
        compiler_params=_params(("parallel",)),
    )(proj, gq, gkv)


def _mla_norm_bwd(proj, dqn, dkvn, dkr, gq, gkv):
    t, p = proj.shape
    ql, kvl = gq.shape[1], gkv.shape[1]
    tm = _tile(t, 256)

    def body(p_ref, dqn_ref, dkvn_ref, dkr_ref, gq_ref, gkv_ref, dp_ref, dgq_ref, dgkv_ref):
        @pl.when(pl.program_id(0) == 0)
        def _():
            dgq_ref[...] = jnp.zeros_like(dgq_ref)
            dgkv_ref[...] = jnp.zeros_like(dgkv_ref)

        for lo, n, g_ref, d_ref, dg_ref in ((0, ql, gq_ref, dqn_ref, dgq_ref),
                                             (ql, kvl, gkv_ref, dkvn_ref, dgkv_ref)):
            dx, dgt = _rms_bwd_math(d_ref[...], p_ref[:, lo:lo + n], g_ref[...])
            dp_ref[:, lo:lo + n] = dx.astype(BF16)
            dg_ref[...] += jnp.sum(dgt, axis=0, keepdims=True)
        dp_ref[:, ql + kvl:] = dkr_ref[...].astype(BF16)

    def row(n):
        return pl.BlockSpec((tm, n), lambda i: (i, 0))

    def vec(n):
        return pl.BlockSpec((1, n), lambda i: (0, 0))

    return pl.pallas_call(
        body, name="mla_norm_bwd", grid=(t // tm,),
        in_specs=[row(p), row(ql), row(kvl), row(LANE), vec(ql), vec(kvl)],
        out_specs=[row(p), vec(ql), vec(kvl)],
        out_shape=[jax.ShapeDtypeStruct((t, p), BF16), jax.ShapeDtypeStruct((1, ql), F32),
                   jax.ShapeDtypeStruct((1, kvl), F32)],
        compiler_params=_params(("arbitrary",)),
    )(proj, dqn, dkvn, dkr, gq, gkv)


def _mla_q_up(qn, wq, tables):
    t, ql = qn.shape
    n = wq.shape[1]
    tm = _tile(t, 1024)
    scale = QK_DIM ** -0.5

    def epilogue(acc, ex, outs, ids):
        outs[0][:, :QK_NOPE] = (scale * acc[:, :QK_NOPE]).astype(BF16)
        hi = _rope_apply(acc[:, QK_NOPE:], ex[0][...], ex[1][...], ex[2][...])
        outs[0][:, QK_NOPE:] = (scale * hi).astype(BF16)

    tab = pl.BlockSpec((tm, LANE), lambda i, j, k: (i, 0))
    return _matmul(
        "mla_q_up", "nn", (t // tm, n // HEAD_PAD, 1),
        qn, pl.BlockSpec((tm, ql), lambda i, j, k: (i, 0)),
        wq, pl.BlockSpec((ql, HEAD_PAD), lambda i, j, k: (0, j)),
        [(tb, tab) for tb in tables],
        [jax.ShapeDtypeStruct((t, n), BF16)], [pl.BlockSpec((tm, HEAD_PAD), lambda i, j, k: (i, j))],
        None, epilogue)[0]


def _mla_kv_up(kvn, wkv, proj, tables, heads):
    t, kvl = kvn.shape
    n = wkv.shape[1]
    p = proj.shape[1]
    tm = _tile(t, 1024)

    def epilogue(acc, ex, outs, ids):
        kr = _rope_apply(ex[0][...], ex[1][...], ex[2][...], ex[3][...])
        outs[0][:, :QK_NOPE] = acc[:, :QK_NOPE].astype(BF16)
        outs[0][:, QK_NOPE:] = (acc[:, QK_NOPE:] + jnp.where(ids[1] < heads, kr, 0.0)).astype(BF16)

    tab = pl.BlockSpec((tm, LANE), lambda i, j, k: (i, 0))
    kr_spec = pl.BlockSpec((tm, LANE), lambda i, j, k: (i, p // LANE - 1))
    return _matmul(
        "mla_kv_up", "nn", (t // tm, n // HEAD_PAD, 1),
        kvn, pl.BlockSpec((tm, kvl), lambda i, j, k: (i, 0)),
        wkv, pl.BlockSpec((kvl, HEAD_PAD), lambda i, j, k: (0, j)),
        [(proj, kr_spec)] + [(tb, tab) for tb in tables],
        [jax.ShapeDtypeStruct((t, n), BF16)], [pl.BlockSpec((tm, HEAD_PAD), lambda i, j, k: (i, j))],
        None, epilogue)[0]


def _chunk_mask(tq, tk):
    r = lax.broadcasted_iota(jnp.int32, (tq, tk), 0) // CHUNK
    c = lax.broadcasted_iota(jnp.int32, (tq, tk), 1) // CHUNK
    return c <= r


def _flash_fwd(qp, kv, heads):
    t = qp.shape[0]
    tb = _tile(t, 512)
    nb = t // tb
    v_off = heads * HEAD_PAD // V_DIM

    def body(q_ref, k_ref, v_ref, o_ref, lse_ref, m_s, l_s, acc_s):
        qi, ki = pl.program_id(1), pl.program_id(2)

        @pl.when(ki == 0)
        def _():
            m_s[...] = jnp.full_like(m_s, -1e30)
            l_s[...] = jnp.zeros_like(l_s)
            acc_s[...] = jnp.zeros_like(acc_s)

        def step(masked):
            s = _dot(q_ref[...], k_ref[...], "nt")
            if masked:
                s = jnp.where(_chunk_mask(tb, tb), s, -1e30)
            m_prev = m_s[...]
            m_new = jnp.maximum(m_prev, jnp.max(s, axis=1, keepdims=True))
            alpha = jnp.exp(m_prev - m_new)
            pr = jnp.exp(s - m_new)
            l_s[...] = alpha * l_s[...] + jnp.sum(pr, axis=1, keepdims=True)
            acc_s[...] = alpha * acc_s[...] + _dot(pr.astype(BF16), v_ref[...], "nn")
            m_s[...] = m_new

        @pl.when(ki < qi)
        def _():
            step(False)

        @pl.when(ki == qi)
        def _():
            step(True)
            l = l_s[...]
            o_ref[...] = (acc_s[...] / l).astype(BF16)
            lse_ref[...] = jnp.broadcast_to(m_s[...] + jnp.log(l), lse_ref.shape)

    return pl.pallas_call(
        body, name="mla_flash_fwd", grid=(heads, nb, nb),
        in_specs=[pl.BlockSpec((tb, HEAD_PAD), lambda h, qi, ki: (qi, h)),
                  pl.BlockSpec((tb, HEAD_PAD), lambda h, qi, ki: (jnp.minimum(ki, qi), h)),
                  pl.BlockSpec((tb, V_DIM), lambda h, qi, ki: (jnp.minimum(ki, qi), v_off + h))],
        out_specs=[pl.BlockSpec((tb, V_DIM), lambda h, qi, ki: (qi, h)),
                   pl.BlockSpec((None, tb, LANE), lambda h, qi, ki: (h, qi, 0))],
        out_shape=[jax.ShapeDtypeStruct((t, heads * V_DIM), BF16),
                   jax.ShapeDtypeStruct((heads, t, LANE), F32)],
        scratch_shapes=[pltpu.VMEM((tb, 1), F32), pltpu.VMEM((tb, 1), F32), pltpu.VMEM((tb, V_DIM), F32)],
        compiler_params=_params(("parallel", "parallel", "arbitrary")),
    )(qp, kv, kv)


def _flash_bwd(qp, kv, o, do, lse, heads):
    t = qp.shape[0]
    tb = _tile(t, 512)
    nb = t // tb
    v_off = heads * HEAD_PAD // V_DIM

    def body(q_ref, k_ref, v_ref, o_ref, do_ref, lse_ref, dq_ref, dk_ref, dv_ref, dk_s, dv_s):
        ki, qi = pl.program_id(1), pl.program_id(2)

        @pl.when(jnp.logical_and(ki == 0, qi == 0))
        def _():
            dq_ref[...] = jnp.zeros_like(dq_ref)

        @pl.when(qi == ki)
        def _():
            dk_s[...] = jnp.zeros_like(dk_s)
            dv_s[...] = jnp.zeros_like(dv_s)

        def step(masked):
            q = q_ref[...]
            k = k_ref[...]
            dov = do_ref[...]
            s = _dot(q, k, "nt")
            pr = jnp.exp(s - lse_ref[:, :1])
            if masked:
                pr = jnp.where(_chunk_mask(tb, tb), pr, 0.0)
            dv_s[...] += _dot(pr.astype(BF16), dov, "tn")
            dp = _dot(dov, v_ref[...], "nt")
            delta = jnp.sum(dov.astype(F32) * o_ref[...].astype(F32), axis=1, keepdims=True)
            ds = (pr * (dp - delta)).astype(BF16)
            rows = pl.ds(pl.multiple_of(qi * tb, tb), tb)
            dq_ref[rows, :] += _dot(ds, k, "nn")
            dk_s[...] += _dot(ds, q, "tn")

        @pl.when(qi > ki)
        def _():
            step(False)

        @pl.when(qi == ki)
        def _():
            step(True)

        @pl.when(qi == nb - 1)
        def _():
            dk_ref[...] = dk_s[...]
            dv_ref[...] = dv_s[...]

    def qrow(width):
        return pl.BlockSpec((tb, width), lambda h, ki, qi: (jnp.maximum(qi, ki), h))

    return pl.pallas_call(
        body, name="mla_flash_bwd", grid=(heads, nb, nb),
        in_specs=[qrow(HEAD_PAD),
                  pl.BlockSpec((tb, HEAD_PAD), lambda h, ki, qi: (ki, h)),
                  pl.BlockSpec((tb, V_DIM), lambda h, ki, qi: (ki, v_off + h)),
                  qrow(V_DIM), qrow(V_DIM),
                  pl.BlockSpec((None, tb, LANE), lambda h, ki, qi: (h, jnp.maximum(qi, ki), 0))],
        out_specs=[pl.BlockSpec((t, HEAD_PAD), lambda h, ki, qi: (0, h)),
                   pl.BlockSpec((tb, HEAD_PAD), lambda h, ki, qi: (ki, h)),
                   pl.BlockSpec((tb, V_DIM), lambda h, ki, qi: (ki, h))],
        out_shape=[jax.ShapeDtypeStruct((t, heads * HEAD_PAD), F32),
                   jax.ShapeDtypeStruct((t, heads * HEAD_PAD), F32),
                   jax.ShapeDtypeStruct((t, heads * V_DIM), F32)],
        scratch_shapes=[pltpu.VMEM((tb, HEAD_PAD), F32), pltpu.VMEM((tb, V_DIM), F32)],
        compiler_params=_params(("parallel", "arbitrary", "arbitrary")),
    )(qp, kv, kv, o, do, lse)


def _mla_attn_post(dqp, dkp, dv, tables, heads):
    t = dqp.shape[0]
    tm = _tile(t, 256)
    scale = QK_DIM ** -0.5
    kw, vw = heads * HEAD_PAD, heads * V_DIM

    def body(dq_ref, dk_ref, dv_ref, c_ref, su_ref, sd_ref, dqb_ref, dkvb_ref, dkr_ref):
        c, su, sd = c_ref[...], su_ref[...], sd_ref[...]
        kr = jnp.zeros((tm, LANE), F32)
        for h in range(heads):
            lo = h * HEAD_PAD
            mid = lo + QK_NOPE
            dqb_ref[:, lo:mid] = (scale * dq_ref[:, lo:mid]).astype(BF16)
            dqb_ref[:, mid:mid + LANE] = (scale * _rope_apply_t(dq_ref[:, mid:mid + LANE], c, su, sd)).astype(BF16)
            kr = kr + dk_ref[:, mid:mid + LANE]
        dkvb_ref[:, :kw] = dk_ref[...].astype(BF16)
        dkvb_ref[:, kw:] = dv_ref[...].astype(BF16)
        dkr_ref[...] = _rope_apply_t(kr, c, su, sd)

    def row(n):
        return pl.BlockSpec((tm, n), lambda i: (i, 0))

    return pl.pallas_call(
        body, name="mla_attn_post", grid=(t // tm,),
        in_specs=[row(kw), row(kw), row(vw), row(LANE), row(LANE), row(LANE)],
        out_specs=[row(kw), row(kw + vw), row(LANE)],
        out_shape=[jax.ShapeDtypeStruct((t, kw), BF16), jax.ShapeDtypeStruct((t, kw + vw), BF16),
                   jax.ShapeDtypeStruct((t, LANE), F32)],
        compiler_params=_params(("parallel",)),
    )(dqp, dkp, dv, *tables)


def _mla_weights(w_in_g, w_q_g, w_kv_g, w_out_g):
    d = w_in_g.shape[0] * w_in_g.shape[1]
    pw = w_in_g.shape[2]
    w_in = jnp.pad(w_in_g.reshape(d, pw), ((0, 0), (0, LANE - QK_ROPE)))
    ql = w_q_g.shape[1]
    wq = jnp.transpose(w_q_g, (1, 0, 2)).reshape(ql, -1, QK_DIM)
    heads = wq.shape[1]
    wq = jnp.pad(wq, ((0, 0), (0, 0), (0, HEAD_PAD - QK_DIM))).reshape(ql, heads * HEAD_PAD)
    kvl = w_kv_g.shape[1]
    wkv = jnp.transpose(w_kv_g, (1, 0, 2)).reshape(kvl, heads, QK_NOPE + V_DIM)
    wk = jnp.pad(wkv[:, :, :QK_NOPE], ((0, 0), (0, 0), (0, HEAD_PAD - QK_NOPE))).reshape(kvl, heads * HEAD_PAD)
    wv = wkv[:, :, QK_NOPE:].reshape(kvl, heads * V_DIM)
    return w_in, wq, jnp.concatenate([wk, wv], axis=1), w_out_g.reshape(heads * V_DIM, -1), heads


def _mla_unpermute(d_w_in, d_wq, d_wkv, heads, pw):
    d = d_w_in.shape[0]
    g_in = d_w_in[:, :pw].reshape(N_CHIPS, d // N_CHIPS, pw)
    ql = d_wq.shape[0]
    g_q = d_wq.reshape(ql, heads, HEAD_PAD)[:, :, :QK_DIM].reshape(ql, N_CHIPS, -1)
    kvl = d_wkv.shape[0]
    g_k = d_wkv[:, :heads * HEAD_PAD].reshape(kvl, heads, HEAD_PAD)[:, :, :QK_NOPE]
    g_v = d_wkv[:, heads * HEAD_PAD:].reshape(kvl, heads, V_DIM)
    g_kv = jnp.concatenate([g_k, g_v], axis=2).reshape(kvl, N_CHIPS, -1)
    return g_in, jnp.transpose(g_q, (1, 0, 2)), jnp.transpose(g_kv, (1, 0, 2))


def _mla_fwd(x, g, wts, gq, gkv, tables):
    w_in, wq, wkv, w_out, heads = wts
    h = _rms_fwd("mla_norm", x, g)
    proj = _mm_nn_full("mla_proj", h, w_in, F32, tn_pref=w_in.shape[1])
    qn, kvn = _mla_norm_fwd(proj, gq, gkv)
    qp = _mla_q_up(qn, wq, tables)
    kv = _mla_kv_up(kvn, wkv, proj, tables, heads)
    o, lse = _flash_fwd(qp, kv, heads)
    y = _mm_residual("mla_out", o, w_out, x, 1.0, tk_pref=2048)
    return y, (x, h, proj, qn, kvn, qp, kv, o, lse)


def _mla_bwd(saved, g, wts, gq, gkv, tables, dx, dxb, pw):
    w_in, wq, wkv, w_out, heads = wts
    x, h, proj, qn, kvn, qp, kv, o, lse = saved
    d_w_out = _mm_tn("mla_dwout", o, dxb)
    do = _mm_nt_full("mla_do", dxb, w_out, BF16)
    dqp, dkp, dv = _flash_bwd(qp, kv, o, do, lse, heads)
    dqb, dkvb, dkr = _mla_attn_post(dqp, dkp, dv, tables, heads)
    d_wq = _mm_tn("mla_dwq", qn, dqb)
    dqn = _mm_nt_k("mla_dqn", dqb, wq, F32)
    d_wkv = _mm_tn("mla_dwkv", kvn, dkvb)
    dkvn = _mm_nt_k("mla_dkvn", dkvb, wkv, F32)
    dproj, dgq, dgkv = _mla_norm_bwd(proj, dqn, dkvn, dkr, gq, gkv)
    d_w_in = _mm_tn("mla_dwin", h, dproj, tn_pref=dproj.shape[1])
    dh = _mm_nt_full("mla_dh", dproj, w_in, F32, tn_pref=1024)
    dx, dxb, dg = _rms_bwd("mla_dnorm", dh, x, g, dx)
    g_in, g_q, g_kv = _mla_unpermute(d_w_in, d_wq, d_wkv, heads, pw)
    small = dict(ln=dg, gq=dgq, gkv=dgkv)
    return dx, dxb, small, g_in, g_q, g_kv, d_w_out.reshape(N_CHIPS, d_w_out.shape[0] // N_CHIPS, -1)


def _place():
    x, y, c = lax.axis_index("x"), lax.axis_index("y"), lax.axis_index("c")
    chips = [(1 - x, y), (x, 1 - y), (1 - x, 1 - y)]
    return x, y, c, chips


ANY = pl.BlockSpec(memory_space=pl.ANY)


def _gather_weights(shards, norms):
    nt = len(shards)

    def body(*refs):
        ins, n_in = refs[:nt], refs[nt]
        outs, n_out = refs[nt + 1:2 * nt + 1], refs[2 * nt + 1]
        send, recv, fsend, frecv, loc, nsend, nrecv = refs[2 * nt + 2:]
        x, y, c, chips = _place()
        me = 2 * x + y
        sib = (x, y, 1 - c)

        local = [pltpu.make_async_copy(ins[t], outs[t].at[me], loc.at[t]) for t in range(nt)]
        local.append(pltpu.make_async_copy(n_in, n_out.at[me], loc.at[nt]))
        for cp in local:
            cp.start()

        def ici(t, j, chip):
            return pltpu.make_async_remote_copy(
                src_ref=ins[t].at[c], dst_ref=outs[t].at[me, c], send_sem=send.at[t, j], recv_sem=recv.at[t, j],
                device_id=(*chip, c), device_id_type=MESH)

        def landed(t, j, chip, half):
            return outs[t].at[2 * chip[0] + chip[1], half]

        def fwd(t, j, chip):
            return pltpu.make_async_remote_copy(
                src_ref=landed(t, j, chip, c), dst_ref=landed(t, j, chip, c), send_sem=fsend.at[t, j],
                recv_sem=frecv.at[t, j], device_id=sib, device_id_type=MESH)

        def nrm(j, chip):
            return pltpu.make_async_remote_copy(
                src_ref=n_in, dst_ref=n_out.at[me], send_sem=nsend.at[j], recv_sem=nrecv.at[j],
                device_id=(*chip, c), device_id_type=MESH)

        firsts = [ici(t, j, chip) for t in range(nt) for j, chip in enumerate(chips)]
        firsts += [nrm(j, chip) for j, chip in enumerate(chips)]
        for cp in firsts:
            cp.start()
        passed = []
        for t in range(nt):
            for j, chip in enumerate(chips):
                pltpu.make_async_remote_copy(
                    src_ref=ins[t].at[c], dst_ref=landed(t, j, chip, c), send_sem=send.at[t, j],
                    recv_sem=recv.at[t, j], device_id=(*chip, c), device_id_type=MESH).wait_recv()
                cp = fwd(t, j, chip)
                cp.start()
                passed.append(cp)
        for t in range(nt):
            for j, chip in enumerate(chips):
                pltpu.make_async_remote_copy(
                    src_ref=landed(t, j, chip, 1 - c), dst_ref=landed(t, j, chip, 1 - c), send_sem=fsend.at[t, j],
                    recv_sem=frecv.at[t, j], device_id=sib, device_id_type=MESH).wait_recv()
        for j, chip in enumerate(chips):
            pltpu.make_async_remote_copy(
                src_ref=n_in, dst_ref=n_out.at[2 * chip[0] + chip[1]], send_sem=nsend.at[j], recv_sem=nrecv.at[j],
                device_id=(*chip, c), device_id_type=MESH).wait_recv()
        for cp in firsts + passed:
            cp.wait_send()
        for cp in local:
            cp.wait()

    out_shape = [jax.ShapeDtypeStruct((N_CHIPS,) + s.shape, s.dtype) for s in shards]
    out_shape.append(jax.ShapeDtypeStruct((N_CHIPS,) + norms.shape, norms.dtype))
    res = pl.pallas_call(
        body, name="gather_weights",
        in_specs=[ANY] * (nt + 1), out_specs=[ANY] * (nt + 1), out_shape=out_shape,
        scratch_shapes=[pltpu.SemaphoreType.DMA((nt, 3)), pltpu.SemaphoreType.DMA((nt, 3)),
                        pltpu.SemaphoreType.DMA((nt, 3)), pltpu.SemaphoreType.DMA((nt, 3)),
                        pltpu.SemaphoreType.DMA((nt + 1,)), pltpu.SemaphoreType.DMA((3,)),
                        pltpu.SemaphoreType.DMA((3,))],
    )(*shards, norms)
    return res[:nt], res[nt]


def _swap_halves(grads):
    nt = len(grads)

    def body(*refs):
        ins, outs, own = refs[:nt], refs[nt:2 * nt], refs[2 * nt:3 * nt]
        send, recv, loc = refs[3 * nt:]
        x, y, c, _ = _place()
        local = [pltpu.make_async_copy(ins[t].at[:, c], own[t], loc.at[t]) for t in range(nt)]
        cps = [pltpu.make_async_remote_copy(
            src_ref=ins[t].at[:, 1 - c], dst_ref=outs[t], send_sem=send.at[t], recv_sem=recv.at[t],
            device_id=(x, y, 1 - c), device_id_type=MESH) for t in range(nt)]
        for cp in local + cps:
            cp.start()
        for cp in cps + local:
            cp.wait()

    shapes = [jax.ShapeDtypeStruct((g.shape[0],) + g.shape[2:], g.dtype) for g in grads]
    res = pl.pallas_call(
        body, name="grad_swap_halves",
        in_specs=[ANY] * nt, out_specs=[ANY] * (2 * nt), out_shape=shapes + shapes,
        scratch_shapes=[pltpu.SemaphoreType.DMA((nt,)), pltpu.SemaphoreType.DMA((nt,)),
                        pltpu.SemaphoreType.DMA((nt,))],
    )(*grads)
    return res[:nt], res[nt:]


def _scatter_chips(parts):
    nt = len(parts)

    def body(*refs):
        ins, outs = refs[:nt], refs[nt:2 * nt]
        send, recv, loc = refs[2 * nt:]
        x, y, c, chips = _place()
        local = [pltpu.make_async_copy(ins[t].at[2 * x + y], outs[t].at[3], loc.at[t]) for t in range(nt)]
        cps = [pltpu.make_async_remote_copy(
            src_ref=ins[t].at[2 * chip[0] + chip[1]], dst_ref=outs[t].at[j], send_sem=send.at[t, j],
            recv_sem=recv.at[t, j], device_id=(*chip, c), device_id_type=MESH)
            for t in range(nt) for j, chip in enumerate(chips)]
        for cp in local + cps:
            cp.start()
        for cp in cps + local:
            cp.wait()

    return pl.pallas_call(
        body, name="grad_scatter_chips",
        in_specs=[ANY] * nt, out_specs=[ANY] * nt,
        out_shape=[jax.ShapeDtypeStruct(p.shape, p.dtype) for p in parts],
        scratch_shapes=[pltpu.SemaphoreType.DMA((nt, 3)), pltpu.SemaphoreType.DMA((nt, 3)),
                        pltpu.SemaphoreType.DMA((nt,))],
    )(*parts)


def _join_halves(halves):
    nt = len(halves)

    def body(*refs):
        ins, outs = refs[:nt], refs[nt:2 * nt]
        send, recv, loc = refs[2 * nt:]
        x, y, c, _ = _place()
        local = [pltpu.make_async_copy(ins[t], outs[t].at[c], loc.at[t]) for t in range(nt)]
        cps = [pltpu.make_async_remote_copy(
            src_ref=ins[t], dst_ref=outs[t].at[c], send_sem=send.at[t], recv_sem=recv.at[t],
            device_id=(x, y, 1 - c), device_id_type=MESH) for t in range(nt)]
        for cp in local + cps:
            cp.start()
        for t in range(nt):
            pltpu.make_async_remote_copy(
                src_ref=ins[t], dst_ref=outs[t].at[1 - c], send_sem=send.at[t], recv_sem=recv.at[t],
                device_id=(x, y, 1 - c), device_id_type=MESH).wait_recv()
        for cp in cps:
            cp.wait_send()
        for cp in local:
            cp.wait()

    return pl.pallas_call(
        body, name="grad_join_halves",
        in_specs=[ANY] * nt, out_specs=[ANY] * nt,
        out_shape=[jax.ShapeDtypeStruct((2,) + h.shape, h.dtype) for h in halves],
        scratch_shapes=[pltpu.SemaphoreType.DMA((nt,)), pltpu.SemaphoreType.DMA((nt,)),
                        pltpu.SemaphoreType.DMA((nt,))],
    )(*halves)


def _gather_all(block):
    m_per, n = block.shape

    def body(x_ref, out_ref, send_sems, recv_sems, local_sem):
        x, y, c, chips = _place()
        me, sibling = (x, y, c), (x, y, 1 - c)

        def rows(px, py, pc):
            return out_ref.at[pl.ds((4 * px + 2 * py + pc) * m_per, m_per), :]

        def copy(k, block_of, to, src=None):
            return pltpu.make_async_remote_copy(
                src_ref=rows(*block_of) if src is None else src, dst_ref=rows(*block_of),
                send_sem=send_sems.at[k], recv_sem=recv_sems.at[k], device_id=to, device_id_type=MESH)

        mine = pltpu.make_async_copy(x_ref, rows(*me), local_sem)
        mine.start()
        first = [copy(0, me, sibling, src=x_ref)]
        first += [copy(1 + j, me, (*chip, c), src=x_ref) for j, chip in enumerate(chips)]
        for cp in first:
            cp.start()
        passed = [copy(4 + j, (*chip, c), sibling) for j, chip in enumerate(chips)]
        for j, chip in enumerate(chips):
            copy(1 + j, (*chip, c), me).wait_recv()
            passed[j].start()
        copy(0, sibling, me).wait_recv()
        for j, chip in enumerate(chips):
            copy(4 + j, (*chip, 1 - c), me).wait_recv()
        for cp in first + passed:
            cp.wait_send()
        mine.wait()

    return pl.pallas_call(
        body, name="gather_small_grads",
        out_shape=jax.ShapeDtypeStruct((N_DEV * m_per, n), block.dtype),
        in_specs=[pl.BlockSpec(memory_space=pltpu.VMEM)],
        out_specs=pl.BlockSpec(memory_space=pltpu.VMEM),
        scratch_shapes=[pltpu.SemaphoreType.DMA((7,)), pltpu.SemaphoreType.DMA((7,)), pltpu.SemaphoreType.DMA],
    )(block)


def _row_tile(r, c, elems=512 * 1024):
    t = max(8, min(r, (elems // c) // 8 * 8))
    while t > 8 and r % t:
        t -= 8
    return t if r % t == 0 else r


def _add_halves(own, other):
    n, r, w = own.shape
    tr = _row_tile(r, w)

    def body(g_ref, o_ref, out_ref):
        out_ref[...] = (g_ref[...].astype(F32) + o_ref[...].astype(F32)).astype(BF16)

    blk = pl.BlockSpec((None, tr, w), lambda k, i: (k, i, 0))
    return pl.pallas_call(
        body, name="grad_add_halves", grid=(n, r // tr), in_specs=[blk, blk], out_specs=blk,
        out_shape=jax.ShapeDtypeStruct((n, r, w), BF16),
        compiler_params=_params(("parallel", "parallel")),
    )(own, other)


def _sum_chips(landed):
    n, r, w = landed.shape
    tr = _row_tile(r, w)

    def body(l_ref, out_ref):
        acc = l_ref[0].astype(F32)
        for j in range(1, n):
            acc = acc + l_ref[j].astype(F32)
        out_ref[...] = acc

    return pl.pallas_call(
        body, name="grad_sum_chips", grid=(r // tr,),
        in_specs=[pl.BlockSpec((n, tr, w), lambda i: (0, i, 0))],
        out_specs=pl.BlockSpec((tr, w), lambda i: (i, 0)),
        out_shape=jax.ShapeDtypeStruct((r, w), F32),
        compiler_params=_params(("parallel",)),
    )(landed)


def _adamw_math(w, g, m, v):
    m = ADAM_B1 * m + (1.0 - ADAM_B1) * g
    v = ADAM_B2 * v + (1.0 - ADAM_B2) * (g * g)
    m_hat = m / (1.0 - ADAM_B1 ** ADAM_STEP)
    v_hat = v / (1.0 - ADAM_B2 ** ADAM_STEP)
    delta = -ADAM_LR * (m_hat / (jnp.sqrt(v_hat) + ADAM_EPS) + ADAM_WD * w)
    return delta, m, v


def _adamw(name, w, g, m, v):
    r, c = w.shape
    tr = _row_tile(r, c, 256 * 1024)

    def body(w_ref, g_ref, m_ref, v_ref, d_ref, nm_ref, nv_ref):
        d_ref[...], nm_ref[...], nv_ref[...] = _adamw_math(w_ref[...], g_ref[...], m_ref[...], v_ref[...])

    blk = pl.BlockSpec((tr, c), lambda i: (i, 0))
    return pl.pallas_call(
        body, name=name, grid=(r // tr,), in_specs=[blk] * 4, out_specs=[blk] * 3,
        out_shape=[jax.ShapeDtypeStruct((r, c), F32)] * 3,
        compiler_params=_params(("parallel",)),
    )(w, g, m, v)


def _adamw_summed(w, parts, m, v):
    r, c = w.shape

    def body(w_ref, p_ref, m_ref, v_ref, g_ref, d_ref, nm_ref, nv_ref):
        g = p_ref[0:r, :]
        for k in range(1, N_DEV):
            g = g + p_ref[k * r:(k + 1) * r, :]
        g_ref[...] = g
        d_ref[...], nm_ref[...], nv_ref[...] = _adamw_math(w_ref[...], g, m_ref[...], v_ref[...])

    return pl.pallas_call(
        body, name="adamw_replicated",
        out_shape=[jax.ShapeDtypeStruct((r, c), F32)] * 4,
        compiler_params=pltpu.CompilerParams(vmem_limit_bytes=VMEM_LIMIT_BYTES),
    )(w, parts, m, v)


def _pack(arrays):
    return jnp.concatenate([a.reshape(-1, LANE) for a in arrays], axis=0)


def _unpack(packed, shapes):
    out, row = [], 0
    for s in shapes:
        n = math.prod(s) // LANE
        out.append(packed[row:row + n].reshape(s))
        row += n
    return out


def _halves(w):
    if w.shape[0] == 2:
        return w
    return w.reshape(2, w.shape[1] // 2, w.shape[2])


def _cast_bf16(name, w):
    r, c = w.shape
    tr = _row_tile(r, c)

    def body(w_ref, o_ref):
        o_ref[...] = w_ref[...].astype(BF16)

    blk = pl.BlockSpec((tr, c), lambda i: (i, 0))
    return pl.pallas_call(
        body, name=name, grid=(r // tr,), in_specs=[blk], out_specs=blk,
        out_shape=jax.ShapeDtypeStruct((r, c), BF16), compiler_params=_params(("parallel",)),
    )(w)


BIG = ["ffn1_w_in", "ffn1_w_out", "ffn2_w_in", "ffn2_w_out", "sgu_w_in", "sgu_w_out",
       "mla_w_in", "mla_w_q_up", "mla_w_kv_up", "mla_w_out"]
REPLICATED = ["ln_ffn1", "ln_mix", "ln_ffn2", "sgu_v_gain", "sgu_v_bias", "sgu_w_spatial", "sgu_b_spatial",
              "ln_final"]
NORM_SHARDS = ["mla_q_norm", "mla_kv_norm"]
WEIGHTS = ["ln_ffn1", "ffn1_w_in", "ffn1_w_out", "ln_mix", "ln_ffn2", "ffn2_w_in", "ffn2_w_out", "sgu_w_in",
           "sgu_v_gain", "sgu_v_bias", "sgu_w_spatial", "sgu_b_spatial", "sgu_w_out", "mla_w_in", "mla_q_norm",
           "mla_w_q_up", "mla_kv_norm", "mla_w_kv_up", "mla_w_out", "ln_final"]


def kernel(x, positions, ln_ffn1, ffn1_w_in, ffn1_w_out, ln_mix, ln_ffn2, ffn2_w_in, ffn2_w_out, sgu_w_in, sgu_v_gain, sgu_v_bias, sgu_w_spatial, sgu_b_spatial, sgu_w_out, mla_w_in, mla_q_norm, mla_w_q_up, mla_kv_norm, mla_w_kv_up, mla_w_out, ln_final, loss_target, m_ln_ffn1, m_ffn1_w_in, m_ffn1_w_out, m_ln_mix, m_ln_ffn2, m_ffn2_w_in, m_ffn2_w_out, m_sgu_w_in, m_sgu_v_gain, m_sgu_v_bias, m_sgu_w_spatial, m_sgu_b_spatial, m_sgu_w_out, m_mla_w_in, m_mla_q_norm, m_mla_w_q_up, m_mla_kv_norm, m_mla_w_kv_up, m_mla_w_out, m_ln_final, v_ln_ffn1, v_ffn1_w_in, v_ffn1_w_out, v_ln_mix, v_ln_ffn2, v_ffn2_w_in, v_ffn2_w_out, v_sgu_w_in, v_sgu_v_gain, v_sgu_v_bias, v_sgu_w_spatial, v_sgu_b_spatial, v_sgu_w_out, v_mla_w_in, v_mla_q_norm, v_mla_w_q_up, v_mla_kv_norm, v_mla_w_kv_up, v_mla_w_out, v_ln_final):
    given = dict(locals())
    w = {n: given[n] for n in WEIGHTS}
    mom = {n: given["m_" + n] for n in WEIGHTS}
    var = {n: given["v_" + n] for n in WEIGHTS}
    t, d = x.shape[1], x.shape[2]
    xs = x.reshape(t, d)
    target = loss_target.reshape(t, d)
    me = 2 * lax.axis_index("x") + lax.axis_index("y")

    shards = []
    for n in BIG:
        h2 = _halves(w[n])
        flat = h2.reshape(-1, h2.shape[-1])
        shards.append(_cast_bf16("cast_" + n, flat).reshape(h2.shape))
    nq = mla_q_norm.shape[1]
    norms = jnp.pad(jnp.concatenate([mla_q_norm, mla_kv_norm], axis=0), ((0, 6), (0, LANE - nq)))
    gathered, norms_g = _gather_weights(shards, norms)
    wg = {}
    for n, a in zip(BIG, gathered):
        if w[n].shape[0] == 2:
            wg[n] = [a[:, 0], a[:, 1]]
        else:
            wg[n] = a.reshape(N_CHIPS, a.shape[1] * a.shape[2], a.shape[3])
    gq = norms_g[:, 0, :nq].reshape(1, N_CHIPS * nq)
    gkv = norms_g[:, 1, :nq].reshape(1, N_CHIPS * nq)

    def row_sharded(a):
        return a.reshape(-1, a.shape[-1])

    mla_wts = _mla_weights(wg["mla_w_in"], wg["mla_w_q_up"], wg["mla_w_kv_up"], wg["mla_w_out"])
    tables = _rope_tables(positions.reshape(t))
    sgu_small = (sgu_v_gain, sgu_v_bias, sgu_w_spatial[0], sgu_b_spatial[0])

    a0, s_f1_0 = _ffn_fwd("l0_ffn1", xs, ln_ffn1[0], wg["ffn1_w_in"][0], row_sharded(wg["ffn1_w_out"][0]))
    a1, s_sgu = _sgu_fwd(a0, ln_mix[0], wg["sgu_w_in"], *sgu_small, row_sharded(wg["sgu_w_out"]))
    a2, s_f2_0 = _ffn_fwd("l0_ffn2", a1, ln_ffn2[0], wg["ffn2_w_in"][0], row_sharded(wg["ffn2_w_out"][0]))
    a3, s_f1_1 = _ffn_fwd("l1_ffn1", a2, ln_ffn1[1], wg["ffn1_w_in"][1], row_sharded(wg["ffn1_w_out"][1]))
    a4, s_mla = _mla_fwd(a3, ln_mix[1], mla_wts, gq, gkv, tables)
    a5, s_f2_1 = _ffn_fwd("l1_ffn2", a4, ln_ffn2[1], wg["ffn2_w_in"][1], row_sharded(wg["ffn2_w_out"][1]))

    loss_part, dx, dxb, dg_final = _loss_bwd(a5, ln_final, target)
    dx, dxb, dg_f2_1, dwin_f2_1, dwout_f2_1 = _ffn_bwd(
        "l1_ffn2", s_f2_1, ln_ffn2[1], wg["ffn2_w_in"][1], row_sharded(wg["ffn2_w_out"][1]), dx, dxb)
    dx, dxb, sm_mla, g_mla_in, g_mla_q, g_mla_kv, g_mla_out = _mla_bwd(
        s_mla, ln_mix[1], mla_wts, gq, gkv, tables, dx, dxb, mla_w_in.shape[2])
    dx, dxb, dg_f1_1, dwin_f1_1, dwout_f1_1 = _ffn_bwd(
        "l1_ffn1", s_f1_1, ln_ffn1[1], wg["ffn1_w_in"][1], row_sharded(wg["ffn1_w_out"][1]), dx, dxb)
    dx, dxb, dg_f2_0, dwin_f2_0, dwout_f2_0 = _ffn_bwd(
        "l0_ffn2", s_f2_0, ln_ffn2[0], wg["ffn2_w_in"][0], row_sharded(wg["ffn2_w_out"][0]), dx, dxb)
    dx, dxb, sm_sgu, g_sgu_in, g_sgu_out = _sgu_bwd(
        s_sgu, ln_mix[0], wg["sgu_w_in"], *sgu_small, row_sharded(wg["sgu_w_out"]), dx, dxb)
    dx, dxb, dg_f1_0, dwin_f1_0, dwout_f1_0 = _ffn_bwd(
        "l0_ffn1", s_f1_0, ln_ffn1[0], wg["ffn1_w_in"][0], row_sharded(wg["ffn1_w_out"][0]), dx, dxb)

    def two_layers(g0, g1):
        return jnp.stack([g0, g1], axis=1)

    def two_rows(g):
        return g.reshape(N_CHIPS, 2, g.shape[1] // 2, g.shape[2])

    local = [two_layers(dwin_f1_0, dwin_f1_1), two_layers(dwout_f1_0, dwout_f1_1),
             two_layers(dwin_f2_0, dwin_f2_1), two_layers(dwout_f2_0, dwout_f2_1),
             two_rows(g_sgu_in), two_rows(g_sgu_out), two_rows(g_mla_in), two_rows(g_mla_q),
             two_rows(g_mla_kv), two_rows(g_mla_out)]
    swapped, own = _swap_halves(local)
    parts = [_add_halves(g, o) for g, o in zip(own, swapped)]
    landed = _scatter_chips(parts)
    halves = [_sum_chips(l) for l in landed]
    joined = _join_halves(halves)
    big_grad = {n: j.reshape(w[n].shape) for n, j in zip(BIG, joined)}

    small_parts = [
        jnp.concatenate([dg_f1_0, dg_f1_1], axis=0), jnp.concatenate([sm_sgu["ln"], sm_mla["ln"]], axis=0),
        jnp.concatenate([dg_f2_0, dg_f2_1], axis=0), sm_sgu["gain"], sm_sgu["bias"], sm_sgu["w_sp"],
        sm_sgu["b_sp"], dg_final]
    rep_shapes = [w[n].shape for n in REPLICATED]
    gq_row = jnp.pad(sm_mla["gq"], ((0, 0), (0, N_CHIPS * (LANE - nq))))
    gkv_row = jnp.pad(sm_mla["gkv"], ((0, 0), (0, N_CHIPS * (LANE - nq))))
    packed = _pack(small_parts + [gq_row, gkv_row, loss_part])
    packed = jnp.pad(packed, ((0, -packed.shape[0] % 8), (0, 0)))
    everyone = _gather_all(packed)
    rows = packed.shape[0]
    zero_rows = jnp.zeros((rows - sum(math.prod(s) // LANE for s in rep_shapes), LANE), F32)
    pw = jnp.concatenate([_pack([w[n] for n in REPLICATED]), zero_rows], axis=0)
    pm = jnp.concatenate([_pack([mom[n] for n in REPLICATED]), zero_rows], axis=0)
    pv = jnp.concatenate([_pack([var[n] for n in REPLICATED]), zero_rows + 1.0], axis=0)
    g_all, d_all, m_all, v_all = _adamw_summed(pw, everyone, pm, pv)
    tail_shapes = [(1, N_CHIPS * LANE), (1, N_CHIPS * LANE), (1, LANE)]
    rep_grad = dict(zip(REPLICATED, _unpack(g_all, rep_shapes + tail_shapes)[:len(REPLICATED)]))
    rep_delta = dict(zip(REPLICATED, _unpack(d_all, rep_shapes)))
    rep_m = dict(zip(REPLICATED, _unpack(m_all, rep_shapes)))
    rep_v = dict(zip(REPLICATED, _unpack(v_all, rep_shapes)))
    tail = _unpack(g_all, rep_shapes + tail_shapes)[len(REPLICATED):]
    loss = tail[2][0, 0]
    norm_grad = {
        "mla_q_norm": lax.dynamic_slice(tail[0], (0, me * nq), (1, nq)),
        "mla_kv_norm": lax.dynamic_slice(tail[1], (0, me * nq), (1, nq)),
    }

    grad, delta, new_m, new_v = {}, {}, {}, {}
    for n in BIG:
        shp = w[n].shape
        flat = lambda a: a.reshape(-1, shp[-1])
        dl, nm, nv = _adamw("adamw_" + n, flat(w[n]), flat(big_grad[n]), flat(mom[n]), flat(var[n]))
        grad[n], delta[n], new_m[n], new_v[n] = big_grad[n], dl.reshape(shp), nm.reshape(shp), nv.reshape(shp)
    for n in REPLICATED:
        grad[n], delta[n], new_m[n], new_v[n] = rep_grad[n], rep_delta[n], rep_m[n], rep_v[n]
    stack = lambda dct: jnp.concatenate([dct[n] for n in NORM_SHARDS], axis=0)
    dl, nm, nv = _adamw("adamw_norm_shards", stack(w), stack(norm_grad), stack(mom), stack(var))
    for i, n in enumerate(NORM_SHARDS):
        grad[n], delta[n], new_m[n], new_v[n] = norm_grad[n], dl[i:i + 1], nm[i:i + 1], nv[i:i + 1]

    grad_x = dx.reshape(x.shape)
    return (loss, grad_x, *[grad[n] for n in WEIGHTS], *[delta[n] for n in WEIGHTS],
            *[new_m[n] for n in WEIGHTS], *[new_v[n] for n in WEIGHTS])
```

```python
import functools
import math

import jax
import jax.numpy as jnp
from jax import lax
from jax.experimental import pallas as pl
from jax.experimental.pallas import tpu as pltpu

F32 = jnp.float32
BF16 = jnp.bfloat16
MESH = pl.DeviceIdType.MESH

EPS = 1e-6
CHUNK = 64
SGU_BLOCK = 128
SGU_GROUPS = 8
QK_NOPE = 128
QK_ROPE = 64
V_DIM = 128
QK_DIM = QK_NOPE + QK_ROPE
HEAD_PAD = 256
ROPE_THETA = 10000.0
N_CHIPS = 4
N_DEV = 8

ADAM_LR = 0.001
ADAM_B1 = 0.9
ADAM_B2 = 0.999
ADAM_EPS = 1e-08
ADAM_WD = 0.01
ADAM_STEP = 10

LANE = 128
VMEM_LIMIT_BYTES = 56 * 1024 * 1024

_DIMS = {
    "nn": (((1,), (0,)), ((), ())),
    "nt": (((1,), (1,)), ((), ())),
    "tn": (((0,), (0,)), ((), ())),
}


def _tile(n, pref):
    t = (min(pref, n) // LANE) * LANE
    while t >= LANE:
        if n % t == 0:
            return t
        t -= LANE
    return n


def _params(sem):
    return pltpu.CompilerParams(dimension_semantics=sem, vmem_limit_bytes=VMEM_LIMIT_BYTES)


def _dot(a, b, mode):
    return lax.dot_general(a, b, _DIMS[mode], preferred_element_type=F32)


def _matmul(name, mode, grid, a, a_spec, b, b_spec, extras, out_shapes, out_specs, acc_shape, epilogue):
    nk = grid[2]
    n_ex = len(extras)
    n_out = len(out_shapes)

    def body(*refs):
        a_ref, b_ref = refs[0], refs[1]
        ex = refs[2:2 + n_ex]
        outs = refs[2 + n_ex:2 + n_ex + n_out]
        ids = (pl.program_id(0), pl.program_id(1))
        part = _dot(a_ref[...], b_ref[...], mode)
        if nk == 1:
            epilogue(part, ex, outs, ids)
        else:
            acc = refs[-1]
            k = pl.program_id(2)

            @pl.when(k == 0)
            def _():
                acc[...] = part

            @pl.when(k > 0)
            def _():
                acc[...] += part

            @pl.when(k == nk - 1)
            def _():
                epilogue(acc[...], ex, outs, ids)

    scratch = [pltpu.VMEM(acc_shape, F32)] if nk > 1 else []
    return pl.pallas_call(
        body,
        name=name,
        grid=grid,
        in_specs=[a_spec, b_spec] + [s for _, s in extras],
        out_specs=out_specs,
        out_shape=out_shapes,
        scratch_shapes=scratch,
        compiler_params=_params(("parallel", "parallel", "arbitrary")),
    )(a, b, *[e for e, _ in extras])


def _store(scale, dtype):
    def epilogue(acc, ex, outs, ids):
        v = acc if scale == 1.0 else acc * scale
        outs[0][...] = v.astype(dtype)

    return epilogue


def _mm_nn_full(name, a, b, out_dtype, tm_pref=1024, tn_pref=512):
    m, kd = a.shape
    n = b.shape[1]
    tm, tn = _tile(m, tm_pref), _tile(n, tn_pref)
    return _matmul(
        name, "nn", (m // tm, n // tn, 1),
        a, pl.BlockSpec((tm, kd), lambda i, j, k: (i, 0)),
        b, pl.BlockSpec((kd, tn), lambda i, j, k: (0, j)),
        [], [jax.ShapeDtypeStruct((m, n), out_dtype)], [pl.BlockSpec((tm, tn), lambda i, j, k: (i, j))],
        None, _store(1.0, out_dtype))[0]


def _mm_nt_full(name, a, b, out_dtype, scale=1.0, tm_pref=1024, tn_pref=512):
    m, kd = a.shape
    n = b.shape[0]
    tm, tn = _tile(m, tm_pref), _tile(n, tn_pref)
    return _matmul(
        name, "nt", (m // tm, n // tn, 1),
        a, pl.BlockSpec((tm, kd), lambda i, j, k: (i, 0)),
        b, pl.BlockSpec((tn, kd), lambda i, j, k: (j, 0)),
        [], [jax.ShapeDtypeStruct((m, n), out_dtype)], [pl.BlockSpec((tm, tn), lambda i, j, k: (i, j))],
        None, _store(scale, out_dtype))[0]


def _mm_nt_k(name, a, b, out_dtype, tk_pref=1024, tm_pref=1024, tn_pref=1024):
    m, kd = a.shape
    n = b.shape[0]
    tm, tn, tk = _tile(m, tm_pref), _tile(n, tn_pref), _tile(kd, tk_pref)
    return _matmul(
        name, "nt", (m // tm, n // tn, kd // tk),
        a, pl.BlockSpec((tm, tk), lambda i, j, k: (i, k)),
        b, pl.BlockSpec((tn, tk), lambda i, j, k: (j, k)),
        [], [jax.ShapeDtypeStruct((m, n), out_dtype)], [pl.BlockSpec((tm, tn), lambda i, j, k: (i, j))],
        (tm, tn), _store(1.0, out_dtype))[0]


def _mm_tn(name, a, b, scale=1.0, tm_pref=1024, tn_pref=1024, tk_pref=512):
    t, m = a.shape
    n = b.shape[1]
    tm, tn, tk = _tile(m, tm_pref), _tile(n, tn_pref), _tile(t, tk_pref)
    return _matmul(
        name, "tn", (m // tm, n // tn, t // tk),
        a, pl.BlockSpec((tk, tm), lambda i, j, k: (k, i)),
        b, pl.BlockSpec((tk, tn), lambda i, j, k: (k, j)),
        [], [jax.ShapeDtypeStruct((m, n), BF16)], [pl.BlockSpec((tm, tn), lambda i, j, k: (i, j))],
        (tm, tn), _store(scale, BF16))[0]


def _mm_residual(name, a, b, x, scale, tm_pref=1024, tn_pref=512, tk_pref=1408):
    m, kd = a.shape
    n = b.shape[1]
    tm, tn, tk = _tile(m, tm_pref), _tile(n, tn_pref), _tile(kd, tk_pref)

    def epilogue(acc, ex, outs, ids):
        outs[0][...] = ex[0][...] + scale * acc

    return _matmul(
        name, "nn", (m // tm, n // tn, kd // tk),
        a, pl.BlockSpec((tm, tk), lambda i, j, k: (i, k)),
        b, pl.BlockSpec((tk, tn), lambda i, j, k: (k, j)),
        [(x, pl.BlockSpec((tm, tn), lambda i, j, k: (i, j)))],
        [jax.ShapeDtypeStruct((m, n), F32)], [pl.BlockSpec((tm, tn), lambda i, j, k: (i, j))],
        (tm, tn), epilogue)[0]


def _rms_fwd(name, x, g):
    t, d = x.shape
    tm = _tile(t, 512)

    def body(x_ref, g_ref, h_ref):
        xv = x_ref[...]
        r = lax.rsqrt(jnp.mean(xv * xv, axis=-1, keepdims=True) + EPS)
        h_ref[...] = (xv * r * g_ref[...]).astype(BF16)

    return pl.pallas_call(
        body, name=name, grid=(t // tm,),
        in_specs=[pl.BlockSpec((tm, d), lambda i: (i, 0)), pl.BlockSpec((1, d), lambda i: (0, 0))],
        out_specs=pl.BlockSpec((tm, d), lambda i: (i, 0)),
        out_shape=jax.ShapeDtypeStruct((t, d), BF16),
        compiler_params=_params(("parallel",)),
    )(x, g.reshape(1, d))


def _rms_bwd_math(dh, xv, g):
    r = lax.rsqrt(jnp.mean(xv * xv, axis=-1, keepdims=True) + EPS)
    xhat = xv * r
    dxh = dh * g
    dx = r * (dxh - xhat * jnp.mean(dxh * xhat, axis=-1, keepdims=True))
    return dx, dh * xhat


def _rms_bwd(name, dh, x, g, dres):
    t, d = x.shape
    tm = _tile(t, 256)

    def body(dh_ref, x_ref, g_ref, dres_ref, dx_ref, dxb_ref, dg_ref):
        dx, dgt = _rms_bwd_math(dh_ref[...].astype(F32), x_ref[...], g_ref[...])
        dx = dres_ref[...] + dx
        dx_ref[...] = dx
        dxb_ref[...] = dx.astype(BF16)

        @pl.when(pl.program_id(0) == 0)
        def _():
            dg_ref[...] = jnp.zeros_like(dg_ref)

        dg_ref[...] += jnp.sum(dgt, axis=0, keepdims=True)

    row = pl.BlockSpec((tm, d), lambda i: (i, 0))
    vec = pl.BlockSpec((1, d), lambda i: (0, 0))
    return pl.pallas_call(
        body, name=name, grid=(t // tm,),
        in_specs=[row, row, vec, row],
        out_specs=[row, row, vec],
        out_shape=[jax.ShapeDtypeStruct((t, d), F32), jax.ShapeDtypeStruct((t, d), BF16),
                   jax.ShapeDtypeStruct((1, d), F32)],
        compiler_params=_params(("arbitrary",)),
    )(dh, x, g.reshape(1, d), dres)


def _loss_bwd(x, g, target):
    t, d = x.shape
    tm = _tile(t, 256)

    def body(x_ref, g_ref, tgt_ref, loss_ref, dx_ref, dxb_ref, dg_ref):
        xv = x_ref[...]
        gv = g_ref[...]
        r = lax.rsqrt(jnp.mean(xv * xv, axis=-1, keepdims=True) + EPS)
        err = xv * r * gv - tgt_ref[...]
        part = 0.5 * jnp.sum(jnp.mean(err * err, axis=-1, keepdims=True), axis=0, keepdims=True)
        dx, dgt = _rms_bwd_math(err * (1.0 / d), xv, gv)
        dx_ref[...] = dx
        dxb_ref[...] = dx.astype(BF16)

        @pl.when(pl.program_id(0) == 0)
        def _():
            dg_ref[...] = jnp.zeros_like(dg_ref)
            loss_ref[...] = jnp.zeros_like(loss_ref)

        dg_ref[...] += jnp.sum(dgt, axis=0, keepdims=True)
        loss_ref[...] += jnp.broadcast_to(part, loss_ref.shape)

    row = pl.BlockSpec((tm, d), lambda i: (i, 0))
    vec = pl.BlockSpec((1, d), lambda i: (0, 0))
    return pl.pallas_call(
        body, name="loss_bwd", grid=(t // tm,),
        in_specs=[row, vec, row],
        out_specs=[pl.BlockSpec((1, LANE), lambda i: (0, 0)), row, row, vec],
        out_shape=[jax.ShapeDtypeStruct((1, LANE), F32), jax.ShapeDtypeStruct((t, d), F32),
                   jax.ShapeDtypeStruct((t, d), BF16), jax.ShapeDtypeStruct((1, d), F32)],
        compiler_params=_params(("arbitrary",)),
    )(x, g.reshape(1, d), target)


def _sigmoid(x):
    return 1.0 / (1.0 + jnp.exp(-x))


def _ffn_up(name, h, w_in):
    t, d = h.shape
    fs = w_in.shape[2]
    f = 2 * fs
    tm, tn = _tile(t, 1024), _tile(fs, 256)
    per = fs // tn

    def body(h_ref, wg_ref, wu_ref, gu_ref, z_ref):
        hv = h_ref[...]
        gate = _dot(hv, wg_ref[...], "nn")
        up = _dot(hv, wu_ref[...], "nn")
        gu_ref[0] = gate.astype(BF16)
        gu_ref[1] = up.astype(BF16)
        z_ref[...] = (gate * _sigmoid(gate) * up).astype(BF16)

    return pl.pallas_call(
        body, name=name, grid=(t // tm, f // tn),
        in_specs=[pl.BlockSpec((tm, d), lambda i, j: (i, 0)),
                  pl.BlockSpec((None, d, tn), lambda i, j: (j // per, 0, j % per)),
                  pl.BlockSpec((None, d, tn), lambda i, j: (2 + j // per, 0, j % per))],
        out_specs=[pl.BlockSpec((2, tm, tn), lambda i, j: (0, i, j)),
                   pl.BlockSpec((tm, tn), lambda i, j: (i, j))],
        out_shape=[jax.ShapeDtypeStruct((2, t, f), BF16), jax.ShapeDtypeStruct((t, f), BF16)],
        compiler_params=_params(("parallel", "parallel")),
    )(h, w_in, w_in)


def _ffn_dact(name, dxb, w_out, gu):
    t, d = dxb.shape
    f = w_out.shape[0]
    tm, tn = _tile(t, 1024), _tile(f, 256)

    def epilogue(acc, ex, outs, ids):
        dz = 0.5 * acc
        gate = ex[0][0].astype(F32)
        up = ex[0][1].astype(F32)
        sg = _sigmoid(gate)
        outs[0][0] = (dz * up * (sg * (1.0 + gate * (1.0 - sg)))).astype(BF16)
        outs[0][1] = (dz * gate * sg).astype(BF16)

    blk = pl.BlockSpec((2, tm, tn), lambda i, j, k: (0, i, j))
    return _matmul(
        name, "nt", (t // tm, f // tn, 1),
        dxb, pl.BlockSpec((tm, d), lambda i, j, k: (i, 0)),
        w_out, pl.BlockSpec((tn, d), lambda i, j, k: (j, 0)),
        [(gu, blk)], [jax.ShapeDtypeStruct((2, t, f), BF16)], [blk], None, epilogue)[0]


def _grad_colsharded(name, h, da):
    t, d = h.shape
    w = da.shape[2]
    ws = w // 2
    tm, tn, tk = _tile(d, 1024), _tile(ws, 1408), _tile(t, 512)
    per = ws // tn
    return _matmul(
        name, "tn", (d // tm, (2 * w) // tn, t // tk),
        h, pl.BlockSpec((tk, tm), lambda i, j, k: (k, i)),
        da, pl.BlockSpec((None, tk, tn), lambda i, j, k: (j // (2 * per), k, j % (2 * per))),
        [], [jax.ShapeDtypeStruct((N_CHIPS, d, ws), BF16)],
        [pl.BlockSpec((None, tm, tn), lambda i, j, k: (j // per, i, j % per))],
        (tm, tn), _store(1.0, BF16))[0]


def _back_colsharded(name, da, w_g):
    _, t, w = da.shape
    d, ws = w_g.shape[1], w_g.shape[2]
    tm, tn, tk = _tile(t, 1024), _tile(d, 1024), _tile(ws, 1408)
    per = ws // tk
    return _matmul(
        name, "nt", (t // tm, d // tn, (2 * w) // tk),
        da, pl.BlockSpec((None, tm, tk), lambda i, j, k: (k // (2 * per), i, k % (2 * per))),
        w_g, pl.BlockSpec((None, tn, tk), lambda i, j, k: (k // per, j, k % per)),
        [], [jax.ShapeDtypeStruct((t, d), F32)], [pl.BlockSpec((tm, tn), lambda i, j, k: (i, j))],
        (tm, tn), _store(1.0, F32))[0]


def _ffn_fwd(tag, x, g, w_in, w_out):
    h = _rms_fwd(tag + "_norm", x, g)
    gu, z = _ffn_up(tag + "_up", h, w_in)
    y = _mm_residual(tag + "_down", z, w_out, x, 0.5)
    return y, (x, h, gu, z)


def _ffn_bwd(tag, saved, g, w_in, w_out, dx, dxb):
    x, h, gu, z = saved
    f = z.shape[1]
    d_w_out = _mm_tn(tag + "_dwout", z, dxb, scale=0.5, tm_pref=1408)
    da = _ffn_dact(tag + "_dact", dxb, w_out, gu)
    d_w_in = _grad_colsharded(tag + "_dwin", h, da)
    dh = _back_colsharded(tag + "_dh", da, w_in)
    dx, dxb, dg = _rms_bwd(tag + "_dnorm", dh, x, g, dx)
    return dx, dxb, dg, d_w_in, d_w_out.reshape(N_CHIPS, f // N_CHIPS, -1)


_GELU_K = math.sqrt(2.0 / math.pi)
_GELU_C = 0.044715


def _gelu(x):
    t = jnp.tanh(_GELU_K * (x + _GELU_C * x * x * x))
    return 0.5 * x * (1.0 + t), t


def _dgelu(x, t):
    return 0.5 * (1.0 + t) + 0.5 * x * (1.0 - t * t) * (_GELU_K * (1.0 + 3.0 * _GELU_C * x * x))


def _causal_block_mask():
    r = lax.broadcasted_iota(jnp.int32, (SGU_BLOCK, SGU_BLOCK), 0) // CHUNK
    c = lax.broadcasted_iota(jnp.int32, (SGU_BLOCK, SGU_BLOCK), 1) // CHUNK
    return r >= c


def _sgu_pre(name, h, w_in):
    t, d = h.shape
    ws = w_in.shape[2]
    w = 2 * ws
    tm, tn = _tile(t, 1024), _tile(ws, 512)
    per = ws // tn
    return _matmul(
        name, "nn", (t // tm, (2 * w) // tn, 1),
        h, pl.BlockSpec((tm, d), lambda i, j, k: (i, 0)),
        w_in, pl.BlockSpec((None, d, tn), lambda i, j, k: (j // per, 0, j % per)),
        [], [jax.ShapeDtypeStruct((2, t, w), BF16)],
        [pl.BlockSpec((None, tm, tn), lambda i, j, k: (j // (2 * per), i, j % (2 * per)))],
        None, _store(1.0, BF16))[0]


def _layernorm_stats(v):
    mu = jnp.mean(v, axis=-1, keepdims=True)
    vc = v - mu
    rstd = lax.rsqrt(jnp.mean(vc * vc, axis=-1, keepdims=True) + EPS)
    return vc * rstd, rstd


def _sgu_mid_fwd(pre, gain, bias, w_sp, b_sp_t):
    _, t, w = pre.shape
    gd = w // SGU_GROUPS

    def body(pre_ref, gain_ref, bias_ref, ws_ref, bt_ref, out_ref):
        mask = _causal_block_mask()
        u, _ = _gelu(pre_ref[0].astype(F32))
        v, _ = _gelu(pre_ref[1].astype(F32))
        vhat, _ = _layernorm_stats(v)
        vln = (vhat * gain_ref[...] + bias_ref[...]).astype(BF16)
        for gi in range(SGU_GROUPS):
            cols = slice(gi * gd, (gi + 1) * gd)
            wg = jnp.where(mask, ws_ref[gi], 0.0).astype(BF16)
            mixed = _dot(wg, vln[:, cols], "nn") + bt_ref[:, gi:gi + 1]
            out_ref[:, cols] = (u[:, cols] * mixed).astype(BF16)

    return pl.pallas_call(
        body, name="sgu_mid_fwd", grid=(t // SGU_BLOCK,),
        in_specs=[pl.BlockSpec((2, SGU_BLOCK, w), lambda n: (0, n, 0)),
                  pl.BlockSpec((1, w), lambda n: (0, 0)), pl.BlockSpec((1, w), lambda n: (0, 0)),
                  pl.BlockSpec((SGU_GROUPS, SGU_BLOCK, SGU_BLOCK), lambda n: (0, 0, 0)),
                  pl.BlockSpec((SGU_BLOCK, SGU_GROUPS), lambda n: (0, 0))],
        out_specs=pl.BlockSpec((SGU_BLOCK, w), lambda n: (n, 0)),
        out_shape=jax.ShapeDtypeStruct((t, w), BF16),
        compiler_params=_params(("parallel",)),
    )(pre, gain, bias, w_sp, b_sp_t)


def _sgu_mid_bwd(pre, dgated, gain, bias, w_sp, b_sp_t):
    _, t, w = pre.shape
    gd = w // SGU_GROUPS

    def body(pre_ref, dg_ref, gain_ref, bias_ref, ws_ref, bt_ref,
             dpre_ref, dgain_ref, dbias_ref, dws_ref, dbt_ref, dvln_s):
        @pl.when(pl.program_id(0) == 0)
        def _():
            dgain_ref[...] = jnp.zeros_like(dgain_ref)
            dbias_ref[...] = jnp.zeros_like(dbias_ref)
            dws_ref[...] = jnp.zeros_like(dws_ref)
            dbt_ref[...] = jnp.zeros_like(dbt_ref)

        mask = _causal_block_mask()
        pu = pre_ref[0].astype(F32)
        pv = pre_ref[1].astype(F32)
        u, tu = _gelu(pu)
        v, tv = _gelu(pv)
        vhat, rstd = _layernorm_stats(v)
        gain_v = gain_ref[...]
        vln = (vhat * gain_v + bias_ref[...]).astype(BF16)
        dgt = dg_ref[...].astype(F32)
        for gi in range(SGU_GROUPS):
            cols = slice(gi * gd, (gi + 1) * gd)
            wg = jnp.where(mask, ws_ref[gi], 0.0).astype(BF16)
            vg = vln[:, cols]
            mixed = _dot(wg, vg, "nn") + bt_ref[:, gi:gi + 1]
            dgg = dgt[:, cols]
            dmixed = dgg * u[:, cols]
            dmb = dmixed.astype(BF16)
            dpre_ref[0, :, cols] = (dgg * mixed * _dgelu(pu[:, cols], tu[:, cols])).astype(BF16)
            dbt_ref[:, gi:gi + 1] += jnp.sum(dmixed, axis=1, keepdims=True)
            dws_ref[gi] += jnp.where(mask, _dot(dmb, vg, "nt"), 0.0)
            dvln_s[:, cols] = _dot(wg, dmb, "tn")
        dvln = dvln_s[...]
        dgain_ref[...] += jnp.sum(dvln * vhat, axis=0, keepdims=True)
        dbias_ref[...] += jnp.sum(dvln, axis=0, keepdims=True)
        dvh = dvln * gain_v
        dv = rstd * (dvh - jnp.mean(dvh, axis=-1, keepdims=True)
                     - vhat * jnp.mean(dvh * vhat, axis=-1, keepdims=True))
        dpre_ref[1] = (dv * _dgelu(pv, tv)).astype(BF16)

    vec = pl.BlockSpec((1, w), lambda n: (0, 0))
    wsb = pl.BlockSpec((SGU_GROUPS, SGU_BLOCK, SGU_BLOCK), lambda n: (0, 0, 0))
    btb = pl.BlockSpec((SGU_BLOCK, SGU_GROUPS), lambda n: (0, 0))
    blk2 = pl.BlockSpec((2, SGU_BLOCK, w), lambda n: (0, n, 0))
    return pl.pallas_call(
        body, name="sgu_mid_bwd", grid=(t // SGU_BLOCK,),
        in_specs=[blk2, pl.BlockSpec((SGU_BLOCK, w), lambda n: (n, 0)), vec, vec, wsb, btb],
        out_specs=[blk2, vec, vec, wsb, btb],
        out_shape=[jax.ShapeDtypeStruct((2, t, w), BF16), jax.ShapeDtypeStruct((1, w), F32),
                   jax.ShapeDtypeStruct((1, w), F32),
                   jax.ShapeDtypeStruct((SGU_GROUPS, SGU_BLOCK, SGU_BLOCK), F32),
                   jax.ShapeDtypeStruct((SGU_BLOCK, SGU_GROUPS), F32)],
        scratch_shapes=[pltpu.VMEM((SGU_BLOCK, w), F32)],
        compiler_params=_params(("arbitrary",)),
    )(pre, dgated, gain, bias, w_sp, b_sp_t)


def _sgu_fwd(x, g, w_in, gain, bias, w_sp, b_sp, w_out):
    h = _rms_fwd("sgu_norm", x, g)
    pre = _sgu_pre("sgu_pre", h, w_in)
    gated = _sgu_mid_fwd(pre, gain, bias, w_sp, b_sp.T)
    y = _mm_residual("sgu_out", gated, w_out, x, 1.0, tk_pref=1024)
    return y, (x, h, pre, gated)


def _sgu_bwd(saved, g, w_in, gain, bias, w_sp, b_sp, w_out, dx, dxb):
    x, h, pre, gated = saved
    w = gated.shape[1]
    d_w_out = _mm_tn("sgu_dwout", gated, dxb)
    dgated = _mm_nt_full("sgu_dgated", dxb, w_out, BF16)
    dpre, dgain, dbias, dws, dbt = _sgu_mid_bwd(pre, dgated, gain, bias, w_sp, b_sp.T)
    d_w_in = _grad_colsharded("sgu_dwin", h, dpre)
    dh = _back_colsharded("sgu_dh", dpre, w_in)
    dx, dxb, dg = _rms_bwd("sgu_dnorm", dh, x, g, dx)
    small = dict(ln=dg, gain=dgain, bias=dbias, w_sp=dws, b_sp=dbt.T)
    return dx, dxb, small, d_w_in, d_w_out.reshape(N_CHIPS, w // N_CHIPS, -1)


def _rope_tables(positions):
    half = QK_ROPE // 2
    inv_freq = 1.0 / (ROPE_THETA ** (jnp.arange(half, dtype=F32) / half))
    ang = positions.astype(F32)[:, None] * inv_freq
    cos, sin = jnp.cos(ang), jnp.sin(ang)
    t = positions.shape[0]
    zeros = jnp.zeros((t, half), F32)
    rest = jnp.zeros((t, LANE - QK_ROPE), F32)
    c = jnp.concatenate([cos, cos, rest + 1.0], axis=1)
    s_up = jnp.concatenate([zeros, sin, rest], axis=1)
    s_dn = jnp.concatenate([-sin, zeros, rest], axis=1)
    return c, s_up, s_dn


def _rope_apply(x, c, s_up, s_dn):
    half = QK_ROPE // 2
    return x * c + pltpu.roll(x, half, 1) * s_up + pltpu.roll(x, LANE - half, 1) * s_dn


def _rope_apply_t(dy, c, s_up, s_dn):
    half = QK_ROPE // 2
    return dy * c - pltpu.roll(dy, LANE - half, 1) * s_dn - pltpu.roll(dy, half, 1) * s_up


def _mla_norm_fwd(proj, gq, gkv):
    t, p = proj.shape
    ql, kvl = gq.shape[1], gkv.shape[1]
    tm = _tile(t, 512)

    def body(p_ref, gq_ref, gkv_ref, qn_ref, kvn_ref):
        for lo, n, g_ref, o_ref in ((0, ql, gq_ref, qn_ref), (ql, kvl, gkv_ref, kvn_ref)):
            xv = p_ref[:, lo:lo + n]
            r = lax.rsqrt(jnp.mean(xv * xv, axis=-1, keepdims=True) + EPS)
            o_ref[...] = (xv * r * g_ref[...]).astype(BF16)

    return pl.pallas_call(
        body, name="mla_norm_fwd", grid=(t // tm,),
        in_specs=[pl.BlockSpec((tm, p), lambda i: (i, 0)), pl.BlockSpec((1, ql), lambda i: (0, 0)),
                  pl.BlockSpec((1, kvl), lambda i: (0, 0))],
        out_specs=[pl.BlockSpec((tm, ql), lambda i: (i, 0)), pl.BlockSpec((tm, kvl), lambda i: (i, 0))],
        out_shape=[jax.ShapeDtypeStruct((t, ql), BF16), jax.ShapeDtypeStruct((t, kvl), BF16)],
        compiler_params=_params(("parallel",)),
    )(proj, gq, gkv)


def _mla_norm_bwd(proj, dqn, dkvn, dkr, gq, gkv):
    t, p = proj.shape
    ql, kvl = gq.shape[1], gkv.shape[1]
    tm = _tile(t, 256)

    def body(p_ref, dqn_ref, dkvn_ref, dkr_ref, gq_ref, gkv_ref, dp_ref, dgq_ref, dgkv_ref):
        @pl.when(pl.program_id(0) == 0)
        def _():
            dgq_ref[...] = jnp.zeros_like(dgq_ref)
            dgkv_ref[...] = jnp.zeros_like(dgkv_ref)

        for lo, n, g_ref, d_ref, dg_ref in ((0, ql, gq_ref, dqn_ref, dgq_ref),
                                             (ql, kvl, gkv_ref, dkvn_ref, dgkv_ref)):
            dx, dgt = _rms_bwd_math(d_ref[...], p_ref[:, lo:lo + n], g_ref[...])
            dp_ref[:, lo:lo + n] = dx.astype(BF16)
            dg_ref[...] += jnp.sum(dgt, axis=0, keepdims=True)
        dp_ref[:, ql + kvl:] = dkr_ref[...].astype(BF16)

    def row(n):
        return pl.BlockSpec((tm, n), lambda i: (i, 0))

    def vec(n):
        return pl.BlockSpec((1, n), lambda i: (0, 0))

    return pl.pallas_call(
        body, name="mla_norm_bwd", grid=(t // tm,),
        in_specs=[row(p), row(ql), row(kvl), row(LANE), vec(ql), vec(kvl)],
        out_specs=[row(p), vec(ql), vec(kvl)],
        out_shape=[jax.ShapeDtypeStruct((t, p), BF16), jax.ShapeDtypeStruct((1, ql), F32),
                   jax.ShapeDtypeStruct((1, kvl), F32)],
        compiler_params=_params(("arbitrary",)),
    )(proj, dqn, dkvn, dkr, gq, gkv)


def _mla_q_up(qn, wq, tables):
    t, ql = qn.shape
    n = wq.shape[1]
    tm = _tile(t, 1024)
    scale = QK_DIM ** -0.5

    def epilogue(acc, ex, outs, ids):
        outs[0][:, :QK_NOPE] = (scale * acc[:, :QK_NOPE]).astype(BF16)
        hi = _rope_apply(acc[:, QK_NOPE:], ex[0][...], ex[1][...], ex[2][...])
        outs[0][:, QK_NOPE:] = (scale * hi).astype(BF16)

    tab = pl.BlockSpec((tm, LANE), lambda i, j, k: (i, 0))
    return _matmul(
        "mla_q_up", "nn", (t // tm, n // HEAD_PAD, 1),
        qn, pl.BlockSpec((tm, ql), lambda i, j, k: (i, 0)),
        wq, pl.BlockSpec((ql, HEAD_PAD), lambda i, j, k: (0, j)),
        [(tb, tab) for tb in tables],
        [jax.ShapeDtypeStruct((t, n), BF16)], [pl.BlockSpec((tm, HEAD_PAD), lambda i, j, k: (i, j))],
        None, epilogue)[0]


def _mla_kv_up(kvn, wkv, proj, tables, heads):
    t, kvl = kvn.shape
    n = wkv.shape[1]
    p = proj.shape[1]
    tm = _tile(t, 1024)

    def epilogue(acc, ex, outs, ids):
        kr = _rope_apply(ex[0][...], ex[1][...], ex[2][...], ex[3][...])
        outs[0][:, :QK_NOPE] = acc[:, :QK_NOPE].astype(BF16)
        outs[0][:, QK_NOPE:] = (acc[:, QK_NOPE:] + jnp.where(ids[1] < heads, kr, 1.0)).astype(BF16)

    tab = pl.BlockSpec((tm, LANE), lambda i, j, k: (i, 0))
    kr_spec = pl.BlockSpec((tm, LANE), lambda i, j, k: (i, p // LANE - 1))
    return _matmul(
        "mla_kv_up", "nn", (t // tm, n // HEAD_PAD, 1),
        kvn, pl.BlockSpec((tm, kvl), lambda i, j, k: (i, 0)),
        wkv, pl.BlockSpec((kvl, HEAD_PAD), lambda i, j, k: (0, j)),
        [(proj, kr_spec)] + [(tb, tab) for tb in tables],
        [jax.ShapeDtypeStruct((t, n), BF16)], [pl.BlockSpec((tm, HEAD_PAD), lambda i, j, k: (i, j))],
        None, epilogue)[0]


def _chunk_mask(tq, tk):
    r = lax.broadcasted_iota(jnp.int32, (tq, tk), 0) // CHUNK
    c = lax.broadcasted_iota(jnp.int32, (tq, tk), 1) // CHUNK
    return c <= r


def _block_pairs(nb, key_major):
    if key_major:
        pairs = [(qi, ki) for ki in range(nb) for qi in range(ki, nb)]
    else:
        pairs = [(qi, ki) for qi in range(nb) for ki in range(qi + 1)]
    return (jnp.asarray([p[0] for p in pairs], jnp.int32), jnp.asarray([p[1] for p in pairs], jnp.int32))


def _flash_fwd(qp, kv, heads):
    t = qp.shape[0]
    tb = _tile(t, 512)
    nb = t // tb
    rep = tb // LANE
    qt, kt = _block_pairs(nb, key_major=False)

    def body(qt_ref, kt_ref, q_ref, k_ref, v_ref, o_ref, lse_ref, m_s, acc_s):
        p = pl.program_id(1)
        qi, ki = qt_ref[p], kt_ref[p]

        @pl.when(ki == 0)
        def _():
            m_s[...] = jnp.full_like(m_s, -1e30)
            acc_s[...] = jnp.zeros_like(acc_s)

        def step(masked):
            s = _dot(q_ref[...], k_ref[...], "nt")
            if masked:
                s = jnp.where(_chunk_mask(tb, tb), s, -1e30)
            m_prev = m_s[...]
            m_new = jnp.maximum(m_prev, jnp.max(s, axis=1, keepdims=True))
            alpha = jnp.exp(m_prev - m_new)
            pr = jnp.exp(s - jnp.tile(m_new, (1, rep))).astype(BF16)
            pv = _dot(pr, v_ref[...], "nn")
            acc_s[:, :V_DIM] = alpha * acc_s[:, :V_DIM] + pv[:, :V_DIM]
            acc_s[:, V_DIM:] = alpha * acc_s[:, V_DIM:] + pv[:, V_DIM:]
            m_s[...] = m_new

        @pl.when(ki < qi)
        def _():
            step(False)

        @pl.when(ki == qi)
        def _():
            step(True)
            l = acc_s[:, V_DIM:]
            o_ref[...] = (acc_s[:, :V_DIM] / l).astype(BF16)
            lse_ref[...] = m_s[...] + jnp.log(l)

    return pl.pallas_call(
        body, name="mla_flash_fwd",
        grid_spec=pltpu.PrefetchScalarGridSpec(
            num_scalar_prefetch=2, grid=(heads, int(qt.shape[0])),
            in_specs=[pl.BlockSpec((tb, HEAD_PAD), lambda h, p, qt, kt: (qt[p], h)),
                      pl.BlockSpec((tb, HEAD_PAD), lambda h, p, qt, kt: (kt[p], h)),
                      pl.BlockSpec((tb, HEAD_PAD), lambda h, p, qt, kt: (kt[p], heads + h))],
            out_specs=[pl.BlockSpec((tb, V_DIM), lambda h, p, qt, kt: (qt[p], h)),
                       pl.BlockSpec((None, tb, LANE), lambda h, p, qt, kt: (h, qt[p], 0))],
            scratch_shapes=[pltpu.VMEM((tb, LANE), F32), pltpu.VMEM((tb, HEAD_PAD), F32)]),
        out_shape=[jax.ShapeDtypeStruct((t, heads * V_DIM), BF16),
                   jax.ShapeDtypeStruct((heads, t, LANE), F32)],
        compiler_params=_params(("parallel", "arbitrary")),
    )(qt, kt, qp, kv, kv)


def _flash_delta(o, do, heads):
    t = o.shape[0]
    tm = _tile(t, 256)

    def body(o_ref, do_ref, d_ref):
        for h in range(heads):
            cols = slice(h * V_DIM, (h + 1) * V_DIM)
            prod = o_ref[:, cols].astype(F32) * do_ref[:, cols].astype(F32)
            d_ref[h] = jnp.broadcast_to(jnp.sum(prod, axis=1, keepdims=True), (tm, LANE))

    row = pl.BlockSpec((tm, heads * V_DIM), lambda i: (i, 0))
    return pl.pallas_call(
        body, name="mla_flash_delta", grid=(t // tm,), in_specs=[row, row],
        out_specs=pl.BlockSpec((heads, tm, LANE), lambda i: (0, i, 0)),
        out_shape=jax.ShapeDtypeStruct((heads, t, LANE), F32),
        compiler_params=_params(("parallel",)),
    )(o, do)


def _flash_bwd(qp, kv, do, lse, delta, heads):
    t = qp.shape[0]
    tb = _tile(t, 512)
    nb = t // tb
    rep = tb // LANE
    qt, kt = _block_pairs(nb, key_major=True)

    def body(qt_ref, kt_ref, q_ref, k_ref, v_ref, do_ref, lse_ref, dl_ref, dq_ref, dk_ref, dv_ref, dk_s, dv_s):
        p = pl.program_id(1)
        qi, ki = qt_ref[p], kt_ref[p]

        @pl.when(p == 0)
        def _():
            dq_ref[...] = jnp.zeros_like(dq_ref)

        @pl.when(qi == ki)
        def _():
            dk_s[...] = jnp.zeros_like(dk_s)
            dv_s[...] = jnp.zeros_like(dv_s)

        def step(masked):
            q = q_ref[...]
            k = k_ref[...]
            dov = do_ref[...]
            s = _dot(q, k, "nt")
            pr = jnp.exp(s - jnp.tile(lse_ref[...], (1, rep)))
            if masked:
                pr = jnp.where(_chunk_mask(tb, tb), pr, 0.0)
            dv_s[...] += _dot(pr.astype(BF16), dov, "tn")
            dp = _dot(dov, v_ref[...], "nt")
            ds = (pr * (dp - jnp.tile(dl_ref[...], (1, rep)))).astype(BF16)
            rows = pl.ds(pl.multiple_of(qi * tb, tb), tb)
            dq_ref[rows, :] += _dot(ds, k, "nn")
            dk_s[...] += _dot(ds, q, "tn")

        @pl.when(qi > ki)
        def _():
            step(False)

        @pl.when(qi == ki)
        def _():
            step(True)

        @pl.when(qi == nb - 1)
        def _():
            dk_ref[...] = dk_s[...]
            dv_ref[...] = dv_s[...]

    def qrow(width):
        return pl.BlockSpec((tb, width), lambda h, p, qt, kt: (qt[p], h))

    def stat():
        return pl.BlockSpec((None, tb, LANE), lambda h, p, qt, kt: (h, qt[p], 0))

    return pl.pallas_call(
        body, name="mla_flash_bwd",
        grid_spec=pltpu.PrefetchScalarGridSpec(
            num_scalar_prefetch=2, grid=(heads, int(qt.shape[0])),
            in_specs=[qrow(HEAD_PAD),
                      pl.BlockSpec((tb, HEAD_PAD), lambda h, p, qt, kt: (kt[p], h)),
                      pl.BlockSpec((tb, V_DIM), lambda h, p, qt, kt: (kt[p], 2 * (heads + h))),
                      qrow(V_DIM), stat(), stat()],
            out_specs=[pl.BlockSpec((t, HEAD_PAD), lambda h, p, qt, kt: (0, h)),
                       pl.BlockSpec((tb, HEAD_PAD), lambda h, p, qt, kt: (kt[p], h)),
                       pl.BlockSpec((tb, V_DIM), lambda h, p, qt, kt: (kt[p], h))],
            scratch_shapes=[pltpu.VMEM((tb, HEAD_PAD), F32), pltpu.VMEM((tb, V_DIM), F32)]),
        out_shape=[jax.ShapeDtypeStruct((t, heads * HEAD_PAD), F32),
                   jax.ShapeDtypeStruct((t, heads * HEAD_PAD), F32),
                   jax.ShapeDtypeStruct((t, heads * V_DIM), F32)],
        compiler_params=_params(("parallel", "arbitrary")),
    )(qt, kt, qp, kv, kv, do, lse, delta)


def _mla_attn_post(dqp, dkp, dv, tables, heads):
    t = dqp.shape[0]
    tm = _tile(t, 256)
    scale = QK_DIM ** -0.5
    kw, vw = heads * HEAD_PAD, heads * V_DIM

    def body(dq_ref, dk_ref, dv_ref, c_ref, su_ref, sd_ref, dqb_ref, dkvb_ref, dkr_ref):
        c, su, sd = c_ref[...], su_ref[...], sd_ref[...]
        kr = jnp.zeros((tm, LANE), F32)
        for h in range(heads):
            lo = h * HEAD_PAD
            mid = lo + QK_NOPE
            dqb_ref[:, lo:mid] = (scale * dq_ref[:, lo:mid]).astype(BF16)
            dqb_ref[:, mid:mid + LANE] = (scale * _rope_apply_t(dq_ref[:, mid:mid + LANE], c, su, sd)).astype(BF16)
            kr = kr + dk_ref[:, mid:mid + LANE]
            dkvb_ref[:, kw + lo:kw + mid] = dv_ref[:, h * V_DIM:(h + 1) * V_DIM].astype(BF16)
            dkvb_ref[:, kw + mid:kw + lo + HEAD_PAD] = jnp.zeros((tm, HEAD_PAD - V_DIM), BF16)
        dkvb_ref[:, :kw] = dk_ref[...].astype(BF16)
        dkr_ref[...] = _rope_apply_t(kr, c, su, sd)

    def row(n):
        return pl.BlockSpec((tm, n), lambda i: (i, 0))

    return pl.pallas_call(
        body, name="mla_attn_post", grid=(t // tm,),
        in_specs=[row(kw), row(kw), row(vw), row(LANE), row(LANE), row(LANE)],
        out_specs=[row(kw), row(2 * kw), row(LANE)],
        out_shape=[jax.ShapeDtypeStruct((t, kw), BF16), jax.ShapeDtypeStruct((t, 2 * kw), BF16),
                   jax.ShapeDtypeStruct((t, LANE), F32)],
        compiler_params=_params(("parallel",)),
    )(dqp, dkp, dv, *tables)


def _mla_weights(w_in_g, w_q_g, w_kv_g, w_out_g):
    d = w_in_g.shape[0] * w_in_g.shape[1]
    pw = w_in_g.shape[2]
    w_in = jnp.pad(w_in_g.reshape(d, pw), ((0, 0), (0, LANE - QK_ROPE)))
    ql = w_q_g.shape[1]
    wq = jnp.transpose(w_q_g, (1, 0, 2)).reshape(ql, -1, QK_DIM)
    heads = wq.shape[1]
    wq = jnp.pad(wq, ((0, 0), (0, 0), (0, HEAD_PAD - QK_DIM))).reshape(ql, heads * HEAD_PAD)
    kvl = w_kv_g.shape[1]
    wkv = jnp.transpose(w_kv_g, (1, 0, 2)).reshape(kvl, heads, QK_NOPE + V_DIM)
    wk = jnp.pad(wkv[:, :, :QK_NOPE], ((0, 0), (0, 0), (0, HEAD_PAD - QK_NOPE))).reshape(kvl, heads * HEAD_PAD)
    wv = jnp.pad(wkv[:, :, QK_NOPE:], ((0, 0), (0, 0), (0, HEAD_PAD - V_DIM))).reshape(kvl, heads * HEAD_PAD)
    return w_in, wq, jnp.concatenate([wk, wv], axis=1), w_out_g.reshape(heads * V_DIM, -1), heads


def _mla_unpermute(d_w_in, d_wq, d_wkv, heads, pw):
    d = d_w_in.shape[0]
    g_in = d_w_in[:, :pw].reshape(N_CHIPS, d // N_CHIPS, pw)
    ql = d_wq.shape[0]
    g_q = d_wq.reshape(ql, heads, HEAD_PAD)[:, :, :QK_DIM].reshape(ql, N_CHIPS, -1)
    kvl = d_wkv.shape[0]
    g_k = d_wkv[:, :heads * HEAD_PAD].reshape(kvl, heads, HEAD_PAD)[:, :, :QK_NOPE]
    g_v = d_wkv[:, heads * HEAD_PAD:].reshape(kvl, heads, HEAD_PAD)[:, :, :V_DIM]
    g_kv = jnp.concatenate([g_k, g_v], axis=2).reshape(kvl, N_CHIPS, -1)
    return g_in, jnp.transpose(g_q, (1, 0, 2)), jnp.transpose(g_kv, (1, 0, 2))


def _mla_fwd(x, g, wts, gq, gkv, tables):
    w_in, wq, wkv, w_out, heads = wts
    h = _rms_fwd("mla_norm", x, g)
    proj = _mm_nn_full("mla_proj", h, w_in, F32, tn_pref=w_in.shape[1])
    qn, kvn = _mla_norm_fwd(proj, gq, gkv)
    qp = _mla_q_up(qn, wq, tables)
    kv = _mla_kv_up(kvn, wkv, proj, tables, heads)
    o, lse = _flash_fwd(qp, kv, heads)
    y = _mm_residual("mla_out", o, w_out, x, 1.0, tk_pref=2048)
    return y, (x, h, proj, qn, kvn, qp, kv, o, lse)


def _mla_bwd(saved, g, wts, gq, gkv, tables, dx, dxb, pw):
    w_in, wq, wkv, w_out, heads = wts
    x, h, proj, qn, kvn, qp, kv, o, lse = saved
    d_w_out = _mm_tn("mla_dwout", o, dxb)
    do = _mm_nt_full("mla_do", dxb, w_out, BF16)
    dqp, dkp, dv = _flash_bwd(qp, kv, do, lse, _flash_delta(o, do, heads), heads)
    dqb, dkvb, dkr = _mla_attn_post(dqp, dkp, dv, tables, heads)
    d_wq = _mm_tn("mla_dwq", qn, dqb)
    dqn = _mm_nt_k("mla_dqn", dqb, wq, F32)
    d_wkv = _mm_tn("mla_dwkv", kvn, dkvb)
    dkvn = _mm_nt_k("mla_dkvn", dkvb, wkv, F32)
    dproj, dgq, dgkv = _mla_norm_bwd(proj, dqn, dkvn, dkr, gq, gkv)
    d_w_in = _mm_tn("mla_dwin", h, dproj, tn_pref=dproj.shape[1])
    dh = _mm_nt_full("mla_dh", dproj, w_in, F32, tn_pref=1024)
    dx, dxb, dg = _rms_bwd("mla_dnorm", dh, x, g, dx)
    g_in, g_q, g_kv = _mla_unpermute(d_w_in, d_wq, d_wkv, heads, pw)
    small = dict(ln=dg, gq=dgq, gkv=dgkv)
    return dx, dxb, small, g_in, g_q, g_kv, d_w_out.reshape(N_CHIPS, d_w_out.shape[0] // N_CHIPS, -1)


def _place():
    x, y, c = lax.axis_index("x"), lax.axis_index("y"), lax.axis_index("c")
    chips = [(1 - x, y), (x, 1 - y), (1 - x, 1 - y)]
    return x, y, c, chips


ANY = pl.BlockSpec(memory_space=pl.ANY)


def _gather_weights(bufs, norms):
    nt = len(bufs)

    def body(*refs):
        n_in = refs[nt]
        outs, n_out = refs[nt + 1:2 * nt + 1], refs[2 * nt + 1]
        send, recv, fsend, frecv, loc, nsend, nrecv = refs[2 * nt + 2:]
        x, y, c, chips = _place()
        me = 2 * x + y
        sib = (x, y, 1 - c)

        local = pltpu.make_async_copy(n_in, n_out.at[me], loc)
        local.start()

        def place(t, chip, half):
            return outs[t].at[2 * chip[0] + chip[1], half]

        def ici(t, j, chip):
            return pltpu.make_async_remote_copy(
                src_ref=place(t, (x, y), c), dst_ref=place(t, (x, y), c), send_sem=send.at[t, j],
                recv_sem=recv.at[t, j], device_id=(*chip, c), device_id_type=MESH)

        def fwd(t, j, chip, half):
            return pltpu.make_async_remote_copy(
                src_ref=place(t, chip, half), dst_ref=place(t, chip, half), send_sem=fsend.at[t, j],
                recv_sem=frecv.at[t, j], device_id=sib, device_id_type=MESH)

        def nrm(j, chip, owner):
            return pltpu.make_async_remote_copy(
                src_ref=n_in, dst_ref=n_out.at[2 * owner[0] + owner[1]], send_sem=nsend.at[j], recv_sem=nrecv.at[j],
                device_id=(*chip, c), device_id_type=MESH)

        firsts = [ici(t, j, chip) for t in range(nt) for j, chip in enumerate(chips)]
        firsts += [nrm(j, chip, (x, y)) for j, chip in enumerate(chips)]
        for cp in firsts:
            cp.start()
        passed = []
        for t in range(nt):
            for j, chip in enumerate(chips):
                pltpu.make_async_remote_copy(
                    src_ref=place(t, chip, c), dst_ref=place(t, chip, c), send_sem=send.at[t, j],
                    recv_sem=recv.at[t, j], device_id=(*chip, c), device_id_type=MESH).wait_recv()
                cp = fwd(t, j, chip, c)
                cp.start()
                passed.append(cp)
        for t in range(nt):
            for j, chip in enumerate(chips):
                fwd(t, j, chip, 1 - c).wait_recv()
        for j, chip in enumerate(chips):
            nrm(j, chip, chip).wait_recv()
        for cp in firsts + passed:
            cp.wait_send()
        local.wait()

    out_shape = [jax.ShapeDtypeStruct(b.shape, b.dtype) for b in bufs]
    out_shape.append(jax.ShapeDtypeStruct((N_CHIPS,) + norms.shape, norms.dtype))
    res = pl.pallas_call(
        body, name="gather_weights",
        in_specs=[ANY] * (nt + 1), out_specs=[ANY] * (nt + 1), out_shape=out_shape,
        input_output_aliases={t: t for t in range(nt)},
        scratch_shapes=[pltpu.SemaphoreType.DMA((nt, 3)), pltpu.SemaphoreType.DMA((nt, 3)),
                        pltpu.SemaphoreType.DMA((nt, 3)), pltpu.SemaphoreType.DMA((nt, 3)),
                        pltpu.SemaphoreType.DMA, pltpu.SemaphoreType.DMA((3,)),
                        pltpu.SemaphoreType.DMA((3,))],
    )(*bufs, norms)
    return res[:nt], res[nt]


def _swap_halves(grads):
    nt = len(grads)

    def body(*refs):
        ins, outs = refs[:nt], refs[nt:2 * nt]
        send, recv = refs[2 * nt:]
        x, y, c, _ = _place()
        cps = [pltpu.make_async_remote_copy(
            src_ref=ins[t].at[:, 1 - c], dst_ref=outs[t], send_sem=send.at[t], recv_sem=recv.at[t],
            device_id=(x, y, 1 - c), device_id_type=MESH) for t in range(nt)]
        for cp in cps:
            cp.start()
        for cp in cps:
            cp.wait()

    return pl.pallas_call(
        body, name="grad_swap_halves",
        in_specs=[ANY] * nt, out_specs=[ANY] * nt,
        out_shape=[jax.ShapeDtypeStruct((g.shape[0],) + g.shape[2:], g.dtype) for g in grads],
        scratch_shapes=[pltpu.SemaphoreType.DMA((nt,)), pltpu.SemaphoreType.DMA((nt,))],
    )(*grads)


def _scatter_chips(parts):
    nt = len(parts)

    def body(*refs):
        ins, outs = refs[:nt], refs[nt:2 * nt]
        send, recv = refs[2 * nt:]
        x, y, c, chips = _place()
        cps = [pltpu.make_async_remote_copy(
            src_ref=ins[t].at[2 * chip[0] + chip[1]], dst_ref=outs[t].at[j], send_sem=send.at[t, j],
            recv_sem=recv.at[t, j], device_id=(*chip, c), device_id_type=MESH)
            for t in range(nt) for j, chip in enumerate(chips)]
        for cp in cps:
            cp.start()
        for cp in cps:
            cp.wait()

    return pl.pallas_call(
        body, name="grad_scatter_chips",
        in_specs=[ANY] * nt, out_specs=[ANY] * nt,
        out_shape=[jax.ShapeDtypeStruct((3,) + p.shape[1:], p.dtype) for p in parts],
        scratch_shapes=[pltpu.SemaphoreType.DMA((nt, 3)), pltpu.SemaphoreType.DMA((nt, 3))],
    )(*parts)


def _join_halves(fulls):
    nt = len(fulls)

    def body(*refs):
        outs = refs[nt:2 * nt]
        send, recv = refs[2 * nt:]
        x, y, c, _ = _place()
        cps = [pltpu.make_async_remote_copy(
            src_ref=outs[t].at[c], dst_ref=outs[t].at[c], send_sem=send.at[t], recv_sem=recv.at[t],
            device_id=(x, y, 1 - c), device_id_type=MESH) for t in range(nt)]
        for cp in cps:
            cp.start()
        for t in range(nt):
            pltpu.make_async_remote_copy(
                src_ref=outs[t].at[1 - c], dst_ref=outs[t].at[1 - c], send_sem=send.at[t], recv_sem=recv.at[t],
                device_id=(x, y, 1 - c), device_id_type=MESH).wait_recv()
        for cp in cps:
            cp.wait_send()

    return pl.pallas_call(
        body, name="grad_join_halves",
        in_specs=[ANY] * nt, out_specs=[ANY] * nt,
        out_shape=[jax.ShapeDtypeStruct(f.shape, f.dtype) for f in fulls],
        input_output_aliases={t: t for t in range(nt)},
        scratch_shapes=[pltpu.SemaphoreType.DMA((nt,)), pltpu.SemaphoreType.DMA((nt,))],
    )(*fulls)


def _gather_all(block):
    m_per, n = block.shape

    def body(x_ref, out_ref, send_sems, recv_sems, local_sem):
        x, y, c, chips = _place()
        me, sibling = (x, y, c), (x, y, 1 - c)

        def rows(px, py, pc):
            return out_ref.at[pl.ds((4 * px + 2 * py + pc) * m_per, m_per), :]

        def copy(k, block_of, to, src=None):
            return pltpu.make_async_remote_copy(
                src_ref=rows(*block_of) if src is None else src, dst_ref=rows(*block_of),
                send_sem=send_sems.at[k], recv_sem=recv_sems.at[k], device_id=to, device_id_type=MESH)

        mine = pltpu.make_async_copy(x_ref, rows(*me), local_sem)
        mine.start()
        first = [copy(0, me, sibling, src=x_ref)]
        first += [copy(1 + j, me, (*chip, c), src=x_ref) for j, chip in enumerate(chips)]
        for cp in first:
            cp.start()
        passed = [copy(4 + j, (*chip, c), sibling) for j, chip in enumerate(chips)]
        for j, chip in enumerate(chips):
            copy(1 + j, (*chip, c), me).wait_recv()
            passed[j].start()
        copy(0, sibling, me).wait_recv()
        for j, chip in enumerate(chips):
            copy(4 + j, (*chip, 1 - c), me).wait_recv()
        for cp in first + passed:
            cp.wait_send()
        mine.wait()

    return pl.pallas_call(
        body, name="gather_small_grads",
        out_shape=jax.ShapeDtypeStruct((N_DEV * m_per, n), block.dtype),
        in_specs=[pl.BlockSpec(memory_space=pltpu.VMEM)],
        out_specs=pl.BlockSpec(memory_space=pltpu.VMEM),
        scratch_shapes=[pltpu.SemaphoreType.DMA((7,)), pltpu.SemaphoreType.DMA((7,)), pltpu.SemaphoreType.DMA],
    )(block)


def _row_tile(r, c, elems=512 * 1024):
    t = max(8, min(r, (elems // c) // 8 * 8))
    while t > 8 and r % t:
        t -= 8
    return t if r % t == 0 else r


def _add_halves(idx, grad, other):
    n, _, r, w = grad.shape
    tr = _row_tile(r, w)

    def body(idx_ref, g_ref, o_ref, out_ref):
        out_ref[...] = (g_ref[...].astype(F32) + o_ref[...].astype(F32)).astype(BF16)

    return pl.pallas_call(
        body, name="grad_add_halves",
        grid_spec=pltpu.PrefetchScalarGridSpec(
            num_scalar_prefetch=1, grid=(n, r // tr),
            in_specs=[pl.BlockSpec((None, None, tr, w), lambda k, i, idx: (k, idx[0], i, 0)),
                      pl.BlockSpec((None, tr, w), lambda k, i, idx: (k, i, 0))],
            out_specs=pl.BlockSpec((None, tr, w), lambda k, i, idx: (k, i, 0))),
        out_shape=jax.ShapeDtypeStruct((n, r, w), BF16),
        compiler_params=_params(("parallel", "parallel")),
    )(idx, grad, other)


def _sum_chips(idx, part, landed):
    _, r, w = part.shape
    tr = _row_tile(r, w)

    def body(idx_ref, p_ref, l_ref, out_ref):
        acc = p_ref[...].astype(F32)
        for j in range(3):
            acc = acc + l_ref[j].astype(F32)
        out_ref[...] = acc

    return pl.pallas_call(
        body, name="grad_sum_chips",
        grid_spec=pltpu.PrefetchScalarGridSpec(
            num_scalar_prefetch=1, grid=(r // tr,),
            in_specs=[pl.BlockSpec((None, tr, w), lambda i, idx: (idx[0], i, 0)),
                      pl.BlockSpec((3, tr, w), lambda i, idx: (0, i, 0))],
            out_specs=pl.BlockSpec((None, tr, w), lambda i, idx: (idx[1], i, 0))),
        out_shape=jax.ShapeDtypeStruct((2, r, w), F32),
        compiler_params=_params(("parallel",)),
    )(idx, part, landed)


def _adamw_math(w, g, m, v):
    m = ADAM_B1 * m + (1.0 - ADAM_B1) * g
    v = ADAM_B2 * v + (1.0 - ADAM_B2) * (g * g)
    m_hat = m / (1.0 - ADAM_B1 ** ADAM_STEP)
    v_hat = v / (1.0 - ADAM_B2 ** ADAM_STEP)
    delta = -ADAM_LR * (m_hat / (jnp.sqrt(v_hat) + ADAM_EPS) + ADAM_WD * w)
    return delta, m, v


def _adamw(name, w, g, m, v):
    r, c = w.shape
    tr = _row_tile(r, c, 256 * 1024)

    def body(w_ref, g_ref, m_ref, v_ref, d_ref, nm_ref, nv_ref):
        d_ref[...], nm_ref[...], nv_ref[...] = _adamw_math(w_ref[...], g_ref[...], m_ref[...], v_ref[...])

    blk = pl.BlockSpec((tr, c), lambda i: (i, 0))
    return pl.pallas_call(
        body, name=name, grid=(r // tr,), in_specs=[blk] * 4, out_specs=[blk] * 3,
        out_shape=[jax.ShapeDtypeStruct((r, c), F32)] * 3,
        compiler_params=_params(("parallel",)),
    )(w, g, m, v)


def _adamw_summed(w, parts, m, v):
    r, c = w.shape

    def body(w_ref, p_ref, m_ref, v_ref, g_ref, d_ref, nm_ref, nv_ref):
        g = p_ref[0:r, :]
        for k in range(1, N_DEV):
            g = g + p_ref[k * r:(k + 1) * r, :]
        g_ref[...] = g
        d_ref[...], nm_ref[...], nv_ref[...] = _adamw_math(w_ref[...], g, m_ref[...], v_ref[...])

    return pl.pallas_call(
        body, name="adamw_replicated",
        out_shape=[jax.ShapeDtypeStruct((r, c), F32)] * 4,
        compiler_params=pltpu.CompilerParams(vmem_limit_bytes=VMEM_LIMIT_BYTES),
    )(w, parts, m, v)


def _pack(arrays):
    return jnp.concatenate([a.reshape(-1, LANE) for a in arrays], axis=0)


def _unpack(packed, shapes):
    out, row = [], 0
    for s in shapes:
        n = math.prod(s) // LANE
        out.append(packed[row:row + n].reshape(s))
        row += n
    return out


def _halves(w):
    if w.shape[0] == 2:
        return w
    return w.reshape(2, w.shape[1] // 2, w.shape[2])


def _cast_into(name, idx, w):
    _, r, c = w.shape
    tr = _row_tile(r, c)

    def body(idx_ref, w_ref, o_ref):
        o_ref[...] = w_ref[...].astype(BF16)

    return pl.pallas_call(
        body, name=name,
        grid_spec=pltpu.PrefetchScalarGridSpec(
            num_scalar_prefetch=1, grid=(2, r // tr),
            in_specs=[pl.BlockSpec((None, tr, c), lambda h, i, idx: (h, i, 0))],
            out_specs=pl.BlockSpec((None, None, tr, c), lambda h, i, idx: (idx[0], h, i, 0))),
        out_shape=jax.ShapeDtypeStruct((N_CHIPS, 2, r, c), BF16),
        compiler_params=_params(("parallel", "parallel")),
    )(idx, w)


BIG = ["ffn1_w_in", "ffn1_w_out", "ffn2_w_in", "ffn2_w_out", "sgu_w_in", "sgu_w_out",
       "mla_w_in", "mla_w_q_up", "mla_w_kv_up", "mla_w_out"]
REPLICATED = ["ln_ffn1", "ln_mix", "ln_ffn2", "sgu_v_gain", "sgu_v_bias", "sgu_w_spatial", "sgu_b_spatial",
              "ln_final"]
NORM_SHARDS = ["mla_q_norm", "mla_kv_norm"]
WEIGHTS = ["ln_ffn1", "ffn1_w_in", "ffn1_w_out", "ln_mix", "ln_ffn2", "ffn2_w_in", "ffn2_w_out", "sgu_w_in",
           "sgu_v_gain", "sgu_v_bias", "sgu_w_spatial", "sgu_b_spatial", "sgu_w_out", "mla_w_in", "mla_q_norm",
           "mla_w_q_up", "mla_kv_norm", "mla_w_kv_up", "mla_w_out", "ln_final"]


def kernel(x, positions, ln_ffn1, ffn1_w_in, ffn1_w_out, ln_mix, ln_ffn2, ffn2_w_in, ffn2_w_out, sgu_w_in, sgu_v_gain, sgu_v_bias, sgu_w_spatial, sgu_b_spatial, sgu_w_out, mla_w_in, mla_q_norm, mla_w_q_up, mla_kv_norm, mla_w_kv_up, mla_w_out, ln_final, loss_target, m_ln_ffn1, m_ffn1_w_in, m_ffn1_w_out, m_ln_mix, m_ln_ffn2, m_ffn2_w_in, m_ffn2_w_out, m_sgu_w_in, m_sgu_v_gain, m_sgu_v_bias, m_sgu_w_spatial, m_sgu_b_spatial, m_sgu_w_out, m_mla_w_in, m_mla_q_norm, m_mla_w_q_up, m_mla_kv_norm, m_mla_w_kv_up, m_mla_w_out, m_ln_final, v_ln_ffn1, v_ffn1_w_in, v_ffn1_w_out, v_ln_mix, v_ln_ffn2, v_ffn2_w_in, v_ffn2_w_out, v_sgu_w_in, v_sgu_v_gain, v_sgu_v_bias, v_sgu_w_spatial, v_sgu_b_spatial, v_sgu_w_out, v_mla_w_in, v_mla_q_norm, v_mla_w_q_up, v_mla_kv_norm, v_mla_w_kv_up, v_mla_w_out, v_ln_final):
    given = dict(locals())
    w = {n: given[n] for n in WEIGHTS}
    mom = {n: given["m_" + n] for n in WEIGHTS}
    var = {n: given["v_" + n] for n in WEIGHTS}
    t, d = x.shape[1], x.shape[2]
    xs = x.reshape(t, d)
    target = loss_target.reshape(t, d)
    me = 2 * lax.axis_index("x") + lax.axis_index("y")

    me_idx = jnp.reshape(me, (1,)).astype(jnp.int32)
    core_idx = jnp.reshape(lax.axis_index("c"), (1,)).astype(jnp.int32)

    shards = [_cast_into("cast_" + n, me_idx, _halves(w[n])) for n in BIG]
    nq = mla_q_norm.shape[1]
    norms = jnp.pad(jnp.concatenate([mla_q_norm, mla_kv_norm], axis=0), ((0, 6), (0, LANE - nq)))
    gathered, norms_g = _gather_weights(shards, norms)
    wg = {}
    for n, a in zip(BIG, gathered):
        if w[n].shape[0] == 2:
            wg[n] = [a[:, 0], a[:, 1]]
        else:
            wg[n] = a.reshape(N_CHIPS, a.shape[1] * a.shape[2], a.shape[3])
    gq = norms_g[:, 0, :nq].reshape(1, N_CHIPS * nq)
    gkv = norms_g[:, 1, :nq].reshape(1, N_CHIPS * nq)

    def row_sharded(a):
        return a.reshape(-1, a.shape[-1])

    mla_wts = _mla_weights(wg["mla_w_in"], wg["mla_w_q_up"], wg["mla_w_kv_up"], wg["mla_w_out"])
    tables = _rope_tables(positions.reshape(t))
    sgu_small = (sgu_v_gain, sgu_v_bias, sgu_w_spatial[0], sgu_b_spatial[0])

    a0, s_f1_0 = _ffn_fwd("l0_ffn1", xs, ln_ffn1[0], wg["ffn1_w_in"][0], row_sharded(wg["ffn1_w_out"][0]))
    a1, s_sgu = _sgu_fwd(a0, ln_mix[0], wg["sgu_w_in"], *sgu_small, row_sharded(wg["sgu_w_out"]))
    a2, s_f2_0 = _ffn_fwd("l0_ffn2", a1, ln_ffn2[0], wg["ffn2_w_in"][0], row_sharded(wg["ffn2_w_out"][0]))
    a3, s_f1_1 = _ffn_fwd("l1_ffn1", a2, ln_ffn1[1], wg["ffn1_w_in"][1], row_sharded(wg["ffn1_w_out"][1]))
    a4, s_mla = _mla_fwd(a3, ln_mix[1], mla_wts, gq, gkv, tables)
    a5, s_f2_1 = _ffn_fwd("l1_ffn2", a4, ln_ffn2[1], wg["ffn2_w_in"][1], row_sharded(wg["ffn2_w_out"][1]))

    loss_part, dx, dxb, dg_final = _loss_bwd(a5, ln_final, target)
    dx, dxb, dg_f2_1, dwin_f2_1, dwout_f2_1 = _ffn_bwd(
        "l1_ffn2", s_f2_1, ln_ffn2[1], wg["ffn2_w_in"][1], row_sharded(wg["ffn2_w_out"][1]), dx, dxb)
    dx, dxb, sm_mla, g_mla_in, g_mla_q, g_mla_kv, g_mla_out = _mla_bwd(
        s_mla, ln_mix[1], mla_wts, gq, gkv, tables, dx, dxb, mla_w_in.shape[2])
    dx, dxb, dg_f1_1, dwin_f1_1, dwout_f1_1 = _ffn_bwd(
        "l1_ffn1", s_f1_1, ln_ffn1[1], wg["ffn1_w_in"][1], row_sharded(wg["ffn1_w_out"][1]), dx, dxb)
    dx, dxb, dg_f2_0, dwin_f2_0, dwout_f2_0 = _ffn_bwd(
        "l0_ffn2", s_f2_0, ln_ffn2[0], wg["ffn2_w_in"][0], row_sharded(wg["ffn2_w_out"][0]), dx, dxb)
    dx, dxb, sm_sgu, g_sgu_in, g_sgu_out = _sgu_bwd(
        s_sgu, ln_mix[0], wg["sgu_w_in"], *sgu_small, row_sharded(wg["sgu_w_out"]), dx, dxb)
    dx, dxb, dg_f1_0, dwin_f1_0, dwout_f1_0 = _ffn_bwd(
        "l0_ffn1", s_f1_0, ln_ffn1[0], wg["ffn1_w_in"][0], row_sharded(wg["ffn1_w_out"][0]), dx, dxb)

    def two_layers(g0, g1):
        return jnp.stack([g0, g1], axis=1)

    def two_rows(g):
        return g.reshape(N_CHIPS, 2, g.shape[1] // 2, g.shape[2])

    local = [two_layers(dwin_f1_0, dwin_f1_1), two_layers(dwout_f1_0, dwout_f1_1),
             two_layers(dwin_f2_0, dwin_f2_1), two_layers(dwout_f2_0, dwout_f2_1),
             two_rows(g_sgu_in), two_rows(g_sgu_out), two_rows(g_mla_in), two_rows(g_mla_q),
             two_rows(g_mla_kv), two_rows(g_mla_out)]
    swapped = _swap_halves(local)
    parts = [_add_halves(core_idx, g, o) for g, o in zip(local, swapped)]
    landed = _scatter_chips(parts)
    both_idx = jnp.concatenate([me_idx, core_idx])
    joined = _join_halves([_sum_chips(both_idx, p, l) for p, l in zip(parts, landed)])
    big_grad = {n: j.reshape(w[n].shape) for n, j in zip(BIG, joined)}

    small_parts = [
        jnp.concatenate([dg_f1_0, dg_f1_1], axis=0), jnp.concatenate([sm_sgu["ln"], sm_mla["ln"]], axis=0),
        jnp.concatenate([dg_f2_0, dg_f2_1], axis=0), sm_sgu["gain"], sm_sgu["bias"], sm_sgu["w_sp"],
        sm_sgu["b_sp"], dg_final]
    rep_shapes = [w[n].shape for n in REPLICATED]
    gq_row = jnp.pad(sm_mla["gq"], ((0, 0), (0, N_CHIPS * (LANE - nq))))
    gkv_row = jnp.pad(sm_mla["gkv"], ((0, 0), (0, N_CHIPS * (LANE - nq))))
    packed = _pack(small_parts + [gq_row, gkv_row, loss_part])
    packed = jnp.pad(packed, ((0, -packed.shape[0] % 8), (0, 0)))
    everyone = _gather_all(packed)
    rows = packed.shape[0]
    zero_rows = jnp.zeros((rows - sum(math.prod(s) // LANE for s in rep_shapes), LANE), F32)
    pw = jnp.concatenate([_pack([w[n] for n in REPLICATED]), zero_rows], axis=0)
    pm = jnp.concatenate([_pack([mom[n] for n in REPLICATED]), zero_rows], axis=0)
    pv = jnp.concatenate([_pack([var[n] for n in REPLICATED]), zero_rows + 1.0], axis=0)
    g_all, d_all, m_all, v_all = _adamw_summed(pw, everyone, pm, pv)
    tail_shapes = [(1, N_CHIPS * LANE), (1, N_CHIPS * LANE), (1, LANE)]
    rep_grad = dict(zip(REPLICATED, _unpack(g_all, rep_shapes + tail_shapes)[:len(REPLICATED)]))
    rep_delta = dict(zip(REPLICATED, _unpack(d_all, rep_shapes)))
    rep_m = dict(zip(REPLICATED, _unpack(m_all, rep_shapes)))
    rep_v = dict(zip(REPLICATED, _unpack(v_all, rep_shapes)))
    tail = _unpack(g_all, rep_shapes + tail_shapes)[len(REPLICATED):]
    loss = tail[2][0, 0]
    norm_grad = {
        "mla_q_norm": lax.dynamic_slice(tail[0], (0, me * nq), (1, nq)),
        "mla_kv_norm": lax.dynamic_slice(tail[1], (0, me * nq), (1, nq)),
    }

    grad, delta, new_m, new_v = {}, {}, {}, {}
    for n in BIG:
        shp = w[n].shape
        flat = lambda a: a.reshape(-1, shp[-1])
        dl, nm, nv = _adamw("adamw_" + n, flat(w[n]), flat(big_grad[n]), flat(mom[n]), flat(var[n]))
        grad[n], delta[n], new_m[n], new_v[n] = big_grad[n], dl.reshape(shp), nm.reshape(shp), nv.reshape(shp)
    for n in REPLICATED:
        grad[n], delta[n], new_m[n], new_v[n] = rep_grad[n], rep_delta[n], rep_m[n], rep_v[n]
    stack = lambda dct: jnp.concatenate([dct[n] for n in NORM_SHARDS], axis=0)
    dl, nm, nv = _adamw("adamw_norm_shards", stack(w), stack(norm_grad), stack(mom), stack(var))
    for i, n in enumerate(NORM_SHARDS):
        grad[n], delta[n], new_m[n], new_v[n] = norm_grad[n], dl[i:i + 1], nm[i:i + 1], nv[i:i + 1]

    grad_x = dx.reshape(x.shape)
    return (loss, grad_x, *[grad[n] for n in WEIGHTS], *[delta[n] for n in WEIGHTS],
            *[new_m[n] for n in WEIGHTS], *[new_v[n] for n in WEIGHTS])
```

```python
import functools
import math

import jax
import jax.numpy as jnp
from jax import lax
from jax.experimental import pallas as pl
from jax.experimental.pallas import tpu as pltpu

F32 = jnp.float32
BF16 = jnp.bfloat16
MESH = pl.DeviceIdType.MESH

EPS = 1e-6
CHUNK = 64
SGU_BLOCK = 128
SGU_GROUPS = 8
QK_NOPE = 128
QK_ROPE = 64
V_DIM = 128
QK_DIM = QK_NOPE + QK_ROPE
HEAD_PAD = 256
ROPE_THETA = 10000.0
N_CHIPS = 4
N_DEV = 8

ADAM_LR = 0.001
ADAM_B1 = 0.9
ADAM_B2 = 0.999
ADAM_EPS = 1e-08
ADAM_WD = 0.01
ADAM_STEP = 10

LANE = 128
VMEM_LIMIT_BYTES = 56 * 1024 * 1024

_DIMS = {
    "nn": (((1,), (0,)), ((), ())),
    "nt": (((1,), (1,)), ((), ())),
    "tn": (((0,), (0,)), ((), ())),
}


def _tile(n, pref):
    t = (min(pref, n) // LANE) * LANE
    while t >= LANE:
        if n % t == 0:
            return t
        t -= LANE
    return n


def _params(sem):
    return pltpu.CompilerParams(dimension_semantics=sem, vmem_limit_bytes=VMEM_LIMIT_BYTES)


def _dot(a, b, mode):
    return lax.dot_general(a, b, _DIMS[mode], preferred_element_type=F32)


def _place():
    x, y, c = lax.axis_index("x"), lax.axis_index("y"), lax.axis_index("c")
    chips = [(1 - x, y), (x, 1 - y), (1 - x, 1 - y)]
    return x, y, c, chips


ANY = pl.BlockSpec(memory_space=pl.ANY)


def _gather_copies(phase, bufs, send, recv, landing):
    x, y, c, chips = _place()
    copies = []
    for t, buf in enumerate(bufs):
        for j, chip in enumerate(chips):
            there = 2 * chip[0] + chip[1]
            if phase == 1:
                src, lands, to = buf.at[2 * x + y, c], buf.at[there, c], (*chip, c)
            else:
                src, lands, to = buf.at[there, c], buf.at[there, 1 - c], (x, y, 1 - c)
            ref = lands if landing else src
            copies.append(pltpu.make_async_remote_copy(
                src_ref=ref, dst_ref=ref, send_sem=send.at[t, j], recv_sem=recv.at[t, j], device_id=to,
                device_id_type=MESH))
    return copies


def _comm_io(comm):
    if comm is None:
        return [], [], [], [], []
    bufs = comm[1]
    shapes = [jax.ShapeDtypeStruct(b.shape, b.dtype) for b in bufs]
    sems = [pltpu.SemaphoreType.DMA((len(bufs), 3)), pltpu.SemaphoreType.DMA((len(bufs), 3))]
    return list(bufs), [ANY] * len(bufs), [ANY] * len(bufs), shapes, sems


def _comm_run(comm, buf_refs, send, recv, first, last):
    @pl.when(first)
    def _():
        for cp in _gather_copies(comm[0], buf_refs, send, recv, landing=False):
            cp.start()

    def finish():
        for cp in _gather_copies(comm[0], buf_refs, send, recv, landing=True):
            cp.wait_recv()
        for cp in _gather_copies(comm[0], buf_refs, send, recv, landing=False):
            cp.wait_send()

    return last, finish


def _matmul(name, mode, grid, a, a_spec, b, b_spec, extras, out_shapes, out_specs, acc_shape, epilogue, comm=None):
    nk = grid[2]
    n_ex = len(extras)
    n_out = len(out_shapes)
    c_in, c_in_specs, c_out_specs, c_shapes, c_sems = _comm_io(comm)
    n_c = len(c_in)

    def body(*refs):
        a_ref, b_ref = refs[0], refs[1]
        ex = refs[2:2 + n_ex]
        outs = refs[2 + n_ex + n_c:2 + n_ex + n_c + n_out]
        ids = (pl.program_id(0), pl.program_id(1))
        k = pl.program_id(2)
        if comm is not None:
            bufs = refs[2 + n_ex + n_c + n_out:2 + n_ex + 2 * n_c + n_out]
            send, recv = refs[2 + n_ex + 2 * n_c + n_out:2 + n_ex + 2 * n_c + n_out + 2]
            first = jnp.logical_and(jnp.logical_and(ids[0] == 0, ids[1] == 0), k == 0)
            last = jnp.logical_and(jnp.logical_and(ids[0] == grid[0] - 1, ids[1] == grid[1] - 1), k == nk - 1)
            last, finish = _comm_run(comm, bufs, send, recv, first, last)
        part = _dot(a_ref[...], b_ref[...], mode)
        if nk == 1:
            epilogue(part, ex, outs, ids)
        else:
            acc = refs[-1]

            @pl.when(k == 0)
            def _():
                acc[...] = part

            @pl.when(k > 0)
            def _():
                acc[...] += part

            @pl.when(k == nk - 1)
            def _():
                epilogue(acc[...], ex, outs, ids)

        if comm is not None:
            pl.when(last)(finish)

    scratch = c_sems + ([pltpu.VMEM(acc_shape, F32)] if nk > 1 else [])
    sem = ("parallel", "parallel", "arbitrary") if comm is None else ("arbitrary",) * 3
    return pl.pallas_call(
        body,
        name=name,
        grid=grid,
        in_specs=[a_spec, b_spec] + [s for _, s in extras] + c_in_specs,
        out_specs=list(out_specs) + c_out_specs,
        out_shape=list(out_shapes) + c_shapes,
        input_output_aliases={2 + n_ex + t: n_out + t for t in range(n_c)},
        scratch_shapes=scratch,
        compiler_params=_params(sem),
    )(a, b, *[e for e, _ in extras], *c_in)


def _store(scale, dtype):
    def epilogue(acc, ex, outs, ids):
        v = acc if scale == 1.0 else acc * scale
        outs[0][...] = v.astype(dtype)

    return epilogue


def _mm_nn_full(name, a, b, out_dtype, tm_pref=1024, tn_pref=512):
    m, kd = a.shape
    n = b.shape[1]
    tm, tn = _tile(m, tm_pref), _tile(n, tn_pref)
    return _matmul(
        name, "nn", (m // tm, n // tn, 1),
        a, pl.BlockSpec((tm, kd), lambda i, j, k: (i, 0)),
        b, pl.BlockSpec((kd, tn), lambda i, j, k: (0, j)),
        [], [jax.ShapeDtypeStruct((m, n), out_dtype)], [pl.BlockSpec((tm, tn), lambda i, j, k: (i, j))],
        None, _store(1.0, out_dtype))[0]


def _mm_nt_full(name, a, b, out_dtype, scale=1.0, tm_pref=1024, tn_pref=512):
    m, kd = a.shape
    n = b.shape[0]
    tm, tn = _tile(m, tm_pref), _tile(n, tn_pref)
    return _matmul(
        name, "nt", (m // tm, n // tn, 1),
        a, pl.BlockSpec((tm, kd), lambda i, j, k: (i, 0)),
        b, pl.BlockSpec((tn, kd), lambda i, j, k: (j, 0)),
        [], [jax.ShapeDtypeStruct((m, n), out_dtype)], [pl.BlockSpec((tm, tn), lambda i, j, k: (i, j))],
        None, _store(scale, out_dtype))[0]


def _mm_nt_k(name, a, b, out_dtype, tk_pref=1024, tm_pref=1024, tn_pref=1024):
    m, kd = a.shape
    n = b.shape[0]
    tm, tn, tk = _tile(m, tm_pref), _tile(n, tn_pref), _tile(kd, tk_pref)
    return _matmul(
        name, "nt", (m // tm, n // tn, kd // tk),
        a, pl.BlockSpec((tm, tk), lambda i, j, k: (i, k)),
        b, pl.BlockSpec((tn, tk), lambda i, j, k: (j, k)),
        [], [jax.ShapeDtypeStruct((m, n), out_dtype)], [pl.BlockSpec((tm, tn), lambda i, j, k: (i, j))],
        (tm, tn), _store(1.0, out_dtype))[0]


def _mm_tn(name, a, b, scale=1.0, tm_pref=1024, tn_pref=1024, tk_pref=512):
    t, m = a.shape
    n = b.shape[1]
    tm, tn, tk = _tile(m, tm_pref), _tile(n, tn_pref), _tile(t, tk_pref)
    return _matmul(
        name, "tn", (m // tm, n // tn, t // tk),
        a, pl.BlockSpec((tk, tm), lambda i, j, k: (k, i)),
        b, pl.BlockSpec((tk, tn), lambda i, j, k: (k, j)),
        [], [jax.ShapeDtypeStruct((m, n), BF16)], [pl.BlockSpec((tm, tn), lambda i, j, k: (i, j))],
        (tm, tn), _store(scale, BF16))[0]


def _mm_residual(name, a, b, x, scale, tm_pref=1024, tn_pref=1024, tk_pref=1408, comm=None):
    m, kd = a.shape
    n = b.shape[1]
    tm, tn, tk = _tile(m, tm_pref), _tile(n, tn_pref), _tile(kd, tk_pref)

    def epilogue(acc, ex, outs, ids):
        outs[0][...] = ex[0][...] + scale * acc

    res = _matmul(
        name, "nn", (m // tm, n // tn, kd // tk),
        a, pl.BlockSpec((tm, tk), lambda i, j, k: (i, k)),
        b, pl.BlockSpec((tk, tn), lambda i, j, k: (k, j)),
        [(x, pl.BlockSpec((tm, tn), lambda i, j, k: (i, j)))],
        [jax.ShapeDtypeStruct((m, n), F32)], [pl.BlockSpec((tm, tn), lambda i, j, k: (i, j))],
        (tm, tn), epilogue, comm=comm)
    return res[0] if comm is None else (res[0], res[1:])


def _rms_fwd(name, x, g):
    t, d = x.shape
    tm = _tile(t, 512)

    def body(x_ref, g_ref, h_ref):
        xv = x_ref[...]
        r = lax.rsqrt(jnp.mean(xv * xv, axis=-1, keepdims=True) + EPS)
        h_ref[...] = (xv * r * g_ref[...]).astype(BF16)

    return pl.pallas_call(
        body, name=name, grid=(t // tm,),
        in_specs=[pl.BlockSpec((tm, d), lambda i: (i, 0)), pl.BlockSpec((1, d), lambda i: (0, 0))],
        out_specs=pl.BlockSpec((tm, d), lambda i: (i, 0)),
        out_shape=jax.ShapeDtypeStruct((t, d), BF16),
        compiler_params=_params(("parallel",)),
    )(x, g.reshape(1, d))


def _rms_bwd_math(dh, xv, g):
    r = lax.rsqrt(jnp.mean(xv * xv, axis=-1, keepdims=True) + EPS)
    xhat = xv * r
    dxh = dh * g
    dx = r * (dxh - xhat * jnp.mean(dxh * xhat, axis=-1, keepdims=True))
    return dx, dh * xhat


def _rms_bwd(name, dh, x, g, dres):
    t, d = x.shape
    tm = _tile(t, 256)

    def body(dh_ref, x_ref, g_ref, dres_ref, dx_ref, dxb_ref, dg_ref):
        dx, dgt = _rms_bwd_math(dh_ref[...].astype(F32), x_ref[...], g_ref[...])
        dx = dres_ref[...] + dx
        dx_ref[...] = dx
        dxb_ref[...] = dx.astype(BF16)

        @pl.when(pl.program_id(0) == 0)
        def _():
            dg_ref[...] = jnp.zeros_like(dg_ref)

        dg_ref[...] += jnp.sum(dgt, axis=0, keepdims=True)

    row = pl.BlockSpec((tm, d), lambda i: (i, 0))
    vec = pl.BlockSpec((1, d), lambda i: (0, 0))
    return pl.pallas_call(
        body, name=name, grid=(t // tm,),
        in_specs=[row, row, vec, row],
        out_specs=[row, row, vec],
        out_shape=[jax.ShapeDtypeStruct((t, d), F32), jax.ShapeDtypeStruct((t, d), BF16),
                   jax.ShapeDtypeStruct((1, d), F32)],
        compiler_params=_params(("arbitrary",)),
    )(dh, x, g.reshape(1, d), dres)


def _loss_bwd(x, g, target):
    t, d = x.shape
    tm = _tile(t, 256)

    def body(x_ref, g_ref, tgt_ref, loss_ref, dx_ref, dxb_ref, dg_ref):
        xv = x_ref[...]
        gv = g_ref[...]
        r = lax.rsqrt(jnp.mean(xv * xv, axis=-1, keepdims=True) + EPS)
        err = xv * r * gv - tgt_ref[...]
        part = 0.5 * jnp.sum(jnp.mean(err * err, axis=-1, keepdims=True), axis=0, keepdims=True)
        dx, dgt = _rms_bwd_math(err * (1.0 / d), xv, gv)
        dx_ref[...] = dx
        dxb_ref[...] = dx.astype(BF16)

        @pl.when(pl.program_id(0) == 0)
        def _():
            dg_ref[...] = jnp.zeros_like(dg_ref)
            loss_ref[...] = jnp.zeros_like(loss_ref)

        dg_ref[...] += jnp.sum(dgt, axis=0, keepdims=True)
        loss_ref[...] += jnp.broadcast_to(part, loss_ref.shape)

    row = pl.BlockSpec((tm, d), lambda i: (i, 0))
    vec = pl.BlockSpec((1, d), lambda i: (0, 0))
    return pl.pallas_call(
        body, name="loss_bwd", grid=(t // tm,),
        in_specs=[row, vec, row],
        out_specs=[pl.BlockSpec((1, LANE), lambda i: (0, 0)), row, row, vec],
        out_shape=[jax.ShapeDtypeStruct((1, LANE), F32), jax.ShapeDtypeStruct((t, d), F32),
                   jax.ShapeDtypeStruct((t, d), BF16), jax.ShapeDtypeStruct((1, d), F32)],
        compiler_params=_params(("arbitrary",)),
    )(x, g.reshape(1, d), target)


def _sigmoid(x):
    return 1.0 / (1.0 + jnp.exp(-x))


def _ffn_up(name, h, w_in, comm=None):
    t, d = h.shape
    fs = w_in.shape[2]
    f = 2 * fs
    tm, tn = _tile(t, 1024), _tile(fs, 256)
    per = fs // tn
    grid = (t // tm, f // tn)
    c_in, c_in_specs, c_out_specs, c_shapes, c_sems = _comm_io(comm)
    n_c = len(c_in)

    def body(*refs):
        h_ref, wg_ref, wu_ref = refs[:3]
        gu_ref, z_ref = refs[3 + n_c:5 + n_c]
        if comm is not None:
            i, j = pl.program_id(0), pl.program_id(1)
            send, recv = refs[5 + 2 * n_c:]
            last, finish = _comm_run(comm, refs[5 + n_c:5 + 2 * n_c], send, recv,
                                     jnp.logical_and(i == 0, j == 0),
                                     jnp.logical_and(i == grid[0] - 1, j == grid[1] - 1))
        hv = h_ref[...]
        gate = _dot(hv, wg_ref[...], "nn")
        up = _dot(hv, wu_ref[...], "nn")
        gu_ref[0] = gate.astype(BF16)
        gu_ref[1] = up.astype(BF16)
        z_ref[...] = (gate * _sigmoid(gate) * up).astype(BF16)
        if comm is not None:
            pl.when(last)(finish)

    res = pl.pallas_call(
        body, name=name, grid=grid,
        in_specs=[pl.BlockSpec((tm, d), lambda i, j: (i, 0)),
                  pl.BlockSpec((None, d, tn), lambda i, j: (j // per, 0, j % per)),
                  pl.BlockSpec((None, d, tn), lambda i, j: (2 + j // per, 0, j % per))] + c_in_specs,
        out_specs=[pl.BlockSpec((2, tm, tn), lambda i, j: (0, i, j)),
                   pl.BlockSpec((tm, tn), lambda i, j: (i, j))] + c_out_specs,
        out_shape=[jax.ShapeDtypeStruct((2, t, f), BF16), jax.ShapeDtypeStruct((t, f), BF16)] + c_shapes,
        input_output_aliases={3 + k: 2 + k for k in range(n_c)},
        scratch_shapes=c_sems,
        compiler_params=_params(("parallel", "parallel") if comm is None else ("arbitrary", "arbitrary")),
    )(h, w_in, w_in, *c_in)
    return (res[0], res[1]) if comm is None else (res[0], res[1], res[2:])


def _ffn_dact(name, dxb, w_out, gu):
    t, d = dxb.shape
    f = w_out.shape[0]
    tm, tn = _tile(t, 1024), _tile(f, 256)

    def epilogue(acc, ex, outs, ids):
        dz = 0.5 * acc
        gate = ex[0][0].astype(F32)
        up = ex[0][1].astype(F32)
        sg = _sigmoid(gate)
        outs[0][0] = (dz * up * (sg * (1.0 + gate * (1.0 - sg)))).astype(BF16)
        outs[0][1] = (dz * gate * sg).astype(BF16)

    blk = pl.BlockSpec((2, tm, tn), lambda i, j, k: (0, i, j))
    return _matmul(
        name, "nt", (t // tm, f // tn, 1),
        dxb, pl.BlockSpec((tm, d), lambda i, j, k: (i, 0)),
        w_out, pl.BlockSpec((tn, d), lambda i, j, k: (j, 0)),
        [(gu, blk)], [jax.ShapeDtypeStruct((2, t, f), BF16)], [blk], None, epilogue)[0]


def _grad_colsharded(name, h, da):
    t, d = h.shape
    w = da.shape[2]
    ws = w // 2
    tm, tn, tk = _tile(d, 1024), _tile(ws, 1408), _tile(t, 512)
    per = ws // tn
    return _matmul(
        name, "tn", (d // tm, (2 * w) // tn, t // tk),
        h, pl.BlockSpec((tk, tm), lambda i, j, k: (k, i)),
        da, pl.BlockSpec((None, tk, tn), lambda i, j, k: (j // (2 * per), k, j % (2 * per))),
        [], [jax.ShapeDtypeStruct((N_CHIPS, d, ws), BF16)],
        [pl.BlockSpec((None, tm, tn), lambda i, j, k: (j // per, i, j % per))],
        (tm, tn), _store(1.0, BF16))[0]


def _back_colsharded(name, da, w_g):
    _, t, w = da.shape
    d, ws = w_g.shape[1], w_g.shape[2]
    tm, tn, tk = _tile(t, 1024), _tile(d, 1024), _tile(ws, 1408)
    per = ws // tk
    return _matmul(
        name, "nt", (t // tm, d // tn, (2 * w) // tk),
        da, pl.BlockSpec((None, tm, tk), lambda i, j, k: (k // (2 * per), i, k % (2 * per))),
        w_g, pl.BlockSpec((None, tn, tk), lambda i, j, k: (k // per, j, k % per)),
        [], [jax.ShapeDtypeStruct((t, d), F32)], [pl.BlockSpec((tm, tn), lambda i, j, k: (i, j))],
        (tm, tn), _store(1.0, F32))[0]


def _ffn_fwd(tag, x, g, w_in, w_out, prefetch):
    h = _rms_fwd(tag + "_norm", x, g)
    if prefetch:
        gu, z, prefetch = _ffn_up(tag + "_up", h, w_in, comm=(1, prefetch))
        y, prefetch = _mm_residual(tag + "_down", z, w_out, x, 0.5, comm=(2, prefetch))
    else:
        gu, z = _ffn_up(tag + "_up", h, w_in)
        y = _mm_residual(tag + "_down", z, w_out, x, 0.5)
    return y, (x, h, gu, z), prefetch


def _ffn_bwd(tag, saved, g, w_in, w_out, dx, dxb):
    x, h, gu, z = saved
    f = z.shape[1]
    d_w_out = _mm_tn(tag + "_dwout", z, dxb, scale=0.5, tm_pref=1408, tn_pref=2048)
    da = _ffn_dact(tag + "_dact", dxb, w_out, gu)
    d_w_in = _grad_colsharded(tag + "_dwin", h, da)
    dh = _back_colsharded(tag + "_dh", da, w_in)
    dx, dxb, dg = _rms_bwd(tag + "_dnorm", dh, x, g, dx)
    return dx, dxb, dg, d_w_in, d_w_out.reshape(N_CHIPS, f // N_CHIPS, -1)


_GELU_K = math.sqrt(2.0 / math.pi)
_GELU_C = 0.044715


def _gelu(x):
    t = jnp.tanh(_GELU_K * (x + _GELU_C * x * x * x))
    return 0.5 * x * (1.0 + t), t


def _dgelu(x, t):
    return 0.5 * (1.0 + t) + 0.5 * x * (1.0 - t * t) * (_GELU_K * (1.0 + 3.0 * _GELU_C * x * x))


def _causal_block_mask():
    r = lax.broadcasted_iota(jnp.int32, (SGU_BLOCK, SGU_BLOCK), 0) // CHUNK
    c = lax.broadcasted_iota(jnp.int32, (SGU_BLOCK, SGU_BLOCK), 1) // CHUNK
    return r >= c


def _sgu_pre(name, h, w_in, comm=None):
    t, d = h.shape
    ws = w_in.shape[2]
    w = 2 * ws
    tm, tn = _tile(t, 1024), _tile(ws, 512)
    per = ws // tn
    res = _matmul(
        name, "nn", (t // tm, (2 * w) // tn, 1),
        h, pl.BlockSpec((tm, d), lambda i, j, k: (i, 0)),
        w_in, pl.BlockSpec((None, d, tn), lambda i, j, k: (j // per, 0, j % per)),
        [], [jax.ShapeDtypeStruct((2, t, w), BF16)],
        [pl.BlockSpec((None, tm, tn), lambda i, j, k: (j // (2 * per), i, j % (2 * per)))],
        None, _store(1.0, BF16), comm=comm)
    return res[0] if comm is None else (res[0], res[1:])


def _layernorm_stats(v):
    mu = jnp.mean(v, axis=-1, keepdims=True)
    vc = v - mu
    rstd = lax.rsqrt(jnp.mean(vc * vc, axis=-1, keepdims=True) + EPS)
    return vc * rstd, rstd


def _sgu_mid_fwd(pre, gain, bias, w_sp, b_sp_t):
    _, t, w = pre.shape
    gd = w // SGU_GROUPS

    def body(pre_ref, gain_ref, bias_ref, ws_ref, bt_ref, out_ref):
        mask = _causal_block_mask()
        u, _ = _gelu(pre_ref[0].astype(F32))
        v, _ = _gelu(pre_ref[1].astype(F32))
        vhat, _ = _layernorm_stats(v)
        vln = (vhat * gain_ref[...] + bias_ref[...]).astype(BF16)
        for gi in range(SGU_GROUPS):
            cols = slice(gi * gd, (gi + 1) * gd)
            wg = jnp.where(mask, ws_ref[gi], 0.0).astype(BF16)
            mixed = _dot(wg, vln[:, cols], "nn") + bt_ref[:, gi:gi + 1]
            out_ref[:, cols] = (u[:, cols] * mixed).astype(BF16)

    return pl.pallas_call(
        body, name="sgu_mid_fwd", grid=(t // SGU_BLOCK,),
        in_specs=[pl.BlockSpec((2, SGU_BLOCK, w), lambda n: (0, n, 0)),
                  pl.BlockSpec((1, w), lambda n: (0, 0)), pl.BlockSpec((1, w), lambda n: (0, 0)),
                  pl.BlockSpec((SGU_GROUPS, SGU_BLOCK, SGU_BLOCK), lambda n: (0, 0, 0)),
                  pl.BlockSpec((SGU_BLOCK, SGU_GROUPS), lambda n: (0, 0))],
        out_specs=pl.BlockSpec((SGU_BLOCK, w), lambda n: (n, 0)),
        out_shape=jax.ShapeDtypeStruct((t, w), BF16),
        compiler_params=_params(("parallel",)),
    )(pre, gain, bias, w_sp, b_sp_t)


def _sgu_mid_bwd(pre, dgated, gain, bias, w_sp, b_sp_t):
    _, t, w = pre.shape
    gd = w // SGU_GROUPS

    def body(pre_ref, dg_ref, gain_ref, bias_ref, ws_ref, bt_ref,
             dpre_ref, dgain_ref, dbias_ref, dws_ref, dbt_ref, dvln_s):
        @pl.when(pl.program_id(0) == 0)
        def _():
            dgain_ref[...] = jnp.zeros_like(dgain_ref)
            dbias_ref[...] = jnp.zeros_like(dbias_ref)
            dws_ref[...] = jnp.zeros_like(dws_ref)
            dbt_ref[...] = jnp.zeros_like(dbt_ref)

        mask = _causal_block_mask()
        pu = pre_ref[0].astype(F32)
        pv = pre_ref[1].astype(F32)
        u, tu = _gelu(pu)
        v, tv = _gelu(pv)
        vhat, rstd = _layernorm_stats(v)
        gain_v = gain_ref[...]
        vln = (vhat * gain_v + bias_ref[...]).astype(BF16)
        dgt = dg_ref[...].astype(F32)
        for gi in range(SGU_GROUPS):
            cols = slice(gi * gd, (gi + 1) * gd)
            wg = jnp.where(mask, ws_ref[gi], 0.0).astype(BF16)
            vg = vln[:, cols]
            mixed = _dot(wg, vg, "nn") + bt_ref[:, gi:gi + 1]
            dgg = dgt[:, cols]
            dmixed = dgg * u[:, cols]
            dmb = dmixed.astype(BF16)
            dpre_ref[0, :, cols] = (dgg * mixed * _dgelu(pu[:, cols], tu[:, cols])).astype(BF16)
            dbt_ref[:, gi:gi + 1] += jnp.sum(dmixed, axis=1, keepdims=True)
            dws_ref[gi] += jnp.where(mask, _dot(dmb, vg, "nt"), 0.0)
            dvln_s[:, cols] = _dot(wg, dmb, "tn")
        dvln = dvln_s[...]
        dgain_ref[...] += jnp.sum(dvln * vhat, axis=0, keepdims=True)
        dbias_ref[...] += jnp.sum(dvln, axis=0, keepdims=True)
        dvh = dvln * gain_v
        dv = rstd * (dvh - jnp.mean(dvh, axis=-1, keepdims=True)
                     - vhat * jnp.mean(dvh * vhat, axis=-1, keepdims=True))
        dpre_ref[1] = (dv * _dgelu(pv, tv)).astype(BF16)

    vec = pl.BlockSpec((1, w), lambda n: (0, 0))
    wsb = pl.BlockSpec((SGU_GROUPS, SGU_BLOCK, SGU_BLOCK), lambda n: (0, 0, 0))
    btb = pl.BlockSpec((SGU_BLOCK, SGU_GROUPS), lambda n: (0, 0))
    blk2 = pl.BlockSpec((2, SGU_BLOCK, w), lambda n: (0, n, 0))
    return pl.pallas_call(
        body, name="sgu_mid_bwd", grid=(t // SGU_BLOCK,),
        in_specs=[blk2, pl.BlockSpec((SGU_BLOCK, w), lambda n: (n, 0)), vec, vec, wsb, btb],
        out_specs=[blk2, vec, vec, wsb, btb],
        out_shape=[jax.ShapeDtypeStruct((2, t, w), BF16), jax.ShapeDtypeStruct((1, w), F32),
                   jax.ShapeDtypeStruct((1, w), F32),
                   jax.ShapeDtypeStruct((SGU_GROUPS, SGU_BLOCK, SGU_BLOCK), F32),
                   jax.ShapeDtypeStruct((SGU_BLOCK, SGU_GROUPS), F32)],
        scratch_shapes=[pltpu.VMEM((SGU_BLOCK, w), F32)],
        compiler_params=_params(("arbitrary",)),
    )(pre, dgated, gain, bias, w_sp, b_sp_t)


def _sgu_fwd(x, g, w_in, gain, bias, w_sp, b_sp, w_out, prefetch):
    h = _rms_fwd("sgu_norm", x, g)
    pre, prefetch = _sgu_pre("sgu_pre", h, w_in, comm=(1, prefetch))
    gated = _sgu_mid_fwd(pre, gain, bias, w_sp, b_sp.T)
    y, prefetch = _mm_residual("sgu_out", gated, w_out, x, 1.0, tk_pref=1024, comm=(2, prefetch))
    return y, (x, h, pre, gated), prefetch


def _sgu_bwd(saved, g, w_in, gain, bias, w_sp, b_sp, w_out, dx, dxb):
    x, h, pre, gated = saved
    w = gated.shape[1]
    d_w_out = _mm_tn("sgu_dwout", gated, dxb)
    dgated = _mm_nt_full("sgu_dgated", dxb, w_out, BF16)
    dpre, dgain, dbias, dws, dbt = _sgu_mid_bwd(pre, dgated, gain, bias, w_sp, b_sp.T)
    d_w_in = _grad_colsharded("sgu_dwin", h, dpre)
    dh = _back_colsharded("sgu_dh", dpre, w_in)
    dx, dxb, dg = _rms_bwd("sgu_dnorm", dh, x, g, dx)
    small = dict(ln=dg, gain=dgain, bias=dbias, w_sp=dws, b_sp=dbt.T)
    return dx, dxb, small, d_w_in, d_w_out.reshape(N_CHIPS, w // N_CHIPS, -1)


def _rope_tables(positions):
    half = QK_ROPE // 2
    inv_freq = 1.0 / (ROPE_THETA ** (jnp.arange(half, dtype=F32) / half))
    ang = positions.astype(F32)[:, None] * inv_freq
    cos, sin = jnp.cos(ang), jnp.sin(ang)
    t = positions.shape[0]
    zeros = jnp.zeros((t, half), F32)
    rest = jnp.zeros((t, LANE - QK_ROPE), F32)
    c = jnp.concatenate([cos, cos, rest + 1.0], axis=1)
    s_up = jnp.concatenate([zeros, sin, rest], axis=1)
    s_dn = jnp.concatenate([-sin, zeros, rest], axis=1)
    return c, s_up, s_dn


def _rope_apply(x, c, s_up, s_dn):
    half = QK_ROPE // 2
    return x * c + pltpu.roll(x, half, 1) * s_up + pltpu.roll(x, LANE - half, 1) * s_dn


def _rope_apply_t(dy, c, s_up, s_dn):
    half = QK_ROPE // 2
    return dy * c - pltpu.roll(dy, LANE - half, 1) * s_dn - pltpu.roll(dy, half, 1) * s_up


def _mla_norm_fwd(proj, gq, gkv):
    t, p = proj.shape
    ql, kvl = gq.shape[1], gkv.shape[1]
    tm = _tile(t, 512)

    def body(p_ref, gq_ref, gkv_ref, qn_ref, kvn_ref):
        for lo, n, g_ref, o_ref in ((0, ql, gq_ref, qn_ref), (ql, kvl, gkv_ref, kvn_ref)):
            xv = p_ref[:, lo:lo + n]
            r = lax.rsqrt(jnp.mean(xv * xv, axis=-1, keepdims=True) + EPS)
            o_ref[...] = (xv * r * g_ref[...]).astype(BF16)

    return pl.pallas_call(
        body, name="mla_norm_fwd", grid=(t // tm,),
        in_specs=[pl.BlockSpec((tm, p), lambda i: (i, 0)), pl.BlockSpec((1, ql), lambda i: (0, 0)),
                  pl.BlockSpec((1, kvl), lambda i: (0, 0))],
        out_specs=[pl.BlockSpec((tm, ql), lambda i: (i, 0)), pl.BlockSpec((tm, kvl), lambda i: (i, 0))],
        out_shape=[jax.ShapeDtypeStruct((t, ql), BF16), jax.ShapeDtypeStruct((t, kvl), BF16)],
        compiler_params=_params(("parallel",)),
    )(proj, gq, gkv)


def _mla_norm_bwd(proj, dqn, dkvn, dkr, gq, gkv):
    t, p = proj.shape
    ql, kvl = gq.shape[1], gkv.shape[1]
    tm = _tile(t, 256)

    def body(p_ref, dqn_ref, dkvn_ref, dkr_ref, gq_ref, gkv_ref, dp_ref, dgq_ref, dgkv_ref):
        @pl.when(pl.program_id(0) == 0)
        def _():
            dgq_ref[...] = jnp.zeros_like(dgq_ref)
            dgkv_ref[...] = jnp.zeros_like(dgkv_ref)

        for lo, n, g_ref, d_ref, dg_ref in ((0, ql, gq_ref, dqn_ref, dgq_ref),
                                             (ql, kvl, gkv_ref, dkvn_ref, dgkv_ref)):
            dx, dgt = _rms_bwd_math(d_ref[...], p_ref[:, lo:lo + n], g_ref[...])
            dp_ref[:, lo:lo + n] = dx.astype(BF16)
            dg_ref[...] += jnp.sum(dgt, axis=0, keepdims=True)
        dp_ref[:, ql + kvl:] = dkr_ref[...].astype(BF16)

    def row(n):
        return pl.BlockSpec((tm, n), lambda i: (i, 0))

    def vec(n):
        return pl.BlockSpec((1, n), lambda i: (0, 0))

    return pl.pallas_call(
        body, name="mla_norm_bwd", grid=(t // tm,),
        in_specs=[row(p), row(ql), row(kvl), row(LANE), vec(ql), vec(kvl)],
        out_specs=[row(p), vec(ql), vec(kvl)],
        out_shape=[jax.ShapeDtypeStruct((t, p), BF16), jax.ShapeDtypeStruct((1, ql), F32),
                   jax.ShapeDtypeStruct((1, kvl), F32)],
        compiler_params=_params(("arbitrary",)),
    )(proj, dqn, dkvn, dkr, gq, gkv)


def _mla_q_up(qn, wq, tables):
    t, ql = qn.shape
    n = wq.shape[1]
    tm = _tile(t, 1024)
    scale = QK_DIM ** -0.5

    def epilogue(acc, ex, outs, ids):
        outs[0][:, :QK_NOPE] = (scale * acc[:, :QK_NOPE]).astype(BF16)
        hi = _rope_apply(acc[:, QK_NOPE:], ex[0][...], ex[1][...], ex[2][...])
        outs[0][:, QK_NOPE:] = (scale * hi).astype(BF16)

    tab = pl.BlockSpec((tm, LANE), lambda i, j, k: (i, 0))
    return _matmul(
        "mla_q_up", "nn", (t // tm, n // HEAD_PAD, 1),
        qn, pl.BlockSpec((tm, ql), lambda i, j, k: (i, 0)),
        wq, pl.BlockSpec((ql, HEAD_PAD), lambda i, j, k: (0, j)),
        [(tb, tab) for tb in tables],
        [jax.ShapeDtypeStruct((t, n), BF16)], [pl.BlockSpec((tm, HEAD_PAD), lambda i, j, k: (i, j))],
        None, epilogue)[0]


def _mla_kv_up(kvn, wkv, proj, tables, heads):
    t, kvl = kvn.shape
    n = wkv.shape[1]
    p = proj.shape[1]
    tm = _tile(t, 1024)

    def epilogue(acc, ex, outs, ids):
        kr = _rope_apply(ex[0][...], ex[1][...], ex[2][...], ex[3][...])
        outs[0][:, :QK_NOPE] = acc[:, :QK_NOPE].astype(BF16)
        outs[0][:, QK_NOPE:] = (acc[:, QK_NOPE:] + jnp.where(ids[1] < heads, kr, 1.0)).astype(BF16)

    tab = pl.BlockSpec((tm, LANE), lambda i, j, k: (i, 0))
    kr_spec = pl.BlockSpec((tm, LANE), lambda i, j, k: (i, p // LANE - 1))
    return _matmul(
        "mla_kv_up", "nn", (t // tm, n // HEAD_PAD, 1),
        kvn, pl.BlockSpec((tm, kvl), lambda i, j, k: (i, 0)),
        wkv, pl.BlockSpec((kvl, HEAD_PAD), lambda i, j, k: (0, j)),
        [(proj, kr_spec)] + [(tb, tab) for tb in tables],
        [jax.ShapeDtypeStruct((t, n), BF16)], [pl.BlockSpec((tm, HEAD_PAD), lambda i, j, k: (i, j))],
        None, epilogue)[0]


def _chunk_mask(tq, tk):
    r = lax.broadcasted_iota(jnp.int32, (tq, tk), 0) // CHUNK
    c = lax.broadcasted_iota(jnp.int32, (tq, tk), 1) // CHUNK
    return c <= r


def _block_pairs(nb, key_major):
    if key_major:
        pairs = [(qi, ki) for ki in range(nb) for qi in range(ki, nb)]
    else:
        pairs = [(qi, ki) for qi in range(nb) for ki in range(qi + 1)]
    return (jnp.asarray([p[0] for p in pairs], jnp.int32), jnp.asarray([p[1] for p in pairs], jnp.int32))


def _flash_fwd(qp, kv, heads):
    t = qp.shape[0]
    tb = _tile(t, 512)
    nb = t // tb
    rep = tb // LANE
    qt, kt = _block_pairs(nb, key_major=False)

    def body(qt_ref, kt_ref, q_ref, k_ref, v_ref, o_ref, lse_ref, m_s, acc_s):
        p = pl.program_id(1)
        qi, ki = qt_ref[p], kt_ref[p]

        @pl.when(ki == 0)
        def _():
            m_s[...] = jnp.full_like(m_s, -1e30)
            acc_s[...] = jnp.zeros_like(acc_s)

        def step(masked):
            s = _dot(q_ref[...], k_ref[...], "nt")
            if masked:
                s = jnp.where(_chunk_mask(tb, tb), s, -1e30)
            m_prev = m_s[...]
            m_new = jnp.maximum(m_prev, jnp.max(s, axis=1, keepdims=True))
            alpha = jnp.exp(m_prev - m_new)
            pr = jnp.exp(s - jnp.tile(m_new, (1, rep))).astype(BF16)
            pv = _dot(pr, v_ref[...], "nn")
            acc_s[:, :V_DIM] = alpha * acc_s[:, :V_DIM] + pv[:, :V_DIM]
            acc_s[:, V_DIM:] = alpha * acc_s[:, V_DIM:] + pv[:, V_DIM:]
            m_s[...] = m_new

        @pl.when(ki < qi)
        def _():
            step(False)

        @pl.when(ki == qi)
        def _():
            step(True)
            l = acc_s[:, V_DIM:]
            o_ref[...] = (acc_s[:, :V_DIM] / l).astype(BF16)
            lse_ref[...] = m_s[...] + jnp.log(l)

    return pl.pallas_call(
        body, name="mla_flash_fwd",
        grid_spec=pltpu.PrefetchScalarGridSpec(
            num_scalar_prefetch=2, grid=(heads, int(qt.shape[0])),
            in_specs=[pl.BlockSpec((tb, HEAD_PAD), lambda h, p, qt, kt: (qt[p], h)),
                      pl.BlockSpec((tb, HEAD_PAD), lambda h, p, qt, kt: (kt[p], h)),
                      pl.BlockSpec((tb, HEAD_PAD), lambda h, p, qt, kt: (kt[p], heads + h))],
            out_specs=[pl.BlockSpec((tb, V_DIM), lambda h, p, qt, kt: (qt[p], h)),
                       pl.BlockSpec((None, tb, LANE), lambda h, p, qt, kt: (h, qt[p], 0))],
            scratch_shapes=[pltpu.VMEM((tb, LANE), F32), pltpu.VMEM((tb, HEAD_PAD), F32)]),
        out_shape=[jax.ShapeDtypeStruct((t, heads * V_DIM), BF16),
                   jax.ShapeDtypeStruct((heads, t, LANE), F32)],
        compiler_params=_params(("parallel", "arbitrary")),
    )(qt, kt, qp, kv, kv)


def _flash_delta(o, do, heads):
    t = o.shape[0]
    tm = _tile(t, 256)

    def body(o_ref, do_ref, d_ref):
        for h in range(heads):
            cols = slice(h * V_DIM, (h + 1) * V_DIM)
            prod = o_ref[:, cols].astype(F32) * do_ref[:, cols].astype(F32)
            d_ref[h] = jnp.broadcast_to(jnp.sum(prod, axis=1, keepdims=True), (tm, LANE))

    row = pl.BlockSpec((tm, heads * V_DIM), lambda i: (i, 0))
    return pl.pallas_call(
        body, name="mla_flash_delta", grid=(t // tm,), in_specs=[row, row],
        out_specs=pl.BlockSpec((heads, tm, LANE), lambda i: (0, i, 0)),
        out_shape=jax.ShapeDtypeStruct((heads, t, LANE), F32),
        compiler_params=_params(("parallel",)),
    )(o, do)


def _flash_bwd(qp, kv, do, lse, delta, heads):
    t = qp.shape[0]
    tb = _tile(t, 512)
    nb = t // tb
    rep = tb // LANE
    qt, kt = _block_pairs(nb, key_major=True)

    def body(qt_ref, kt_ref, q_ref, k_ref, v_ref, do_ref, lse_ref, dl_ref, dq_ref, dk_ref, dv_ref, dk_s, dv_s):
        p = pl.program_id(1)
        qi, ki = qt_ref[p], kt_ref[p]

        @pl.when(p == 0)
        def _():
            dq_ref[...] = jnp.zeros_like(dq_ref)

        @pl.when(qi == ki)
        def _():
            dk_s[...] = jnp.zeros_like(dk_s)
            dv_s[...] = jnp.zeros_like(dv_s)

        def step(masked):
            q = q_ref[...]
            k = k_ref[...]
            dov = do_ref[...]
            s = _dot(q, k, "nt")
            pr = jnp.exp(s - jnp.tile(lse_ref[...], (1, rep)))
            if masked:
                pr = jnp.where(_chunk_mask(tb, tb), pr, 0.0)
            dv_s[...] += _dot(pr.astype(BF16), dov, "tn")
            dp = _dot(dov, v_ref[...], "nt")
            ds = (pr * (dp - jnp.tile(dl_ref[...], (1, rep)))).astype(BF16)
            rows = pl.ds(pl.multiple_of(qi * tb, tb), tb)
            dq_ref[rows, :] += _dot(ds, k, "nn")
            dk_s[...] += _dot(ds, q, "tn")

        @pl.when(qi > ki)
        def _():
            step(False)

        @pl.when(qi == ki)
        def _():
            step(True)

        @pl.when(qi == nb - 1)
        def _():
            dk_ref[...] = dk_s[...]
            dv_ref[...] = dv_s[...]

    def qrow(width):
        return pl.BlockSpec((tb, width), lambda h, p, qt, kt: (qt[p], h))

    def stat():
        return pl.BlockSpec((None, tb, LANE), lambda h, p, qt, kt: (h, qt[p], 0))

    return pl.pallas_call(
        body, name="mla_flash_bwd",
        grid_spec=pltpu.PrefetchScalarGridSpec(
            num_scalar_prefetch=2, grid=(heads, int(qt.shape[0])),
            in_specs=[qrow(HEAD_PAD),
                      pl.BlockSpec((tb, HEAD_PAD), lambda h, p, qt, kt: (kt[p], h)),
                      pl.BlockSpec((tb, V_DIM), lambda h, p, qt, kt: (kt[p], 2 * (heads + h))),
                      qrow(V_DIM), stat(), stat()],
            out_specs=[pl.BlockSpec((t, HEAD_PAD), lambda h, p, qt, kt: (0, h)),
                       pl.BlockSpec((tb, HEAD_PAD), lambda h, p, qt, kt: (kt[p], h)),
                       pl.BlockSpec((tb, V_DIM), lambda h, p, qt, kt: (kt[p], h))],
            scratch_shapes=[pltpu.VMEM((tb, HEAD_PAD), F32), pltpu.VMEM((tb, V_DIM), F32)]),
        out_shape=[jax.ShapeDtypeStruct((t, heads * HEAD_PAD), F32),
                   jax.ShapeDtypeStruct((t, heads * HEAD_PAD), F32),
                   jax.ShapeDtypeStruct((t, heads * V_DIM), F32)],
        compiler_params=_params(("parallel", "arbitrary")),
    )(qt, kt, qp, kv, kv, do, lse, delta)


def _mla_attn_post(dqp, dkp, dv, tables, heads):
    t = dqp.shape[0]
    tm = _tile(t, 256)
    scale = QK_DIM ** -0.5
    kw, vw = heads * HEAD_PAD, heads * V_DIM

    def body(dq_ref, dk_ref, dv_ref, c_ref, su_ref, sd_ref, dqb_ref, dkvb_ref, dkr_ref):
        c, su, sd = c_ref[...], su_ref[...], sd_ref[...]
        kr = jnp.zeros((tm, LANE), F32)
        for h in range(heads):
            lo = h * HEAD_PAD
            mid = lo + QK_NOPE
            dqb_ref[:, lo:mid] = (scale * dq_ref[:, lo:mid]).astype(BF16)
            dqb_ref[:, mid:mid + LANE] = (scale * _rope_apply_t(dq_ref[:, mid:mid + LANE], c, su, sd)).astype(BF16)
            kr = kr + dk_ref[:, mid:mid + LANE]
            dkvb_ref[:, kw + lo:kw + mid] = dv_ref[:, h * V_DIM:(h + 1) * V_DIM].astype(BF16)
            dkvb_ref[:, kw + mid:kw + lo + HEAD_PAD] = jnp.zeros((tm, HEAD_PAD - V_DIM), BF16)
        dkvb_ref[:, :kw] = dk_ref[...].astype(BF16)
        dkr_ref[...] = _rope_apply_t(kr, c, su, sd)

    def row(n):
        return pl.BlockSpec((tm, n), lambda i: (i, 0))

    return pl.pallas_call(
        body, name="mla_attn_post", grid=(t // tm,),
        in_specs=[row(kw), row(kw), row(vw), row(LANE), row(LANE), row(LANE)],
        out_specs=[row(kw), row(2 * kw), row(LANE)],
        out_shape=[jax.ShapeDtypeStruct((t, kw), BF16), jax.ShapeDtypeStruct((t, 2 * kw), BF16),
                   jax.ShapeDtypeStruct((t, LANE), F32)],
        compiler_params=_params(("parallel",)),
    )(dqp, dkp, dv, *tables)


def _mla_weights(w_in_g, w_q_g, w_kv_g, w_out_g):
    d = w_in_g.shape[0] * w_in_g.shape[1]
    pw = w_in_g.shape[2]
    w_in = jnp.pad(w_in_g.reshape(d, pw), ((0, 0), (0, LANE - QK_ROPE)))
    ql = w_q_g.shape[1]
    wq = jnp.transpose(w_q_g, (1, 0, 2)).reshape(ql, -1, QK_DIM)
    heads = wq.shape[1]
    wq = jnp.pad(wq, ((0, 0), (0, 0), (0, HEAD_PAD - QK_DIM))).reshape(ql, heads * HEAD_PAD)
    kvl = w_kv_g.shape[1]
    wkv = jnp.transpose(w_kv_g, (1, 0, 2)).reshape(kvl, heads, QK_NOPE + V_DIM)
    wk = jnp.pad(wkv[:, :, :QK_NOPE], ((0, 0), (0, 0), (0, HEAD_PAD - QK_NOPE))).reshape(kvl, heads * HEAD_PAD)
    wv = jnp.pad(wkv[:, :, QK_NOPE:], ((0, 0), (0, 0), (0, HEAD_PAD - V_DIM))).reshape(kvl, heads * HEAD_PAD)
    return w_in, wq, jnp.concatenate([wk, wv], axis=1), w_out_g.reshape(heads * V_DIM, -1), heads


def _mla_unpermute(d_w_in, d_wq, d_wkv, heads, pw):
    d = d_w_in.shape[0]
    g_in = d_w_in[:, :pw].reshape(N_CHIPS, d // N_CHIPS, pw)
    ql = d_wq.shape[0]
    g_q = d_wq.reshape(ql, heads, HEAD_PAD)[:, :, :QK_DIM].reshape(ql, N_CHIPS, -1)
    kvl = d_wkv.shape[0]
    g_k = d_wkv[:, :heads * HEAD_PAD].reshape(kvl, heads, HEAD_PAD)[:, :, :QK_NOPE]
    g_v = d_wkv[:, heads * HEAD_PAD:].reshape(kvl, heads, HEAD_PAD)[:, :, :V_DIM]
    g_kv = jnp.concatenate([g_k, g_v], axis=2).reshape(kvl, N_CHIPS, -1)
    return g_in, jnp.transpose(g_q, (1, 0, 2)), jnp.transpose(g_kv, (1, 0, 2))


def _mla_fwd(x, g, wts, gq, gkv, tables):
    w_in, wq, wkv, w_out, heads = wts
    h = _rms_fwd("mla_norm", x, g)
    proj = _mm_nn_full("mla_proj", h, w_in, F32, tn_pref=w_in.shape[1])
    qn, kvn = _mla_norm_fwd(proj, gq, gkv)
    qp = _mla_q_up(qn, wq, tables)
    kv = _mla_kv_up(kvn, wkv, proj, tables, heads)
    o, lse = _flash_fwd(qp, kv, heads)
    y = _mm_residual("mla_out", o, w_out, x, 1.0, tk_pref=2048)
    return y, (x, h, proj, qn, kvn, qp, kv, o, lse)


def _mla_bwd(saved, g, wts, gq, gkv, tables, dx, dxb, pw):
    w_in, wq, wkv, w_out, heads = wts
    x, h, proj, qn, kvn, qp, kv, o, lse = saved
    d_w_out = _mm_tn("mla_dwout", o, dxb)
    do = _mm_nt_full("mla_do", dxb, w_out, BF16)
    dqp, dkp, dv = _flash_bwd(qp, kv, do, lse, _flash_delta(o, do, heads), heads)
    dqb, dkvb, dkr = _mla_attn_post(dqp, dkp, dv, tables, heads)
    d_wq = _mm_tn("mla_dwq", qn, dqb)
    dqn = _mm_nt_k("mla_dqn", dqb, wq, F32)
    d_wkv = _mm_tn("mla_dwkv", kvn, dkvb)
    dkvn = _mm_nt_k("mla_dkvn", dkvb, wkv, F32)
    dproj, dgq, dgkv = _mla_norm_bwd(proj, dqn, dkvn, dkr, gq, gkv)
    d_w_in = _mm_tn("mla_dwin", h, dproj, tn_pref=dproj.shape[1])
    dh = _mm_nt_full("mla_dh", dproj, w_in, F32, tn_pref=1024)
    dx, dxb, dg = _rms_bwd("mla_dnorm", dh, x, g, dx)
    g_in, g_q, g_kv = _mla_unpermute(d_w_in, d_wq, d_wkv, heads, pw)
    small = dict(ln=dg, gq=dgq, gkv=dgkv)
    return dx, dxb, small, g_in, g_q, g_kv, d_w_out.reshape(N_CHIPS, d_w_out.shape[0] // N_CHIPS, -1)


def _gather_weights(bufs, norms):
    nt = len(bufs)

    def body(*refs):
        n_in = refs[nt]
        outs, n_out = refs[nt + 1:2 * nt + 1], refs[2 * nt + 1]
        send, recv, fsend, frecv, loc, nsend, nrecv = refs[2 * nt + 2:]
        x, y, c, chips = _place()
        me = 2 * x + y
        sib = (x, y, 1 - c)

        local = pltpu.make_async_copy(n_in, n_out.at[me], loc)
        local.start()

        def place(t, chip, half):
            return outs[t].at[2 * chip[0] + chip[1], half]

        def ici(t, j, chip):
            return pltpu.make_async_remote_copy(
                src_ref=place(t, (x, y), c), dst_ref=place(t, (x, y), c), send_sem=send.at[t, j],
                recv_sem=recv.at[t, j], device_id=(*chip, c), device_id_type=MESH)

        def fwd(t, j, chip, half):
            return pltpu.make_async_remote_copy(
                src_ref=place(t, chip, half), dst_ref=place(t, chip, half), send_sem=fsend.at[t, j],
                recv_sem=frecv.at[t, j], device_id=sib, device_id_type=MESH)

        def nrm(j, chip, owner):
            return pltpu.make_async_remote_copy(
                src_ref=n_in, dst_ref=n_out.at[2 * owner[0] + owner[1]], send_sem=nsend.at[j], recv_sem=nrecv.at[j],
                device_id=(*chip, c), device_id_type=MESH)

        firsts = [ici(t, j, chip) for t in range(nt) for j, chip in enumerate(chips)]
        firsts += [nrm(j, chip, (x, y)) for j, chip in enumerate(chips)]
        for cp in firsts:
            cp.start()
        passed = []
        for t in range(nt):
            for j, chip in enumerate(chips):
                pltpu.make_async_remote_copy(
                    src_ref=place(t, chip, c), dst_ref=place(t, chip, c), send_sem=send.at[t, j],
                    recv_sem=recv.at[t, j], device_id=(*chip, c), device_id_type=MESH).wait_recv()
                cp = fwd(t, j, chip, c)
                cp.start()
                passed.append(cp)
        for t in range(nt):
            for j, chip in enumerate(chips):
                fwd(t, j, chip, 1 - c).wait_recv()
        for j, chip in enumerate(chips):
            nrm(j, chip, chip).wait_recv()
        for cp in firsts + passed:
            cp.wait_send()
        local.wait()

    out_shape = [jax.ShapeDtypeStruct(b.shape, b.dtype) for b in bufs]
    out_shape.append(jax.ShapeDtypeStruct((N_CHIPS,) + norms.shape, norms.dtype))
    res = pl.pallas_call(
        body, name="gather_weights",
        in_specs=[ANY] * (nt + 1), out_specs=[ANY] * (nt + 1), out_shape=out_shape,
        input_output_aliases={t: t for t in range(nt)},
        scratch_shapes=[pltpu.SemaphoreType.DMA((nt, 3)), pltpu.SemaphoreType.DMA((nt, 3)),
                        pltpu.SemaphoreType.DMA((nt, 3)), pltpu.SemaphoreType.DMA((nt, 3)),
                        pltpu.SemaphoreType.DMA, pltpu.SemaphoreType.DMA((3,)),
                        pltpu.SemaphoreType.DMA((3,))],
    )(*bufs, norms)
    return res[:nt], res[nt]


def _swap_halves(grads):
    nt = len(grads)

    def body(*refs):
        ins, outs = refs[:nt], refs[nt:2 * nt]
        send, recv = refs[2 * nt:]
        x, y, c, _ = _place()
        cps = [pltpu.make_async_remote_copy(
            src_ref=ins[t].at[:, 1 - c], dst_ref=outs[t], send_sem=send.at[t], recv_sem=recv.at[t],
            device_id=(x, y, 1 - c), device_id_type=MESH) for t in range(nt)]
        for cp in cps:
            cp.start()
        for cp in cps:
            cp.wait()

    return pl.pallas_call(
        body, name="grad_swap_halves",
        in_specs=[ANY] * nt, out_specs=[ANY] * nt,
        out_shape=[jax.ShapeDtypeStruct((g.shape[0],) + g.shape[2:], g.dtype) for g in grads],
        scratch_shapes=[pltpu.SemaphoreType.DMA((nt,)), pltpu.SemaphoreType.DMA((nt,))],
    )(*grads)


def _scatter_chips(parts):
    nt = len(parts)

    def body(*refs):
        ins, outs = refs[:nt], refs[nt:2 * nt]
        send, recv = refs[2 * nt:]
        x, y, c, chips = _place()
        cps = [pltpu.make_async_remote_copy(
            src_ref=ins[t].at[2 * chip[0] + chip[1]], dst_ref=outs[t].at[j], send_sem=send.at[t, j],
            recv_sem=recv.at[t, j], device_id=(*chip, c), device_id_type=MESH)
            for t in range(nt) for j, chip in enumerate(chips)]
        for cp in cps:
            cp.start()
        for cp in cps:
            cp.wait()

    return pl.pallas_call(
        body, name="grad_scatter_chips",
        in_specs=[ANY] * nt, out_specs=[ANY] * nt,
        out_shape=[jax.ShapeDtypeStruct((3,) + p.shape[1:], p.dtype) for p in parts],
        scratch_shapes=[pltpu.SemaphoreType.DMA((nt, 3)), pltpu.SemaphoreType.DMA((nt, 3))],
    )(*parts)


def _join_halves(fulls):
    nt = len(fulls)
    where = [(t, l) for t in range(nt) for l in range(fulls[t].shape[0])]

    def body(*refs):
        outs = refs[nt:2 * nt]
        send, recv = refs[2 * nt:]
        x, y, c, _ = _place()

        def copy(t, l, half):
            return pltpu.make_async_remote_copy(
                src_ref=outs[t].at[l, half], dst_ref=outs[t].at[l, half], send_sem=send.at[t, l],
                recv_sem=recv.at[t, l], device_id=(x, y, 1 - c), device_id_type=MESH)

        cps = [copy(t, l, c) for t, l in where]
        for cp in cps:
            cp.start()
        for t, l in where:
            copy(t, l, 1 - c).wait_recv()
        for cp in cps:
            cp.wait_send()

    return pl.pallas_call(
        body, name="grad_join_halves",
        in_specs=[ANY] * nt, out_specs=[ANY] * nt,
        out_shape=[jax.ShapeDtypeStruct(f.shape, f.dtype) for f in fulls],
        input_output_aliases={t: t for t in range(nt)},
        scratch_shapes=[pltpu.SemaphoreType.DMA((nt, 2)), pltpu.SemaphoreType.DMA((nt, 2))],
    )(*fulls)


def _gather_all(block):
    m_per, n = block.shape

    def body(x_ref, out_ref, send_sems, recv_sems, local_sem):
        x, y, c, chips = _place()
        me, sibling = (x, y, c), (x, y, 1 - c)

        def rows(px, py, pc):
            return out_ref.at[pl.ds((4 * px + 2 * py + pc) * m_per, m_per), :]

        def copy(k, block_of, to, src=None):
            return pltpu.make_async_remote_copy(
                src_ref=rows(*block_of) if src is None else src, dst_ref=rows(*block_of),
                send_sem=send_sems.at[k], recv_sem=recv_sems.at[k], device_id=to, device_id_type=MESH)

        mine = pltpu.make_async_copy(x_ref, rows(*me), local_sem)
        mine.start()
        first = [copy(0, me, sibling, src=x_ref)]
        first += [copy(1 + j, me, (*chip, c), src=x_ref) for j, chip in enumerate(chips)]
        for cp in first:
            cp.start()
        passed = [copy(4 + j, (*chip, c), sibling) for j, chip in enumerate(chips)]
        for j, chip in enumerate(chips):
            copy(1 + j, (*chip, c), me).wait_recv()
            passed[j].start()
        copy(0, sibling, me).wait_recv()
        for j, chip in enumerate(chips):
            copy(4 + j, (*chip, 1 - c), me).wait_recv()
        for cp in first + passed:
            cp.wait_send()
        mine.wait()

    return pl.pallas_call(
        body, name="gather_small_grads",
        out_shape=jax.ShapeDtypeStruct((N_DEV * m_per, n), block.dtype),
        in_specs=[pl.BlockSpec(memory_space=pltpu.VMEM)],
        out_specs=pl.BlockSpec(memory_space=pltpu.VMEM),
        scratch_shapes=[pltpu.SemaphoreType.DMA((7,)), pltpu.SemaphoreType.DMA((7,)), pltpu.SemaphoreType.DMA],
    )(block)


def _row_tile(r, c, elems=512 * 1024):
    t = max(8, min(r, (elems // c) // 8 * 8))
    while t > 8 and r % t:
        t -= 8
    return t if r % t == 0 else r


def _add_halves(idx, grad, other):
    n, _, r, w = grad.shape
    tr = _row_tile(r, w)

    def body(idx_ref, g_ref, o_ref, out_ref):
        out_ref[...] = (g_ref[...].astype(F32) + o_ref[...].astype(F32)).astype(BF16)

    return pl.pallas_call(
        body, name="grad_add_halves",
        grid_spec=pltpu.PrefetchScalarGridSpec(
            num_scalar_prefetch=1, grid=(n, r // tr),
            in_specs=[pl.BlockSpec((None, None, tr, w), lambda k, i, idx: (k, idx[0], i, 0)),
                      pl.BlockSpec((None, tr, w), lambda k, i, idx: (k, i, 0))],
            out_specs=pl.BlockSpec((None, tr, w), lambda k, i, idx: (k, i, 0))),
        out_shape=jax.ShapeDtypeStruct((n, r, w), BF16),
        compiler_params=_params(("parallel", "parallel")),
    )(idx, grad, other)


def _sum_chips(idx, part, landed, layer, n_layers, prev):
    _, r, w = part.shape
    tr = _row_tile(r, w)

    def body(idx_ref, p_ref, l_ref, *rest):
        acc = p_ref[...].astype(F32)
        for j in range(3):
            acc = acc + l_ref[j].astype(F32)
        rest[-1][...] = acc

    return pl.pallas_call(
        body, name="grad_sum_chips",
        grid_spec=pltpu.PrefetchScalarGridSpec(
            num_scalar_prefetch=1, grid=(r // tr,),
            in_specs=[pl.BlockSpec((None, tr, w), lambda i, idx: (idx[0], i, 0)),
                      pl.BlockSpec((3, tr, w), lambda i, idx: (0, i, 0))] + ([] if prev is None else [ANY]),
            out_specs=pl.BlockSpec((None, None, tr, w), lambda i, idx: (layer, idx[1], i, 0))),
        out_shape=jax.ShapeDtypeStruct((n_layers, 2, r, w), F32),
        input_output_aliases={} if prev is None else {3: 0},
        compiler_params=_params(("parallel",)),
    )(idx, part, landed, *([] if prev is None else [prev]))


def _adamw_math(w, g, m, v):
    m = ADAM_B1 * m + (1.0 - ADAM_B1) * g
    v = ADAM_B2 * v + (1.0 - ADAM_B2) * (g * g)
    m_hat = m / (1.0 - ADAM_B1 ** ADAM_STEP)
    v_hat = v / (1.0 - ADAM_B2 ** ADAM_STEP)
    delta = -ADAM_LR * (m_hat / (jnp.sqrt(v_hat) + ADAM_EPS) + ADAM_WD * w)
    return delta, m, v


def _adamw(name, w, g, m, v):
    r, c = w.shape
    tr = _row_tile(r, c, 256 * 1024)

    def body(w_ref, g_ref, m_ref, v_ref, d_ref, nm_ref, nv_ref):
        d_ref[...], nm_ref[...], nv_ref[...] = _adamw_math(w_ref[...], g_ref[...], m_ref[...], v_ref[...])

    blk = pl.BlockSpec((tr, c), lambda i: (i, 0))
    return pl.pallas_call(
        body, name=name, grid=(r // tr,), in_specs=[blk] * 4, out_specs=[blk] * 3,
        out_shape=[jax.ShapeDtypeStruct((r, c), F32)] * 3,
        compiler_params=_params(("parallel",)),
    )(w, g, m, v)


def _adamw_summed(w, parts, m, v):
    r, c = w.shape

    def body(w_ref, p_ref, m_ref, v_ref, g_ref, d_ref, nm_ref, nv_ref):
        g = p_ref[0:r, :]
        for k in range(1, N_DEV):
            g = g + p_ref[k * r:(k + 1) * r, :]
        g_ref[...] = g
        d_ref[...], nm_ref[...], nv_ref[...] = _adamw_math(w_ref[...], g, m_ref[...], v_ref[...])

    return pl.pallas_call(
        body, name="adamw_replicated",
        out_shape=[jax.ShapeDtypeStruct((r, c), F32)] * 4,
        compiler_params=pltpu.CompilerParams(vmem_limit_bytes=VMEM_LIMIT_BYTES),
    )(w, parts, m, v)


def _pack(arrays):
    return jnp.concatenate([a.reshape(-1, LANE) for a in arrays], axis=0)


def _unpack(packed, shapes):
    out, row = [], 0
    for s in shapes:
        n = math.prod(s) // LANE
        out.append(packed[row:row + n].reshape(s))
        row += n
    return out


def _cast_into(name, idx, w, layer):
    _, rows, c = w.shape
    r = rows // 2
    tr = _row_tile(r, c)
    per = r // tr

    def body(idx_ref, w_ref, o_ref):
        o_ref[...] = w_ref[...].astype(BF16)

    return pl.pallas_call(
        body, name=name,
        grid_spec=pltpu.PrefetchScalarGridSpec(
            num_scalar_prefetch=1, grid=(2, per),
            in_specs=[pl.BlockSpec((None, tr, c), lambda h, i, idx: (layer, h * per + i, 0))],
            out_specs=pl.BlockSpec((None, None, tr, c), lambda h, i, idx: (idx[0], h, i, 0))),
        out_shape=jax.ShapeDtypeStruct((N_CHIPS, 2, r, c), BF16),
        compiler_params=_params(("parallel", "parallel")),
    )(idx, w)


BIG = ["ffn1_w_in", "ffn1_w_out", "ffn2_w_in", "ffn2_w_out", "sgu_w_in", "sgu_w_out",
       "mla_w_in", "mla_w_q_up", "mla_w_kv_up", "mla_w_out"]
STAGES = [
    [("ffn1_w_in", 0), ("ffn1_w_out", 0)],
    [("sgu_w_in", 0), ("sgu_w_out", 0)],
    [("ffn2_w_in", 0), ("ffn2_w_out", 0)],
    [("ffn1_w_in", 1), ("ffn1_w_out", 1)],
    [("mla_w_in", 0), ("mla_w_q_up", 0), ("mla_w_kv_up", 0), ("mla_w_out", 0)],
    [("ffn2_w_in", 1), ("ffn2_w_out", 1)],
]
REPLICATED = ["ln_ffn1", "ln_mix", "ln_ffn2", "sgu_v_gain", "sgu_v_bias", "sgu_w_spatial", "sgu_b_spatial",
              "ln_final"]
NORM_SHARDS = ["mla_q_norm", "mla_kv_norm"]
WEIGHTS = ["ln_ffn1", "ffn1_w_in", "ffn1_w_out", "ln_mix", "ln_ffn2", "ffn2_w_in", "ffn2_w_out", "sgu_w_in",
           "sgu_v_gain", "sgu_v_bias", "sgu_w_spatial", "sgu_b_spatial", "sgu_w_out", "mla_w_in", "mla_q_norm",
           "mla_w_q_up", "mla_kv_norm", "mla_w_kv_up", "mla_w_out", "ln_final"]


def kernel(x, positions, ln_ffn1, ffn1_w_in, ffn1_w_out, ln_mix, ln_ffn2, ffn2_w_in, ffn2_w_out, sgu_w_in, sgu_v_gain, sgu_v_bias, sgu_w_spatial, sgu_b_spatial, sgu_w_out, mla_w_in, mla_q_norm, mla_w_q_up, mla_kv_norm, mla_w_kv_up, mla_w_out, ln_final, loss_target, m_ln_ffn1, m_ffn1_w_in, m_ffn1_w_out, m_ln_mix, m_ln_ffn2, m_ffn2_w_in, m_ffn2_w_out, m_sgu_w_in, m_sgu_v_gain, m_sgu_v_bias, m_sgu_w_spatial, m_sgu_b_spatial, m_sgu_w_out, m_mla_w_in, m_mla_q_norm, m_mla_w_q_up, m_mla_kv_norm, m_mla_w_kv_up, m_mla_w_out, m_ln_final, v_ln_ffn1, v_ffn1_w_in, v_ffn1_w_out, v_ln_mix, v_ln_ffn2, v_ffn2_w_in, v_ffn2_w_out, v_sgu_w_in, v_sgu_v_gain, v_sgu_v_bias, v_sgu_w_spatial, v_sgu_b_spatial, v_sgu_w_out, v_mla_w_in, v_mla_q_norm, v_mla_w_q_up, v_mla_kv_norm, v_mla_w_kv_up, v_mla_w_out, v_ln_final):
    given = dict(locals())
    w = {n: given[n] for n in WEIGHTS}
    mom = {n: given["m_" + n] for n in WEIGHTS}
    var = {n: given["v_" + n] for n in WEIGHTS}
    t, d = x.shape[1], x.shape[2]
    xs = x.reshape(t, d)
    target = loss_target.reshape(t, d)
    me = 2 * lax.axis_index("x") + lax.axis_index("y")

    me_idx = jnp.reshape(me, (1,)).astype(jnp.int32)
    core_idx = jnp.reshape(lax.axis_index("c"), (1,)).astype(jnp.int32)

    bufs = {(n, l): _cast_into(f"cast_{n}_{l}", me_idx, w[n], l) for n in BIG for l in range(w[n].shape[0])}
    nq = mla_q_norm.shape[1]
    norms = jnp.pad(jnp.concatenate([mla_q_norm, mla_kv_norm], axis=0), ((0, 6), (0, LANE - nq)))

    def take(stage):
        return [bufs[u] for u in stage]

    def put(stage, arrays):
        bufs.update(zip(stage, arrays))

    def full(unit):
        a = bufs[unit]
        return a.reshape(N_CHIPS, a.shape[1] * a.shape[2], a.shape[3])

    def rows(unit):
        a = bufs[unit]
        return a.reshape(-1, a.shape[-1])

    first, norms_g = _gather_weights(take(STAGES[0]), norms)
    put(STAGES[0], first)
    gq = norms_g[:, 0, :nq].reshape(1, N_CHIPS * nq)
    gkv = norms_g[:, 1, :nq].reshape(1, N_CHIPS * nq)
    tables = _rope_tables(positions.reshape(t))
    sgu_small = (sgu_v_gain, sgu_v_bias, sgu_w_spatial[0], sgu_b_spatial[0])

    a0, s_f1_0, got = _ffn_fwd("l0_ffn1", xs, ln_ffn1[0], full(("ffn1_w_in", 0)), rows(("ffn1_w_out", 0)),
                               take(STAGES[1]))
    put(STAGES[1], got)
    a1, s_sgu, got = _sgu_fwd(a0, ln_mix[0], full(("sgu_w_in", 0)), *sgu_small, rows(("sgu_w_out", 0)),
                              take(STAGES[2]))
    put(STAGES[2], got)
    a2, s_f2_0, got = _ffn_fwd("l0_ffn2", a1, ln_ffn2[0], full(("ffn2_w_in", 0)), rows(("ffn2_w_out", 0)),
                               take(STAGES[3]))
    put(STAGES[3], got)
    a3, s_f1_1, got = _ffn_fwd("l1_ffn1", a2, ln_ffn1[1], full(("ffn1_w_in", 1)), rows(("ffn1_w_out", 1)),
                               take(STAGES[4] + STAGES[5]))
    put(STAGES[4] + STAGES[5], got)
    mla_wts = _mla_weights(full(("mla_w_in", 0)), full(("mla_w_q_up", 0)), full(("mla_w_kv_up", 0)),
                           full(("mla_w_out", 0)))
    a4, s_mla = _mla_fwd(a3, ln_mix[1], mla_wts, gq, gkv, tables)
    a5, s_f2_1, _ = _ffn_fwd("l1_ffn2", a4, ln_ffn2[1], full(("ffn2_w_in", 1)), rows(("ffn2_w_out", 1)), [])

    loss_part, dx, dxb, dg_final = _loss_bwd(a5, ln_final, target)
    gr = {}
    dx, dxb, dg_f2_1, gr["ffn2_w_in", 1], gr["ffn2_w_out", 1] = _ffn_bwd(
        "l1_ffn2", s_f2_1, ln_ffn2[1], full(("ffn2_w_in", 1)), rows(("ffn2_w_out", 1)), dx, dxb)
    (dx, dxb, sm_mla, gr["mla_w_in", 0], gr["mla_w_q_up", 0], gr["mla_w_kv_up", 0],
     gr["mla_w_out", 0]) = _mla_bwd(s_mla, ln_mix[1], mla_wts, gq, gkv, tables, dx, dxb, mla_w_in.shape[2])
    dx, dxb, dg_f1_1, gr["ffn1_w_in", 1], gr["ffn1_w_out", 1] = _ffn_bwd(
        "l1_ffn1", s_f1_1, ln_ffn1[1], full(("ffn1_w_in", 1)), rows(("ffn1_w_out", 1)), dx, dxb)
    dx, dxb, dg_f2_0, gr["ffn2_w_in", 0], gr["ffn2_w_out", 0] = _ffn_bwd(
        "l0_ffn2", s_f2_0, ln_ffn2[0], full(("ffn2_w_in", 0)), rows(("ffn2_w_out", 0)), dx, dxb)
    dx, dxb, sm_sgu, gr["sgu_w_in", 0], gr["sgu_w_out", 0] = _sgu_bwd(
        s_sgu, ln_mix[0], full(("sgu_w_in", 0)), *sgu_small, rows(("sgu_w_out", 0)), dx, dxb)
    dx, dxb, dg_f1_0, gr["ffn1_w_in", 0], gr["ffn1_w_out", 0] = _ffn_bwd(
        "l0_ffn1", s_f1_0, ln_ffn1[0], full(("ffn1_w_in", 0)), rows(("ffn1_w_out", 0)), dx, dxb)

    units = [u for stage in STAGES for u in stage]
    local = [gr[u].reshape(N_CHIPS, 2, gr[u].shape[1] // 2, gr[u].shape[2]) for u in units]
    swapped = _swap_halves(local)
    parts = [_add_halves(core_idx, g, o) for g, o in zip(local, swapped)]
    landed = _scatter_chips(parts)
    both_idx = jnp.concatenate([me_idx, core_idx])
    reduced = {}
    for (n, l), p, ld in zip(units, parts, landed):
        reduced[n] = _sum_chips(both_idx, p, ld, l, w[n].shape[0], reduced.get(n))
    joined = _join_halves([reduced[n] for n in BIG])
    big_grad = {n: j.reshape(w[n].shape) for n, j in zip(BIG, joined)}

    small_parts = [
        jnp.concatenate([dg_f1_0, dg_f1_1], axis=0), jnp.concatenate([sm_sgu["ln"], sm_mla["ln"]], axis=0),
        jnp.concatenate([dg_f2_0, dg_f2_1], axis=0), sm_sgu["gain"], sm_sgu["bias"], sm_sgu["w_sp"],
        sm_sgu["b_sp"], dg_final]
    rep_shapes = [w[n].shape for n in REPLICATED]
    gq_row = jnp.pad(sm_mla["gq"], ((0, 0), (0, N_CHIPS * (LANE - nq))))
    gkv_row = jnp.pad(sm_mla["gkv"], ((0, 0), (0, N_CHIPS * (LANE - nq))))
    packed = _pack(small_parts + [gq_row, gkv_row, loss_part])
    packed = jnp.pad(packed, ((0, -packed.shape[0] % 8), (0, 0)))
    everyone = _gather_all(packed)
    rows = packed.shape[0]
    zero_rows = jnp.zeros((rows - sum(math.prod(s) // LANE for s in rep_shapes), LANE), F32)
    pw = jnp.concatenate([_pack([w[n] for n in REPLICATED]), zero_rows], axis=0)
    pm = jnp.concatenate([_pack([mom[n] for n in REPLICATED]), zero_rows], axis=0)
    pv = jnp.concatenate([_pack([var[n] for n in REPLICATED]), zero_rows + 1.0], axis=0)
    g_all, d_all, m_all, v_all = _adamw_summed(pw, everyone, pm, pv)
    tail_shapes = [(1, N_CHIPS * LANE), (1, N_CHIPS * LANE), (1, LANE)]
    rep_grad = dict(zip(REPLICATED, _unpack(g_all, rep_shapes + tail_shapes)[:len(REPLICATED)]))
    rep_delta = dict(zip(REPLICATED, _unpack(d_all, rep_shapes)))
    rep_m = dict(zip(REPLICATED, _unpack(m_all, rep_shapes)))
    rep_v = dict(zip(REPLICATED, _unpack(v_all, rep_shapes)))
    tail = _unpack(g_all, rep_shapes + tail_shapes)[len(REPLICATED):]
    loss = tail[2][0, 0]
    norm_grad = {
        "mla_q_norm": lax.dynamic_slice(tail[0], (0, me * nq), (1, nq)),
        "mla_kv_norm": lax.dynamic_slice(tail[1], (0, me * nq), (1, nq)),
    }

    grad, delta, new_m, new_v = {}, {}, {}, {}
    for n in BIG:
        shp = w[n].shape
        flat = lambda a: a.reshape(-1, shp[-1])
        dl, nm, nv = _adamw("adamw_" + n, flat(w[n]), flat(big_grad[n]), flat(mom[n]), flat(var[n]))
        grad[n], delta[n], new_m[n], new_v[n] = big_grad[n], dl.reshape(shp), nm.reshape(shp), nv.reshape(shp)
    for n in REPLICATED:
        grad[n], delta[n], new_m[n], new_v[n] = rep_grad[n], rep_delta[n], rep_m[n], rep_v[n]
    stack = lambda dct: jnp.concatenate([dct[n] for n in NORM_SHARDS], axis=0)
    dl, nm, nv = _adamw("adamw_norm_shards", stack(w), stack(norm_grad), stack(mom), stack(var))
    for i, n in enumerate(NORM_SHARDS):
        grad[n], delta[n], new_m[n], new_v[n] = norm_grad[n], dl[i:i + 1], nm[i:i + 1], nv[i:i + 1]

    grad_x = dx.reshape(x.shape)
    return (loss, grad_x, *[grad[n] for n in WEIGHTS], *[delta[n] for n in WEIGHTS],
            *[new_m[n] for n in WEIGHTS], *[new_v[n] for n in WEIGHTS])
```

```python
import functools
import math

import jax
import jax.numpy as jnp
from jax import lax
from jax.experimental import pallas as pl
from jax.experimental.pallas import tpu as pltpu

F32 = jnp.float32
BF16 = jnp.bfloat16
MESH = pl.DeviceIdType.MESH

EPS = 1e-6
CHUNK = 64
SGU_BLOCK = 128
SGU_GROUPS = 8
QK_NOPE = 128
QK_ROPE = 64
V_DIM = 128
QK_DIM = QK_NOPE + QK_ROPE
HEAD_PAD = 256
ROPE_THETA = 10000.0
N_CHIPS = 4
N_DEV = 8

ADAM_LR = 0.001
ADAM_B1 = 0.9
ADAM_B2 = 0.999
ADAM_EPS = 1e-08
ADAM_WD = 0.01
ADAM_STEP = 10

LANE = 128
VMEM_LIMIT_BYTES = 56 * 1024 * 1024

_DIMS = {
    "nn": (((1,), (0,)), ((), ())),
    "nt": (((1,), (1,)), ((), ())),
    "tn": (((0,), (0,)), ((), ())),
}


def _tile(n, pref):
    t = (min(pref, n) // LANE) * LANE
    while t >= LANE:
        if n % t == 0:
            return t
        t -= LANE
    return n


def _params(sem):
    return pltpu.CompilerParams(dimension_semantics=sem, vmem_limit_bytes=VMEM_LIMIT_BYTES)


def _dot(a, b, mode):
    return lax.dot_general(a, b, _DIMS[mode], preferred_element_type=F32)


def _place():
    x, y, c = lax.axis_index("x"), lax.axis_index("y"), lax.axis_index("c")
    chips = [(1 - x, y), (x, 1 - y), (1 - x, 1 - y)]
    return x, y, c, chips


ANY = pl.BlockSpec(memory_space=pl.ANY)


def _gather_copies(phase, bufs, send, recv, landing):
    x, y, c, chips = _place()
    copies = []
    for t, buf in enumerate(bufs):
        for j, chip in enumerate(chips):
            there = 2 * chip[0] + chip[1]
            if phase == 1:
                src, lands, to = buf.at[2 * x + y, c], buf.at[there, c], (*chip, c)
            else:
                src, lands, to = buf.at[there, c], buf.at[there, 1 - c], (x, y, 1 - c)
            ref = lands if landing else src
            copies.append(pltpu.make_async_remote_copy(
                src_ref=ref, dst_ref=ref, send_sem=send.at[t, j], recv_sem=recv.at[t, j], device_id=to,
                device_id_type=MESH))
    return copies


class _Phase:
    def __init__(self, ins, fresh, sems, copies):
        self.ins, self.fresh, self.sems, self.copies = list(ins), list(fresh), sems, copies


def _gather_phase(phase, bufs):
    return _Phase(bufs, [], (len(bufs), 3),
                  lambda ins, outs, send, recv, landing: _gather_copies(phase, outs, send, recv, landing))


def _comm_io(comm):
    if comm is None:
        return [], [], [], [], [], False
    shapes = comm.fresh or [jax.ShapeDtypeStruct(b.shape, b.dtype) for b in comm.ins]
    sems = [pltpu.SemaphoreType.DMA(comm.sems), pltpu.SemaphoreType.DMA(comm.sems)]
    return comm.ins, [ANY] * len(comm.ins), [ANY] * len(shapes), shapes, sems, not comm.fresh


def _comm_run(comm, in_refs, out_refs, send, recv, first, last):
    @pl.when(first)
    def _():
        for cp in comm.copies(in_refs, out_refs, send, recv, False):
            cp.start()

    def finish():
        for cp in comm.copies(in_refs, out_refs, send, recv, True):
            cp.wait_recv()
        for cp in comm.copies(in_refs, out_refs, send, recv, False):
            cp.wait_send()

    return last, finish


def _matmul(name, mode, grid, a, a_spec, b, b_spec, extras, out_shapes, out_specs, acc_shape, epilogue, comm=None):
    nk = grid[2]
    n_ex = len(extras)
    n_out = len(out_shapes)
    c_in, c_in_specs, c_out_specs, c_shapes, c_sems, in_place = _comm_io(comm)
    n_c, n_co = len(c_in), len(c_shapes)

    def body(*refs):
        a_ref, b_ref = refs[0], refs[1]
        ex = refs[2:2 + n_ex]
        outs = refs[2 + n_ex + n_c:2 + n_ex + n_c + n_out]
        ids = (pl.program_id(0), pl.program_id(1))
        k = pl.program_id(2)
        if comm is not None:
            c_ins = refs[2 + n_ex:2 + n_ex + n_c]
            c_outs = refs[2 + n_ex + n_c + n_out:2 + n_ex + n_c + n_out + n_co]
            send, recv = refs[2 + n_ex + n_c + n_out + n_co:2 + n_ex + n_c + n_out + n_co + 2]
            first = jnp.logical_and(jnp.logical_and(ids[0] == 0, ids[1] == 0), k == 0)
            last = jnp.logical_and(jnp.logical_and(ids[0] == grid[0] - 1, ids[1] == grid[1] - 1), k == nk - 1)
            last, finish = _comm_run(comm, c_ins, c_outs, send, recv, first, last)
        part = _dot(a_ref[...], b_ref[...], mode)
        if nk == 1:
            epilogue(part, ex, outs, ids)
        else:
            acc = refs[-1]

            @pl.when(k == 0)
            def _():
                acc[...] = part

            @pl.when(k > 0)
            def _():
                acc[...] += part

            @pl.when(k == nk - 1)
            def _():
                epilogue(acc[...], ex, outs, ids)

        if comm is not None:
            pl.when(last)(finish)

    scratch = c_sems + ([pltpu.VMEM(acc_shape, F32)] if nk > 1 else [])
    sem = ("parallel", "parallel", "arbitrary") if comm is None else ("arbitrary",) * 3
    return pl.pallas_call(
        body,
        name=name,
        grid=grid,
        in_specs=[a_spec, b_spec] + [s for _, s in extras] + c_in_specs,
        out_specs=list(out_specs) + c_out_specs,
        out_shape=list(out_shapes) + c_shapes,
        input_output_aliases={2 + n_ex + t: n_out + t for t in range(n_c)} if in_place else {},
        scratch_shapes=scratch,
        compiler_params=_params(sem),
    )(a, b, *[e for e, _ in extras], *c_in)


def _store(scale, dtype):
    def epilogue(acc, ex, outs, ids):
        v = acc if scale == 1.0 else acc * scale
        outs[0][...] = v.astype(dtype)

    return epilogue


def _mm_nn_full(name, a, b, out_dtype, tm_pref=1024, tn_pref=512):
    m, kd = a.shape
    n = b.shape[1]
    tm, tn = _tile(m, tm_pref), _tile(n, tn_pref)
    return _matmul(
        name, "nn", (m // tm, n // tn, 1),
        a, pl.BlockSpec((tm, kd), lambda i, j, k: (i, 0)),
        b, pl.BlockSpec((kd, tn), lambda i, j, k: (0, j)),
        [], [jax.ShapeDtypeStruct((m, n), out_dtype)], [pl.BlockSpec((tm, tn), lambda i, j, k: (i, j))],
        None, _store(1.0, out_dtype))[0]


def _mm_nt_full(name, a, b, out_dtype, scale=1.0, tm_pref=1024, tn_pref=512):
    m, kd = a.shape
    n = b.shape[0]
    tm, tn = _tile(m, tm_pref), _tile(n, tn_pref)
    return _matmul(
        name, "nt", (m // tm, n // tn, 1),
        a, pl.BlockSpec((tm, kd), lambda i, j, k: (i, 0)),
        b, pl.BlockSpec((tn, kd), lambda i, j, k: (j, 0)),
        [], [jax.ShapeDtypeStruct((m, n), out_dtype)], [pl.BlockSpec((tm, tn), lambda i, j, k: (i, j))],
        None, _store(scale, out_dtype))[0]


def _mm_nt_k(name, a, b, out_dtype, tk_pref=1024, tm_pref=1024, tn_pref=1024):
    m, kd = a.shape
    n = b.shape[0]
    tm, tn, tk = _tile(m, tm_pref), _tile(n, tn_pref), _tile(kd, tk_pref)
    return _matmul(
        name, "nt", (m // tm, n // tn, kd // tk),
        a, pl.BlockSpec((tm, tk), lambda i, j, k: (i, k)),
        b, pl.BlockSpec((tn, tk), lambda i, j, k: (j, k)),
        [], [jax.ShapeDtypeStruct((m, n), out_dtype)], [pl.BlockSpec((tm, tn), lambda i, j, k: (i, j))],
        (tm, tn), _store(1.0, out_dtype))[0]


def _mm_tn(name, a, b, scale=1.0, tm_pref=1024, tn_pref=1024, tk_pref=512, comm=None):
    t, m = a.shape
    n = b.shape[1]
    tm, tn, tk = _tile(m, tm_pref), _tile(n, tn_pref), _tile(t, tk_pref)
    res = _matmul(
        name, "tn", (m // tm, n // tn, t // tk),
        a, pl.BlockSpec((tk, tm), lambda i, j, k: (k, i)),
        b, pl.BlockSpec((tk, tn), lambda i, j, k: (k, j)),
        [], [jax.ShapeDtypeStruct((m, n), BF16)], [pl.BlockSpec((tm, tn), lambda i, j, k: (i, j))],
        (tm, tn), _store(scale, BF16), comm=comm)
    return res[0] if comm is None else (res[0], res[1:])


def _mm_residual(name, a, b, x, scale, tm_pref=1024, tn_pref=1024, tk_pref=1408, comm=None):
    m, kd = a.shape
    n = b.shape[1]
    tm, tn, tk = _tile(m, tm_pref), _tile(n, tn_pref), _tile(kd, tk_pref)

    def epilogue(acc, ex, outs, ids):
        outs[0][...] = ex[0][...] + scale * acc

    res = _matmul(
        name, "nn", (m // tm, n // tn, kd // tk),
        a, pl.BlockSpec((tm, tk), lambda i, j, k: (i, k)),
        b, pl.BlockSpec((tk, tn), lambda i, j, k: (k, j)),
        [(x, pl.BlockSpec((tm, tn), lambda i, j, k: (i, j)))],
        [jax.ShapeDtypeStruct((m, n), F32)], [pl.BlockSpec((tm, tn), lambda i, j, k: (i, j))],
        (tm, tn), epilogue, comm=comm)
    return res[0] if comm is None else (res[0], res[1:])


def _rms_fwd(name, x, g, with_transpose=False):
    t, d = x.shape
    tm = _tile(t, 512)

    def body(x_ref, g_ref, h_ref, *ht_ref):
        xv = x_ref[...]
        r = lax.rsqrt(jnp.mean(xv * xv, axis=-1, keepdims=True) + EPS)
        h = xv * r * g_ref[...]
        h_ref[...] = h.astype(BF16)
        if with_transpose:
            ht_ref[0][...] = h.T.astype(BF16)

    out_specs = [pl.BlockSpec((tm, d), lambda i: (i, 0))]
    out_shape = [jax.ShapeDtypeStruct((t, d), BF16)]
    if with_transpose:
        out_specs.append(pl.BlockSpec((d, tm), lambda i: (0, i)))
        out_shape.append(jax.ShapeDtypeStruct((d, t), BF16))
    res = pl.pallas_call(
        body, name=name, grid=(t // tm,),
        in_specs=[pl.BlockSpec((tm, d), lambda i: (i, 0)), pl.BlockSpec((1, d), lambda i: (0, 0))],
        out_specs=out_specs, out_shape=out_shape,
        compiler_params=_params(("parallel",)),
    )(x, g.reshape(1, d))
    return res if with_transpose else res[0]


def _rms_bwd_math(dh, xv, g):
    r = lax.rsqrt(jnp.mean(xv * xv, axis=-1, keepdims=True) + EPS)
    xhat = xv * r
    dxh = dh * g
    dx = r * (dxh - xhat * jnp.mean(dxh * xhat, axis=-1, keepdims=True))
    return dx, dh * xhat


def _rms_bwd(name, dh, x, g, dres):
    t, d = x.shape
    tm = _tile(t, 256)

    def body(dh_ref, x_ref, g_ref, dres_ref, dx_ref, dxb_ref, dg_ref):
        dx, dgt = _rms_bwd_math(dh_ref[...].astype(F32), x_ref[...], g_ref[...])
        dx = dres_ref[...] + dx
        dx_ref[...] = dx
        dxb_ref[...] = dx.astype(BF16)

        @pl.when(pl.program_id(0) == 0)
        def _():
            dg_ref[...] = jnp.zeros_like(dg_ref)

        dg_ref[...] += jnp.sum(dgt, axis=0, keepdims=True)

    row = pl.BlockSpec((tm, d), lambda i: (i, 0))
    vec = pl.BlockSpec((1, d), lambda i: (0, 0))
    return pl.pallas_call(
        body, name=name, grid=(t // tm,),
        in_specs=[row, row, vec, row],
        out_specs=[row, row, vec],
        out_shape=[jax.ShapeDtypeStruct((t, d), F32), jax.ShapeDtypeStruct((t, d), BF16),
                   jax.ShapeDtypeStruct((1, d), F32)],
        compiler_params=_params(("arbitrary",)),
    )(dh, x, g.reshape(1, d), dres)


def _loss_bwd(x, g, target):
    t, d = x.shape
    tm = _tile(t, 256)

    def body(x_ref, g_ref, tgt_ref, loss_ref, dx_ref, dxb_ref, dg_ref):
        xv = x_ref[...]
        gv = g_ref[...]
        r = lax.rsqrt(jnp.mean(xv * xv, axis=-1, keepdims=True) + EPS)
        err = xv * r * gv - tgt_ref[...]
        part = 0.5 * jnp.sum(jnp.mean(err * err, axis=-1, keepdims=True), axis=0, keepdims=True)
        dx, dgt = _rms_bwd_math(err * (1.0 / d), xv, gv)
        dx_ref[...] = dx
        dxb_ref[...] = dx.astype(BF16)

        @pl.when(pl.program_id(0) == 0)
        def _():
            dg_ref[...] = jnp.zeros_like(dg_ref)
            loss_ref[...] = jnp.zeros_like(loss_ref)

        dg_ref[...] += jnp.sum(dgt, axis=0, keepdims=True)
        loss_ref[...] += jnp.broadcast_to(part, loss_ref.shape)

    row = pl.BlockSpec((tm, d), lambda i: (i, 0))
    vec = pl.BlockSpec((1, d), lambda i: (0, 0))
    return pl.pallas_call(
        body, name="loss_bwd", grid=(t // tm,),
        in_specs=[row, vec, row],
        out_specs=[pl.BlockSpec((1, LANE), lambda i: (0, 0)), row, row, vec],
        out_shape=[jax.ShapeDtypeStruct((1, LANE), F32), jax.ShapeDtypeStruct((t, d), F32),
                   jax.ShapeDtypeStruct((t, d), BF16), jax.ShapeDtypeStruct((1, d), F32)],
        compiler_params=_params(("arbitrary",)),
    )(x, g.reshape(1, d), target)


def _sigmoid(x):
    return 1.0 / (1.0 + jnp.exp(-x))


def _ffn_up(name, h, w_in, comm=None):
    t, d = h.shape
    fs = w_in.shape[2]
    f = 2 * fs
    tm, tn = _tile(t, 1024), _tile(fs, 256)
    per = fs // tn
    grid = (t // tm, f // tn)
    c_in, c_in_specs, c_out_specs, c_shapes, c_sems, in_place = _comm_io(comm)
    n_c, n_co = len(c_in), len(c_shapes)

    def body(*refs):
        h_ref, wg_ref, wu_ref = refs[:3]
        gu_ref, z_ref = refs[3 + n_c:5 + n_c]
        if comm is not None:
            i, j = pl.program_id(0), pl.program_id(1)
            send, recv = refs[5 + n_c + n_co:]
            last, finish = _comm_run(comm, refs[3:3 + n_c], refs[5 + n_c:5 + n_c + n_co], send, recv,
                                     jnp.logical_and(i == 0, j == 0),
                                     jnp.logical_and(i == grid[0] - 1, j == grid[1] - 1))
        hv = h_ref[...]
        gate = _dot(hv, wg_ref[...], "nn")
        up = _dot(hv, wu_ref[...], "nn")
        gu_ref[0] = gate.astype(BF16)
        gu_ref[1] = up.astype(BF16)
        z_ref[...] = (gate * _sigmoid(gate) * up).astype(BF16)
        if comm is not None:
            pl.when(last)(finish)

    res = pl.pallas_call(
        body, name=name, grid=grid,
        in_specs=[pl.BlockSpec((tm, d), lambda i, j: (i, 0)),
                  pl.BlockSpec((None, d, tn), lambda i, j: (j // per, 0, j % per)),
                  pl.BlockSpec((None, d, tn), lambda i, j: (2 + j // per, 0, j % per))] + c_in_specs,
        out_specs=[pl.BlockSpec((2, tm, tn), lambda i, j: (0, i, j)),
                   pl.BlockSpec((tm, tn), lambda i, j: (i, j))] + c_out_specs,
        out_shape=[jax.ShapeDtypeStruct((2, t, f), BF16), jax.ShapeDtypeStruct((t, f), BF16)] + c_shapes,
        input_output_aliases={3 + k: 2 + k for k in range(n_c)} if in_place else {},
        scratch_shapes=c_sems,
        compiler_params=_params(("parallel", "parallel") if comm is None else ("arbitrary", "arbitrary")),
    )(h, w_in, w_in, *c_in)
    return (res[0], res[1]) if comm is None else (res[0], res[1], res[2:])


def _ffn_dact(name, dxb, w_out, gu):
    t, d = dxb.shape
    f = w_out.shape[0]
    tm, tn = _tile(t, 1024), _tile(f, 256)

    def epilogue(acc, ex, outs, ids):
        dz = 0.5 * acc
        gate = ex[0][0].astype(F32)
        up = ex[0][1].astype(F32)
        sg = _sigmoid(gate)
        outs[0][0] = (dz * up * (sg * (1.0 + gate * (1.0 - sg)))).astype(BF16)
        outs[0][1] = (dz * gate * sg).astype(BF16)

    blk = pl.BlockSpec((2, tm, tn), lambda i, j, k: (0, i, j))
    return _matmul(
        name, "nt", (t // tm, f // tn, 1),
        dxb, pl.BlockSpec((tm, d), lambda i, j, k: (i, 0)),
        w_out, pl.BlockSpec((tn, d), lambda i, j, k: (j, 0)),
        [(gu, blk)], [jax.ShapeDtypeStruct((2, t, f), BF16)], [blk], None, epilogue)[0]


def _grad_colsharded(name, ht, da, comm=None):
    d, t = ht.shape
    w = da.shape[2]
    ws = w // 2
    tm, tn, tk = _tile(d, 1024), _tile(ws, 1408), _tile(t, 512)
    per = ws // tn
    res = _matmul(
        name, "nn", (d // tm, (2 * w) // tn, t // tk),
        ht, pl.BlockSpec((tm, tk), lambda i, j, k: (i, k)),
        da, pl.BlockSpec((None, tk, tn), lambda i, j, k: (j // (2 * per), k, j % (2 * per))),
        [], [jax.ShapeDtypeStruct((N_CHIPS, d, ws), BF16)],
        [pl.BlockSpec((None, tm, tn), lambda i, j, k: (j // per, i, j % per))],
        (tm, tn), _store(1.0, BF16), comm=comm)
    return res[0] if comm is None else (res[0], res[1:])


def _back_colsharded(name, da, w_g, comm=None):
    _, t, w = da.shape
    d, ws = w_g.shape[1], w_g.shape[2]
    tm, tn, tk = _tile(t, 1024), _tile(d, 1024), _tile(ws, 1408)
    per = ws // tk
    res = _matmul(
        name, "nt", (t // tm, d // tn, (2 * w) // tk),
        da, pl.BlockSpec((None, tm, tk), lambda i, j, k: (k // (2 * per), i, k % (2 * per))),
        w_g, pl.BlockSpec((None, tn, tk), lambda i, j, k: (k // per, j, k % per)),
        [], [jax.ShapeDtypeStruct((t, d), F32)], [pl.BlockSpec((tm, tn), lambda i, j, k: (i, j))],
        (tm, tn), _store(1.0, F32), comm=comm)
    return res[0] if comm is None else (res[0], res[1:])


def _ffn_fwd(tag, x, g, w_in, w_out, prefetch):
    h, ht = _rms_fwd(tag + "_norm", x, g, with_transpose=True)
    if prefetch:
        gu, z, prefetch = _ffn_up(tag + "_up", h, w_in, comm=_gather_phase(1, prefetch))
        y, prefetch = _mm_residual(tag + "_down", z, w_out, x, 0.5, comm=_gather_phase(2, prefetch))
    else:
        gu, z = _ffn_up(tag + "_up", h, w_in)
        y = _mm_residual(tag + "_down", z, w_out, x, 0.5)
    return y, (x, ht, gu, z), prefetch


def _ffn_bwd(tag, saved, g, w_in, w_out, dx, dxb, red):
    x, ht, gu, z = saved
    f = z.shape[1]
    d_w_out = red.behind(0, _mm_tn, tag + "_dwout", z, dxb, scale=0.5, tm_pref=1408, tn_pref=2048)
    da = _ffn_dact(tag + "_dact", dxb, w_out, gu)
    d_w_in = red.behind(1, _grad_colsharded, tag + "_dwin", ht, da)
    dh = red.behind(2, _back_colsharded, tag + "_dh", da, w_in)
    dx, dxb, dg = _rms_bwd(tag + "_dnorm", dh, x, g, dx)
    return dx, dxb, dg, d_w_in, d_w_out.reshape(N_CHIPS, f // N_CHIPS, -1)


_GELU_K = math.sqrt(2.0 / math.pi)
_GELU_C = 0.044715


def _gelu(x):
    t = jnp.tanh(_GELU_K * (x + _GELU_C * x * x * x))
    return 0.5 * x * (1.0 + t), t


def _dgelu(x, t):
    return 0.5 * (1.0 + t) + 0.5 * x * (1.0 - t * t) * (_GELU_K * (1.0 + 3.0 * _GELU_C * x * x))


def _causal_block_mask():
    r = lax.broadcasted_iota(jnp.int32, (SGU_BLOCK, SGU_BLOCK), 0) // CHUNK
    c = lax.broadcasted_iota(jnp.int32, (SGU_BLOCK, SGU_BLOCK), 1) // CHUNK
    return r >= c


def _sgu_pre(name, h, w_in, comm=None):
    t, d = h.shape
    ws = w_in.shape[2]
    w = 2 * ws
    tm, tn = _tile(t, 1024), _tile(ws, 512)
    per = ws // tn
    res = _matmul(
        name, "nn", (t // tm, (2 * w) // tn, 1),
        h, pl.BlockSpec((tm, d), lambda i, j, k: (i, 0)),
        w_in, pl.BlockSpec((None, d, tn), lambda i, j, k: (j // per, 0, j % per)),
        [], [jax.ShapeDtypeStruct((2, t, w), BF16)],
        [pl.BlockSpec((None, tm, tn), lambda i, j, k: (j // (2 * per), i, j % (2 * per)))],
        None, _store(1.0, BF16), comm=comm)
    return res[0] if comm is None else (res[0], res[1:])


def _layernorm_stats(v):
    mu = jnp.mean(v, axis=-1, keepdims=True)
    vc = v - mu
    rstd = lax.rsqrt(jnp.mean(vc * vc, axis=-1, keepdims=True) + EPS)
    return vc * rstd, rstd


def _sgu_mid_fwd(pre, gain, bias, w_sp, b_sp_t):
    _, t, w = pre.shape
    gd = w // SGU_GROUPS

    def body(pre_ref, gain_ref, bias_ref, ws_ref, bt_ref, out_ref):
        mask = _causal_block_mask()
        u, _ = _gelu(pre_ref[0].astype(F32))
        v, _ = _gelu(pre_ref[1].astype(F32))
        vhat, _ = _layernorm_stats(v)
        vln = (vhat * gain_ref[...] + bias_ref[...]).astype(BF16)
        for gi in range(SGU_GROUPS):
            cols = slice(gi * gd, (gi + 1) * gd)
            wg = jnp.where(mask, ws_ref[gi], 0.0).astype(BF16)
            mixed = _dot(wg, vln[:, cols], "nn") + bt_ref[:, gi:gi + 1]
            out_ref[:, cols] = (u[:, cols] * mixed).astype(BF16)

    return pl.pallas_call(
        body, name="sgu_mid_fwd", grid=(t // SGU_BLOCK,),
        in_specs=[pl.BlockSpec((2, SGU_BLOCK, w), lambda n: (0, n, 0)),
                  pl.BlockSpec((1, w), lambda n: (0, 0)), pl.BlockSpec((1, w), lambda n: (0, 0)),
                  pl.BlockSpec((SGU_GROUPS, SGU_BLOCK, SGU_BLOCK), lambda n: (0, 0, 0)),
                  pl.BlockSpec((SGU_BLOCK, SGU_GROUPS), lambda n: (0, 0))],
        out_specs=pl.BlockSpec((SGU_BLOCK, w), lambda n: (n, 0)),
        out_shape=jax.ShapeDtypeStruct((t, w), BF16),
        compiler_params=_params(("parallel",)),
    )(pre, gain, bias, w_sp, b_sp_t)


def _sgu_mid_bwd(pre, dgated, gain, bias, w_sp, b_sp_t):
    _, t, w = pre.shape
    gd = w // SGU_GROUPS

    def body(pre_ref, dg_ref, gain_ref, bias_ref, ws_ref, bt_ref,
             dpre_ref, dgain_ref, dbias_ref, dws_ref, dbt_ref, dvln_s):
        @pl.when(pl.program_id(0) == 0)
        def _():
            dgain_ref[...] = jnp.zeros_like(dgain_ref)
            dbias_ref[...] = jnp.zeros_like(dbias_ref)
            dws_ref[...] = jnp.zeros_like(dws_ref)
            dbt_ref[...] = jnp.zeros_like(dbt_ref)

        mask = _causal_block_mask()
        pu = pre_ref[0].astype(F32)
        pv = pre_ref[1].astype(F32)
        u, tu = _gelu(pu)
        v, tv = _gelu(pv)
        vhat, rstd = _layernorm_stats(v)
        gain_v = gain_ref[...]
        vln = (vhat * gain_v + bias_ref[...]).astype(BF16)
        dgt = dg_ref[...].astype(F32)
        for gi in range(SGU_GROUPS):
            cols = slice(gi * gd, (gi + 1) * gd)
            wg = jnp.where(mask, ws_ref[gi], 0.0).astype(BF16)
            vg = vln[:, cols]
            mixed = _dot(wg, vg, "nn") + bt_ref[:, gi:gi + 1]
            dgg = dgt[:, cols]
            dmixed = dgg * u[:, cols]
            dmb = dmixed.astype(BF16)
            dpre_ref[0, :, cols] = (dgg * mixed * _dgelu(pu[:, cols], tu[:, cols])).astype(BF16)
            dbt_ref[:, gi:gi + 1] += jnp.sum(dmixed, axis=1, keepdims=True)
            dws_ref[gi] += jnp.where(mask, _dot(dmb, vg, "nt"), 0.0)
            dvln_s[:, cols] = _dot(wg, dmb, "tn")
        dvln = dvln_s[...]
        dgain_ref[...] += jnp.sum(dvln * vhat, axis=0, keepdims=True)
        dbias_ref[...] += jnp.sum(dvln, axis=0, keepdims=True)
        dvh = dvln * gain_v
        dv = rstd * (dvh - jnp.mean(dvh, axis=-1, keepdims=True)
                     - vhat * jnp.mean(dvh * vhat, axis=-1, keepdims=True))
        dpre_ref[1] = (dv * _dgelu(pv, tv)).astype(BF16)

    vec = pl.BlockSpec((1, w), lambda n: (0, 0))
    wsb = pl.BlockSpec((SGU_GROUPS, SGU_BLOCK, SGU_BLOCK), lambda n: (0, 0, 0))
    btb = pl.BlockSpec((SGU_BLOCK, SGU_GROUPS), lambda n: (0, 0))
    blk2 = pl.BlockSpec((2, SGU_BLOCK, w), lambda n: (0, n, 0))
    return pl.pallas_call(
        body, name="sgu_mid_bwd", grid=(t // SGU_BLOCK,),
        in_specs=[blk2, pl.BlockSpec((SGU_BLOCK, w), lambda n: (n, 0)), vec, vec, wsb, btb],
        out_specs=[blk2, vec, vec, wsb, btb],
        out_shape=[jax.ShapeDtypeStruct((2, t, w), BF16), jax.ShapeDtypeStruct((1, w), F32),
                   jax.ShapeDtypeStruct((1, w), F32),
                   jax.ShapeDtypeStruct((SGU_GROUPS, SGU_BLOCK, SGU_BLOCK), F32),
                   jax.ShapeDtypeStruct((SGU_BLOCK, SGU_GROUPS), F32)],
        scratch_shapes=[pltpu.VMEM((SGU_BLOCK, w), F32)],
        compiler_params=_params(("arbitrary",)),
    )(pre, dgated, gain, bias, w_sp, b_sp_t)


def _sgu_fwd(x, g, w_in, gain, bias, w_sp, b_sp, w_out, prefetch):
    h, ht = _rms_fwd("sgu_norm", x, g, with_transpose=True)
    pre, prefetch = _sgu_pre("sgu_pre", h, w_in, comm=_gather_phase(1, prefetch))
    gated = _sgu_mid_fwd(pre, gain, bias, w_sp, b_sp.T)
    y, prefetch = _mm_residual("sgu_out", gated, w_out, x, 1.0, tk_pref=1024, comm=_gather_phase(2, prefetch))
    return y, (x, ht, pre, gated), prefetch


def _sgu_bwd(saved, g, w_in, gain, bias, w_sp, b_sp, w_out, dx, dxb, red):
    x, ht, pre, gated = saved
    w = gated.shape[1]
    d_w_out = red.behind(0, _mm_tn, "sgu_dwout", gated, dxb)
    dgated = _mm_nt_full("sgu_dgated", dxb, w_out, BF16)
    dpre, dgain, dbias, dws, dbt = _sgu_mid_bwd(pre, dgated, gain, bias, w_sp, b_sp.T)
    d_w_in = red.behind(1, _grad_colsharded, "sgu_dwin", ht, dpre)
    dh = red.behind(2, _back_colsharded, "sgu_dh", dpre, w_in)
    dx, dxb, dg = _rms_bwd("sgu_dnorm", dh, x, g, dx)
    small = dict(ln=dg, gain=dgain, bias=dbias, w_sp=dws, b_sp=dbt.T)
    return dx, dxb, small, d_w_in, d_w_out.reshape(N_CHIPS, w // N_CHIPS, -1)


def _rope_tables(positions):
    half = QK_ROPE // 2
    inv_freq = 1.0 / (ROPE_THETA ** (jnp.arange(half, dtype=F32) / half))
    ang = positions.astype(F32)[:, None] * inv_freq
    cos, sin = jnp.cos(ang), jnp.sin(ang)
    t = positions.shape[0]
    zeros = jnp.zeros((t, half), F32)
    rest = jnp.zeros((t, LANE - QK_ROPE), F32)
    c = jnp.concatenate([cos, cos, rest + 1.0], axis=1)
    s_up = jnp.concatenate([zeros, sin, rest], axis=1)
    s_dn = jnp.concatenate([-sin, zeros, rest], axis=1)
    return c, s_up, s_dn


def _rope_apply(x, c, s_up, s_dn):
    half = QK_ROPE // 2
    return x * c + pltpu.roll(x, half, 1) * s_up + pltpu.roll(x, LANE - half, 1) * s_dn


def _rope_apply_t(dy, c, s_up, s_dn):
    half = QK_ROPE // 2
    return dy * c - pltpu.roll(dy, LANE - half, 1) * s_dn - pltpu.roll(dy, half, 1) * s_up


def _mla_norm_fwd(proj, gq, gkv):
    t, p = proj.shape
    ql, kvl = gq.shape[1], gkv.shape[1]
    tm = _tile(t, 512)

    def body(p_ref, gq_ref, gkv_ref, qn_ref, kvn_ref):
        for lo, n, g_ref, o_ref in ((0, ql, gq_ref, qn_ref), (ql, kvl, gkv_ref, kvn_ref)):
            xv = p_ref[:, lo:lo + n]
            r = lax.rsqrt(jnp.mean(xv * xv, axis=-1, keepdims=True) + EPS)
            o_ref[...] = (xv * r * g_ref[...]).astype(BF16)

    return pl.pallas_call(
        body, name="mla_norm_fwd", grid=(t // tm,),
        in_specs=[pl.BlockSpec((tm, p), lambda i: (i, 0)), pl.BlockSpec((1, ql), lambda i: (0, 0)),
                  pl.BlockSpec((1, kvl), lambda i: (0, 0))],
        out_specs=[pl.BlockSpec((tm, ql), lambda i: (i, 0)), pl.BlockSpec((tm, kvl), lambda i: (i, 0))],
        out_shape=[jax.ShapeDtypeStruct((t, ql), BF16), jax.ShapeDtypeStruct((t, kvl), BF16)],
        compiler_params=_params(("parallel",)),
    )(proj, gq, gkv)


def _mla_norm_bwd(proj, dqn, dkvn, dkr, gq, gkv):
    t, p = proj.shape
    ql, kvl = gq.shape[1], gkv.shape[1]
    tm = _tile(t, 256)

    def body(p_ref, dqn_ref, dkvn_ref, dkr_ref, gq_ref, gkv_ref, dp_ref, dgq_ref, dgkv_ref):
        @pl.when(pl.program_id(0) == 0)
        def _():
            dgq_ref[...] = jnp.zeros_like(dgq_ref)
            dgkv_ref[...] = jnp.zeros_like(dgkv_ref)

        for lo, n, g_ref, d_ref, dg_ref in ((0, ql, gq_ref, dqn_ref, dgq_ref),
                                             (ql, kvl, gkv_ref, dkvn_ref, dgkv_ref)):
            dx, dgt = _rms_bwd_math(d_ref[...], p_ref[:, lo:lo + n], g_ref[...])
            dp_ref[:, lo:lo + n] = dx.astype(BF16)
            dg_ref[...] += jnp.sum(dgt, axis=0, keepdims=True)
        dp_ref[:, ql + kvl:] = dkr_ref[...].astype(BF16)

    def row(n):
        return pl.BlockSpec((tm, n), lambda i: (i, 0))

    def vec(n):
        return pl.BlockSpec((1, n), lambda i: (0, 0))

    return pl.pallas_call(
        body, name="mla_norm_bwd", grid=(t // tm,),
        in_specs=[row(p), row(ql), row(kvl), row(LANE), vec(ql), vec(kvl)],
        out_specs=[row(p), vec(ql), vec(kvl)],
        out_shape=[jax.ShapeDtypeStruct((t, p), BF16), jax.ShapeDtypeStruct((1, ql), F32),
                   jax.ShapeDtypeStruct((1, kvl), F32)],
        compiler_params=_params(("arbitrary",)),
    )(proj, dqn, dkvn, dkr, gq, gkv)


def _mla_q_up(qn, wq, tables):
    t, ql = qn.shape
    n = wq.shape[1]
    tm = _tile(t, 1024)
    scale = QK_DIM ** -0.5

    def epilogue(acc, ex, outs, ids):
        outs[0][:, :QK_NOPE] = (scale * acc[:, :QK_NOPE]).astype(BF16)
        hi = _rope_apply(acc[:, QK_NOPE:], ex[0][...], ex[1][...], ex[2][...])
        outs[0][:, QK_NOPE:] = (scale * hi).astype(BF16)

    tab = pl.BlockSpec((tm, LANE), lambda i, j, k: (i, 0))
    return _matmul(
        "mla_q_up", "nn", (t // tm, n // HEAD_PAD, 1),
        qn, pl.BlockSpec((tm, ql), lambda i, j, k: (i, 0)),
        wq, pl.BlockSpec((ql, HEAD_PAD), lambda i, j, k: (0, j)),
        [(tb, tab) for tb in tables],
        [jax.ShapeDtypeStruct((t, n), BF16)], [pl.BlockSpec((tm, HEAD_PAD), lambda i, j, k: (i, j))],
        None, epilogue)[0]


def _mla_kv_up(kvn, wkv, proj, tables, heads):
    t, kvl = kvn.shape
    n = wkv.shape[1]
    p = proj.shape[1]
    tm = _tile(t, 1024)

    def epilogue(acc, ex, outs, ids):
        kr = _rope_apply(ex[0][...], ex[1][...], ex[2][...], ex[3][...])
        outs[0][:, :QK_NOPE] = acc[:, :QK_NOPE].astype(BF16)
        outs[0][:, QK_NOPE:] = (acc[:, QK_NOPE:] + jnp.where(ids[1] < heads, kr, 1.0)).astype(BF16)

    tab = pl.BlockSpec((tm, LANE), lambda i, j, k: (i, 0))
    kr_spec = pl.BlockSpec((tm, LANE), lambda i, j, k: (i, p // LANE - 1))
    return _matmul(
        "mla_kv_up", "nn", (t // tm, n // HEAD_PAD, 1),
        kvn, pl.BlockSpec((tm, kvl), lambda i, j, k: (i, 0)),
        wkv, pl.BlockSpec((kvl, HEAD_PAD), lambda i, j, k: (0, j)),
        [(proj, kr_spec)] + [(tb, tab) for tb in tables],
        [jax.ShapeDtypeStruct((t, n), BF16)], [pl.BlockSpec((tm, HEAD_PAD), lambda i, j, k: (i, j))],
        None, epilogue)[0]


def _chunk_mask(tq, tk):
    r = lax.broadcasted_iota(jnp.int32, (tq, tk), 0) // CHUNK
    c = lax.broadcasted_iota(jnp.int32, (tq, tk), 1) // CHUNK
    return c <= r


def _block_pairs(nb, key_major):
    if key_major:
        pairs = [(qi, ki) for ki in range(nb) for qi in range(ki, nb)]
    else:
        pairs = [(qi, ki) for qi in range(nb) for ki in range(qi + 1)]
    return (jnp.asarray([p[0] for p in pairs], jnp.int32), jnp.asarray([p[1] for p in pairs], jnp.int32))


def _flash_fwd(qp, kv, heads):
    t = qp.shape[0]
    tb = _tile(t, 512)
    nb = t // tb
    rep = tb // LANE
    qt, kt = _block_pairs(nb, key_major=False)

    def body(qt_ref, kt_ref, q_ref, k_ref, v_ref, o_ref, lse_ref, m_s, acc_s):
        p = pl.program_id(1)
        qi, ki = qt_ref[p], kt_ref[p]

        @pl.when(ki == 0)
        def _():
            m_s[...] = jnp.full_like(m_s, -1e30)
            acc_s[...] = jnp.zeros_like(acc_s)

        def step(masked):
            s = _dot(q_ref[...], k_ref[...], "nt")
            if masked:
                s = jnp.where(_chunk_mask(tb, tb), s, -1e30)
            m_prev = m_s[...]
            m_new = jnp.maximum(m_prev, jnp.max(s, axis=1, keepdims=True))
            alpha = jnp.exp(m_prev - m_new)
            pr = jnp.exp(s - jnp.tile(m_new, (1, rep))).astype(BF16)
            pv = _dot(pr, v_ref[...], "nn")
            acc_s[:, :V_DIM] = alpha * acc_s[:, :V_DIM] + pv[:, :V_DIM]
            acc_s[:, V_DIM:] = alpha * acc_s[:, V_DIM:] + pv[:, V_DIM:]
            m_s[...] = m_new

        @pl.when(ki < qi)
        def _():
            step(False)

        @pl.when(ki == qi)
        def _():
            step(True)
            l = acc_s[:, V_DIM:]
            o_ref[...] = (acc_s[:, :V_DIM] / l).astype(BF16)
            lse_ref[...] = m_s[...] + jnp.log(l)

    return pl.pallas_call(
        body, name="mla_flash_fwd",
        grid_spec=pltpu.PrefetchScalarGridSpec(
            num_scalar_prefetch=2, grid=(heads, int(qt.shape[0])),
            in_specs=[pl.BlockSpec((tb, HEAD_PAD), lambda h, p, qt, kt: (qt[p], h)),
                      pl.BlockSpec((tb, HEAD_PAD), lambda h, p, qt, kt: (kt[p], h)),
                      pl.BlockSpec((tb, HEAD_PAD), lambda h, p, qt, kt: (kt[p], heads + h))],
            out_specs=[pl.BlockSpec((tb, V_DIM), lambda h, p, qt, kt: (qt[p], h)),
                       pl.BlockSpec((None, tb, LANE), lambda h, p, qt, kt: (h, qt[p], 0))],
            scratch_shapes=[pltpu.VMEM((tb, LANE), F32), pltpu.VMEM((tb, HEAD_PAD), F32)]),
        out_shape=[jax.ShapeDtypeStruct((t, heads * V_DIM), BF16),
                   jax.ShapeDtypeStruct((heads, t, LANE), F32)],
        compiler_params=_params(("parallel", "arbitrary")),
    )(qt, kt, qp, kv, kv)


def _flash_delta(o, do, heads):
    t = o.shape[0]
    tm = _tile(t, 256)

    def body(o_ref, do_ref, d_ref):
        for h in range(heads):
            cols = slice(h * V_DIM, (h + 1) * V_DIM)
            prod = o_ref[:, cols].astype(F32) * do_ref[:, cols].astype(F32)
            d_ref[h] = jnp.broadcast_to(jnp.sum(prod, axis=1, keepdims=True), (tm, LANE))

    row = pl.BlockSpec((tm, heads * V_DIM), lambda i: (i, 0))
    return pl.pallas_call(
        body, name="mla_flash_delta", grid=(t // tm,), in_specs=[row, row],
        out_specs=pl.BlockSpec((heads, tm, LANE), lambda i: (0, i, 0)),
        out_shape=jax.ShapeDtypeStruct((heads, t, LANE), F32),
        compiler_params=_params(("parallel",)),
    )(o, do)


def _flash_bwd(qp, kv, do, lse, delta, heads):
    t = qp.shape[0]
    tb = _tile(t, 512)
    nb = t // tb
    rep = tb // LANE
    qt, kt = _block_pairs(nb, key_major=True)

    def body(qt_ref, kt_ref, q_ref, k_ref, v_ref, do_ref, lse_ref, dl_ref, dq_ref, dk_ref, dv_ref, dk_s, dv_s):
        p = pl.program_id(1)
        qi, ki = qt_ref[p], kt_ref[p]

        @pl.when(p == 0)
        def _():
            dq_ref[...] = jnp.zeros_like(dq_ref)

        @pl.when(qi == ki)
        def _():
            dk_s[...] = jnp.zeros_like(dk_s)
            dv_s[...] = jnp.zeros_like(dv_s)

        def step(masked):
            q = q_ref[...]
            k = k_ref[...]
            dov = do_ref[...]
            s = _dot(q, k, "nt")
            pr = jnp.exp(s - jnp.tile(lse_ref[...], (1, rep)))
            if masked:
                pr = jnp.where(_chunk_mask(tb, tb), pr, 0.0)
            dv_s[...] += _dot(pr.astype(BF16), dov, "tn")
            dp = _dot(dov, v_ref[...], "nt")
            ds = (pr * (dp - jnp.tile(dl_ref[...], (1, rep)))).astype(BF16)
            rows = pl.ds(pl.multiple_of(qi * tb, tb), tb)
            dq_ref[rows, :] += _dot(ds, k, "nn")
            dk_s[...] += _dot(ds, q, "tn")

        @pl.when(qi > ki)
        def _():
            step(False)

        @pl.when(qi == ki)
        def _():
            step(True)

        @pl.when(qi == nb - 1)
        def _():
            dk_ref[...] = dk_s[...]
            dv_ref[...] = dv_s[...]

    def qrow(width):
        return pl.BlockSpec((tb, width), lambda h, p, qt, kt: (qt[p], h))

    def stat():
        return pl.BlockSpec((None, tb, LANE), lambda h, p, qt, kt: (h, qt[p], 0))

    return pl.pallas_call(
        body, name="mla_flash_bwd",
        grid_spec=pltpu.PrefetchScalarGridSpec(
            num_scalar_prefetch=2, grid=(heads, int(qt.shape[0])),
            in_specs=[qrow(HEAD_PAD),
                      pl.BlockSpec((tb, HEAD_PAD), lambda h, p, qt, kt: (kt[p], h)),
                      pl.BlockSpec((tb, V_DIM), lambda h, p, qt, kt: (kt[p], 2 * (heads + h))),
                      qrow(V_DIM), stat(), stat()],
            out_specs=[pl.BlockSpec((t, HEAD_PAD), lambda h, p, qt, kt: (0, h)),
                       pl.BlockSpec((tb, HEAD_PAD), lambda h, p, qt, kt: (kt[p], h)),
                       pl.BlockSpec((tb, V_DIM), lambda h, p, qt, kt: (kt[p], h))],
            scratch_shapes=[pltpu.VMEM((tb, HEAD_PAD), F32), pltpu.VMEM((tb, V_DIM), F32)]),
        out_shape=[jax.ShapeDtypeStruct((t, heads * HEAD_PAD), F32),
                   jax.ShapeDtypeStruct((t, heads * HEAD_PAD), F32),
                   jax.ShapeDtypeStruct((t, heads * V_DIM), F32)],
        compiler_params=_params(("parallel", "arbitrary")),
    )(qt, kt, qp, kv, kv, do, lse, delta)


def _mla_attn_post(dqp, dkp, dv, tables, heads):
    t = dqp.shape[0]
    tm = _tile(t, 256)
    scale = QK_DIM ** -0.5
    kw, vw = heads * HEAD_PAD, heads * V_DIM

    def body(dq_ref, dk_ref, dv_ref, c_ref, su_ref, sd_ref, dqb_ref, dkvb_ref, dkr_ref):
        c, su, sd = c_ref[...], su_ref[...], sd_ref[...]
        kr = jnp.zeros((tm, LANE), F32)
        for h in range(heads):
            lo = h * HEAD_PAD
            mid = lo + QK_NOPE
            dqb_ref[:, lo:mid] = (scale * dq_ref[:, lo:mid]).astype(BF16)
            dqb_ref[:, mid:mid + LANE] = (scale * _rope_apply_t(dq_ref[:, mid:mid + LANE], c, su, sd)).astype(BF16)
            kr = kr + dk_ref[:, mid:mid + LANE]
            dkvb_ref[:, kw + lo:kw + mid] = dv_ref[:, h * V_DIM:(h + 1) * V_DIM].astype(BF16)
            dkvb_ref[:, kw + mid:kw + lo + HEAD_PAD] = jnp.zeros((tm, HEAD_PAD - V_DIM), BF16)
        dkvb_ref[:, :kw] = dk_ref[...].astype(BF16)
        dkr_ref[...] = _rope_apply_t(kr, c, su, sd)

    def row(n):
        return pl.BlockSpec((tm, n), lambda i: (i, 0))

    return pl.pallas_call(
        body, name="mla_attn_post", grid=(t // tm,),
        in_specs=[row(kw), row(kw), row(vw), row(LANE), row(LANE), row(LANE)],
        out_specs=[row(kw), row(2 * kw), row(LANE)],
        out_shape=[jax.ShapeDtypeStruct((t, kw), BF16), jax.ShapeDtypeStruct((t, 2 * kw), BF16),
                   jax.ShapeDtypeStruct((t, LANE), F32)],
        compiler_params=_params(("parallel",)),
    )(dqp, dkp, dv, *tables)


def _mla_weights(w_in_g, w_q_g, w_kv_g, w_out_g):
    d = w_in_g.shape[0] * w_in_g.shape[1]
    pw = w_in_g.shape[2]
    w_in = jnp.pad(w_in_g.reshape(d, pw), ((0, 0), (0, LANE - QK_ROPE)))
    ql = w_q_g.shape[1]
    wq = jnp.transpose(w_q_g, (1, 0, 2)).reshape(ql, -1, QK_DIM)
    heads = wq.shape[1]
    wq = jnp.pad(wq, ((0, 0), (0, 0), (0, HEAD_PAD - QK_DIM))).reshape(ql, heads * HEAD_PAD)
    kvl = w_kv_g.shape[1]
    wkv = jnp.transpose(w_kv_g, (1, 0, 2)).reshape(kvl, heads, QK_NOPE + V_DIM)
    wk = jnp.pad(wkv[:, :, :QK_NOPE], ((0, 0), (0, 0), (0, HEAD_PAD - QK_NOPE))).reshape(kvl, heads * HEAD_PAD)
    wv = jnp.pad(wkv[:, :, QK_NOPE:], ((0, 0), (0, 0), (0, HEAD_PAD - V_DIM))).reshape(kvl, heads * HEAD_PAD)
    return w_in, wq, jnp.concatenate([wk, wv], axis=1), w_out_g.reshape(heads * V_DIM, -1), heads


def _mla_unpermute(d_w_in, d_wq, d_wkv, heads, pw):
    d = d_w_in.shape[0]
    g_in = d_w_in[:, :pw].reshape(N_CHIPS, d // N_CHIPS, pw)
    ql = d_wq.shape[0]
    g_q = d_wq.reshape(ql, heads, HEAD_PAD)[:, :, :QK_DIM].reshape(ql, N_CHIPS, -1)
    kvl = d_wkv.shape[0]
    g_k = d_wkv[:, :heads * HEAD_PAD].reshape(kvl, heads, HEAD_PAD)[:, :, :QK_NOPE]
    g_v = d_wkv[:, heads * HEAD_PAD:].reshape(kvl, heads, HEAD_PAD)[:, :, :V_DIM]
    g_kv = jnp.concatenate([g_k, g_v], axis=2).reshape(kvl, N_CHIPS, -1)
    return g_in, jnp.transpose(g_q, (1, 0, 2)), jnp.transpose(g_kv, (1, 0, 2))


def _mla_fwd(x, g, wts, gq, gkv, tables):
    w_in, wq, wkv, w_out, heads = wts
    h = _rms_fwd("mla_norm", x, g)
    proj = _mm_nn_full("mla_proj", h, w_in, F32, tn_pref=w_in.shape[1])
    qn, kvn = _mla_norm_fwd(proj, gq, gkv)
    qp = _mla_q_up(qn, wq, tables)
    kv = _mla_kv_up(kvn, wkv, proj, tables, heads)
    o, lse = _flash_fwd(qp, kv, heads)
    y = _mm_residual("mla_out", o, w_out, x, 1.0, tk_pref=2048)
    return y, (x, h, proj, qn, kvn, qp, kv, o, lse)


def _mla_bwd(saved, g, wts, gq, gkv, tables, dx, dxb, pw):
    w_in, wq, wkv, w_out, heads = wts
    x, h, proj, qn, kvn, qp, kv, o, lse = saved
    d_w_out = _mm_tn("mla_dwout", o, dxb)
    do = _mm_nt_full("mla_do", dxb, w_out, BF16)
    dqp, dkp, dv = _flash_bwd(qp, kv, do, lse, _flash_delta(o, do, heads), heads)
    dqb, dkvb, dkr = _mla_attn_post(dqp, dkp, dv, tables, heads)
    d_wq = _mm_tn("mla_dwq", qn, dqb)
    dqn = _mm_nt_k("mla_dqn", dqb, wq, F32)
    d_wkv = _mm_tn("mla_dwkv", kvn, dkvb)
    dkvn = _mm_nt_k("mla_dkvn", dkvb, wkv, F32)
    dproj, dgq, dgkv = _mla_norm_bwd(proj, dqn, dkvn, dkr, gq, gkv)
    d_w_in = _mm_tn("mla_dwin", h, dproj, tn_pref=dproj.shape[1])
    dh = _mm_nt_full("mla_dh", dproj, w_in, F32, tn_pref=1024)
    dx, dxb, dg = _rms_bwd("mla_dnorm", dh, x, g, dx)
    g_in, g_q, g_kv = _mla_unpermute(d_w_in, d_wq, d_wkv, heads, pw)
    small = dict(ln=dg, gq=dgq, gkv=dgkv)
    return dx, dxb, small, g_in, g_q, g_kv, d_w_out.reshape(N_CHIPS, d_w_out.shape[0] // N_CHIPS, -1)


def _gather_weights(bufs, norms):
    nt = len(bufs)

    def body(*refs):
        n_in = refs[nt]
        outs, n_out = refs[nt + 1:2 * nt + 1], refs[2 * nt + 1]
        send, recv, fsend, frecv, loc, nsend, nrecv = refs[2 * nt + 2:]
        x, y, c, chips = _place()
        me = 2 * x + y
        sib = (x, y, 1 - c)

        local = pltpu.make_async_copy(n_in, n_out.at[me], loc)
        local.start()

        def place(t, chip, half):
            return outs[t].at[2 * chip[0] + chip[1], half]

        def ici(t, j, chip):
            return pltpu.make_async_remote_copy(
                src_ref=place(t, (x, y), c), dst_ref=place(t, (x, y), c), send_sem=send.at[t, j],
                recv_sem=recv.at[t, j], device_id=(*chip, c), device_id_type=MESH)

        def fwd(t, j, chip, half):
            return pltpu.make_async_remote_copy(
                src_ref=place(t, chip, half), dst_ref=place(t, chip, half), send_sem=fsend.at[t, j],
                recv_sem=frecv.at[t, j], device_id=sib, device_id_type=MESH)

        def nrm(j, chip, owner):
            return pltpu.make_async_remote_copy(
                src_ref=n_in, dst_ref=n_out.at[2 * owner[0] + owner[1]], send_sem=nsend.at[j], recv_sem=nrecv.at[j],
                device_id=(*chip, c), device_id_type=MESH)

        firsts = [ici(t, j, chip) for t in range(nt) for j, chip in enumerate(chips)]
        firsts += [nrm(j, chip, (x, y)) for j, chip in enumerate(chips)]
        for cp in firsts:
            cp.start()
        passed = []
        for t in range(nt):
            for j, chip in enumerate(chips):
                pltpu.make_async_remote_copy(
                    src_ref=place(t, chip, c), dst_ref=place(t, chip, c), send_sem=send.at[t, j],
                    recv_sem=recv.at[t, j], device_id=(*chip, c), device_id_type=MESH).wait_recv()
                cp = fwd(t, j, chip, c)
                cp.start()
                passed.append(cp)
        for t in range(nt):
            for j, chip in enumerate(chips):
                fwd(t, j, chip, 1 - c).wait_recv()
        for j, chip in enumerate(chips):
            nrm(j, chip, chip).wait_recv()
        for cp in firsts + passed:
            cp.wait_send()
        local.wait()

    out_shape = [jax.ShapeDtypeStruct(b.shape, b.dtype) for b in bufs]
    out_shape.append(jax.ShapeDtypeStruct((N_CHIPS,) + norms.shape, norms.dtype))
    res = pl.pallas_call(
        body, name="gather_weights",
        in_specs=[ANY] * (nt + 1), out_specs=[ANY] * (nt + 1), out_shape=out_shape,
        input_output_aliases={t: t for t in range(nt)},
        scratch_shapes=[pltpu.SemaphoreType.DMA((nt, 3)), pltpu.SemaphoreType.DMA((nt, 3)),
                        pltpu.SemaphoreType.DMA((nt, 3)), pltpu.SemaphoreType.DMA((nt, 3)),
                        pltpu.SemaphoreType.DMA, pltpu.SemaphoreType.DMA((3,)),
                        pltpu.SemaphoreType.DMA((3,))],
    )(*bufs, norms)
    return res[:nt], res[nt]


def _run_phase(name, phase):
    c_in, in_specs, out_specs, shapes, sems, in_place = _comm_io(phase)
    n_c, n_co = len(c_in), len(shapes)

    def body(*refs):
        ins, outs = refs[:n_c], refs[n_c:n_c + n_co]
        send, recv = refs[n_c + n_co:]
        for cp in phase.copies(ins, outs, send, recv, False):
            cp.start()
        for cp in phase.copies(ins, outs, send, recv, True):
            cp.wait_recv()
        for cp in phase.copies(ins, outs, send, recv, False):
            cp.wait_send()

    return pl.pallas_call(
        body, name=name, in_specs=in_specs, out_specs=out_specs, out_shape=shapes,
        input_output_aliases={t: t for t in range(n_c)} if in_place else {}, scratch_shapes=sems,
    )(*c_in)


def _remote(ref_src, ref_dst, send, recv, t, j, to):
    return pltpu.make_async_remote_copy(src_ref=ref_src, dst_ref=ref_dst, send_sem=send.at[t, j],
                                        recv_sem=recv.at[t, j], device_id=to, device_id_type=MESH)


class _Reduce:
    def __init__(self, units, grads, n_layers, reduced, me_idx, core_idx):
        self.units, self.n_layers, self.reduced = list(units), n_layers, reduced
        self.me_idx, self.core_idx = me_idx, core_idx
        self.local = [g.reshape(N_CHIPS, 2, g.shape[1] // 2, g.shape[2]) for g in grads]
        self.parts = None

    def _swap(self):
        fresh = [jax.ShapeDtypeStruct((g.shape[0],) + g.shape[2:], g.dtype) for g in self.local]

        def copies(ins, outs, send, recv, landing):
            x, y, c, _ = _place()
            return [_remote(outs[t] if landing else ins[t].at[:, 1 - c], outs[t], send, recv, t, 0, (x, y, 1 - c))
                    for t in range(len(ins))]

        return _Phase(self.local, fresh, (len(self.local), 1), copies)

    def _scatter(self):
        fresh = [jax.ShapeDtypeStruct((3,) + p.shape[1:], p.dtype) for p in self.parts]

        def copies(ins, outs, send, recv, landing):
            x, y, c, chips = _place()
            return [_remote(outs[t].at[j] if landing else ins[t].at[2 * chip[0] + chip[1]], outs[t].at[j],
                            send, recv, t, j, (*chip, c))
                    for t in range(len(ins)) for j, chip in enumerate(chips)]

        return _Phase(self.parts, fresh, (len(self.parts), 3), copies)

    def _join(self):
        layers = [l for _, l in self.units]

        def copies(ins, outs, send, recv, landing):
            x, y, c, _ = _place()
            refs = [outs[t].at[l, 1 - c if landing else c] for t, l in enumerate(layers)]
            return [_remote(r, r, send, recv, t, 0, (x, y, 1 - c)) for t, r in enumerate(refs)]

        return _Phase([self.reduced[n] for n, _ in self.units], [], (len(self.units), 1), copies)

    def _after(self, step, got):
        if step == 0:
            self.parts = [_add_halves(self.core_idx, g, o) for g, o in zip(self.local, got)]
        elif step == 1:
            for (n, l), p, ld in zip(self.units, self.parts, got):
                self.reduced[n] = _sum_chips(self.me_idx, self.core_idx, p, ld, l, self.n_layers[n],
                                             self.reduced.get(n))
        else:
            for (n, _), joined in zip(self.units, got):
                self.reduced[n] = joined

    def _phase(self, step):
        return (self._swap, self._scatter, self._join)[step]()

    def behind(self, step, fn, *args, **kw):
        if not self.units:
            return fn(*args, **kw)
        out, got = fn(*args, comm=self._phase(step), **kw)
        self._after(step, got)
        return out

    def alone(self):
        for step, name in enumerate(("grad_swap_halves", "grad_scatter_chips", "grad_join_halves")):
            self._after(step, _run_phase(name, self._phase(step)))


def _gather_all(block):
    m_per, n = block.shape

    def body(x_ref, out_ref, send_sems, recv_sems, local_sem):
        x, y, c, chips = _place()
        me, sibling = (x, y, c), (x, y, 1 - c)

        def rows(px, py, pc):
            return out_ref.at[pl.ds((4 * px + 2 * py + pc) * m_per, m_per), :]

        def copy(k, block_of, to, src=None):
            return pltpu.make_async_remote_copy(
                src_ref=rows(*block_of) if src is None else src, dst_ref=rows(*block_of),
                send_sem=send_sems.at[k], recv_sem=recv_sems.at[k], device_id=to, device_id_type=MESH)

        mine = pltpu.make_async_copy(x_ref, rows(*me), local_sem)
        mine.start()
        first = [copy(0, me, sibling, src=x_ref)]
        first += [copy(1 + j, me, (*chip, c), src=x_ref) for j, chip in enumerate(chips)]
        for cp in first:
            cp.start()
        passed = [copy(4 + j, (*chip, c), sibling) for j, chip in enumerate(chips)]
        for j, chip in enumerate(chips):
            copy(1 + j, (*chip, c), me).wait_recv()
            passed[j].start()
        copy(0, sibling, me).wait_recv()
        for j, chip in enumerate(chips):
            copy(4 + j, (*chip, 1 - c), me).wait_recv()
        for cp in first + passed:
            cp.wait_send()
        mine.wait()

    return pl.pallas_call(
        body, name="gather_small_grads",
        out_shape=jax.ShapeDtypeStruct((N_DEV * m_per, n), block.dtype),
        in_specs=[pl.BlockSpec(memory_space=pltpu.VMEM)],
        out_specs=pl.BlockSpec(memory_space=pltpu.VMEM),
        scratch_shapes=[pltpu.SemaphoreType.DMA((7,)), pltpu.SemaphoreType.DMA((7,)), pltpu.SemaphoreType.DMA],
    )(block)


def _row_tile(r, c, elems=512 * 1024):
    t = max(8, min(r, (elems // c) // 8 * 8))
    while t > 8 and r % t:
        t -= 8
    return t if r % t == 0 else r


def _add_halves(idx, grad, other):
    n, _, r, w = grad.shape
    tr = _row_tile(r, w)

    def body(idx_ref, g_ref, o_ref, out_ref):
        out_ref[...] = (g_ref[...].astype(F32) + o_ref[...].astype(F32)).astype(BF16)

    return pl.pallas_call(
        body, name="grad_add_halves",
        grid_spec=pltpu.PrefetchScalarGridSpec(
            num_scalar_prefetch=1, grid=(n, r // tr),
            in_specs=[pl.BlockSpec((None, None, tr, w), lambda k, i, idx: (k, idx[0], i, 0)),
                      pl.BlockSpec((None, tr, w), lambda k, i, idx: (k, i, 0))],
            out_specs=pl.BlockSpec((None, tr, w), lambda k, i, idx: (k, i, 0))),
        out_shape=jax.ShapeDtypeStruct((n, r, w), BF16),
        compiler_params=_params(("parallel", "parallel")),
    )(idx, grad, other)


def _sum_chips(me_idx, core_idx, part, landed, layer, n_layers, prev):
    _, r, w = part.shape
    tr = _row_tile(r, w)

    def body(me_ref, c_ref, p_ref, l_ref, *rest):
        acc = p_ref[...].astype(F32)
        for j in range(3):
            acc = acc + l_ref[j].astype(F32)
        rest[-1][...] = acc

    return pl.pallas_call(
        body, name="grad_sum_chips",
        grid_spec=pltpu.PrefetchScalarGridSpec(
            num_scalar_prefetch=2, grid=(r // tr,),
            in_specs=[pl.BlockSpec((None, tr, w), lambda i, me, c: (me[0], i, 0)),
                      pl.BlockSpec((3, tr, w), lambda i, me, c: (0, i, 0))] + ([] if prev is None else [ANY]),
            out_specs=pl.BlockSpec((None, None, tr, w), lambda i, me, c: (layer, c[0], i, 0))),
        out_shape=jax.ShapeDtypeStruct((n_layers, 2, r, w), F32),
        input_output_aliases={} if prev is None else {4: 0},
        compiler_params=_params(("parallel",)),
    )(me_idx, core_idx, part, landed, *([] if prev is None else [prev]))


def _adamw_math(w, g, m, v):
    m = ADAM_B1 * m + (1.0 - ADAM_B1) * g
    v = ADAM_B2 * v + (1.0 - ADAM_B2) * (g * g)
    m_hat = m / (1.0 - ADAM_B1 ** ADAM_STEP)
    v_hat = v / (1.0 - ADAM_B2 ** ADAM_STEP)
    delta = -ADAM_LR * (m_hat / (jnp.sqrt(v_hat) + ADAM_EPS) + ADAM_WD * w)
    return delta, m, v


def _adamw(name, w, g, m, v):
    r, c = w.shape
    tr = _row_tile(r, c, 256 * 1024)

    def body(w_ref, g_ref, m_ref, v_ref, d_ref, nm_ref, nv_ref):
        d_ref[...], nm_ref[...], nv_ref[...] = _adamw_math(w_ref[...], g_ref[...], m_ref[...], v_ref[...])

    blk = pl.BlockSpec((tr, c), lambda i: (i, 0))
    return pl.pallas_call(
        body, name=name, grid=(r // tr,), in_specs=[blk] * 4, out_specs=[blk] * 3,
        out_shape=[jax.ShapeDtypeStruct((r, c), F32)] * 3,
        compiler_params=_params(("parallel",)),
    )(w, g, m, v)


def _adamw_summed(w, parts, m, v):
    r, c = w.shape

    def body(w_ref, p_ref, m_ref, v_ref, g_ref, d_ref, nm_ref, nv_ref):
        g = p_ref[0:r, :]
        for k in range(1, N_DEV):
            g = g + p_ref[k * r:(k + 1) * r, :]
        g_ref[...] = g
        d_ref[...], nm_ref[...], nv_ref[...] = _adamw_math(w_ref[...], g, m_ref[...], v_ref[...])

    return pl.pallas_call(
        body, name="adamw_replicated",
        out_shape=[jax.ShapeDtypeStruct((r, c), F32)] * 4,
        compiler_params=pltpu.CompilerParams(vmem_limit_bytes=VMEM_LIMIT_BYTES),
    )(w, parts, m, v)


def _pack(arrays):
    return jnp.concatenate([a.reshape(-1, LANE) for a in arrays], axis=0)


def _unpack(packed, shapes):
    out, row = [], 0
    for s in shapes:
        n = math.prod(s) // LANE
        out.append(packed[row:row + n].reshape(s))
        row += n
    return out


def _cast_into(name, idx, w, layer):
    _, rows, c = w.shape
    r = rows // 2
    tr = _row_tile(r, c)
    per = r // tr

    def body(idx_ref, w_ref, o_ref):
        o_ref[...] = w_ref[...].astype(BF16)

    return pl.pallas_call(
        body, name=name,
        grid_spec=pltpu.PrefetchScalarGridSpec(
            num_scalar_prefetch=1, grid=(2, per),
            in_specs=[pl.BlockSpec((None, tr, c), lambda h, i, idx: (layer, h * per + i, 0))],
            out_specs=pl.BlockSpec((None, None, tr, c), lambda h, i, idx: (idx[0], h, i, 0))),
        out_shape=jax.ShapeDtypeStruct((N_CHIPS, 2, r, c), BF16),
        compiler_params=_params(("parallel", "parallel")),
    )(idx, w)


BIG = ["ffn1_w_in", "ffn1_w_out", "ffn2_w_in", "ffn2_w_out", "sgu_w_in", "sgu_w_out",
       "mla_w_in", "mla_w_q_up", "mla_w_kv_up", "mla_w_out"]
STAGES = [
    [("ffn1_w_in", 0), ("ffn1_w_out", 0)],
    [("sgu_w_in", 0), ("sgu_w_out", 0)],
    [("ffn2_w_in", 0), ("ffn2_w_out", 0)],
    [("ffn1_w_in", 1), ("ffn1_w_out", 1)],
    [("mla_w_in", 0), ("mla_w_q_up", 0), ("mla_w_kv_up", 0), ("mla_w_out", 0)],
    [("ffn2_w_in", 1), ("ffn2_w_out", 1)],
]
REPLICATED = ["ln_ffn1", "ln_mix", "ln_ffn2", "sgu_v_gain", "sgu_v_bias", "sgu_w_spatial", "sgu_b_spatial",
              "ln_final"]
NORM_SHARDS = ["mla_q_norm", "mla_kv_norm"]
WEIGHTS = ["ln_ffn1", "ffn1_w_in", "ffn1_w_out", "ln_mix", "ln_ffn2", "ffn2_w_in", "ffn2_w_out", "sgu_w_in",
           "sgu_v_gain", "sgu_v_bias", "sgu_w_spatial", "sgu_b_spatial", "sgu_w_out", "mla_w_in", "mla_q_norm",
           "mla_w_q_up", "mla_kv_norm", "mla_w_kv_up", "mla_w_out", "ln_final"]


def kernel(x, positions, ln_ffn1, ffn1_w_in, ffn1_w_out, ln_mix, ln_ffn2, ffn2_w_in, ffn2_w_out, sgu_w_in, sgu_v_gain, sgu_v_bias, sgu_w_spatial, sgu_b_spatial, sgu_w_out, mla_w_in, mla_q_norm, mla_w_q_up, mla_kv_norm, mla_w_kv_up, mla_w_out, ln_final, loss_target, m_ln_ffn1, m_ffn1_w_in, m_ffn1_w_out, m_ln_mix, m_ln_ffn2, m_ffn2_w_in, m_ffn2_w_out, m_sgu_w_in, m_sgu_v_gain, m_sgu_v_bias, m_sgu_w_spatial, m_sgu_b_spatial, m_sgu_w_out, m_mla_w_in, m_mla_q_norm, m_mla_w_q_up, m_mla_kv_norm, m_mla_w_kv_up, m_mla_w_out, m_ln_final, v_ln_ffn1, v_ffn1_w_in, v_ffn1_w_out, v_ln_mix, v_ln_ffn2, v_ffn2_w_in, v_ffn2_w_out, v_sgu_w_in, v_sgu_v_gain, v_sgu_v_bias, v_sgu_w_spatial, v_sgu_b_spatial, v_sgu_w_out, v_mla_w_in, v_mla_q_norm, v_mla_w_q_up, v_mla_kv_norm, v_mla_w_kv_up, v_mla_w_out, v_ln_final):
    given = dict(locals())
    w = {n: given[n] for n in WEIGHTS}
    mom = {n: given["m_" + n] for n in WEIGHTS}
    var = {n: given["v_" + n] for n in WEIGHTS}
    t, d = x.shape[1], x.shape[2]
    xs = x.reshape(t, d)
    target = loss_target.reshape(t, d)
    me = 2 * lax.axis_index("x") + lax.axis_index("y")

    me_idx = jnp.reshape(me, (1,)).astype(jnp.int32)
    core_idx = jnp.reshape(lax.axis_index("c"), (1,)).astype(jnp.int32)

    bufs = {(n, l): _cast_into(f"cast_{n}_{l}", me_idx, w[n], l) for n in BIG for l in range(w[n].shape[0])}
    nq = mla_q_norm.shape[1]
    norms = jnp.pad(jnp.concatenate([mla_q_norm, mla_kv_norm], axis=0), ((0, 6), (0, LANE - nq)))

    def take(stage):
        return [bufs[u] for u in stage]

    def put(stage, arrays):
        bufs.update(zip(stage, arrays))

    def full(unit):
        a = bufs[unit]
        return a.reshape(N_CHIPS, a.shape[1] * a.shape[2], a.shape[3])

    def rows(unit):
        a = bufs[unit]
        return a.reshape(-1, a.shape[-1])

    first, norms_g = _gather_weights(take(STAGES[0]), norms)
    put(STAGES[0], first)
    gq = norms_g[:, 0, :nq].reshape(1, N_CHIPS * nq)
    gkv = norms_g[:, 1, :nq].reshape(1, N_CHIPS * nq)
    tables = _rope_tables(positions.reshape(t))
    sgu_small = (sgu_v_gain, sgu_v_bias, sgu_w_spatial[0], sgu_b_spatial[0])

    a0, s_f1_0, got = _ffn_fwd("l0_ffn1", xs, ln_ffn1[0], full(("ffn1_w_in", 0)), rows(("ffn1_w_out", 0)),
                               take(STAGES[1]))
    put(STAGES[1], got)
    a1, s_sgu, got = _sgu_fwd(a0, ln_mix[0], full(("sgu_w_in", 0)), *sgu_small, rows(("sgu_w_out", 0)),
                              take(STAGES[2]))
    put(STAGES[2], got)
    a2, s_f2_0, got = _ffn_fwd("l0_ffn2", a1, ln_ffn2[0], full(("ffn2_w_in", 0)), rows(("ffn2_w_out", 0)),
                               take(STAGES[3]))
    put(STAGES[3], got)
    a3, s_f1_1, got = _ffn_fwd("l1_ffn1", a2, ln_ffn1[1], full(("ffn1_w_in", 1)), rows(("ffn1_w_out", 1)),
                               take(STAGES[4] + STAGES[5]))
    put(STAGES[4] + STAGES[5], got)
    mla_wts = _mla_weights(full(("mla_w_in", 0)), full(("mla_w_q_up", 0)), full(("mla_w_kv_up", 0)),
                           full(("mla_w_out", 0)))
    a4, s_mla = _mla_fwd(a3, ln_mix[1], mla_wts, gq, gkv, tables)
    a5, s_f2_1, _ = _ffn_fwd("l1_ffn2", a4, ln_ffn2[1], full(("ffn2_w_in", 1)), rows(("ffn2_w_out", 1)), [])

    gr, reduced = {}, {}
    n_layers = {n: w[n].shape[0] for n in BIG}

    def reduce_of(stages):
        units = [u for s in stages for u in STAGES[s]]
        return _Reduce(units, [gr[u] for u in units], n_layers, reduced, me_idx, core_idx)

    loss_part, dx, dxb, dg_final = _loss_bwd(a5, ln_final, target)
    dx, dxb, dg_f2_1, gr["ffn2_w_in", 1], gr["ffn2_w_out", 1] = _ffn_bwd(
        "l1_ffn2", s_f2_1, ln_ffn2[1], full(("ffn2_w_in", 1)), rows(("ffn2_w_out", 1)), dx, dxb, reduce_of([]))
    (dx, dxb, sm_mla, gr["mla_w_in", 0], gr["mla_w_q_up", 0], gr["mla_w_kv_up", 0],
     gr["mla_w_out", 0]) = _mla_bwd(s_mla, ln_mix[1], mla_wts, gq, gkv, tables, dx, dxb, mla_w_in.shape[2])
    dx, dxb, dg_f1_1, gr["ffn1_w_in", 1], gr["ffn1_w_out", 1] = _ffn_bwd(
        "l1_ffn1", s_f1_1, ln_ffn1[1], full(("ffn1_w_in", 1)), rows(("ffn1_w_out", 1)), dx, dxb, reduce_of([5, 4]))
    dx, dxb, dg_f2_0, gr["ffn2_w_in", 0], gr["ffn2_w_out", 0] = _ffn_bwd(
        "l0_ffn2", s_f2_0, ln_ffn2[0], full(("ffn2_w_in", 0)), rows(("ffn2_w_out", 0)), dx, dxb, reduce_of([3]))
    dx, dxb, sm_sgu, gr["sgu_w_in", 0], gr["sgu_w_out", 0] = _sgu_bwd(
        s_sgu, ln_mix[0], full(("sgu_w_in", 0)), *sgu_small, rows(("sgu_w_out", 0)), dx, dxb, reduce_of([2]))
    dx, dxb, dg_f1_0, gr["ffn1_w_in", 0], gr["ffn1_w_out", 0] = _ffn_bwd(
        "l0_ffn1", s_f1_0, ln_ffn1[0], full(("ffn1_w_in", 0)), rows(("ffn1_w_out", 0)), dx, dxb, reduce_of([1]))
    reduce_of([0]).alone()
    big_grad = {n: reduced[n].reshape(w[n].shape) for n in BIG}

    small_parts = [
        jnp.concatenate([dg_f1_0, dg_f1_1], axis=0), jnp.concatenate([sm_sgu["ln"], sm_mla["ln"]], axis=0),
        jnp.concatenate([dg_f2_0, dg_f2_1], axis=0), sm_sgu["gain"], sm_sgu["bias"], sm_sgu["w_sp"],
        sm_sgu["b_sp"], dg_final]
    rep_shapes = [w[n].shape for n in REPLICATED]
    gq_row = jnp.pad(sm_mla["gq"], ((0, 0), (0, N_CHIPS * (LANE - nq))))
    gkv_row = jnp.pad(sm_mla["gkv"], ((0, 0), (0, N_CHIPS * (LANE - nq))))
    packed = _pack(small_parts + [gq_row, gkv_row, loss_part])
    packed = jnp.pad(packed, ((0, -packed.shape[0] % 8), (0, 0)))
    everyone = _gather_all(packed)
    rows = packed.shape[0]
    zero_rows = jnp.zeros((rows - sum(math.prod(s) // LANE for s in rep_shapes), LANE), F32)
    pw = jnp.concatenate([_pack([w[n] for n in REPLICATED]), zero_rows], axis=0)
    pm = jnp.concatenate([_pack([mom[n] for n in REPLICATED]), zero_rows], axis=0)
    pv = jnp.concatenate([_pack([var[n] for n in REPLICATED]), zero_rows + 1.0], axis=0)
    g_all, d_all, m_all, v_all = _adamw_summed(pw, everyone, pm, pv)
    tail_shapes = [(1, N_CHIPS * LANE), (1, N_CHIPS * LANE), (1, LANE)]
    rep_grad = dict(zip(REPLICATED, _unpack(g_all, rep_shapes + tail_shapes)[:len(REPLICATED)]))
    rep_delta = dict(zip(REPLICATED, _unpack(d_all, rep_shapes)))
    rep_m = dict(zip(REPLICATED, _unpack(m_all, rep_shapes)))
    rep_v = dict(zip(REPLICATED, _unpack(v_all, rep_shapes)))
    tail = _unpack(g_all, rep_shapes + tail_shapes)[len(REPLICATED):]
    loss = tail[2][0, 0]
    norm_grad = {
        "mla_q_norm": lax.dynamic_slice(tail[0], (0, me * nq), (1, nq)),
        "mla_kv_norm": lax.dynamic_slice(tail[1], (0, me * nq), (1, nq)),
    }

    grad, delta, new_m, new_v = {}, {}, {}, {}
    for n in BIG:
        shp = w[n].shape
        flat = lambda a: a.reshape(-1, shp[-1])
        dl, nm, nv = _adamw("adamw_" + n, flat(w[n]), flat(big_grad[n]), flat(mom[n]), flat(var[n]))
        grad[n], delta[n], new_m[n], new_v[n] = big_grad[n], dl.reshape(shp), nm.reshape(shp), nv.reshape(shp)
    for n in REPLICATED:
        grad[n], delta[n], new_m[n], new_v[n] = rep_grad[n], rep_delta[n], rep_m[n], rep_v[n]
    stack = lambda dct: jnp.concatenate([dct[n] for n in NORM_SHARDS], axis=0)
    dl, nm, nv = _adamw("adamw_norm_shards", stack(w), stack(norm_grad), stack(mom), stack(var))
    for i, n in enumerate(NORM_SHARDS):
        grad[n], delta[n], new_m[n], new_v[n] = norm_grad[n], dl[i:i + 1], nm[i:i + 1], nv[i:i + 1]

    grad_x = dx.reshape(x.shape)
    return (loss, grad_x, *[grad[n] for n in WEIGHTS], *[delta[n] for n in WEIGHTS],
            *[new_m[n] for n in WEIGHTS], *[new_v[n] for n in WEIGHTS])
```

```python
import functools
import math

import jax
import jax.numpy as jnp
from jax import lax
from jax.experimental import pallas as pl
from jax.experimental.pallas import tpu as pltpu

F32 = jnp.float32
BF16 = jnp.bfloat16
MESH = pl.DeviceIdType.MESH

EPS = 1e-6
CHUNK = 64
SGU_BLOCK = 128
SGU_GROUPS = 8
QK_NOPE = 128
QK_ROPE = 64
V_DIM = 128
QK_DIM = QK_NOPE + QK_ROPE
HEAD_PAD = 256
ROPE_THETA = 10000.0
N_CHIPS = 4
N_DEV = 8

ADAM_LR = 0.001
ADAM_B1 = 0.9
ADAM_B2 = 0.999
ADAM_EPS = 1e-08
ADAM_WD = 0.01
ADAM_STEP = 10

LANE = 128
VMEM_LIMIT_BYTES = 56 * 1024 * 1024

_DIMS = {
    "nn": (((1,), (0,)), ((), ())),
    "nt": (((1,), (1,)), ((), ())),
    "tn": (((0,), (0,)), ((), ())),
}


def _tile(n, pref):
    t = (min(pref, n) // LANE) * LANE
    while t >= LANE:
        if n % t == 0:
            return t
        t -= LANE
    return n


def _params(sem):
    return pltpu.CompilerParams(dimension_semantics=sem, vmem_limit_bytes=VMEM_LIMIT_BYTES)


def _dot(a, b, mode):
    return lax.dot_general(a, b, _DIMS[mode], preferred_element_type=F32)


def _place():
    x, y, c = lax.axis_index("x"), lax.axis_index("y"), lax.axis_index("c")
    chips = [(1 - x, y), (x, 1 - y), (1 - x, 1 - y)]
    return x, y, c, chips


ANY = pl.BlockSpec(memory_space=pl.ANY)


def _gather_copies(phase, bufs, send, recv, landing):
    x, y, c, chips = _place()
    copies = []
    for t, buf in enumerate(bufs):
        for j, chip in enumerate(chips):
            there = 2 * chip[0] + chip[1]
            if phase == 1:
                src, lands, to = buf.at[2 * x + y, c], buf.at[there, c], (*chip, c)
            else:
                src, lands, to = buf.at[there, c], buf.at[there, 1 - c], (x, y, 1 - c)
            ref = lands if landing else src
            copies.append(pltpu.make_async_remote_copy(
                src_ref=ref, dst_ref=ref, send_sem=send.at[t, j], recv_sem=recv.at[t, j], device_id=to,
                device_id_type=MESH))
    return copies


class _Phase:
    def __init__(self, ins, fresh, sems, copies):
        self.ins, self.fresh, self.sems, self.copies = list(ins), list(fresh), sems, copies


def _gather_phase(phase, bufs):
    return _Phase(bufs, [], (len(bufs), 3),
                  lambda ins, outs, send, recv, landing: _gather_copies(phase, outs, send, recv, landing))


def _comm_io(comm):
    if comm is None:
        return [], [], [], [], [], False
    shapes = comm.fresh or [jax.ShapeDtypeStruct(b.shape, b.dtype) for b in comm.ins]
    sems = [pltpu.SemaphoreType.DMA(comm.sems), pltpu.SemaphoreType.DMA(comm.sems)]
    return comm.ins, [ANY] * len(comm.ins), [ANY] * len(shapes), shapes, sems, not comm.fresh


def _comm_run(comm, in_refs, out_refs, send, recv, first, last):
    @pl.when(first)
    def _():
        for cp in comm.copies(in_refs, out_refs, send, recv, False):
            cp.start()

    def finish():
        for cp in comm.copies(in_refs, out_refs, send, recv, True):
            cp.wait_recv()
        for cp in comm.copies(in_refs, out_refs, send, recv, False):
            cp.wait_send()

    return last, finish


def _matmul(name, mode, grid, a, a_spec, b, b_spec, extras, out_shapes, out_specs, acc_shape, epilogue, comm=None):
    nk = grid[2]
    n_ex = len(extras)
    n_out = len(out_shapes)
    c_in, c_in_specs, c_out_specs, c_shapes, c_sems, in_place = _comm_io(comm)
    n_c, n_co = len(c_in), len(c_shapes)

    def body(*refs):
        a_ref, b_ref = refs[0], refs[1]
        ex = refs[2:2 + n_ex]
        outs = refs[2 + n_ex + n_c:2 + n_ex + n_c + n_out]
        ids = (pl.program_id(0), pl.program_id(1))
        k = pl.program_id(2)
        if comm is not None:
            c_ins = refs[2 + n_ex:2 + n_ex + n_c]
            c_outs = refs[2 + n_ex + n_c + n_out:2 + n_ex + n_c + n_out + n_co]
            send, recv = refs[2 + n_ex + n_c + n_out + n_co:2 + n_ex + n_c + n_out + n_co + 2]
            first = jnp.logical_and(jnp.logical_and(ids[0] == 0, ids[1] == 0), k == 0)
            last = jnp.logical_and(jnp.logical_and(ids[0] == grid[0] - 1, ids[1] == grid[1] - 1), k == nk - 1)
            last, finish = _comm_run(comm, c_ins, c_outs, send, recv, first, last)
        part = _dot(a_ref[...], b_ref[...], mode)
        if nk == 1:
            epilogue(part, ex, outs, ids)
        else:
            acc = refs[-1]

            @pl.when(k == 0)
            def _():
                acc[...] = part

            @pl.when(k > 0)
            def _():
                acc[...] += part

            @pl.when(k == nk - 1)
            def _():
                epilogue(acc[...], ex, outs, ids)

        if comm is not None:
            pl.when(last)(finish)

    scratch = c_sems + ([pltpu.VMEM(acc_shape, F32)] if nk > 1 else [])
    sem = ("parallel", "parallel", "arbitrary") if comm is None else ("arbitrary",) * 3
    return pl.pallas_call(
        body,
        name=name,
        grid=grid,
        in_specs=[a_spec, b_spec] + [s for _, s in extras] + c_in_specs,
        out_specs=list(out_specs) + c_out_specs,
        out_shape=list(out_shapes) + c_shapes,
        input_output_aliases={2 + n_ex + t: n_out + t for t in range(n_c)} if in_place else {},
        scratch_shapes=scratch,
        compiler_params=_params(sem),
    )(a, b, *[e for e, _ in extras], *c_in)


def _store(scale, dtype):
    def epilogue(acc, ex, outs, ids):
        v = acc if scale == 1.0 else acc * scale
        outs[0][...] = v.astype(dtype)

    return epilogue


def _mm_nn_full(name, a, b, out_dtype, tm_pref=1024, tn_pref=512):
    m, kd = a.shape
    n = b.shape[1]
    tm, tn = _tile(m, tm_pref), _tile(n, tn_pref)
    return _matmul(
        name, "nn", (m // tm, n // tn, 1),
        a, pl.BlockSpec((tm, kd), lambda i, j, k: (i, 0)),
        b, pl.BlockSpec((kd, tn), lambda i, j, k: (0, j)),
        [], [jax.ShapeDtypeStruct((m, n), out_dtype)], [pl.BlockSpec((tm, tn), lambda i, j, k: (i, j))],
        None, _store(1.0, out_dtype))[0]


def _mm_nt_full(name, a, b, out_dtype, scale=1.0, tm_pref=1024, tn_pref=512):
    m, kd = a.shape
    n = b.shape[0]
    tm, tn = _tile(m, tm_pref), _tile(n, tn_pref)
    return _matmul(
        name, "nt", (m // tm, n // tn, 1),
        a, pl.BlockSpec((tm, kd), lambda i, j, k: (i, 0)),
        b, pl.BlockSpec((tn, kd), lambda i, j, k: (j, 0)),
        [], [jax.ShapeDtypeStruct((m, n), out_dtype)], [pl.BlockSpec((tm, tn), lambda i, j, k: (i, j))],
        None, _store(scale, out_dtype))[0]


def _mm_nt_k(name, a, b, out_dtype, tk_pref=1024, tm_pref=1024, tn_pref=1024):
    m, kd = a.shape
    n = b.shape[0]
    tm, tn, tk = _tile(m, tm_pref), _tile(n, tn_pref), _tile(kd, tk_pref)
    return _matmul(
        name, "nt", (m // tm, n // tn, kd // tk),
        a, pl.BlockSpec((tm, tk), lambda i, j, k: (i, k)),
        b, pl.BlockSpec((tn, tk), lambda i, j, k: (j, k)),
        [], [jax.ShapeDtypeStruct((m, n), out_dtype)], [pl.BlockSpec((tm, tn), lambda i, j, k: (i, j))],
        (tm, tn), _store(1.0, out_dtype))[0]


def _mm_tn(name, a, b, scale=1.0, tm_pref=1024, tn_pref=1024, tk_pref=1024, comm=None):
    t, m = a.shape
    n = b.shape[1]
    tm, tn, tk = _tile(m, tm_pref), _tile(n, tn_pref), _tile(t, tk_pref)
    res = _matmul(
        name, "tn", (m // tm, n // tn, t // tk),
        a, pl.BlockSpec((tk, tm), lambda i, j, k: (k, i)),
        b, pl.BlockSpec((tk, tn), lambda i, j, k: (k, j)),
        [], [jax.ShapeDtypeStruct((m, n), BF16)], [pl.BlockSpec((tm, tn), lambda i, j, k: (i, j))],
        (tm, tn), _store(scale, BF16), comm=comm)
    return res[0] if comm is None else (res[0], res[1:])


def _mm_residual(name, a, b, x, scale, tm_pref=1024, tn_pref=1024, tk_pref=1408, comm=None):
    m, kd = a.shape
    n = b.shape[1]
    tm, tn, tk = _tile(m, tm_pref), _tile(n, tn_pref), _tile(kd, tk_pref)

    def epilogue(acc, ex, outs, ids):
        outs[0][...] = ex[0][...] + scale * acc

    res = _matmul(
        name, "nn", (m // tm, n // tn, kd // tk),
        a, pl.BlockSpec((tm, tk), lambda i, j, k: (i, k)),
        b, pl.BlockSpec((tk, tn), lambda i, j, k: (k, j)),
        [(x, pl.BlockSpec((tm, tn), lambda i, j, k: (i, j)))],
        [jax.ShapeDtypeStruct((m, n), F32)], [pl.BlockSpec((tm, tn), lambda i, j, k: (i, j))],
        (tm, tn), epilogue, comm=comm)
    return res[0] if comm is None else (res[0], res[1:])


def _rms_fwd(name, x, g, with_transpose=False):
    t, d = x.shape
    tm = _tile(t, 512)

    def body(x_ref, g_ref, h_ref, *ht_ref):
        xv = x_ref[...]
        r = lax.rsqrt(jnp.mean(xv * xv, axis=-1, keepdims=True) + EPS)
        h = xv * r * g_ref[...]
        h_ref[...] = h.astype(BF16)
        if with_transpose:
            ht_ref[0][...] = h.T.astype(BF16)

    out_specs = [pl.BlockSpec((tm, d), lambda i: (i, 0))]
    out_shape = [jax.ShapeDtypeStruct((t, d), BF16)]
    if with_transpose:
        out_specs.append(pl.BlockSpec((d, tm), lambda i: (0, i)))
        out_shape.append(jax.ShapeDtypeStruct((d, t), BF16))
    res = pl.pallas_call(
        body, name=name, grid=(t // tm,),
        in_specs=[pl.BlockSpec((tm, d), lambda i: (i, 0)), pl.BlockSpec((1, d), lambda i: (0, 0))],
        out_specs=out_specs, out_shape=out_shape,
        compiler_params=_params(("parallel",)),
    )(x, g.reshape(1, d))
    return res if with_transpose else res[0]


def _rms_bwd_math(dh, xv, g):
    r = lax.rsqrt(jnp.mean(xv * xv, axis=-1, keepdims=True) + EPS)
    xhat = xv * r
    dxh = dh * g
    dx = r * (dxh - xhat * jnp.mean(dxh * xhat, axis=-1, keepdims=True))
    return dx, dh * xhat


def _rms_bwd(name, dh, x, g, dres):
    t, d = x.shape
    tm = _tile(t, 256)

    def body(dh_ref, x_ref, g_ref, dres_ref, dx_ref, dxb_ref, dg_ref):
        dx, dgt = _rms_bwd_math(dh_ref[...].astype(F32), x_ref[...], g_ref[...])
        dx = dres_ref[...] + dx
        dx_ref[...] = dx
        dxb_ref[...] = dx.astype(BF16)

        @pl.when(pl.program_id(0) == 0)
        def _():
            dg_ref[...] = jnp.zeros_like(dg_ref)

        dg_ref[...] += jnp.sum(dgt, axis=0, keepdims=True)

    row = pl.BlockSpec((tm, d), lambda i: (i, 0))
    vec = pl.BlockSpec((1, d), lambda i: (0, 0))
    return pl.pallas_call(
        body, name=name, grid=(t // tm,),
        in_specs=[row, row, vec, row],
        out_specs=[row, row, vec],
        out_shape=[jax.ShapeDtypeStruct((t, d), F32), jax.ShapeDtypeStruct((t, d), BF16),
                   jax.ShapeDtypeStruct((1, d), F32)],
        compiler_params=_params(("arbitrary",)),
    )(dh, x, g.reshape(1, d), dres)


def _loss_bwd(x, g, target):
    t, d = x.shape
    tm = _tile(t, 256)

    def body(x_ref, g_ref, tgt_ref, loss_ref, dx_ref, dxb_ref, dg_ref):
        xv = x_ref[...]
        gv = g_ref[...]
        r = lax.rsqrt(jnp.mean(xv * xv, axis=-1, keepdims=True) + EPS)
        err = xv * r * gv - tgt_ref[...]
        part = 0.5 * jnp.sum(jnp.mean(err * err, axis=-1, keepdims=True), axis=0, keepdims=True)
        dx, dgt = _rms_bwd_math(err * (1.0 / d), xv, gv)
        dx_ref[...] = dx
        dxb_ref[...] = dx.astype(BF16)

        @pl.when(pl.program_id(0) == 0)
        def _():
            dg_ref[...] = jnp.zeros_like(dg_ref)
            loss_ref[...] = jnp.zeros_like(loss_ref)

        dg_ref[...] += jnp.sum(dgt, axis=0, keepdims=True)
        loss_ref[...] += jnp.broadcast_to(part, loss_ref.shape)

    row = pl.BlockSpec((tm, d), lambda i: (i, 0))
    vec = pl.BlockSpec((1, d), lambda i: (0, 0))
    return pl.pallas_call(
        body, name="loss_bwd", grid=(t // tm,),
        in_specs=[row, vec, row],
        out_specs=[pl.BlockSpec((1, LANE), lambda i: (0, 0)), row, row, vec],
        out_shape=[jax.ShapeDtypeStruct((1, LANE), F32), jax.ShapeDtypeStruct((t, d), F32),
                   jax.ShapeDtypeStruct((t, d), BF16), jax.ShapeDtypeStruct((1, d), F32)],
        compiler_params=_params(("arbitrary",)),
    )(x, g.reshape(1, d), target)


def _sigmoid(x):
    return 0.5 * jnp.tanh(0.5 * x) + 0.5


def _ffn_up(name, h, w_in, comm=None):
    t, d = h.shape
    fs = w_in.shape[2]
    f = 2 * fs
    tm, tn = _tile(t, 1024), _tile(fs, 256)
    per = fs // tn
    grid = (t // tm, f // tn)
    c_in, c_in_specs, c_out_specs, c_shapes, c_sems, in_place = _comm_io(comm)
    n_c, n_co = len(c_in), len(c_shapes)

    def body(*refs):
        h_ref, wg_ref, wu_ref = refs[:3]
        gu_ref, z_ref = refs[3 + n_c:5 + n_c]
        if comm is not None:
            i, j = pl.program_id(0), pl.program_id(1)
            send, recv = refs[5 + n_c + n_co:]
            last, finish = _comm_run(comm, refs[3:3 + n_c], refs[5 + n_c:5 + n_c + n_co], send, recv,
                                     jnp.logical_and(i == 0, j == 0),
                                     jnp.logical_and(i == grid[0] - 1, j == grid[1] - 1))
        hv = h_ref[...]
        gate = _dot(hv, wg_ref[...], "nn")
        up = _dot(hv, wu_ref[...], "nn")
        gu_ref[0] = gate.astype(BF16)
        gu_ref[1] = up.astype(BF16)
        z_ref[...] = (gate * _sigmoid(gate) * up).astype(BF16)
        if comm is not None:
            pl.when(last)(finish)

    res = pl.pallas_call(
        body, name=name, grid=grid,
        in_specs=[pl.BlockSpec((tm, d), lambda i, j: (i, 0)),
                  pl.BlockSpec((None, d, tn), lambda i, j: (j // per, 0, j % per)),
                  pl.BlockSpec((None, d, tn), lambda i, j: (2 + j // per, 0, j % per))] + c_in_specs,
        out_specs=[pl.BlockSpec((2, tm, tn), lambda i, j: (0, i, j)),
                   pl.BlockSpec((tm, tn), lambda i, j: (i, j))] + c_out_specs,
        out_shape=[jax.ShapeDtypeStruct((2, t, f), BF16), jax.ShapeDtypeStruct((t, f), BF16)] + c_shapes,
        input_output_aliases={3 + k: 2 + k for k in range(n_c)} if in_place else {},
        scratch_shapes=c_sems,
        compiler_params=_params(("parallel", "parallel") if comm is None else ("arbitrary", "arbitrary")),
    )(h, w_in, w_in, *c_in)
    return (res[0], res[1]) if comm is None else (res[0], res[1], res[2:])


def _ffn_dact(name, dxb, w_out, gu):
    t, d = dxb.shape
    f = w_out.shape[0]
    tm, tn = _tile(t, 1024), _tile(f, 256)

    def epilogue(acc, ex, outs, ids):
        dz = 0.5 * acc
        gate = ex[0][0].astype(F32)
        up = ex[0][1].astype(F32)
        sg = _sigmoid(gate)
        outs[0][0] = (dz * up * (sg * (1.0 + gate * (1.0 - sg)))).astype(BF16)
        outs[0][1] = (dz * gate * sg).astype(BF16)

    blk = pl.BlockSpec((2, tm, tn), lambda i, j, k: (0, i, j))
    return _matmul(
        name, "nt", (t // tm, f // tn, 1),
        dxb, pl.BlockSpec((tm, d), lambda i, j, k: (i, 0)),
        w_out, pl.BlockSpec((tn, d), lambda i, j, k: (j, 0)),
        [(gu, blk)], [jax.ShapeDtypeStruct((2, t, f), BF16)], [blk], None, epilogue)[0]


def _grad_colsharded(name, ht, da, comm=None):
    d, t = ht.shape
    w = da.shape[2]
    ws = w // 2
    tm, tn, tk = _tile(d, 1024), _tile(ws, 1408), _tile(t, 2048)
    per = ws // tn
    res = _matmul(
        name, "nn", (d // tm, (2 * w) // tn, t // tk),
        ht, pl.BlockSpec((tm, tk), lambda i, j, k: (i, k)),
        da, pl.BlockSpec((None, tk, tn), lambda i, j, k: (j // (2 * per), k, j % (2 * per))),
        [], [jax.ShapeDtypeStruct((N_CHIPS, d, ws), BF16)],
        [pl.BlockSpec((None, tm, tn), lambda i, j, k: (j // per, i, j % per))],
        (tm, tn), _store(1.0, BF16), comm=comm)
    return res[0] if comm is None else (res[0], res[1:])


def _back_colsharded(name, da, w_g, comm=None):
    _, t, w = da.shape
    d, ws = w_g.shape[1], w_g.shape[2]
    tm, tn, tk = _tile(t, 1024), _tile(d, 1024), _tile(ws, 2816)
    per = ws // tk
    res = _matmul(
        name, "nt", (t // tm, d // tn, (2 * w) // tk),
        da, pl.BlockSpec((None, tm, tk), lambda i, j, k: (k // (2 * per), i, k % (2 * per))),
        w_g, pl.BlockSpec((None, tn, tk), lambda i, j, k: (k // per, j, k % per)),
        [], [jax.ShapeDtypeStruct((t, d), F32)], [pl.BlockSpec((tm, tn), lambda i, j, k: (i, j))],
        (tm, tn), _store(1.0, F32), comm=comm)
    return res[0] if comm is None else (res[0], res[1:])


def _ffn_fwd(tag, x, g, w_in, w_out, prefetch):
    h, ht = _rms_fwd(tag + "_norm", x, g, with_transpose=True)
    if prefetch:
        gu, z, prefetch = _ffn_up(tag + "_up", h, w_in, comm=_gather_phase(1, prefetch))
        y, prefetch = _mm_residual(tag + "_down", z, w_out, x, 0.5, comm=_gather_phase(2, prefetch))
    else:
        gu, z = _ffn_up(tag + "_up", h, w_in)
        y = _mm_residual(tag + "_down", z, w_out, x, 0.5)
    return y, (x, ht, gu, z), prefetch


def _ffn_bwd(tag, saved, g, w_in, w_out, dx, dxb, red):
    x, ht, gu, z = saved
    f = z.shape[1]
    d_w_out = red.behind(0, _mm_tn, tag + "_dwout", z, dxb, scale=0.5, tm_pref=1408, tn_pref=2048)
    da = _ffn_dact(tag + "_dact", dxb, w_out, gu)
    d_w_in = red.behind(1, _grad_colsharded, tag + "_dwin", ht, da)
    dh = red.behind(2, _back_colsharded, tag + "_dh", da, w_in)
    dx, dxb, dg = _rms_bwd(tag + "_dnorm", dh, x, g, dx)
    return dx, dxb, dg, d_w_in, d_w_out.reshape(N_CHIPS, f // N_CHIPS, -1)


_GELU_K = math.sqrt(2.0 / math.pi)
_GELU_C = 0.044715


def _gelu(x):
    t = jnp.tanh(_GELU_K * (x + _GELU_C * x * x * x))
    return 0.5 * x * (1.0 + t), t


def _dgelu(x, t):
    return 0.5 * (1.0 + t) + 0.5 * x * (1.0 - t * t) * (_GELU_K * (1.0 + 3.0 * _GELU_C * x * x))


def _causal_block_mask():
    r = lax.broadcasted_iota(jnp.int32, (SGU_BLOCK, SGU_BLOCK), 0) // CHUNK
    c = lax.broadcasted_iota(jnp.int32, (SGU_BLOCK, SGU_BLOCK), 1) // CHUNK
    return r >= c


def _sgu_pre(name, h, w_in, comm=None):
    t, d = h.shape
    ws = w_in.shape[2]
    w = 2 * ws
    tm, tn = _tile(t, 1024), _tile(ws, 512)
    per = ws // tn
    res = _matmul(
        name, "nn", (t // tm, (2 * w) // tn, 1),
        h, pl.BlockSpec((tm, d), lambda i, j, k: (i, 0)),
        w_in, pl.BlockSpec((None, d, tn), lambda i, j, k: (j // per, 0, j % per)),
        [], [jax.ShapeDtypeStruct((2, t, w), BF16)],
        [pl.BlockSpec((None, tm, tn), lambda i, j, k: (j // (2 * per), i, j % (2 * per)))],
        None, _store(1.0, BF16), comm=comm)
    return res[0] if comm is None else (res[0], res[1:])


def _layernorm_stats(v):
    mu = jnp.mean(v, axis=-1, keepdims=True)
    vc = v - mu
    rstd = lax.rsqrt(jnp.mean(vc * vc, axis=-1, keepdims=True) + EPS)
    return vc * rstd, rstd


def _sgu_mid_fwd(pre, gain, bias, w_sp, b_sp_t):
    _, t, w = pre.shape
    gd = w // SGU_GROUPS

    def body(pre_ref, gain_ref, bias_ref, ws_ref, bt_ref, out_ref):
        mask = _causal_block_mask()
        u, _ = _gelu(pre_ref[0].astype(F32))
        v, _ = _gelu(pre_ref[1].astype(F32))
        vhat, _ = _layernorm_stats(v)
        vln = (vhat * gain_ref[...] + bias_ref[...]).astype(BF16)
        for gi in range(SGU_GROUPS):
            cols = slice(gi * gd, (gi + 1) * gd)
            wg = jnp.where(mask, ws_ref[gi], 0.0).astype(BF16)
            mixed = _dot(wg, vln[:, cols], "nn") + bt_ref[:, gi:gi + 1]
            out_ref[:, cols] = (u[:, cols] * mixed).astype(BF16)

    return pl.pallas_call(
        body, name="sgu_mid_fwd", grid=(t // SGU_BLOCK,),
        in_specs=[pl.BlockSpec((2, SGU_BLOCK, w), lambda n: (0, n, 0)),
                  pl.BlockSpec((1, w), lambda n: (0, 0)), pl.BlockSpec((1, w), lambda n: (0, 0)),
                  pl.BlockSpec((SGU_GROUPS, SGU_BLOCK, SGU_BLOCK), lambda n: (0, 0, 0)),
                  pl.BlockSpec((SGU_BLOCK, SGU_GROUPS), lambda n: (0, 0))],
        out_specs=pl.BlockSpec((SGU_BLOCK, w), lambda n: (n, 0)),
        out_shape=jax.ShapeDtypeStruct((t, w), BF16),
        compiler_params=_params(("parallel",)),
    )(pre, gain, bias, w_sp, b_sp_t)


def _sgu_mid_bwd(pre, dgated, gain, bias, w_sp, b_sp_t):
    _, t, w = pre.shape
    gd = w // SGU_GROUPS

    def body(pre_ref, dg_ref, gain_ref, bias_ref, ws_ref, bt_ref,
             dpre_ref, dgain_ref, dbias_ref, dws_ref, dbt_ref, dvln_s):
        @pl.when(pl.program_id(0) == 0)
        def _():
            dgain_ref[...] = jnp.zeros_like(dgain_ref)
            dbias_ref[...] = jnp.zeros_like(dbias_ref)
            dws_ref[...] = jnp.zeros_like(dws_ref)
            dbt_ref[...] = jnp.zeros_like(dbt_ref)

        mask = _causal_block_mask()
        pu = pre_ref[0].astype(F32)
        pv = pre_ref[1].astype(F32)
        u, tu = _gelu(pu)
        v, tv = _gelu(pv)
        vhat, rstd = _layernorm_stats(v)
        gain_v = gain_ref[...]
        vln = (vhat * gain_v + bias_ref[...]).astype(BF16)
        dgt = dg_ref[...].astype(F32)
        for gi in range(SGU_GROUPS):
            cols = slice(gi * gd, (gi + 1) * gd)
            wg = jnp.where(mask, ws_ref[gi], 0.0).astype(BF16)
            vg = vln[:, cols]
            mixed = _dot(wg, vg, "nn") + bt_ref[:, gi:gi + 1]
            dgg = dgt[:, cols]
            dmixed = dgg * u[:, cols]
            dmb = dmixed.astype(BF16)
            dpre_ref[0, :, cols] = (dgg * mixed * _dgelu(pu[:, cols], tu[:, cols])).astype(BF16)
            dbt_ref[:, gi:gi + 1] += jnp.sum(dmixed, axis=1, keepdims=True)
            dws_ref[gi] += jnp.where(mask, _dot(dmb, vg, "nt"), 0.0)
            dvln_s[:, cols] = _dot(wg, dmb, "tn")
        dvln = dvln_s[...]
        dgain_ref[...] += jnp.sum(dvln * vhat, axis=0, keepdims=True)
        dbias_ref[...] += jnp.sum(dvln, axis=0, keepdims=True)
        dvh = dvln * gain_v
        dv = rstd * (dvh - jnp.mean(dvh, axis=-1, keepdims=True)
                     - vhat * jnp.mean(dvh * vhat, axis=-1, keepdims=True))
        dpre_ref[1] = (dv * _dgelu(pv, tv)).astype(BF16)

    vec = pl.BlockSpec((1, w), lambda n: (0, 0))
    wsb = pl.BlockSpec((SGU_GROUPS, SGU_BLOCK, SGU_BLOCK), lambda n: (0, 0, 0))
    btb = pl.BlockSpec((SGU_BLOCK, SGU_GROUPS), lambda n: (0, 0))
    blk2 = pl.BlockSpec((2, SGU_BLOCK, w), lambda n: (0, n, 0))
    return pl.pallas_call(
        body, name="sgu_mid_bwd", grid=(t // SGU_BLOCK,),
        in_specs=[blk2, pl.BlockSpec((SGU_BLOCK, w), lambda n: (n, 0)), vec, vec, wsb, btb],
        out_specs=[blk2, vec, vec, wsb, btb],
        out_shape=[jax.ShapeDtypeStruct((2, t, w), BF16), jax.ShapeDtypeStruct((1, w), F32),
                   jax.ShapeDtypeStruct((1, w), F32),
                   jax.ShapeDtypeStruct((SGU_GROUPS, SGU_BLOCK, SGU_BLOCK), F32),
                   jax.ShapeDtypeStruct((SGU_BLOCK, SGU_GROUPS), F32)],
        scratch_shapes=[pltpu.VMEM((SGU_BLOCK, w), F32)],
        compiler_params=_params(("arbitrary",)),
    )(pre, dgated, gain, bias, w_sp, b_sp_t)


def _sgu_fwd(x, g, w_in, gain, bias, w_sp, b_sp, w_out, prefetch):
    h, ht = _rms_fwd("sgu_norm", x, g, with_transpose=True)
    pre, prefetch = _sgu_pre("sgu_pre", h, w_in, comm=_gather_phase(1, prefetch))
    gated = _sgu_mid_fwd(pre, gain, bias, w_sp, b_sp.T)
    y, prefetch = _mm_residual("sgu_out", gated, w_out, x, 1.0, tk_pref=1024, comm=_gather_phase(2, prefetch))
    return y, (x, ht, pre, gated), prefetch


def _sgu_bwd(saved, g, w_in, gain, bias, w_sp, b_sp, w_out, dx, dxb, red):
    x, ht, pre, gated = saved
    w = gated.shape[1]
    d_w_out = red.behind(0, _mm_tn, "sgu_dwout", gated, dxb)
    dgated = _mm_nt_full("sgu_dgated", dxb, w_out, BF16)
    dpre, dgain, dbias, dws, dbt = _sgu_mid_bwd(pre, dgated, gain, bias, w_sp, b_sp.T)
    d_w_in = red.behind(1, _grad_colsharded, "sgu_dwin", ht, dpre)
    dh = red.behind(2, _back_colsharded, "sgu_dh", dpre, w_in)
    dx, dxb, dg = _rms_bwd("sgu_dnorm", dh, x, g, dx)
    small = dict(ln=dg, gain=dgain, bias=dbias, w_sp=dws, b_sp=dbt.T)
    return dx, dxb, small, d_w_in, d_w_out.reshape(N_CHIPS, w // N_CHIPS, -1)


def _rope_tables(positions):
    half = QK_ROPE // 2
    inv_freq = 1.0 / (ROPE_THETA ** (jnp.arange(half, dtype=F32) / half))
    ang = positions.astype(F32)[:, None] * inv_freq
    cos, sin = jnp.cos(ang), jnp.sin(ang)
    t = positions.shape[0]
    zeros = jnp.zeros((t, half), F32)
    rest = jnp.zeros((t, LANE - QK_ROPE), F32)
    c = jnp.concatenate([cos, cos, rest + 1.0], axis=1)
    s_up = jnp.concatenate([zeros, sin, rest], axis=1)
    s_dn = jnp.concatenate([-sin, zeros, rest], axis=1)
    return c, s_up, s_dn


def _rope_apply(x, c, s_up, s_dn):
    half = QK_ROPE // 2
    return x * c + pltpu.roll(x, half, 1) * s_up + pltpu.roll(x, LANE - half, 1) * s_dn


def _rope_apply_t(dy, c, s_up, s_dn):
    half = QK_ROPE // 2
    return dy * c - pltpu.roll(dy, LANE - half, 1) * s_dn - pltpu.roll(dy, half, 1) * s_up


def _mla_norm_fwd(proj, gq, gkv):
    t, p = proj.shape
    ql, kvl = gq.shape[1], gkv.shape[1]
    tm = _tile(t, 512)

    def body(p_ref, gq_ref, gkv_ref, qn_ref, kvn_ref):
        for lo, n, g_ref, o_ref in ((0, ql, gq_ref, qn_ref), (ql, kvl, gkv_ref, kvn_ref)):
            xv = p_ref[:, lo:lo + n]
            r = lax.rsqrt(jnp.mean(xv * xv, axis=-1, keepdims=True) + EPS)
            o_ref[...] = (xv * r * g_ref[...]).astype(BF16)

    return pl.pallas_call(
        body, name="mla_norm_fwd", grid=(t // tm,),
        in_specs=[pl.BlockSpec((tm, p), lambda i: (i, 0)), pl.BlockSpec((1, ql), lambda i: (0, 0)),
                  pl.BlockSpec((1, kvl), lambda i: (0, 0))],
        out_specs=[pl.BlockSpec((tm, ql), lambda i: (i, 0)), pl.BlockSpec((tm, kvl), lambda i: (i, 0))],
        out_shape=[jax.ShapeDtypeStruct((t, ql), BF16), jax.ShapeDtypeStruct((t, kvl), BF16)],
        compiler_params=_params(("parallel",)),
    )(proj, gq, gkv)


def _mla_norm_bwd(proj, dqn, dkvn, dkr, gq, gkv):
    t, p = proj.shape
    ql, kvl = gq.shape[1], gkv.shape[1]
    tm = _tile(t, 256)

    def body(p_ref, dqn_ref, dkvn_ref, dkr_ref, gq_ref, gkv_ref, dp_ref, dgq_ref, dgkv_ref):
        @pl.when(pl.program_id(0) == 0)
        def _():
            dgq_ref[...] = jnp.zeros_like(dgq_ref)
            dgkv_ref[...] = jnp.zeros_like(dgkv_ref)

        for lo, n, g_ref, d_ref, dg_ref in ((0, ql, gq_ref, dqn_ref, dgq_ref),
                                             (ql, kvl, gkv_ref, dkvn_ref, dgkv_ref)):
            dx, dgt = _rms_bwd_math(d_ref[...], p_ref[:, lo:lo + n], g_ref[...])
            dp_ref[:, lo:lo + n] = dx.astype(BF16)
            dg_ref[...] += jnp.sum(dgt, axis=0, keepdims=True)
        dp_ref[:, ql + kvl:] = dkr_ref[...].astype(BF16)

    def row(n):
        return pl.BlockSpec((tm, n), lambda i: (i, 0))

    def vec(n):
        return pl.BlockSpec((1, n), lambda i: (0, 0))

    return pl.pallas_call(
        body, name="mla_norm_bwd", grid=(t // tm,),
        in_specs=[row(p), row(ql), row(kvl), row(LANE), vec(ql), vec(kvl)],
        out_specs=[row(p), vec(ql), vec(kvl)],
        out_shape=[jax.ShapeDtypeStruct((t, p), BF16), jax.ShapeDtypeStruct((1, ql), F32),
                   jax.ShapeDtypeStruct((1, kvl), F32)],
        compiler_params=_params(("arbitrary",)),
    )(proj, dqn, dkvn, dkr, gq, gkv)


def _mla_q_up(qn, wq, tables):
    t, ql = qn.shape
    n = wq.shape[1]
    tm = _tile(t, 1024)
    scale = QK_DIM ** -0.5

    def epilogue(acc, ex, outs, ids):
        outs[0][:, :QK_NOPE] = (scale * acc[:, :QK_NOPE]).astype(BF16)
        hi = _rope_apply(acc[:, QK_NOPE:], ex[0][...], ex[1][...], ex[2][...])
        outs[0][:, QK_NOPE:] = (scale * hi).astype(BF16)

    tab = pl.BlockSpec((tm, LANE), lambda i, j, k: (i, 0))
    return _matmul(
        "mla_q_up", "nn", (t // tm, n // HEAD_PAD, 1),
        qn, pl.BlockSpec((tm, ql), lambda i, j, k: (i, 0)),
        wq, pl.BlockSpec((ql, HEAD_PAD), lambda i, j, k: (0, j)),
        [(tb, tab) for tb in tables],
        [jax.ShapeDtypeStruct((t, n), BF16)], [pl.BlockSpec((tm, HEAD_PAD), lambda i, j, k: (i, j))],
        None, epilogue)[0]


def _mla_kv_up(kvn, wkv, proj, tables, heads):
    t, kvl = kvn.shape
    n = wkv.shape[1]
    p = proj.shape[1]
    tm = _tile(t, 1024)

    def epilogue(acc, ex, outs, ids):
        kr = _rope_apply(ex[0][...], ex[1][...], ex[2][...], ex[3][...])
        outs[0][:, :QK_NOPE] = acc[:, :QK_NOPE].astype(BF16)
        outs[0][:, QK_NOPE:] = (acc[:, QK_NOPE:] + jnp.where(ids[1] < heads, kr, 1.0)).astype(BF16)

    tab = pl.BlockSpec((tm, LANE), lambda i, j, k: (i, 0))
    kr_spec = pl.BlockSpec((tm, LANE), lambda i, j, k: (i, p // LANE - 1))
    return _matmul(
        "mla_kv_up", "nn", (t // tm, n // HEAD_PAD, 1),
        kvn, pl.BlockSpec((tm, kvl), lambda i, j, k: (i, 0)),
        wkv, pl.BlockSpec((kvl, HEAD_PAD), lambda i, j, k: (0, j)),
        [(proj, kr_spec)] + [(tb, tab) for tb in tables],
        [jax.ShapeDtypeStruct((t, n), BF16)], [pl.BlockSpec((tm, HEAD_PAD), lambda i, j, k: (i, j))],
        None, epilogue)[0]


FLASH_TQ = 512
FLASH_TK = 1024


def _chunk_mask(tq, tk, qi, ki):
    r = (qi * tq + lax.broadcasted_iota(jnp.int32, (tq, tk), 0)) // CHUNK
    c = (ki * tk + lax.broadcasted_iota(jnp.int32, (tq, tk), 1)) // CHUNK
    return c <= r


def _block_pairs(t, tq, tk, key_major):
    def visible(qi, ki):
        return (ki * tk) // CHUNK <= (qi * tq + tq - 1) // CHUNK

    def masked(qi, ki):
        return (ki * tk + tk - 1) // CHUNK > (qi * tq) // CHUNK

    nq, nk = t // tq, t // tk
    if key_major:
        sweeps = [[(qi, ki) for qi in range(nq) if visible(qi, ki)] for ki in range(nk)]
    else:
        sweeps = [[(qi, ki) for ki in range(nk) if visible(qi, ki)] for qi in range(nq)]
    qs, ks, fs = [], [], []
    for sweep in sweeps:
        for n, (qi, ki) in enumerate(sweep):
            qs.append(qi)
            ks.append(ki)
            fs.append((1 if masked(qi, ki) else 0) + (2 if n == 0 else 0) + (4 if n == len(sweep) - 1 else 0))
    return tuple(jnp.asarray(v, jnp.int32) for v in (qs, ks, fs))


def _flash_fwd(qp, kv, heads):
    t = qp.shape[0]
    tq, tk = _tile(t, FLASH_TQ), _tile(t, FLASH_TK)
    rep = tk // LANE
    qt, kt, ft = _block_pairs(t, tq, tk, key_major=False)

    def body(qt_ref, kt_ref, ft_ref, q_ref, k_ref, v_ref, o_ref, lse_ref, m_s, acc_s):
        p = pl.program_id(1)
        qi, ki, flags = qt_ref[p], kt_ref[p], ft_ref[p]

        @pl.when(flags & 2 != 0)
        def _():
            m_s[...] = jnp.full_like(m_s, -1e30)
            acc_s[...] = jnp.zeros_like(acc_s)

        def step(masked):
            s = _dot(q_ref[...], k_ref[...], "nt")
            if masked:
                s = jnp.where(_chunk_mask(tq, tk, qi, ki), s, -1e30)
            m_prev = m_s[...]
            m_new = jnp.maximum(m_prev, jnp.max(s, axis=1, keepdims=True))
            alpha = jnp.exp(m_prev - m_new)
            pr = jnp.exp(s - jnp.tile(m_new, (1, rep))).astype(BF16)
            pv = _dot(pr, v_ref[...], "nn")
            acc_s[:, :V_DIM] = alpha * acc_s[:, :V_DIM] + pv[:, :V_DIM]
            acc_s[:, V_DIM:] = alpha * acc_s[:, V_DIM:] + pv[:, V_DIM:]
            m_s[...] = m_new

        @pl.when(flags & 1 == 0)
        def _():
            step(False)

        @pl.when(flags & 1 != 0)
        def _():
            step(True)

        @pl.when(flags & 4 != 0)
        def _():
            l = acc_s[:, V_DIM:]
            o_ref[...] = (acc_s[:, :V_DIM] / l).astype(BF16)
            lse_ref[...] = m_s[...] + jnp.log(l)

    return pl.pallas_call(
        body, name="mla_flash_fwd",
        grid_spec=pltpu.PrefetchScalarGridSpec(
            num_scalar_prefetch=3, grid=(heads, int(qt.shape[0])),
            in_specs=[pl.BlockSpec((tq, HEAD_PAD), lambda h, p, qt, kt, ft: (qt[p], h)),
                      pl.BlockSpec((tk, HEAD_PAD), lambda h, p, qt, kt, ft: (kt[p], h)),
                      pl.BlockSpec((tk, HEAD_PAD), lambda h, p, qt, kt, ft: (kt[p], heads + h))],
            out_specs=[pl.BlockSpec((tq, V_DIM), lambda h, p, qt, kt, ft: (qt[p], h)),
                       pl.BlockSpec((None, tq, LANE), lambda h, p, qt, kt, ft: (h, qt[p], 0))],
            scratch_shapes=[pltpu.VMEM((tq, LANE), F32), pltpu.VMEM((tq, HEAD_PAD), F32)]),
        out_shape=[jax.ShapeDtypeStruct((t, heads * V_DIM), BF16),
                   jax.ShapeDtypeStruct((heads, t, LANE), F32)],
        compiler_params=_params(("parallel", "arbitrary")),
    )(qt, kt, ft, qp, kv, kv)


def _flash_delta(o, do, heads):
    t = o.shape[0]
    tm = _tile(t, 256)

    def body(o_ref, do_ref, d_ref):
        for h in range(heads):
            cols = slice(h * V_DIM, (h + 1) * V_DIM)
            prod = o_ref[:, cols].astype(F32) * do_ref[:, cols].astype(F32)
            d_ref[h] = jnp.broadcast_to(jnp.sum(prod, axis=1, keepdims=True), (tm, LANE))

    row = pl.BlockSpec((tm, heads * V_DIM), lambda i: (i, 0))
    return pl.pallas_call(
        body, name="mla_flash_delta", grid=(t // tm,), in_specs=[row, row],
        out_specs=pl.BlockSpec((heads, tm, LANE), lambda i: (0, i, 0)),
        out_shape=jax.ShapeDtypeStruct((heads, t, LANE), F32),
        compiler_params=_params(("parallel",)),
    )(o, do)


def _flash_bwd(qp, kv, do, lse, delta, heads):
    t = qp.shape[0]
    tq, tk = _tile(t, FLASH_TQ), _tile(t, FLASH_TK)
    rep = tk // LANE
    qt, kt, ft = _block_pairs(t, tq, tk, key_major=True)

    def body(qt_ref, kt_ref, ft_ref, q_ref, k_ref, v_ref, do_ref, lse_ref, dl_ref, dq_ref, dk_ref, dv_ref,
             dk_s, dv_s):
        p = pl.program_id(1)
        qi, ki, flags = qt_ref[p], kt_ref[p], ft_ref[p]

        @pl.when(p == 0)
        def _():
            dq_ref[...] = jnp.zeros_like(dq_ref)

        @pl.when(flags & 2 != 0)
        def _():
            dk_s[...] = jnp.zeros_like(dk_s)
            dv_s[...] = jnp.zeros_like(dv_s)

        def step(masked):
            q = q_ref[...]
            k = k_ref[...]
            dov = do_ref[...]
            s = _dot(q, k, "nt")
            pr = jnp.exp(s - jnp.tile(lse_ref[...], (1, rep)))
            if masked:
                pr = jnp.where(_chunk_mask(tq, tk, qi, ki), pr, 0.0)
            dv_s[...] += _dot(pr.astype(BF16), dov, "tn")
            dp = _dot(dov, v_ref[...], "nt")
            ds = (pr * (dp - jnp.tile(dl_ref[...], (1, rep)))).astype(BF16)
            rows = pl.ds(pl.multiple_of(qi * tq, tq), tq)
            dq_ref[rows, :] += _dot(ds, k, "nn")
            dk_s[...] += _dot(ds, q, "tn")

        @pl.when(flags & 1 == 0)
        def _():
            step(False)

        @pl.when(flags & 1 != 0)
        def _():
            step(True)

        @pl.when(flags & 4 != 0)
        def _():
            dk_ref[...] = dk_s[...]
            dv_ref[...] = dv_s[...]

    def qrow(width):
        return pl.BlockSpec((tq, width), lambda h, p, qt, kt, ft: (qt[p], h))

    def stat():
        return pl.BlockSpec((None, tq, LANE), lambda h, p, qt, kt, ft: (h, qt[p], 0))

    return pl.pallas_call(
        body, name="mla_flash_bwd",
        grid_spec=pltpu.PrefetchScalarGridSpec(
            num_scalar_prefetch=3, grid=(heads, int(qt.shape[0])),
            in_specs=[qrow(HEAD_PAD),
                      pl.BlockSpec((tk, HEAD_PAD), lambda h, p, qt, kt, ft: (kt[p], h)),
                      pl.BlockSpec((tk, V_DIM), lambda h, p, qt, kt, ft: (kt[p], 2 * (heads + h))),
                      qrow(V_DIM), stat(), stat()],
            out_specs=[pl.BlockSpec((t, HEAD_PAD), lambda h, p, qt, kt, ft: (0, h)),
                       pl.BlockSpec((tk, HEAD_PAD), lambda h, p, qt, kt, ft: (kt[p], h)),
                       pl.BlockSpec((tk, V_DIM), lambda h, p, qt, kt, ft: (kt[p], h))],
            scratch_shapes=[pltpu.VMEM((tk, HEAD_PAD), F32), pltpu.VMEM((tk, V_DIM), F32)]),
        out_shape=[jax.ShapeDtypeStruct((t, heads * HEAD_PAD), F32),
                   jax.ShapeDtypeStruct((t, heads * HEAD_PAD), F32),
                   jax.ShapeDtypeStruct((t, heads * V_DIM), F32)],
        compiler_params=_params(("parallel", "arbitrary")),
    )(qt, kt, ft, qp, kv, kv, do, lse, delta)


def _mla_attn_post(dqp, dkp, dv, tables, heads):
    t = dqp.shape[0]
    tm = _tile(t, 256)
    scale = QK_DIM ** -0.5
    kw, vw = heads * HEAD_PAD, heads * V_DIM

    def body(dq_ref, dk_ref, dv_ref, c_ref, su_ref, sd_ref, dqb_ref, dkvb_ref, dkr_ref):
        c, su, sd = c_ref[...], su_ref[...], sd_ref[...]
        kr = jnp.zeros((tm, LANE), F32)
        for h in range(heads):
            lo = h * HEAD_PAD
            mid = lo + QK_NOPE
            dqb_ref[:, lo:mid] = (scale * dq_ref[:, lo:mid]).astype(BF16)
            dqb_ref[:, mid:mid + LANE] = (scale * _rope_apply_t(dq_ref[:, mid:mid + LANE], c, su, sd)).astype(BF16)
            kr = kr + dk_ref[:, mid:mid + LANE]
            dkvb_ref[:, kw + lo:kw + mid] = dv_ref[:, h * V_DIM:(h + 1) * V_DIM].astype(BF16)
            dkvb_ref[:, kw + mid:kw + lo + HEAD_PAD] = jnp.zeros((tm, HEAD_PAD - V_DIM), BF16)
        dkvb_ref[:, :kw] = dk_ref[...].astype(BF16)
        dkr_ref[...] = _rope_apply_t(kr, c, su, sd)

    def row(n):
        return pl.BlockSpec((tm, n), lambda i: (i, 0))

    return pl.pallas_call(
        body, name="mla_attn_post", grid=(t // tm,),
        in_specs=[row(kw), row(kw), row(vw), row(LANE), row(LANE), row(LANE)],
        out_specs=[row(kw), row(2 * kw), row(LANE)],
        out_shape=[jax.ShapeDtypeStruct((t, kw), BF16), jax.ShapeDtypeStruct((t, 2 * kw), BF16),
                   jax.ShapeDtypeStruct((t, LANE), F32)],
        compiler_params=_params(("parallel",)),
    )(dqp, dkp, dv, *tables)


def _mla_weights(w_in_g, w_q_g, w_kv_g, w_out_g):
    d = w_in_g.shape[0] * w_in_g.shape[1]
    pw = w_in_g.shape[2]
    w_in = jnp.pad(w_in_g.reshape(d, pw), ((0, 0), (0, LANE - QK_ROPE)))
    ql = w_q_g.shape[1]
    wq = jnp.transpose(w_q_g, (1, 0, 2)).reshape(ql, -1, QK_DIM)
    heads = wq.shape[1]
    wq = jnp.pad(wq, ((0, 0), (0, 0), (0, HEAD_PAD - QK_DIM))).reshape(ql, heads * HEAD_PAD)
    kvl = w_kv_g.shape[1]
    wkv = jnp.transpose(w_kv_g, (1, 0, 2)).reshape(kvl, heads, QK_NOPE + V_DIM)
    wk = jnp.pad(wkv[:, :, :QK_NOPE], ((0, 0), (0, 0), (0, HEAD_PAD - QK_NOPE))).reshape(kvl, heads * HEAD_PAD)
    wv = jnp.pad(wkv[:, :, QK_NOPE:], ((0, 0), (0, 0), (0, HEAD_PAD - V_DIM))).reshape(kvl, heads * HEAD_PAD)
    return w_in, wq, jnp.concatenate([wk, wv], axis=1), w_out_g.reshape(heads * V_DIM, -1), heads


def _mla_unpermute(d_w_in, d_wq, d_wkv, heads, pw):
    d = d_w_in.shape[0]
    g_in = d_w_in[:, :pw].reshape(N_CHIPS, d // N_CHIPS, pw)
    ql = d_wq.shape[0]
    g_q = d_wq.reshape(ql, heads, HEAD_PAD)[:, :, :QK_DIM].reshape(ql, N_CHIPS, -1)
    kvl = d_wkv.shape[0]
    g_k = d_wkv[:, :heads * HEAD_PAD].reshape(kvl, heads, HEAD_PAD)[:, :, :QK_NOPE]
    g_v = d_wkv[:, heads * HEAD_PAD:].reshape(kvl, heads, HEAD_PAD)[:, :, :V_DIM]
    g_kv = jnp.concatenate([g_k, g_v], axis=2).reshape(kvl, N_CHIPS, -1)
    return g_in, jnp.transpose(g_q, (1, 0, 2)), jnp.transpose(g_kv, (1, 0, 2))


def _mla_fwd(x, g, wts, gq, gkv, tables):
    w_in, wq, wkv, w_out, heads = wts
    h = _rms_fwd("mla_norm", x, g)
    proj = _mm_nn_full("mla_proj", h, w_in, F32, tn_pref=w_in.shape[1])
    qn, kvn = _mla_norm_fwd(proj, gq, gkv)
    qp = _mla_q_up(qn, wq, tables)
    kv = _mla_kv_up(kvn, wkv, proj, tables, heads)
    o, lse = _flash_fwd(qp, kv, heads)
    y = _mm_residual("mla_out", o, w_out, x, 1.0, tk_pref=2048)
    return y, (x, h, proj, qn, kvn, qp, kv, o, lse)


def _mla_bwd(saved, g, wts, gq, gkv, tables, dx, dxb, pw):
    w_in, wq, wkv, w_out, heads = wts
    x, h, proj, qn, kvn, qp, kv, o, lse = saved
    d_w_out = _mm_tn("mla_dwout", o, dxb)
    do = _mm_nt_full("mla_do", dxb, w_out, BF16)
    dqp, dkp, dv = _flash_bwd(qp, kv, do, lse, _flash_delta(o, do, heads), heads)
    dqb, dkvb, dkr = _mla_attn_post(dqp, dkp, dv, tables, heads)
    d_wq = _mm_tn("mla_dwq", qn, dqb)
    dqn = _mm_nt_k("mla_dqn", dqb, wq, F32)
    d_wkv = _mm_tn("mla_dwkv", kvn, dkvb)
    dkvn = _mm_nt_k("mla_dkvn", dkvb, wkv, F32)
    dproj, dgq, dgkv = _mla_norm_bwd(proj, dqn, dkvn, dkr, gq, gkv)
    d_w_in = _mm_tn("mla_dwin", h, dproj, tn_pref=dproj.shape[1])
    dh = _mm_nt_full("mla_dh", dproj, w_in, F32, tn_pref=1024)
    dx, dxb, dg = _rms_bwd("mla_dnorm", dh, x, g, dx)
    g_in, g_q, g_kv = _mla_unpermute(d_w_in, d_wq, d_wkv, heads, pw)
    small = dict(ln=dg, gq=dgq, gkv=dgkv)
    return dx, dxb, small, g_in, g_q, g_kv, d_w_out.reshape(N_CHIPS, d_w_out.shape[0] // N_CHIPS, -1)


def _gather_weights(bufs, norms):
    nt = len(bufs)

    def body(*refs):
        n_in = refs[nt]
        outs, n_out = refs[nt + 1:2 * nt + 1], refs[2 * nt + 1]
        send, recv, fsend, frecv, loc, nsend, nrecv = refs[2 * nt + 2:]
        x, y, c, chips = _place()
        me = 2 * x + y
        sib = (x, y, 1 - c)

        local = pltpu.make_async_copy(n_in, n_out.at[me], loc)
        local.start()

        def place(t, chip, half):
            return outs[t].at[2 * chip[0] + chip[1], half]

        def ici(t, j, chip):
            return pltpu.make_async_remote_copy(
                src_ref=place(t, (x, y), c), dst_ref=place(t, (x, y), c), send_sem=send.at[t, j],
                recv_sem=recv.at[t, j], device_id=(*chip, c), device_id_type=MESH)

        def fwd(t, j, chip, half):
            return pltpu.make_async_remote_copy(
                src_ref=place(t, chip, half), dst_ref=place(t, chip, half), send_sem=fsend.at[t, j],
                recv_sem=frecv.at[t, j], device_id=sib, device_id_type=MESH)

        def nrm(j, chip, owner):
            return pltpu.make_async_remote_copy(
                src_ref=n_in, dst_ref=n_out.at[2 * owner[0] + owner[1]], send_sem=nsend.at[j], recv_sem=nrecv.at[j],
                device_id=(*chip, c), device_id_type=MESH)

        firsts = [ici(t, j, chip) for t in range(nt) for j, chip in enumerate(chips)]
        firsts += [nrm(j, chip, (x, y)) for j, chip in enumerate(chips)]
        for cp in firsts:
            cp.start()
        passed = []
        for t in range(nt):
            for j, chip in enumerate(chips):
                pltpu.make_async_remote_copy(
                    src_ref=place(t, chip, c), dst_ref=place(t, chip, c), send_sem=send.at[t, j],
                    recv_sem=recv.at[t, j], device_id=(*chip, c), device_id_type=MESH).wait_recv()
                cp = fwd(t, j, chip, c)
                cp.start()
                passed.append(cp)
        for t in range(nt):
            for j, chip in enumerate(chips):
                fwd(t, j, chip, 1 - c).wait_recv()
        for j, chip in enumerate(chips):
            nrm(j, chip, chip).wait_recv()
        for cp in firsts + passed:
            cp.wait_send()
        local.wait()

    out_shape = [jax.ShapeDtypeStruct(b.shape, b.dtype) for b in bufs]
    out_shape.append(jax.ShapeDtypeStruct((N_CHIPS,) + norms.shape, norms.dtype))
    res = pl.pallas_call(
        body, name="gather_weights",
        in_specs=[ANY] * (nt + 1), out_specs=[ANY] * (nt + 1), out_shape=out_shape,
        input_output_aliases={t: t for t in range(nt)},
        scratch_shapes=[pltpu.SemaphoreType.DMA((nt, 3)), pltpu.SemaphoreType.DMA((nt, 3)),
                        pltpu.SemaphoreType.DMA((nt, 3)), pltpu.SemaphoreType.DMA((nt, 3)),
                        pltpu.SemaphoreType.DMA, pltpu.SemaphoreType.DMA((3,)),
                        pltpu.SemaphoreType.DMA((3,))],
    )(*bufs, norms)
    return res[:nt], res[nt]


def _run_phase(name, phase):
    c_in, in_specs, out_specs, shapes, sems, in_place = _comm_io(phase)
    n_c, n_co = len(c_in), len(shapes)

    def body(*refs):
        ins, outs = refs[:n_c], refs[n_c:n_c + n_co]
        send, recv = refs[n_c + n_co:]
        for cp in phase.copies(ins, outs, send, recv, False):
            cp.start()
        for cp in phase.copies(ins, outs, send, recv, True):
            cp.wait_recv()
        for cp in phase.copies(ins, outs, send, recv, False):
            cp.wait_send()

    return pl.pallas_call(
        body, name=name, in_specs=in_specs, out_specs=out_specs, out_shape=shapes,
        input_output_aliases={t: t for t in range(n_c)} if in_place else {}, scratch_shapes=sems,
    )(*c_in)


def _remote(ref_src, ref_dst, send, recv, t, j, to):
    return pltpu.make_async_remote_copy(src_ref=ref_src, dst_ref=ref_dst, send_sem=send.at[t, j],
                                        recv_sem=recv.at[t, j], device_id=to, device_id_type=MESH)


class _Reduce:
    def __init__(self, units, grads, n_layers, reduced, me_idx, core_idx):
        self.units, self.n_layers, self.reduced = list(units), n_layers, reduced
        self.me_idx, self.core_idx = me_idx, core_idx
        self.local = [g.reshape(N_CHIPS, 2, g.shape[1] // 2, g.shape[2]) for g in grads]
        self.parts = None

    def _swap(self):
        fresh = [jax.ShapeDtypeStruct((g.shape[0],) + g.shape[2:], g.dtype) for g in self.local]

        def copies(ins, outs, send, recv, landing):
            x, y, c, _ = _place()
            return [_remote(outs[t] if landing else ins[t].at[:, 1 - c], outs[t], send, recv, t, 0, (x, y, 1 - c))
                    for t in range(len(ins))]

        return _Phase(self.local, fresh, (len(self.local), 1), copies)

    def _scatter(self):
        fresh = [jax.ShapeDtypeStruct((3,) + p.shape[1:], p.dtype) for p in self.parts]

        def copies(ins, outs, send, recv, landing):
            x, y, c, chips = _place()
            return [_remote(outs[t].at[j] if landing else ins[t].at[2 * chip[0] + chip[1]], outs[t].at[j],
                            send, recv, t, j, (*chip, c))
                    for t in range(len(ins)) for j, chip in enumerate(chips)]

        return _Phase(self.parts, fresh, (len(self.parts), 3), copies)

    def _join(self):
        layers = [l for _, l in self.units]

        def copies(ins, outs, send, recv, landing):
            x, y, c, _ = _place()
            refs = [outs[t].at[l, 1 - c if landing else c] for t, l in enumerate(layers)]
            return [_remote(r, r, send, recv, t, 0, (x, y, 1 - c)) for t, r in enumerate(refs)]

        return _Phase([self.reduced[n] for n, _ in self.units], [], (len(self.units), 1), copies)

    def _after(self, step, got):
        if step == 0:
            self.parts = [_add_halves(self.core_idx, g, o) for g, o in zip(self.local, got)]
        elif step == 1:
            for (n, l), p, ld in zip(self.units, self.parts, got):
                self.reduced[n] = _sum_chips(self.me_idx, self.core_idx, p, ld, l, self.n_layers[n],
                                             self.reduced.get(n))
        else:
            for (n, _), joined in zip(self.units, got):
                self.reduced[n] = joined

    def _phase(self, step):
        return (self._swap, self._scatter, self._join)[step]()

    def behind(self, step, fn, *args, **kw):
        if not self.units:
            return fn(*args, **kw)
        out, got = fn(*args, comm=self._phase(step), **kw)
        self._after(step, got)
        return out

    def alone(self):
        for step, name in enumerate(("grad_swap_halves", "grad_scatter_chips", "grad_join_halves")):
            self._after(step, _run_phase(name, self._phase(step)))


def _gather_all(block):
    m_per, n = block.shape

    def body(x_ref, out_ref, send_sems, recv_sems, local_sem):
        x, y, c, chips = _place()
        me, sibling = (x, y, c), (x, y, 1 - c)

        def rows(px, py, pc):
            return out_ref.at[pl.ds((4 * px + 2 * py + pc) * m_per, m_per), :]

        def copy(k, block_of, to, src=None):
            return pltpu.make_async_remote_copy(
                src_ref=rows(*block_of) if src is None else src, dst_ref=rows(*block_of),
                send_sem=send_sems.at[k], recv_sem=recv_sems.at[k], device_id=to, device_id_type=MESH)

        mine = pltpu.make_async_copy(x_ref, rows(*me), local_sem)
        mine.start()
        first = [copy(0, me, sibling, src=x_ref)]
        first += [copy(1 + j, me, (*chip, c), src=x_ref) for j, chip in enumerate(chips)]
        for cp in first:
            cp.start()
        passed = [copy(4 + j, (*chip, c), sibling) for j, chip in enumerate(chips)]
        for j, chip in enumerate(chips):
            copy(1 + j, (*chip, c), me).wait_recv()
            passed[j].start()
        copy(0, sibling, me).wait_recv()
        for j, chip in enumerate(chips):
            copy(4 + j, (*chip, 1 - c), me).wait_recv()
        for cp in first + passed:
            cp.wait_send()
        mine.wait()

    return pl.pallas_call(
        body, name="gather_small_grads",
        out_shape=jax.ShapeDtypeStruct((N_DEV * m_per, n), block.dtype),
        in_specs=[pl.BlockSpec(memory_space=pltpu.VMEM)],
        out_specs=pl.BlockSpec(memory_space=pltpu.VMEM),
        scratch_shapes=[pltpu.SemaphoreType.DMA((7,)), pltpu.SemaphoreType.DMA((7,)), pltpu.SemaphoreType.DMA],
    )(block)


def _row_tile(r, c, elems=512 * 1024):
    t = max(8, min(r, (elems // c) // 8 * 8))
    while t > 8 and r % t:
        t -= 8
    return t if r % t == 0 else r


def _add_halves(idx, grad, other):
    n, _, r, w = grad.shape
    tr = _row_tile(r, w)

    def body(idx_ref, g_ref, o_ref, out_ref):
        out_ref[...] = (g_ref[...].astype(F32) + o_ref[...].astype(F32)).astype(BF16)

    return pl.pallas_call(
        body, name="grad_add_halves",
        grid_spec=pltpu.PrefetchScalarGridSpec(
            num_scalar_prefetch=1, grid=(n, r // tr),
            in_specs=[pl.BlockSpec((None, None, tr, w), lambda k, i, idx: (k, idx[0], i, 0)),
                      pl.BlockSpec((None, tr, w), lambda k, i, idx: (k, i, 0))],
            out_specs=pl.BlockSpec((None, tr, w), lambda k, i, idx: (k, i, 0))),
        out_shape=jax.ShapeDtypeStruct((n, r, w), BF16),
        compiler_params=_params(("parallel", "parallel")),
    )(idx, grad, other)


def _sum_chips(me_idx, core_idx, part, landed, layer, n_layers, prev):
    _, r, w = part.shape
    tr = _row_tile(r, w)

    def body(me_ref, c_ref, p_ref, l_ref, *rest):
        acc = p_ref[...].astype(F32)
        for j in range(3):
            acc = acc + l_ref[j].astype(F32)
        rest[-1][...] = acc

    return pl.pallas_call(
        body, name="grad_sum_chips",
        grid_spec=pltpu.PrefetchScalarGridSpec(
            num_scalar_prefetch=2, grid=(r // tr,),
            in_specs=[pl.BlockSpec((None, tr, w), lambda i, me, c: (me[0], i, 0)),
                      pl.BlockSpec((3, tr, w), lambda i, me, c: (0, i, 0))] + ([] if prev is None else [ANY]),
            out_specs=pl.BlockSpec((None, None, tr, w), lambda i, me, c: (layer, c[0], i, 0))),
        out_shape=jax.ShapeDtypeStruct((n_layers, 2, r, w), F32),
        input_output_aliases={} if prev is None else {4: 0},
        compiler_params=_params(("parallel",)),
    )(me_idx, core_idx, part, landed, *([] if prev is None else [prev]))


def _adamw_math(w, g, m, v):
    m = ADAM_B1 * m + (1.0 - ADAM_B1) * g
    v = ADAM_B2 * v + (1.0 - ADAM_B2) * (g * g)
    m_hat = m / (1.0 - ADAM_B1 ** ADAM_STEP)
    v_hat = v / (1.0 - ADAM_B2 ** ADAM_STEP)
    delta = -ADAM_LR * (m_hat / (jnp.sqrt(v_hat) + ADAM_EPS) + ADAM_WD * w)
    return delta, m, v


def _adamw(name, w, g, m, v):
    r, c = w.shape
    tr = _row_tile(r, c, 256 * 1024)

    def body(w_ref, g_ref, m_ref, v_ref, d_ref, nm_ref, nv_ref):
        d_ref[...], nm_ref[...], nv_ref[...] = _adamw_math(w_ref[...], g_ref[...], m_ref[...], v_ref[...])

    blk = pl.BlockSpec((tr, c), lambda i: (i, 0))
    return pl.pallas_call(
        body, name=name, grid=(r // tr,), in_specs=[blk] * 4, out_specs=[blk] * 3,
        out_shape=[jax.ShapeDtypeStruct((r, c), F32)] * 3,
        compiler_params=_params(("parallel",)),
    )(w, g, m, v)


def _adamw_summed(w, parts, m, v):
    r, c = w.shape

    def body(w_ref, p_ref, m_ref, v_ref, g_ref, d_ref, nm_ref, nv_ref):
        g = p_ref[0:r, :]
        for k in range(1, N_DEV):
            g = g + p_ref[k * r:(k + 1) * r, :]
        g_ref[...] = g
        d_ref[...], nm_ref[...], nv_ref[...] = _adamw_math(w_ref[...], g, m_ref[...], v_ref[...])

    return pl.pallas_call(
        body, name="adamw_replicated",
        out_shape=[jax.ShapeDtypeStruct((r, c), F32)] * 4,
        compiler_params=pltpu.CompilerParams(vmem_limit_bytes=VMEM_LIMIT_BYTES),
    )(w, parts, m, v)


def _pack(arrays):
    return jnp.concatenate([a.reshape(-1, LANE) for a in arrays], axis=0)


def _unpack(packed, shapes):
    out, row = [], 0
    for s in shapes:
        n = math.prod(s) // LANE
        out.append(packed[row:row + n].reshape(s))
        row += n
    return out


def _cast_into(name, idx, w, layer):
    _, rows, c = w.shape
    r = rows // 2
    tr = _row_tile(r, c)
    per = r // tr

    def body(idx_ref, w_ref, o_ref):
        o_ref[...] = w_ref[...].astype(BF16)

    return pl.pallas_call(
        body, name=name,
        grid_spec=pltpu.PrefetchScalarGridSpec(
            num_scalar_prefetch=1, grid=(2, per),
            in_specs=[pl.BlockSpec((None, tr, c), lambda h, i, idx: (layer, h * per + i, 0))],
            out_specs=pl.BlockSpec((None, None, tr, c), lambda h, i, idx: (idx[0], h, i, 0))),
        out_shape=jax.ShapeDtypeStruct((N_CHIPS, 2, r, c), BF16),
        compiler_params=_params(("parallel", "parallel")),
    )(idx, w)


BIG = ["ffn1_w_in", "ffn1_w_out", "ffn2_w_in", "ffn2_w_out", "sgu_w_in", "sgu_w_out",
       "mla_w_in", "mla_w_q_up", "mla_w_kv_up", "mla_w_out"]
STAGES = [
    [("ffn1_w_in", 0), ("ffn1_w_out", 0)],
    [("sgu_w_in", 0), ("sgu_w_out", 0)],
    [("ffn2_w_in", 0), ("ffn2_w_out", 0)],
    [("ffn1_w_in", 1), ("ffn1_w_out", 1)],
    [("mla_w_in", 0), ("mla_w_q_up", 0), ("mla_w_kv_up", 0), ("mla_w_out", 0)],
    [("ffn2_w_in", 1), ("ffn2_w_out", 1)],
]
REPLICATED = ["ln_ffn1", "ln_mix", "ln_ffn2", "sgu_v_gain", "sgu_v_bias", "sgu_w_spatial", "sgu_b_spatial",
              "ln_final"]
NORM_SHARDS = ["mla_q_norm", "mla_kv_norm"]
WEIGHTS = ["ln_ffn1", "ffn1_w_in", "ffn1_w_out", "ln_mix", "ln_ffn2", "ffn2_w_in", "ffn2_w_out", "sgu_w_in",
           "sgu_v_gain", "sgu_v_bias", "sgu_w_spatial", "sgu_b_spatial", "sgu_w_out", "mla_w_in", "mla_q_norm",
           "mla_w_q_up", "mla_kv_norm", "mla_w_kv_up", "mla_w_out", "ln_final"]


def kernel(x, positions, ln_ffn1, ffn1_w_in, ffn1_w_out, ln_mix, ln_ffn2, ffn2_w_in, ffn2_w_out, sgu_w_in, sgu_v_gain, sgu_v_bias, sgu_w_spatial, sgu_b_spatial, sgu_w_out, mla_w_in, mla_q_norm, mla_w_q_up, mla_kv_norm, mla_w_kv_up, mla_w_out, ln_final, loss_target, m_ln_ffn1, m_ffn1_w_in, m_ffn1_w_out, m_ln_mix, m_ln_ffn2, m_ffn2_w_in, m_ffn2_w_out, m_sgu_w_in, m_sgu_v_gain, m_sgu_v_bias, m_sgu_w_spatial, m_sgu_b_spatial, m_sgu_w_out, m_mla_w_in, m_mla_q_norm, m_mla_w_q_up, m_mla_kv_norm, m_mla_w_kv_up, m_mla_w_out, m_ln_final, v_ln_ffn1, v_ffn1_w_in, v_ffn1_w_out, v_ln_mix, v_ln_ffn2, v_ffn2_w_in, v_ffn2_w_out, v_sgu_w_in, v_sgu_v_gain, v_sgu_v_bias, v_sgu_w_spatial, v_sgu_b_spatial, v_sgu_w_out, v_mla_w_in, v_mla_q_norm, v_mla_w_q_up, v_mla_kv_norm, v_mla_w_kv_up, v_mla_w_out, v_ln_final):
    given = dict(locals())
    w = {n: given[n] for n in WEIGHTS}
    mom = {n: given["m_" + n] for n in WEIGHTS}
    var = {n: given["v_" + n] for n in WEIGHTS}
    t, d = x.shape[1], x.shape[2]
    xs = x.reshape(t, d)
    target = loss_target.reshape(t, d)
    me = 2 * lax.axis_index("x") + lax.axis_index("y")

    me_idx = jnp.reshape(me, (1,)).astype(jnp.int32)
    core_idx = jnp.reshape(lax.axis_index("c"), (1,)).astype(jnp.int32)

    bufs = {(n, l): _cast_into(f"cast_{n}_{l}", me_idx, w[n], l) for n in BIG for l in range(w[n].shape[0])}
    nq = mla_q_norm.shape[1]
    norms = jnp.pad(jnp.concatenate([mla_q_norm, mla_kv_norm], axis=0), ((0, 6), (0, LANE - nq)))

    def take(stage):
        return [bufs[u] for u in stage]

    def put(stage, arrays):
        bufs.update(zip(stage, arrays))

    def full(unit):
        a = bufs[unit]
        return a.reshape(N_CHIPS, a.shape[1] * a.shape[2], a.shape[3])

    def rows(unit):
        a = bufs[unit]
        return a.reshape(-1, a.shape[-1])

    first, norms_g = _gather_weights(take(STAGES[0]), norms)
    put(STAGES[0], first)
    gq = norms_g[:, 0, :nq].reshape(1, N_CHIPS * nq)
    gkv = norms_g[:, 1, :nq].reshape(1, N_CHIPS * nq)
    tables = _rope_tables(positions.reshape(t))
    sgu_small = (sgu_v_gain, sgu_v_bias, sgu_w_spatial[0], sgu_b_spatial[0])

    a0, s_f1_0, got = _ffn_fwd("l0_ffn1", xs, ln_ffn1[0], full(("ffn1_w_in", 0)), rows(("ffn1_w_out", 0)),
                               take(STAGES[1]))
    put(STAGES[1], got)
    a1, s_sgu, got = _sgu_fwd(a0, ln_mix[0], full(("sgu_w_in", 0)), *sgu_small, rows(("sgu_w_out", 0)),
                              take(STAGES[2]))
    put(STAGES[2], got)
    a2, s_f2_0, got = _ffn_fwd("l0_ffn2", a1, ln_ffn2[0], full(("ffn2_w_in", 0)), rows(("ffn2_w_out", 0)),
                               take(STAGES[3]))
    put(STAGES[3], got)
    a3, s_f1_1, got = _ffn_fwd("l1_ffn1", a2, ln_ffn1[1], full(("ffn1_w_in", 1)), rows(("ffn1_w_out", 1)),
                               take(STAGES[4] + STAGES[5]))
    put(STAGES[4] + STAGES[5], got)
    mla_wts = _mla_weights(full(("mla_w_in", 0)), full(("mla_w_q_up", 0)), full(("mla_w_kv_up", 0)),
                           full(("mla_w_out", 0)))
    a4, s_mla = _mla_fwd(a3, ln_mix[1], mla_wts, gq, gkv, tables)
    a5, s_f2_1, _ = _ffn_fwd("l1_ffn2", a4, ln_ffn2[1], full(("ffn2_w_in", 1)), rows(("ffn2_w_out", 1)), [])

    gr, reduced = {}, {}
    n_layers = {n: w[n].shape[0] for n in BIG}

    def reduce_of(stages):
        units = [u for s in stages for u in STAGES[s]]
        return _Reduce(units, [gr[u] for u in units], n_layers, reduced, me_idx, core_idx)

    loss_part, dx, dxb, dg_final = _loss_bwd(a5, ln_final, target)
    dx, dxb, dg_f2_1, gr["ffn2_w_in", 1], gr["ffn2_w_out", 1] = _ffn_bwd(
        "l1_ffn2", s_f2_1, ln_ffn2[1], full(("ffn2_w_in", 1)), rows(("ffn2_w_out", 1)), dx, dxb, reduce_of([]))
    (dx, dxb, sm_mla, gr["mla_w_in", 0], gr["mla_w_q_up", 0], gr["mla_w_kv_up", 0],
     gr["mla_w_out", 0]) = _mla_bwd(s_mla, ln_mix[1], mla_wts, gq, gkv, tables, dx, dxb, mla_w_in.shape[2])
    dx, dxb, dg_f1_1, gr["ffn1_w_in", 1], gr["ffn1_w_out", 1] = _ffn_bwd(
        "l1_ffn1", s_f1_1, ln_ffn1[1], full(("ffn1_w_in", 1)), rows(("ffn1_w_out", 1)), dx, dxb, reduce_of([5, 4]))
    dx, dxb, dg_f2_0, gr["ffn2_w_in", 0], gr["ffn2_w_out", 0] = _ffn_bwd(
        "l0_ffn2", s_f2_0, ln_ffn2[0], full(("ffn2_w_in", 0)), rows(("ffn2_w_out", 0)), dx, dxb, reduce_of([3]))
    dx, dxb, sm_sgu, gr["sgu_w_in", 0], gr["sgu_w_out", 0] = _sgu_bwd(
        s_sgu, ln_mix[0], full(("sgu_w_in", 0)), *sgu_small, rows(("sgu_w_out", 0)), dx, dxb, reduce_of([2]))
    dx, dxb, dg_f1_0, gr["ffn1_w_in", 0], gr["ffn1_w_out", 0] = _ffn_bwd(
        "l0_ffn1", s_f1_0, ln_ffn1[0], full(("ffn1_w_in", 0)), rows(("ffn1_w_out", 0)), dx, dxb, reduce_of([1]))
    reduce_of([0]).alone()
    big_grad = {n: reduced[n].reshape(w[n].shape) for n in BIG}

    small_parts = [
        jnp.concatenate([dg_f1_0, dg_f1_1], axis=0), jnp.concatenate([sm_sgu["ln"], sm_mla["ln"]], axis=0),
        jnp.concatenate([dg_f2_0, dg_f2_1], axis=0), sm_sgu["gain"], sm_sgu["bias"], sm_sgu["w_sp"],
        sm_sgu["b_sp"], dg_final]
    rep_shapes = [w[n].shape for n in REPLICATED]
    gq_row = jnp.pad(sm_mla["gq"], ((0, 0), (0, N_CHIPS * (LANE - nq))))
    gkv_row = jnp.pad(sm_mla["gkv"], ((0, 0), (0, N_CHIPS * (LANE - nq))))
    packed = _pack(small_parts + [gq_row, gkv_row, loss_part])
    packed = jnp.pad(packed, ((0, -packed.shape[0] % 8), (0, 0)))
    everyone = _gather_all(packed)
    rows = packed.shape[0]
    zero_rows = jnp.zeros((rows - sum(math.prod(s) // LANE for s in rep_shapes), LANE), F32)
    pw = jnp.concatenate([_pack([w[n] for n in REPLICATED]), zero_rows], axis=0)
    pm = jnp.concatenate([_pack([mom[n] for n in REPLICATED]), zero_rows], axis=0)
    pv = jnp.concatenate([_pack([var[n] for n in REPLICATED]), zero_rows + 1.0], axis=0)
    g_all, d_all, m_all, v_all = _adamw_summed(pw, everyone, pm, pv)
    tail_shapes = [(1, N_CHIPS * LANE), (1, N_CHIPS * LANE), (1, LANE)]
    rep_grad = dict(zip(REPLICATED, _unpack(g_all, rep_shapes + tail_shapes)[:len(REPLICATED)]))
    rep_delta = dict(zip(REPLICATED, _unpack(d_all, rep_shapes)))
    rep_m = dict(zip(REPLICATED, _unpack(m_all, rep_shapes)))
    rep_v = dict(zip(REPLICATED, _unpack(v_all, rep_shapes)))
    tail = _unpack(g_all, rep_shapes + tail_shapes)[len(REPLICATED):]
    loss = tail[2][0, 0]
    norm_grad = {
        "mla_q_norm": lax.dynamic_slice(tail[0], (0, me * nq), (1, nq)),
        "mla_kv_norm": lax.dynamic_slice(tail[1], (0, me * nq), (1, nq)),
    }

    grad, delta, new_m, new_v = {}, {}, {}, {}
    for n in BIG:
        shp = w[n].shape
        flat = lambda a: a.reshape(-1, shp[-1])
        dl, nm, nv = _adamw("adamw_" + n, flat(w[n]), flat(big_grad[n]), flat(mom[n]), flat(var[n]))
        grad[n], delta[n], new_m[n], new_v[n] = big_grad[n], dl.reshape(shp), nm.reshape(shp), nv.reshape(shp)
    for n in REPLICATED:
        grad[n], delta[n], new_m[n], new_v[n] = rep_grad[n], rep_delta[n], rep_m[n], rep_v[n]
    stack = lambda dct: jnp.concatenate([dct[n] for n in NORM_SHARDS], axis=0)
    dl, nm, nv = _adamw("adamw_norm_shards", stack(w), stack(norm_grad), stack(mom), stack(var))
    for i, n in enumerate(NORM_SHARDS):
        grad[n], delta[n], new_m[n], new_v[n] = norm_grad[n], dl[i:i + 1], nm[i:i + 1], nv[i:i + 1]

    grad_x = dx.reshape(x.shape)
    return (loss, grad_x, *[grad[n] for n in WEIGHTS], *[delta[n] for n in WEIGHTS],
            *[new_m[n] for n in WEIGHTS], *[new_v[n] for n in WEIGHTS])
```

```python
import functools
import math

import jax
import jax.numpy as jnp
from jax import lax
from jax.experimental import pallas as pl
from jax.experimental.pallas import tpu as pltpu

F32 = jnp.float32
BF16 = jnp.bfloat16
MESH = pl.DeviceIdType.MESH

EPS = 1e-6
CHUNK = 64
SGU_BLOCK = 128
SGU_GROUPS = 8
QK_NOPE = 128
QK_ROPE = 64
V_DIM = 128
QK_DIM = QK_NOPE + QK_ROPE
HEAD_PAD = 256
ROPE_THETA = 10000.0
N_CHIPS = 4
N_DEV = 8

ADAM_LR = 0.001
ADAM_B1 = 0.9
ADAM_B2 = 0.999
ADAM_EPS = 1e-08
ADAM_WD = 0.01
ADAM_STEP = 10

LANE = 128
VMEM_LIMIT_BYTES = 56 * 1024 * 1024

_DIMS = {
    "nn": (((1,), (0,)), ((), ())),
    "nt": (((1,), (1,)), ((), ())),
    "tn": (((0,), (0,)), ((), ())),
}


def _tile(n, pref):
    t = (min(pref, n) // LANE) * LANE
    while t >= LANE:
        if n % t == 0:
            return t
        t -= LANE
    return n


def _params(sem):
    return pltpu.CompilerParams(dimension_semantics=sem, vmem_limit_bytes=VMEM_LIMIT_BYTES)


def _dot(a, b, mode):
    return lax.dot_general(a, b, _DIMS[mode], preferred_element_type=F32)


def _place():
    x, y, c = lax.axis_index("x"), lax.axis_index("y"), lax.axis_index("c")
    chips = [(1 - x, y), (x, 1 - y), (1 - x, 1 - y)]
    return x, y, c, chips


ANY = pl.BlockSpec(memory_space=pl.ANY)


def _gather_copies(phase, bufs, send, recv, landing):
    x, y, c, chips = _place()
    copies = []
    for t, buf in enumerate(bufs):
        for j, chip in enumerate(chips):
            there = 2 * chip[0] + chip[1]
            if phase == 1:
                src, lands, to = buf.at[2 * x + y, c], buf.at[there, c], (*chip, c)
            else:
                src, lands, to = buf.at[there, c], buf.at[there, 1 - c], (x, y, 1 - c)
            ref = lands if landing else src
            copies.append(pltpu.make_async_remote_copy(
                src_ref=ref, dst_ref=ref, send_sem=send.at[t, j], recv_sem=recv.at[t, j], device_id=to,
                device_id_type=MESH))
    return copies


class _Phase:
    def __init__(self, ins, fresh, sems, copies):
        self.ins, self.fresh, self.sems, self.copies = list(ins), list(fresh), sems, copies


def _gather_phase(phase, bufs):
    return _Phase(bufs, [], (len(bufs), 3),
                  lambda ins, outs, send, recv, landing: _gather_copies(phase, outs, send, recv, landing))


def _comm_io(comm):
    if comm is None:
        return [], [], [], [], [], False
    shapes = comm.fresh or [jax.ShapeDtypeStruct(b.shape, b.dtype) for b in comm.ins]
    sems = [pltpu.SemaphoreType.DMA(comm.sems), pltpu.SemaphoreType.DMA(comm.sems)]
    return comm.ins, [ANY] * len(comm.ins), [ANY] * len(shapes), shapes, sems, not comm.fresh


def _comm_run(comm, in_refs, out_refs, send, recv, first, last):
    @pl.when(first)
    def _():
        for cp in comm.copies(in_refs, out_refs, send, recv, False):
            cp.start()

    def finish():
        for cp in comm.copies(in_refs, out_refs, send, recv, True):
            cp.wait_recv()
        for cp in comm.copies(in_refs, out_refs, send, recv, False):
            cp.wait_send()

    return last, finish


def _matmul(name, mode, grid, a, a_spec, b, b_spec, extras, out_shapes, out_specs, acc_shape, epilogue, comm=None):
    nk = grid[2]
    n_ex = len(extras)
    n_out = len(out_shapes)
    c_in, c_in_specs, c_out_specs, c_shapes, c_sems, in_place = _comm_io(comm)
    n_c, n_co = len(c_in), len(c_shapes)

    def body(*refs):
        a_ref, b_ref = refs[0], refs[1]
        ex = refs[2:2 + n_ex]
        outs = refs[2 + n_ex + n_c:2 + n_ex + n_c + n_out]
        ids = (pl.program_id(0), pl.program_id(1))
        k = pl.program_id(2)
        if comm is not None:
            c_ins = refs[2 + n_ex:2 + n_ex + n_c]
            c_outs = refs[2 + n_ex + n_c + n_out:2 + n_ex + n_c + n_out + n_co]
            send, recv = refs[2 + n_ex + n_c + n_out + n_co:2 + n_ex + n_c + n_out + n_co + 2]
            first = jnp.logical_and(jnp.logical_and(ids[0] == 0, ids[1] == 0), k == 0)
            last = jnp.logical_and(jnp.logical_and(ids[0] == grid[0] - 1, ids[1] == grid[1] - 1), k == nk - 1)
            last, finish = _comm_run(comm, c_ins, c_outs, send, recv, first, last)
        part = _dot(a_ref[...], b_ref[...], mode)
        if nk == 1:
            epilogue(part, ex, outs, ids)
        else:
            acc = refs[-1]

            @pl.when(k == 0)
            def _():
                acc[...] = part

            @pl.when(k > 0)
            def _():
                acc[...] += part

            @pl.when(k == nk - 1)
            def _():
                epilogue(acc[...], ex, outs, ids)

        if comm is not None:
            pl.when(last)(finish)

    scratch = c_sems + ([pltpu.VMEM(acc_shape, F32)] if nk > 1 else [])
    sem = ("parallel", "parallel", "arbitrary") if comm is None else ("arbitrary",) * 3
    return pl.pallas_call(
        body,
        name=name,
        grid=grid,
        in_specs=[a_spec, b_spec] + [s for _, s in extras] + c_in_specs,
        out_specs=list(out_specs) + c_out_specs,
        out_shape=list(out_shapes) + c_shapes,
        input_output_aliases={2 + n_ex + t: n_out + t for t in range(n_c)} if in_place else {},
        scratch_shapes=scratch,
        compiler_params=_params(sem),
    )(a, b, *[e for e, _ in extras], *c_in)


def _store(scale, dtype):
    def epilogue(acc, ex, outs, ids):
        v = acc if scale == 1.0 else acc * scale
        outs[0][...] = v.astype(dtype)

    return epilogue


def _mm_nn_full(name, a, b, out_dtype, tm_pref=1024, tn_pref=512):
    m, kd = a.shape
    n = b.shape[1]
    tm, tn = _tile(m, tm_pref), _tile(n, tn_pref)
    return _matmul(
        name, "nn", (m // tm, n // tn, 1),
        a, pl.BlockSpec((tm, kd), lambda i, j, k: (i, 0)),
        b, pl.BlockSpec((kd, tn), lambda i, j, k: (0, j)),
        [], [jax.ShapeDtypeStruct((m, n), out_dtype)], [pl.BlockSpec((tm, tn), lambda i, j, k: (i, j))],
        None, _store(1.0, out_dtype))[0]


def _mm_nt_full(name, a, b, out_dtype, scale=1.0, tm_pref=1024, tn_pref=512):
    m, kd = a.shape
    n = b.shape[0]
    tm, tn = _tile(m, tm_pref), _tile(n, tn_pref)
    return _matmul(
        name, "nt", (m // tm, n // tn, 1),
        a, pl.BlockSpec((tm, kd), lambda i, j, k: (i, 0)),
        b, pl.BlockSpec((tn, kd), lambda i, j, k: (j, 0)),
        [], [jax.ShapeDtypeStruct((m, n), out_dtype)], [pl.BlockSpec((tm, tn), lambda i, j, k: (i, j))],
        None, _store(scale, out_dtype))[0]


def _mm_nt_k(name, a, b, out_dtype, tk_pref=1024, tm_pref=1024, tn_pref=1024):
    m, kd = a.shape
    n = b.shape[0]
    tm, tn, tk = _tile(m, tm_pref), _tile(n, tn_pref), _tile(kd, tk_pref)
    return _matmul(
        name, "nt", (m // tm, n // tn, kd // tk),
        a, pl.BlockSpec((tm, tk), lambda i, j, k: (i, k)),
        b, pl.BlockSpec((tn, tk), lambda i, j, k: (j, k)),
        [], [jax.ShapeDtypeStruct((m, n), out_dtype)], [pl.BlockSpec((tm, tn), lambda i, j, k: (i, j))],
        (tm, tn), _store(1.0, out_dtype))[0]


def _mm_tn(name, a, b, scale=1.0, tm_pref=1024, tn_pref=1024, tk_pref=1024, comm=None):
    t, m = a.shape
    n = b.shape[1]
    tm, tn, tk = _tile(m, tm_pref), _tile(n, tn_pref), _tile(t, tk_pref)
    res = _matmul(
        name, "tn", (m // tm, n // tn, t // tk),
        a, pl.BlockSpec((tk, tm), lambda i, j, k: (k, i)),
        b, pl.BlockSpec((tk, tn), lambda i, j, k: (k, j)),
        [], [jax.ShapeDtypeStruct((m, n), BF16)], [pl.BlockSpec((tm, tn), lambda i, j, k: (i, j))],
        (tm, tn), _store(scale, BF16), comm=comm)
    return res[0] if comm is None else (res[0], res[1:])


def _mm_residual(name, a, b, x, scale, tm_pref=1024, tn_pref=1024, tk_pref=1408, comm=None):
    m, kd = a.shape
    n = b.shape[1]
    tm, tn, tk = _tile(m, tm_pref), _tile(n, tn_pref), _tile(kd, tk_pref)

    def epilogue(acc, ex, outs, ids):
        outs[0][...] = ex[0][...] + scale * acc

    res = _matmul(
        name, "nn", (m // tm, n // tn, kd // tk),
        a, pl.BlockSpec((tm, tk), lambda i, j, k: (i, k)),
        b, pl.BlockSpec((tk, tn), lambda i, j, k: (k, j)),
        [(x, pl.BlockSpec((tm, tn), lambda i, j, k: (i, j)))],
        [jax.ShapeDtypeStruct((m, n), F32)], [pl.BlockSpec((tm, tn), lambda i, j, k: (i, j))],
        (tm, tn), epilogue, comm=comm)
    return res[0] if comm is None else (res[0], res[1:])


def _rms_fwd(name, x, g, with_transpose=False):
    t, d = x.shape
    tm = _tile(t, 512)

    def body(x_ref, g_ref, h_ref, *ht_ref):
        xv = x_ref[...]
        r = lax.rsqrt(jnp.mean(xv * xv, axis=-1, keepdims=True) + EPS)
        h = xv * r * g_ref[...]
        h_ref[...] = h.astype(BF16)
        if with_transpose:
            ht_ref[0][...] = h.T.astype(BF16)

    out_specs = [pl.BlockSpec((tm, d), lambda i: (i, 0))]
    out_shape = [jax.ShapeDtypeStruct((t, d), BF16)]
    if with_transpose:
        out_specs.append(pl.BlockSpec((d, tm), lambda i: (0, i)))
        out_shape.append(jax.ShapeDtypeStruct((d, t), BF16))
    res = pl.pallas_call(
        body, name=name, grid=(t // tm,),
        in_specs=[pl.BlockSpec((tm, d), lambda i: (i, 0)), pl.BlockSpec((1, d), lambda i: (0, 0))],
        out_specs=out_specs, out_shape=out_shape,
        compiler_params=_params(("parallel",)),
    )(x, g.reshape(1, d))
    return res if with_transpose else res[0]


def _rms_bwd_math(dh, xv, g):
    r = lax.rsqrt(jnp.mean(xv * xv, axis=-1, keepdims=True) + EPS)
    xhat = xv * r
    dxh = dh * g
    dx = r * (dxh - xhat * jnp.mean(dxh * xhat, axis=-1, keepdims=True))
    return dx, dh * xhat


def _rms_bwd(name, dh, x, g, dres):
    t, d = x.shape
    tm = _tile(t, 256)

    def body(dh_ref, x_ref, g_ref, dres_ref, dx_ref, dxb_ref, dg_ref):
        dx, dgt = _rms_bwd_math(dh_ref[...].astype(F32), x_ref[...], g_ref[...])
        dx = dres_ref[...] + dx
        dx_ref[...] = dx
        dxb_ref[...] = dx.astype(BF16)

        @pl.when(pl.program_id(0) == 0)
        def _():
            dg_ref[...] = jnp.zeros_like(dg_ref)

        dg_ref[...] += jnp.sum(dgt, axis=0, keepdims=True)

    row = pl.BlockSpec((tm, d), lambda i: (i, 0))
    vec = pl.BlockSpec((1, d), lambda i: (0, 0))
    return pl.pallas_call(
        body, name=name, grid=(t // tm,),
        in_specs=[row, row, vec, row],
        out_specs=[row, row, vec],
        out_shape=[jax.ShapeDtypeStruct((t, d), F32), jax.ShapeDtypeStruct((t, d), BF16),
                   jax.ShapeDtypeStruct((1, d), F32)],
        compiler_params=_params(("arbitrary",)),
    )(dh, x, g.reshape(1, d), dres)


def _loss_bwd(x, g, target):
    t, d = x.shape
    tm = _tile(t, 256)

    def body(x_ref, g_ref, tgt_ref, loss_ref, dx_ref, dxb_ref, dg_ref):
        xv = x_ref[...]
        gv = g_ref[...]
        r = lax.rsqrt(jnp.mean(xv * xv, axis=-1, keepdims=True) + EPS)
        err = xv * r * gv - tgt_ref[...]
        part = 0.5 * jnp.sum(jnp.mean(err * err, axis=-1, keepdims=True), axis=0, keepdims=True)
        dx, dgt = _rms_bwd_math(err * (1.0 / d), xv, gv)
        dx_ref[...] = dx
        dxb_ref[...] = dx.astype(BF16)

        @pl.when(pl.program_id(0) == 0)
        def _():
            dg_ref[...] = jnp.zeros_like(dg_ref)
            loss_ref[...] = jnp.zeros_like(loss_ref)

        dg_ref[...] += jnp.sum(dgt, axis=0, keepdims=True)
        loss_ref[...] += jnp.broadcast_to(part, loss_ref.shape)

    row = pl.BlockSpec((tm, d), lambda i: (i, 0))
    vec = pl.BlockSpec((1, d), lambda i: (0, 0))
    return pl.pallas_call(
        body, name="loss_bwd", grid=(t // tm,),
        in_specs=[row, vec, row],
        out_specs=[pl.BlockSpec((1, LANE), lambda i: (0, 0)), row, row, vec],
        out_shape=[jax.ShapeDtypeStruct((1, LANE), F32), jax.ShapeDtypeStruct((t, d), F32),
                   jax.ShapeDtypeStruct((t, d), BF16), jax.ShapeDtypeStruct((1, d), F32)],
        compiler_params=_params(("arbitrary",)),
    )(x, g.reshape(1, d), target)


def _sigmoid(x):
    return 0.5 * jnp.tanh(0.5 * x) + 0.5


def _ffn_up(name, h, w_in, comm=None):
    t, d = h.shape
    fs = w_in.shape[2]
    f = 2 * fs
    tm, tn = _tile(t, 1024), _tile(fs, 256)
    per = fs // tn
    grid = (t // tm, f // tn)
    c_in, c_in_specs, c_out_specs, c_shapes, c_sems, in_place = _comm_io(comm)
    n_c, n_co = len(c_in), len(c_shapes)

    def body(*refs):
        h_ref, wg_ref, wu_ref = refs[:3]
        gu_ref, z_ref = refs[3 + n_c:5 + n_c]
        if comm is not None:
            i, j = pl.program_id(0), pl.program_id(1)
            send, recv = refs[5 + n_c + n_co:]
            last, finish = _comm_run(comm, refs[3:3 + n_c], refs[5 + n_c:5 + n_c + n_co], send, recv,
                                     jnp.logical_and(i == 0, j == 0),
                                     jnp.logical_and(i == grid[0] - 1, j == grid[1] - 1))
        hv = h_ref[...]
        gate = _dot(hv, wg_ref[...], "nn")
        up = _dot(hv, wu_ref[...], "nn")
        sg = _sigmoid(gate)
        silu = gate * sg
        gu_ref[0] = (sg * (1.0 + gate * (1.0 - sg)) * up).astype(BF16)
        gu_ref[1] = silu.astype(BF16)
        z_ref[...] = (silu * up).astype(BF16)
        if comm is not None:
            pl.when(last)(finish)

    res = pl.pallas_call(
        body, name=name, grid=grid,
        in_specs=[pl.BlockSpec((tm, d), lambda i, j: (i, 0)),
                  pl.BlockSpec((None, d, tn), lambda i, j: (j // per, 0, j % per)),
                  pl.BlockSpec((None, d, tn), lambda i, j: (2 + j // per, 0, j % per))] + c_in_specs,
        out_specs=[pl.BlockSpec((2, tm, tn), lambda i, j: (0, i, j)),
                   pl.BlockSpec((tm, tn), lambda i, j: (i, j))] + c_out_specs,
        out_shape=[jax.ShapeDtypeStruct((2, t, f), BF16), jax.ShapeDtypeStruct((t, f), BF16)] + c_shapes,
        input_output_aliases={3 + k: 2 + k for k in range(n_c)} if in_place else {},
        scratch_shapes=c_sems,
        compiler_params=_params(("parallel", "parallel") if comm is None else ("arbitrary", "arbitrary")),
    )(h, w_in, w_in, *c_in)
    return (res[0], res[1]) if comm is None else (res[0], res[1], res[2:])


def _ffn_dact(name, dxb, w_out, gu):
    t, d = dxb.shape
    f = w_out.shape[0]
    tm, tn = _tile(t, 1024), _tile(f, 512)

    def epilogue(acc, ex, outs, ids):
        dz = 0.5 * acc
        outs[0][0] = (dz * ex[0][0].astype(F32)).astype(BF16)
        outs[0][1] = (dz * ex[0][1].astype(F32)).astype(BF16)

    blk = pl.BlockSpec((2, tm, tn), lambda i, j, k: (0, i, j))
    return _matmul(
        name, "nt", (t // tm, f // tn, 1),
        dxb, pl.BlockSpec((tm, d), lambda i, j, k: (i, 0)),
        w_out, pl.BlockSpec((tn, d), lambda i, j, k: (j, 0)),
        [(gu, blk)], [jax.ShapeDtypeStruct((2, t, f), BF16)], [blk], None, epilogue)[0]


def _grad_colsharded(name, ht, da, comm=None):
    d, t = ht.shape
    w = da.shape[2]
    ws = w // 2
    tm, tn, tk = _tile(d, 1024), _tile(ws, 1408), _tile(t, 2048)
    per = ws // tn
    res = _matmul(
        name, "nn", (d // tm, (2 * w) // tn, t // tk),
        ht, pl.BlockSpec((tm, tk), lambda i, j, k: (i, k)),
        da, pl.BlockSpec((None, tk, tn), lambda i, j, k: (j // (2 * per), k, j % (2 * per))),
        [], [jax.ShapeDtypeStruct((N_CHIPS, d, ws), BF16)],
        [pl.BlockSpec((None, tm, tn), lambda i, j, k: (j // per, i, j % per))],
        (tm, tn), _store(1.0, BF16), comm=comm)
    return res[0] if comm is None else (res[0], res[1:])


def _back_colsharded(name, da, w_g, comm=None):
    _, t, w = da.shape
    d, ws = w_g.shape[1], w_g.shape[2]
    tm, tn, tk = _tile(t, 1024), _tile(d, 1024), _tile(ws, 2816)
    per = ws // tk
    res = _matmul(
        name, "nt", (t // tm, d // tn, (2 * w) // tk),
        da, pl.BlockSpec((None, tm, tk), lambda i, j, k: (k // (2 * per), i, k % (2 * per))),
        w_g, pl.BlockSpec((None, tn, tk), lambda i, j, k: (k // per, j, k % per)),
        [], [jax.ShapeDtypeStruct((t, d), F32)], [pl.BlockSpec((tm, tn), lambda i, j, k: (i, j))],
        (tm, tn), _store(1.0, F32), comm=comm)
    return res[0] if comm is None else (res[0], res[1:])


def _ffn_fwd(tag, x, g, w_in, w_out, prefetch):
    h, ht = _rms_fwd(tag + "_norm", x, g, with_transpose=True)
    if prefetch:
        gu, z, prefetch = _ffn_up(tag + "_up", h, w_in, comm=_gather_phase(1, prefetch))
        y, prefetch = _mm_residual(tag + "_down", z, w_out, x, 0.5, tk_pref=2816, comm=_gather_phase(2, prefetch))
    else:
        gu, z = _ffn_up(tag + "_up", h, w_in)
        y = _mm_residual(tag + "_down", z, w_out, x, 0.5, tk_pref=2816)
    return y, (x, ht, gu, z), prefetch


def _ffn_bwd(tag, saved, g, w_in, w_out, dx, dxb, red):
    x, ht, gu, z = saved
    f = z.shape[1]
    d_w_out = red.behind(0, _mm_tn, tag + "_dwout", z, dxb, scale=0.5, tm_pref=1408, tn_pref=2048)
    da = _ffn_dact(tag + "_dact", dxb, w_out, gu)
    d_w_in = red.behind(1, _grad_colsharded, tag + "_dwin", ht, da)
    dh = red.behind(2, _back_colsharded, tag + "_dh", da, w_in)
    dx, dxb, dg = _rms_bwd(tag + "_dnorm", dh, x, g, dx)
    return dx, dxb, dg, d_w_in, d_w_out.reshape(N_CHIPS, f // N_CHIPS, -1)


_GELU_K = math.sqrt(2.0 / math.pi)
_GELU_C = 0.044715


def _gelu(x):
    t = jnp.tanh(_GELU_K * (x + _GELU_C * x * x * x))
    return 0.5 * x * (1.0 + t), t


def _dgelu(x, t):
    return 0.5 * (1.0 + t) + 0.5 * x * (1.0 - t * t) * (_GELU_K * (1.0 + 3.0 * _GELU_C * x * x))


def _causal_block_mask():
    r = lax.broadcasted_iota(jnp.int32, (SGU_BLOCK, SGU_BLOCK), 0) // CHUNK
    c = lax.broadcasted_iota(jnp.int32, (SGU_BLOCK, SGU_BLOCK), 1) // CHUNK
    return r >= c


def _sgu_pre(name, h, w_in, comm=None):
    t, d = h.shape
    ws = w_in.shape[2]
    w = 2 * ws
    tm, tn = _tile(t, 1024), _tile(ws, 512)
    per = ws // tn
    res = _matmul(
        name, "nn", (t // tm, (2 * w) // tn, 1),
        h, pl.BlockSpec((tm, d), lambda i, j, k: (i, 0)),
        w_in, pl.BlockSpec((None, d, tn), lambda i, j, k: (j // per, 0, j % per)),
        [], [jax.ShapeDtypeStruct((2, t, w), BF16)],
        [pl.BlockSpec((None, tm, tn), lambda i, j, k: (j // (2 * per), i, j % (2 * per)))],
        None, _store(1.0, BF16), comm=comm)
    return res[0] if comm is None else (res[0], res[1:])


def _layernorm_stats(v):
    mu = jnp.mean(v, axis=-1, keepdims=True)
    vc = v - mu
    rstd = lax.rsqrt(jnp.mean(vc * vc, axis=-1, keepdims=True) + EPS)
    return vc * rstd, rstd


def _sgu_mid_fwd(pre, gain, bias, w_sp, b_sp_t):
    _, t, w = pre.shape
    gd = w // SGU_GROUPS

    def body(pre_ref, gain_ref, bias_ref, ws_ref, bt_ref, out_ref):
        mask = _causal_block_mask()
        u, _ = _gelu(pre_ref[0].astype(F32))
        v, _ = _gelu(pre_ref[1].astype(F32))
        vhat, _ = _layernorm_stats(v)
        vln = (vhat * gain_ref[...] + bias_ref[...]).astype(BF16)
        for gi in range(SGU_GROUPS):
            cols = slice(gi * gd, (gi + 1) * gd)
            wg = jnp.where(mask, ws_ref[gi], 0.0).astype(BF16)
            mixed = _dot(wg, vln[:, cols], "nn") + bt_ref[:, gi:gi + 1]
            out_ref[:, cols] = (u[:, cols] * mixed).astype(BF16)

    return pl.pallas_call(
        body, name="sgu_mid_fwd", grid=(t // SGU_BLOCK,),
        in_specs=[pl.BlockSpec((2, SGU_BLOCK, w), lambda n: (0, n, 0)),
                  pl.BlockSpec((1, w), lambda n: (0, 0)), pl.BlockSpec((1, w), lambda n: (0, 0)),
                  pl.BlockSpec((SGU_GROUPS, SGU_BLOCK, SGU_BLOCK), lambda n: (0, 0, 0)),
                  pl.BlockSpec((SGU_BLOCK, SGU_GROUPS), lambda n: (0, 0))],
        out_specs=pl.BlockSpec((SGU_BLOCK, w), lambda n: (n, 0)),
        out_shape=jax.ShapeDtypeStruct((t, w), BF16),
        compiler_params=_params(("parallel",)),
    )(pre, gain, bias, w_sp, b_sp_t)


def _sgu_mid_bwd(pre, dgated, gain, bias, w_sp, b_sp_t):
    _, t, w = pre.shape
    gd = w // SGU_GROUPS

    def body(pre_ref, dg_ref, gain_ref, bias_ref, ws_ref, bt_ref,
             dpre_ref, dgain_ref, dbias_ref, dws_ref, dbt_ref, dvln_s):
        @pl.when(pl.program_id(0) == 0)
        def _():
            dgain_ref[...] = jnp.zeros_like(dgain_ref)
            dbias_ref[...] = jnp.zeros_like(dbias_ref)
            dws_ref[...] = jnp.zeros_like(dws_ref)
            dbt_ref[...] = jnp.zeros_like(dbt_ref)

        mask = _causal_block_mask()
        pu = pre_ref[0].astype(F32)
        pv = pre_ref[1].astype(F32)
        u, tu = _gelu(pu)
        v, tv = _gelu(pv)
        vhat, rstd = _layernorm_stats(v)
        gain_v = gain_ref[...]
        vln = (vhat * gain_v + bias_ref[...]).astype(BF16)
        dgt = dg_ref[...].astype(F32)
        for gi in range(SGU_GROUPS):
            cols = slice(gi * gd, (gi + 1) * gd)
            wg = jnp.where(mask, ws_ref[gi], 0.0).astype(BF16)
            vg = vln[:, cols]
            mixed = _dot(wg, vg, "nn") + bt_ref[:, gi:gi + 1]
            dgg = dgt[:, cols]
            dmixed = dgg * u[:, cols]
            dmb = dmixed.astype(BF16)
            dpre_ref[0, :, cols] = (dgg * mixed * _dgelu(pu[:, cols], tu[:, cols])).astype(BF16)
            dbt_ref[:, gi:gi + 1] += jnp.sum(dmixed, axis=1, keepdims=True)
            dws_ref[gi] += jnp.where(mask, _dot(dmb, vg, "nt"), 0.0)
            dvln_s[:, cols] = _dot(wg, dmb, "tn")
        dvln = dvln_s[...]
        dgain_ref[...] += jnp.sum(dvln * vhat, axis=0, keepdims=True)
        dbias_ref[...] += jnp.sum(dvln, axis=0, keepdims=True)
        dvh = dvln * gain_v
        dv = rstd * (dvh - jnp.mean(dvh, axis=-1, keepdims=True)
                     - vhat * jnp.mean(dvh * vhat, axis=-1, keepdims=True))
        dpre_ref[1] = (dv * _dgelu(pv, tv)).astype(BF16)

    vec = pl.BlockSpec((1, w), lambda n: (0, 0))
    wsb = pl.BlockSpec((SGU_GROUPS, SGU_BLOCK, SGU_BLOCK), lambda n: (0, 0, 0))
    btb = pl.BlockSpec((SGU_BLOCK, SGU_GROUPS), lambda n: (0, 0))
    blk2 = pl.BlockSpec((2, SGU_BLOCK, w), lambda n: (0, n, 0))
    return pl.pallas_call(
        body, name="sgu_mid_bwd", grid=(t // SGU_BLOCK,),
        in_specs=[blk2, pl.BlockSpec((SGU_BLOCK, w), lambda n: (n, 0)), vec, vec, wsb, btb],
        out_specs=[blk2, vec, vec, wsb, btb],
        out_shape=[jax.ShapeDtypeStruct((2, t, w), BF16), jax.ShapeDtypeStruct((1, w), F32),
                   jax.ShapeDtypeStruct((1, w), F32),
                   jax.ShapeDtypeStruct((SGU_GROUPS, SGU_BLOCK, SGU_BLOCK), F32),
                   jax.ShapeDtypeStruct((SGU_BLOCK, SGU_GROUPS), F32)],
        scratch_shapes=[pltpu.VMEM((SGU_BLOCK, w), F32)],
        compiler_params=_params(("arbitrary",)),
    )(pre, dgated, gain, bias, w_sp, b_sp_t)


def _sgu_fwd(x, g, w_in, gain, bias, w_sp, b_sp, w_out, prefetch):
    h, ht = _rms_fwd("sgu_norm", x, g, with_transpose=True)
    pre, prefetch = _sgu_pre("sgu_pre", h, w_in, comm=_gather_phase(1, prefetch))
    gated = _sgu_mid_fwd(pre, gain, bias, w_sp, b_sp.T)
    y, prefetch = _mm_residual("sgu_out", gated, w_out, x, 1.0, tk_pref=1024, comm=_gather_phase(2, prefetch))
    return y, (x, ht, pre, gated), prefetch


def _sgu_bwd(saved, g, w_in, gain, bias, w_sp, b_sp, w_out, dx, dxb, red):
    x, ht, pre, gated = saved
    w = gated.shape[1]
    d_w_out = red.behind(0, _mm_tn, "sgu_dwout", gated, dxb)
    dgated = _mm_nt_full("sgu_dgated", dxb, w_out, BF16)
    dpre, dgain, dbias, dws, dbt = _sgu_mid_bwd(pre, dgated, gain, bias, w_sp, b_sp.T)
    d_w_in = red.behind(1, _grad_colsharded, "sgu_dwin", ht, dpre)
    dh = red.behind(2, _back_colsharded, "sgu_dh", dpre, w_in)
    dx, dxb, dg = _rms_bwd("sgu_dnorm", dh, x, g, dx)
    small = dict(ln=dg, gain=dgain, bias=dbias, w_sp=dws, b_sp=dbt.T)
    return dx, dxb, small, d_w_in, d_w_out.reshape(N_CHIPS, w // N_CHIPS, -1)


def _rope_tables(positions):
    half = QK_ROPE // 2
    inv_freq = 1.0 / (ROPE_THETA ** (jnp.arange(half, dtype=F32) / half))
    ang = positions.astype(F32)[:, None] * inv_freq
    cos, sin = jnp.cos(ang), jnp.sin(ang)
    t = positions.shape[0]
    zeros = jnp.zeros((t, half), F32)
    rest = jnp.zeros((t, LANE - QK_ROPE), F32)
    c = jnp.concatenate([cos, cos, rest + 1.0], axis=1)
    s_up = jnp.concatenate([zeros, sin, rest], axis=1)
    s_dn = jnp.concatenate([-sin, zeros, rest], axis=1)
    return c, s_up, s_dn


def _rope_apply(x, c, s_up, s_dn):
    half = QK_ROPE // 2
    return x * c + pltpu.roll(x, half, 1) * s_up + pltpu.roll(x, LANE - half, 1) * s_dn


def _rope_apply_t(dy, c, s_up, s_dn):
    half = QK_ROPE // 2
    return dy * c - pltpu.roll(dy, LANE - half, 1) * s_dn - pltpu.roll(dy, half, 1) * s_up


def _mla_norm_fwd(proj, gq, gkv):
    t, p = proj.shape
    ql, kvl = gq.shape[1], gkv.shape[1]
    tm = _tile(t, 512)

    def body(p_ref, gq_ref, gkv_ref, qn_ref, kvn_ref):
        for lo, n, g_ref, o_ref in ((0, ql, gq_ref, qn_ref), (ql, kvl, gkv_ref, kvn_ref)):
            xv = p_ref[:, lo:lo + n]
            r = lax.rsqrt(jnp.mean(xv * xv, axis=-1, keepdims=True) + EPS)
            o_ref[...] = (xv * r * g_ref[...]).astype(BF16)

    return pl.pallas_call(
        body, name="mla_norm_fwd", grid=(t // tm,),
        in_specs=[pl.BlockSpec((tm, p), lambda i: (i, 0)), pl.BlockSpec((1, ql), lambda i: (0, 0)),
                  pl.BlockSpec((1, kvl), lambda i: (0, 0))],
        out_specs=[pl.BlockSpec((tm, ql), lambda i: (i, 0)), pl.BlockSpec((tm, kvl), lambda i: (i, 0))],
        out_shape=[jax.ShapeDtypeStruct((t, ql), BF16), jax.ShapeDtypeStruct((t, kvl), BF16)],
        compiler_params=_params(("parallel",)),
    )(proj, gq, gkv)


def _mla_norm_bwd(proj, dqn, dkvn, dkr, gq, gkv):
    t, p = proj.shape
    ql, kvl = gq.shape[1], gkv.shape[1]
    tm = _tile(t, 256)

    def body(p_ref, dqn_ref, dkvn_ref, dkr_ref, gq_ref, gkv_ref, dp_ref, dgq_ref, dgkv_ref):
        @pl.when(pl.program_id(0) == 0)
        def _():
            dgq_ref[...] = jnp.zeros_like(dgq_ref)
            dgkv_ref[...] = jnp.zeros_like(dgkv_ref)

        for lo, n, g_ref, d_ref, dg_ref in ((0, ql, gq_ref, dqn_ref, dgq_ref),
                                             (ql, kvl, gkv_ref, dkvn_ref, dgkv_ref)):
            dx, dgt = _rms_bwd_math(d_ref[...], p_ref[:, lo:lo + n], g_ref[...])
            dp_ref[:, lo:lo + n] = dx.astype(BF16)
            dg_ref[...] += jnp.sum(dgt, axis=0, keepdims=True)
        dp_ref[:, ql + kvl:] = dkr_ref[...].astype(BF16)

    def row(n):
        return pl.BlockSpec((tm, n), lambda i: (i, 0))

    def vec(n):
        return pl.BlockSpec((1, n), lambda i: (0, 0))

    return pl.pallas_call(
        body, name="mla_norm_bwd", grid=(t // tm,),
        in_specs=[row(p), row(ql), row(kvl), row(LANE), vec(ql), vec(kvl)],
        out_specs=[row(p), vec(ql), vec(kvl)],
        out_shape=[jax.ShapeDtypeStruct((t, p), BF16), jax.ShapeDtypeStruct((1, ql), F32),
                   jax.ShapeDtypeStruct((1, kvl), F32)],
        compiler_params=_params(("arbitrary",)),
    )(proj, dqn, dkvn, dkr, gq, gkv)


def _mla_q_up(qn, wq, tables):
    t, ql = qn.shape
    n = wq.shape[1]
    tm = _tile(t, 1024)
    scale = QK_DIM ** -0.5

    def epilogue(acc, ex, outs, ids):
        outs[0][:, :QK_NOPE] = (scale * acc[:, :QK_NOPE]).astype(BF16)
        hi = _rope_apply(acc[:, QK_NOPE:], ex[0][...], ex[1][...], ex[2][...])
        outs[0][:, QK_NOPE:] = (scale * hi).astype(BF16)

    tab = pl.BlockSpec((tm, LANE), lambda i, j, k: (i, 0))
    return _matmul(
        "mla_q_up", "nn", (t // tm, n // HEAD_PAD, 1),
        qn, pl.BlockSpec((tm, ql), lambda i, j, k: (i, 0)),
        wq, pl.BlockSpec((ql, HEAD_PAD), lambda i, j, k: (0, j)),
        [(tb, tab) for tb in tables],
        [jax.ShapeDtypeStruct((t, n), BF16)], [pl.BlockSpec((tm, HEAD_PAD), lambda i, j, k: (i, j))],
        None, epilogue)[0]


def _mla_kv_up(kvn, wkv, proj, tables, heads):
    t, kvl = kvn.shape
    n = wkv.shape[1]
    p = proj.shape[1]
    tm = _tile(t, 1024)

    def epilogue(acc, ex, outs, ids):
        kr = _rope_apply(ex[0][...], ex[1][...], ex[2][...], ex[3][...])
        outs[0][:, :QK_NOPE] = acc[:, :QK_NOPE].astype(BF16)
        outs[0][:, QK_NOPE:] = (acc[:, QK_NOPE:] + jnp.where(ids[1] < heads, kr, 1.0)).astype(BF16)

    tab = pl.BlockSpec((tm, LANE), lambda i, j, k: (i, 0))
    kr_spec = pl.BlockSpec((tm, LANE), lambda i, j, k: (i, p // LANE - 1))
    return _matmul(
        "mla_kv_up", "nn", (t // tm, n // HEAD_PAD, 1),
        kvn, pl.BlockSpec((tm, kvl), lambda i, j, k: (i, 0)),
        wkv, pl.BlockSpec((kvl, HEAD_PAD), lambda i, j, k: (0, j)),
        [(proj, kr_spec)] + [(tb, tab) for tb in tables],
        [jax.ShapeDtypeStruct((t, n), BF16)], [pl.BlockSpec((tm, HEAD_PAD), lambda i, j, k: (i, j))],
        None, epilogue)[0]


FLASH_TQ = 1024
FLASH_TK = 1024
FLASH_SPLIT = 4
FLASH_SPLIT_BWD = 2


def _chunk_mask(tq, tk, qi, ki):
    r = (qi * tq + lax.broadcasted_iota(jnp.int32, (tq, tk), 0)) // CHUNK
    c = (ki * tk + lax.broadcasted_iota(jnp.int32, (tq, tk), 1)) // CHUNK
    return c <= r


def _block_pairs(t, tq, tk, key_major):
    def visible(qi, ki):
        return (ki * tk) // CHUNK <= (qi * tq + tq - 1) // CHUNK

    def masked(qi, ki):
        return (ki * tk + tk - 1) // CHUNK > (qi * tq) // CHUNK

    nq, nk = t // tq, t // tk
    if key_major:
        sweeps = [[(qi, ki) for qi in range(nq) if visible(qi, ki)] for ki in range(nk)]
    else:
        sweeps = [[(qi, ki) for ki in range(nk) if visible(qi, ki)] for qi in range(nq)]
    qs, ks, fs = [], [], []
    for sweep in sweeps:
        for n, (qi, ki) in enumerate(sweep):
            qs.append(qi)
            ks.append(ki)
            fs.append((1 if masked(qi, ki) else 0) + (2 if n == 0 else 0) + (4 if n == len(sweep) - 1 else 0))
    return tuple(jnp.asarray(v, jnp.int32) for v in (qs, ks, fs))


def _flash_fwd(qp, kv, heads):
    t = qp.shape[0]
    tq, tk = _tile(t, FLASH_TQ), _tile(t, FLASH_TK)
    rep = tk // LANE
    qt, kt, ft = _block_pairs(t, tq, tk, key_major=False)

    def body(qt_ref, kt_ref, ft_ref, q_ref, k_ref, v_ref, o_ref, lse_ref, m_s, acc_s):
        p = pl.program_id(1)
        qi, ki, flags = qt_ref[p], kt_ref[p], ft_ref[p]

        @pl.when(flags & 2 != 0)
        def _():
            m_s[...] = jnp.full_like(m_s, -1e30)
            acc_s[...] = jnp.zeros_like(acc_s)

        def step(masked):
            k, v = k_ref[...], v_ref[...]
            mask = _chunk_mask(tq, tk, qi, ki) if masked else None
            for r0 in range(0, tq, tq // FLASH_SPLIT):
                rows = slice(r0, r0 + tq // FLASH_SPLIT)
                s = _dot(q_ref[rows, :], k, "nt")
                if masked:
                    s = jnp.where(mask[rows], s, -1e30)
                m_prev = m_s[rows, :]
                m_new = jnp.maximum(m_prev, jnp.max(s, axis=1, keepdims=True))
                alpha = jnp.exp(m_prev - m_new)
                pr = jnp.exp(s - jnp.tile(m_new, (1, rep))).astype(BF16)
                pv = _dot(pr, v, "nn")
                acc_s[rows, :V_DIM] = alpha * acc_s[rows, :V_DIM] + pv[:, :V_DIM]
                acc_s[rows, V_DIM:] = alpha * acc_s[rows, V_DIM:] + pv[:, V_DIM:]
                m_s[rows, :] = m_new

        @pl.when(flags & 1 == 0)
        def _():
            step(False)

        @pl.when(flags & 1 != 0)
        def _():
            step(True)

        @pl.when(flags & 4 != 0)
        def _():
            l = acc_s[:, V_DIM:]
            o_ref[...] = (acc_s[:, :V_DIM] / l).astype(BF16)
            lse_ref[...] = m_s[...] + jnp.log(l)

    return pl.pallas_call(
        body, name="mla_flash_fwd",
        grid_spec=pltpu.PrefetchScalarGridSpec(
            num_scalar_prefetch=3, grid=(heads, int(qt.shape[0])),
            in_specs=[pl.BlockSpec((tq, HEAD_PAD), lambda h, p, qt, kt, ft: (qt[p], h)),
                      pl.BlockSpec((tk, HEAD_PAD), lambda h, p, qt, kt, ft: (kt[p], h)),
                      pl.BlockSpec((tk, HEAD_PAD), lambda h, p, qt, kt, ft: (kt[p], heads + h))],
            out_specs=[pl.BlockSpec((tq, V_DIM), lambda h, p, qt, kt, ft: (qt[p], h)),
                       pl.BlockSpec((None, tq, LANE), lambda h, p, qt, kt, ft: (h, qt[p], 0))],
            scratch_shapes=[pltpu.VMEM((tq, LANE), F32), pltpu.VMEM((tq, HEAD_PAD), F32)]),
        out_shape=[jax.ShapeDtypeStruct((t, heads * V_DIM), BF16),
                   jax.ShapeDtypeStruct((heads, t, LANE), F32)],
        compiler_params=_params(("parallel", "arbitrary")),
    )(qt, kt, ft, qp, kv, kv)


def _flash_delta(o, do, heads):
    t = o.shape[0]
    tm = _tile(t, 256)

    def body(o_ref, do_ref, d_ref):
        for h in range(heads):
            cols = slice(h * V_DIM, (h + 1) * V_DIM)
            prod = o_ref[:, cols].astype(F32) * do_ref[:, cols].astype(F32)
            d_ref[h] = jnp.broadcast_to(jnp.sum(prod, axis=1, keepdims=True), (tm, LANE))

    row = pl.BlockSpec((tm, heads * V_DIM), lambda i: (i, 0))
    return pl.pallas_call(
        body, name="mla_flash_delta", grid=(t // tm,), in_specs=[row, row],
        out_specs=pl.BlockSpec((heads, tm, LANE), lambda i: (0, i, 0)),
        out_shape=jax.ShapeDtypeStruct((heads, t, LANE), F32),
        compiler_params=_params(("parallel",)),
    )(o, do)


def _flash_bwd(qp, kv, do, lse, delta, heads):
    t = qp.shape[0]
    tq, tk = _tile(t, FLASH_TQ), _tile(t, FLASH_TK)
    rep = tk // LANE
    qt, kt, ft = _block_pairs(t, tq, tk, key_major=True)

    def body(qt_ref, kt_ref, ft_ref, q_ref, k_ref, v_ref, do_ref, lse_ref, dl_ref, dq_ref, dk_ref, dv_ref,
             dk_s, dv_s):
        p = pl.program_id(1)
        qi, ki, flags = qt_ref[p], kt_ref[p], ft_ref[p]

        @pl.when(p == 0)
        def _():
            dq_ref[...] = jnp.zeros_like(dq_ref)

        @pl.when(flags & 2 != 0)
        def _():
            dk_s[...] = jnp.zeros_like(dk_s)
            dv_s[...] = jnp.zeros_like(dv_s)

        def step(masked):
            k, v = k_ref[...], v_ref[...]
            mask = _chunk_mask(tq, tk, qi, ki) if masked else None
            tg = tq // FLASH_SPLIT_BWD
            for g in range(FLASH_SPLIT_BWD):
                rows = slice(g * tg, (g + 1) * tg)
                q = q_ref[rows, :]
                dov = do_ref[rows, :]
                s = _dot(q, k, "nt")
                pr = jnp.exp(s - jnp.tile(lse_ref[rows, :], (1, rep)))
                if masked:
                    pr = jnp.where(mask[rows], pr, 0.0)
                dv_s[...] += _dot(pr.astype(BF16), dov, "tn")
                dp = _dot(dov, v, "nt")
                ds = (pr * (dp - jnp.tile(dl_ref[rows, :], (1, rep)))).astype(BF16)
                dq_rows = pl.ds(pl.multiple_of(qi * tq + g * tg, tg), tg)
                dq_ref[dq_rows, :] += _dot(ds, k, "nn")
                dk_s[...] += _dot(ds, q, "tn")

        @pl.when(flags & 1 == 0)
        def _():
            step(False)

        @pl.when(flags & 1 != 0)
        def _():
            step(True)

        @pl.when(flags & 4 != 0)
        def _():
            dk_ref[...] = dk_s[...]
            dv_ref[...] = dv_s[...]

    def qrow(width):
        return pl.BlockSpec((tq, width), lambda h, p, qt, kt, ft: (qt[p], h))

    def stat():
        return pl.BlockSpec((None, tq, LANE), lambda h, p, qt, kt, ft: (h, qt[p], 0))

    return pl.pallas_call(
        body, name="mla_flash_bwd",
        grid_spec=pltpu.PrefetchScalarGridSpec(
            num_scalar_prefetch=3, grid=(heads, int(qt.shape[0])),
            in_specs=[qrow(HEAD_PAD),
                      pl.BlockSpec((tk, HEAD_PAD), lambda h, p, qt, kt, ft: (kt[p], h)),
                      pl.BlockSpec((tk, V_DIM), lambda h, p, qt, kt, ft: (kt[p], 2 * (heads + h))),
                      qrow(V_DIM), stat(), stat()],
            out_specs=[pl.BlockSpec((t, HEAD_PAD), lambda h, p, qt, kt, ft: (0, h)),
                       pl.BlockSpec((tk, HEAD_PAD), lambda h, p, qt, kt, ft: (kt[p], h)),
                       pl.BlockSpec((tk, V_DIM), lambda h, p, qt, kt, ft: (kt[p], h))],
            scratch_shapes=[pltpu.VMEM((tk, HEAD_PAD), F32), pltpu.VMEM((tk, V_DIM), F32)]),
        out_shape=[jax.ShapeDtypeStruct((t, heads * HEAD_PAD), F32),
                   jax.ShapeDtypeStruct((t, heads * HEAD_PAD), F32),
                   jax.ShapeDtypeStruct((t, heads * V_DIM), F32)],
        compiler_params=_params(("parallel", "arbitrary")),
    )(qt, kt, ft, qp, kv, kv, do, lse, delta)


def _mla_attn_post(dqp, dkp, dv, tables, heads):
    t = dqp.shape[0]
    tm = _tile(t, 256)
    scale = QK_DIM ** -0.5
    kw, vw = heads * HEAD_PAD, heads * V_DIM

    def body(dq_ref, dk_ref, dv_ref, c_ref, su_ref, sd_ref, dqb_ref, dkvb_ref, dkr_ref):
        c, su, sd = c_ref[...], su_ref[...], sd_ref[...]
        kr = jnp.zeros((tm, LANE), F32)
        for h in range(heads):
            lo = h * HEAD_PAD
            mid = lo + QK_NOPE
            dqb_ref[:, lo:mid] = (scale * dq_ref[:, lo:mid]).astype(BF16)
            dqb_ref[:, mid:mid + LANE] = (scale * _rope_apply_t(dq_ref[:, mid:mid + LANE], c, su, sd)).astype(BF16)
            kr = kr + dk_ref[:, mid:mid + LANE]
            dkvb_ref[:, kw + lo:kw + mid] = dv_ref[:, h * V_DIM:(h + 1) * V_DIM].astype(BF16)
            dkvb_ref[:, kw + mid:kw + lo + HEAD_PAD] = jnp.zeros((tm, HEAD_PAD - V_DIM), BF16)
        dkvb_ref[:, :kw] = dk_ref[...].astype(BF16)
        dkr_ref[...] = _rope_apply_t(kr, c, su, sd)

    def row(n):
        return pl.BlockSpec((tm, n), lambda i: (i, 0))

    return pl.pallas_call(
        body, name="mla_attn_post", grid=(t // tm,),
        in_specs=[row(kw), row(kw), row(vw), row(LANE), row(LANE), row(LANE)],
        out_specs=[row(kw), row(2 * kw), row(LANE)],
        out_shape=[jax.ShapeDtypeStruct((t, kw), BF16), jax.ShapeDtypeStruct((t, 2 * kw), BF16),
                   jax.ShapeDtypeStruct((t, LANE), F32)],
        compiler_params=_params(("parallel",)),
    )(dqp, dkp, dv, *tables)


def _mla_weights(w_in_g, w_q_g, w_kv_g, w_out_g):
    d = w_in_g.shape[0] * w_in_g.shape[1]
    pw = w_in_g.shape[2]
    w_in = jnp.pad(w_in_g.reshape(d, pw), ((0, 0), (0, LANE - QK_ROPE)))
    ql = w_q_g.shape[1]
    wq = jnp.transpose(w_q_g, (1, 0, 2)).reshape(ql, -1, QK_DIM)
    heads = wq.shape[1]
    wq = jnp.pad(wq, ((0, 0), (0, 0), (0, HEAD_PAD - QK_DIM))).reshape(ql, heads * HEAD_PAD)
    kvl = w_kv_g.shape[1]
    wkv = jnp.transpose(w_kv_g, (1, 0, 2)).reshape(kvl, heads, QK_NOPE + V_DIM)
    wk = jnp.pad(wkv[:, :, :QK_NOPE], ((0, 0), (0, 0), (0, HEAD_PAD - QK_NOPE))).reshape(kvl, heads * HEAD_PAD)
    wv = jnp.pad(wkv[:, :, QK_NOPE:], ((0, 0), (0, 0), (0, HEAD_PAD - V_DIM))).reshape(kvl, heads * HEAD_PAD)
    return w_in, wq, jnp.concatenate([wk, wv], axis=1), w_out_g.reshape(heads * V_DIM, -1), heads


def _mla_unpermute(d_w_in, d_wq, d_wkv, heads, pw):
    d = d_w_in.shape[0]
    g_in = d_w_in[:, :pw].reshape(N_CHIPS, d // N_CHIPS, pw)
    ql = d_wq.shape[0]
    g_q = d_wq.reshape(ql, heads, HEAD_PAD)[:, :, :QK_DIM].reshape(ql, N_CHIPS, -1)
    kvl = d_wkv.shape[0]
    g_k = d_wkv[:, :heads * HEAD_PAD].reshape(kvl, heads, HEAD_PAD)[:, :, :QK_NOPE]
    g_v = d_wkv[:, heads * HEAD_PAD:].reshape(kvl, heads, HEAD_PAD)[:, :, :V_DIM]
    g_kv = jnp.concatenate([g_k, g_v], axis=2).reshape(kvl, N_CHIPS, -1)
    return g_in, jnp.transpose(g_q, (1, 0, 2)), jnp.transpose(g_kv, (1, 0, 2))


def _mla_fwd(x, g, wts, gq, gkv, tables):
    w_in, wq, wkv, w_out, heads = wts
    h = _rms_fwd("mla_norm", x, g)
    proj = _mm_nn_full("mla_proj", h, w_in, F32, tn_pref=w_in.shape[1])
    qn, kvn = _mla_norm_fwd(proj, gq, gkv)
    qp = _mla_q_up(qn, wq, tables)
    kv = _mla_kv_up(kvn, wkv, proj, tables, heads)
    o, lse = _flash_fwd(qp, kv, heads)
    y = _mm_residual("mla_out", o, w_out, x, 1.0, tk_pref=2048)
    return y, (x, h, proj, qn, kvn, qp, kv, o, lse)


def _mla_bwd(saved, g, wts, gq, gkv, tables, dx, dxb, pw):
    w_in, wq, wkv, w_out, heads = wts
    x, h, proj, qn, kvn, qp, kv, o, lse = saved
    d_w_out = _mm_tn("mla_dwout", o, dxb)
    do = _mm_nt_full("mla_do", dxb, w_out, BF16)
    dqp, dkp, dv = _flash_bwd(qp, kv, do, lse, _flash_delta(o, do, heads), heads)
    dqb, dkvb, dkr = _mla_attn_post(dqp, dkp, dv, tables, heads)
    d_wq = _mm_tn("mla_dwq", qn, dqb)
    dqn = _mm_nt_k("mla_dqn", dqb, wq, F32)
    d_wkv = _mm_tn("mla_dwkv", kvn, dkvb)
    dkvn = _mm_nt_k("mla_dkvn", dkvb, wkv, F32)
    dproj, dgq, dgkv = _mla_norm_bwd(proj, dqn, dkvn, dkr, gq, gkv)
    d_w_in = _mm_tn("mla_dwin", h, dproj, tn_pref=dproj.shape[1])
    dh = _mm_nt_full("mla_dh", dproj, w_in, F32, tn_pref=1024)
    dx, dxb, dg = _rms_bwd("mla_dnorm", dh, x, g, dx)
    g_in, g_q, g_kv = _mla_unpermute(d_w_in, d_wq, d_wkv, heads, pw)
    small = dict(ln=dg, gq=dgq, gkv=dgkv)
    return dx, dxb, small, g_in, g_q, g_kv, d_w_out.reshape(N_CHIPS, d_w_out.shape[0] // N_CHIPS, -1)


def _gather_weights(bufs, norms):
    nt = len(bufs)

    def body(*refs):
        n_in = refs[nt]
        outs, n_out = refs[nt + 1:2 * nt + 1], refs[2 * nt + 1]
        send, recv, fsend, frecv, loc, nsend, nrecv = refs[2 * nt + 2:]
        x, y, c, chips = _place()
        me = 2 * x + y
        sib = (x, y, 1 - c)

        local = pltpu.make_async_copy(n_in, n_out.at[me], loc)
        local.start()

        def place(t, chip, half):
            return outs[t].at[2 * chip[0] + chip[1], half]

        def ici(t, j, chip):
            return pltpu.make_async_remote_copy(
                src_ref=place(t, (x, y), c), dst_ref=place(t, (x, y), c), send_sem=send.at[t, j],
                recv_sem=recv.at[t, j], device_id=(*chip, c), device_id_type=MESH)

        def fwd(t, j, chip, half):
            return pltpu.make_async_remote_copy(
                src_ref=place(t, chip, half), dst_ref=place(t, chip, half), send_sem=fsend.at[t, j],
                recv_sem=frecv.at[t, j], device_id=sib, device_id_type=MESH)

        def nrm(j, chip, owner):
            return pltpu.make_async_remote_copy(
                src_ref=n_in, dst_ref=n_out.at[2 * owner[0] + owner[1]], send_sem=nsend.at[j], recv_sem=nrecv.at[j],
                device_id=(*chip, c), device_id_type=MESH)

        firsts = [ici(t, j, chip) for t in range(nt) for j, chip in enumerate(chips)]
        firsts += [nrm(j, chip, (x, y)) for j, chip in enumerate(chips)]
        for cp in firsts:
            cp.start()
        passed = []
        for t in range(nt):
            for j, chip in enumerate(chips):
                pltpu.make_async_remote_copy(
                    src_ref=place(t, chip, c), dst_ref=place(t, chip, c), send_sem=send.at[t, j],
                    recv_sem=recv.at[t, j], device_id=(*chip, c), device_id_type=MESH).wait_recv()
                cp = fwd(t, j, chip, c)
                cp.start()
                passed.append(cp)
        for t in range(nt):
            for j, chip in enumerate(chips):
                fwd(t, j, chip, 1 - c).wait_recv()
        for j, chip in enumerate(chips):
            nrm(j, chip, chip).wait_recv()
        for cp in firsts + passed:
            cp.wait_send()
        local.wait()

    out_shape = [jax.ShapeDtypeStruct(b.shape, b.dtype) for b in bufs]
    out_shape.append(jax.ShapeDtypeStruct((N_CHIPS,) + norms.shape, norms.dtype))
    res = pl.pallas_call(
        body, name="gather_weights",
        in_specs=[ANY] * (nt + 1), out_specs=[ANY] * (nt + 1), out_shape=out_shape,
        input_output_aliases={t: t for t in range(nt)},
        scratch_shapes=[pltpu.SemaphoreType.DMA((nt, 3)), pltpu.SemaphoreType.DMA((nt, 3)),
                        pltpu.SemaphoreType.DMA((nt, 3)), pltpu.SemaphoreType.DMA((nt, 3)),
                        pltpu.SemaphoreType.DMA, pltpu.SemaphoreType.DMA((3,)),
                        pltpu.SemaphoreType.DMA((3,))],
    )(*bufs, norms)
    return res[:nt], res[nt]


def _run_phase(name, phase):
    c_in, in_specs, out_specs, shapes, sems, in_place = _comm_io(phase)
    n_c, n_co = len(c_in), len(shapes)

    def body(*refs):
        ins, outs = refs[:n_c], refs[n_c:n_c + n_co]
        send, recv = refs[n_c + n_co:]
        for cp in phase.copies(ins, outs, send, recv, False):
            cp.start()
        for cp in phase.copies(ins, outs, send, recv, True):
            cp.wait_recv()
        for cp in phase.copies(ins, outs, send, recv, False):
            cp.wait_send()

    return pl.pallas_call(
        body, name=name, in_specs=in_specs, out_specs=out_specs, out_shape=shapes,
        input_output_aliases={t: t for t in range(n_c)} if in_place else {}, scratch_shapes=sems,
    )(*c_in)


def _remote(ref_src, ref_dst, send, recv, t, j, to):
    return pltpu.make_async_remote_copy(src_ref=ref_src, dst_ref=ref_dst, send_sem=send.at[t, j],
                                        recv_sem=recv.at[t, j], device_id=to, device_id_type=MESH)


class _Reduce:
    def __init__(self, units, grads, n_layers, reduced, me_idx, core_idx):
        self.units, self.n_layers, self.reduced = list(units), n_layers, reduced
        self.me_idx, self.core_idx = me_idx, core_idx
        self.local = [g.reshape(N_CHIPS, 2, g.shape[1] // 2, g.shape[2]) for g in grads]
        self.parts = None

    def _swap(self):
        fresh = [jax.ShapeDtypeStruct((g.shape[0],) + g.shape[2:], g.dtype) for g in self.local]

        def copies(ins, outs, send, recv, landing):
            x, y, c, _ = _place()
            return [_remote(outs[t] if landing else ins[t].at[:, 1 - c], outs[t], send, recv, t, 0, (x, y, 1 - c))
                    for t in range(len(ins))]

        return _Phase(self.local, fresh, (len(self.local), 1), copies)

    def _scatter(self):
        fresh = [jax.ShapeDtypeStruct((3,) + p.shape[1:], p.dtype) for p in self.parts]

        def copies(ins, outs, send, recv, landing):
            x, y, c, chips = _place()
            return [_remote(outs[t].at[j] if landing else ins[t].at[2 * chip[0] + chip[1]], outs[t].at[j],
                            send, recv, t, j, (*chip, c))
                    for t in range(len(ins)) for j, chip in enumerate(chips)]

        return _Phase(self.parts, fresh, (len(self.parts), 3), copies)

    def _join(self):
        layers = [l for _, l in self.units]

        def copies(ins, outs, send, recv, landing):
            x, y, c, _ = _place()
            refs = [outs[t].at[l, 1 - c if landing else c] for t, l in enumerate(layers)]
            return [_remote(r, r, send, recv, t, 0, (x, y, 1 - c)) for t, r in enumerate(refs)]

        return _Phase([self.reduced[n] for n, _ in self.units], [], (len(self.units), 1), copies)

    def _after(self, step, got):
        if step == 0:
            self.parts = [_add_halves(self.core_idx, g, o) for g, o in zip(self.local, got)]
        elif step == 1:
            for (n, l), p, ld in zip(self.units, self.parts, got):
                self.reduced[n] = _sum_chips(self.me_idx, self.core_idx, p, ld, l, self.n_layers[n],
                                             self.reduced.get(n))
        else:
            for (n, _), joined in zip(self.units, got):
                self.reduced[n] = joined

    def _phase(self, step):
        return (self._swap, self._scatter, self._join)[step]()

    def behind(self, step, fn, *args, **kw):
        if not self.units:
            return fn(*args, **kw)
        out, got = fn(*args, comm=self._phase(step), **kw)
        self._after(step, got)
        return out

    def alone(self):
        for step, name in enumerate(("grad_swap_halves", "grad_scatter_chips", "grad_join_halves")):
            self._after(step, _run_phase(name, self._phase(step)))


def _gather_all(block):
    m_per, n = block.shape

    def body(x_ref, out_ref, send_sems, recv_sems, local_sem):
        x, y, c, chips = _place()
        me, sibling = (x, y, c), (x, y, 1 - c)

        def rows(px, py, pc):
            return out_ref.at[pl.ds((4 * px + 2 * py + pc) * m_per, m_per), :]

        def copy(k, block_of, to, src=None):
            return pltpu.make_async_remote_copy(
                src_ref=rows(*block_of) if src is None else src, dst_ref=rows(*block_of),
                send_sem=send_sems.at[k], recv_sem=recv_sems.at[k], device_id=to, device_id_type=MESH)

        mine = pltpu.make_async_copy(x_ref, rows(*me), local_sem)
        mine.start()
        first = [copy(0, me, sibling, src=x_ref)]
        first += [copy(1 + j, me, (*chip, c), src=x_ref) for j, chip in enumerate(chips)]
        for cp in first:
            cp.start()
        passed = [copy(4 + j, (*chip, c), sibling) for j, chip in enumerate(chips)]
        for j, chip in enumerate(chips):
            copy(1 + j, (*chip, c), me).wait_recv()
            passed[j].start()
        copy(0, sibling, me).wait_recv()
        for j, chip in enumerate(chips):
            copy(4 + j, (*chip, 1 - c), me).wait_recv()
        for cp in first + passed:
            cp.wait_send()
        mine.wait()

    return pl.pallas_call(
        body, name="gather_small_grads",
        out_shape=jax.ShapeDtypeStruct((N_DEV * m_per, n), block.dtype),
        in_specs=[pl.BlockSpec(memory_space=pltpu.VMEM)],
        out_specs=pl.BlockSpec(memory_space=pltpu.VMEM),
        scratch_shapes=[pltpu.SemaphoreType.DMA((7,)), pltpu.SemaphoreType.DMA((7,)), pltpu.SemaphoreType.DMA],
    )(block)


def _row_tile(r, c, elems=512 * 1024):
    t = max(8, min(r, (elems // c) // 8 * 8))
    while t > 8 and r % t:
        t -= 8
    return t if r % t == 0 else r


def _add_halves(idx, grad, other):
    n, _, r, w = grad.shape
    tr = _row_tile(r, w)

    def body(idx_ref, g_ref, o_ref, out_ref):
        out_ref[...] = (g_ref[...].astype(F32) + o_ref[...].astype(F32)).astype(BF16)

    return pl.pallas_call(
        body, name="grad_add_halves",
        grid_spec=pltpu.PrefetchScalarGridSpec(
            num_scalar_prefetch=1, grid=(n, r // tr),
            in_specs=[pl.BlockSpec((None, None, tr, w), lambda k, i, idx: (k, idx[0], i, 0)),
                      pl.BlockSpec((None, tr, w), lambda k, i, idx: (k, i, 0))],
            out_specs=pl.BlockSpec((None, tr, w), lambda k, i, idx: (k, i, 0))),
        out_shape=jax.ShapeDtypeStruct((n, r, w), BF16),
        compiler_params=_params(("parallel", "parallel")),
    )(idx, grad, other)


def _sum_chips(me_idx, core_idx, part, landed, layer, n_layers, prev):
    _, r, w = part.shape
    tr = _row_tile(r, w)

    def body(me_ref, c_ref, p_ref, l_ref, *rest):
        acc = p_ref[...].astype(F32)
        for j in range(3):
            acc = acc + l_ref[j].astype(F32)
        rest[-1][...] = acc

    return pl.pallas_call(
        body, name="grad_sum_chips",
        grid_spec=pltpu.PrefetchScalarGridSpec(
            num_scalar_prefetch=2, grid=(r // tr,),
            in_specs=[pl.BlockSpec((None, tr, w), lambda i, me, c: (me[0], i, 0)),
                      pl.BlockSpec((3, tr, w), lambda i, me, c: (0, i, 0))] + ([] if prev is None else [ANY]),
            out_specs=pl.BlockSpec((None, None, tr, w), lambda i, me, c: (layer, c[0], i, 0))),
        out_shape=jax.ShapeDtypeStruct((n_layers, 2, r, w), F32),
        input_output_aliases={} if prev is None else {4: 0},
        compiler_params=_params(("parallel",)),
    )(me_idx, core_idx, part, landed, *([] if prev is None else [prev]))


def _adamw_math(w, g, m, v):
    m = ADAM_B1 * m + (1.0 - ADAM_B1) * g
    v = ADAM_B2 * v + (1.0 - ADAM_B2) * (g * g)
    m_hat = m / (1.0 - ADAM_B1 ** ADAM_STEP)
    v_hat = v / (1.0 - ADAM_B2 ** ADAM_STEP)
    delta = -ADAM_LR * (m_hat / (jnp.sqrt(v_hat) + ADAM_EPS) + ADAM_WD * w)
    return delta, m, v


def _adamw(name, w, g, m, v):
    r, c = w.shape
    tr = _row_tile(r, c, 256 * 1024)

    def body(w_ref, g_ref, m_ref, v_ref, d_ref, nm_ref, nv_ref):
        d_ref[...], nm_ref[...], nv_ref[...] = _adamw_math(w_ref[...], g_ref[...], m_ref[...], v_ref[...])

    blk = pl.BlockSpec((tr, c), lambda i: (i, 0))
    return pl.pallas_call(
        body, name=name, grid=(r // tr,), in_specs=[blk] * 4, out_specs=[blk] * 3,
        out_shape=[jax.ShapeDtypeStruct((r, c), F32)] * 3,
        compiler_params=_params(("parallel",)),
    )(w, g, m, v)


def _adamw_summed(w, parts, m, v):
    r, c = w.shape

    def body(w_ref, p_ref, m_ref, v_ref, g_ref, d_ref, nm_ref, nv_ref):
        g = p_ref[0:r, :]
        for k in range(1, N_DEV):
            g = g + p_ref[k * r:(k + 1) * r, :]
        g_ref[...] = g
        d_ref[...], nm_ref[...], nv_ref[...] = _adamw_math(w_ref[...], g, m_ref[...], v_ref[...])

    return pl.pallas_call(
        body, name="adamw_replicated",
        out_shape=[jax.ShapeDtypeStruct((r, c), F32)] * 4,
        compiler_params=pltpu.CompilerParams(vmem_limit_bytes=VMEM_LIMIT_BYTES),
    )(w, parts, m, v)


def _pack(arrays):
    return jnp.concatenate([a.reshape(-1, LANE) for a in arrays], axis=0)


def _unpack(packed, shapes):
    out, row = [], 0
    for s in shapes:
        n = math.prod(s) // LANE
        out.append(packed[row:row + n].reshape(s))
        row += n
    return out


def _cast_into(name, idx, w, layer):
    _, rows, c = w.shape
    r = rows // 2
    tr = _row_tile(r, c)
    per = r // tr

    def body(idx_ref, w_ref, o_ref):
        o_ref[...] = w_ref[...].astype(BF16)

    return pl.pallas_call(
        body, name=name,
        grid_spec=pltpu.PrefetchScalarGridSpec(
            num_scalar_prefetch=1, grid=(2, per),
            in_specs=[pl.BlockSpec((None, tr, c), lambda h, i, idx: (layer, h * per + i, 0))],
            out_specs=pl.BlockSpec((None, None, tr, c), lambda h, i, idx: (idx[0], h, i, 0))),
        out_shape=jax.ShapeDtypeStruct((N_CHIPS, 2, r, c), BF16),
        compiler_params=_params(("parallel", "parallel")),
    )(idx, w)


BIG = ["ffn1_w_in", "ffn1_w_out", "ffn2_w_in", "ffn2_w_out", "sgu_w_in", "sgu_w_out",
       "mla_w_in", "mla_w_q_up", "mla_w_kv_up", "mla_w_out"]
STAGES = [
    [("ffn1_w_in", 0), ("ffn1_w_out", 0)],
    [("sgu_w_in", 0), ("sgu_w_out", 0)],
    [("ffn2_w_in", 0), ("ffn2_w_out", 0)],
    [("ffn1_w_in", 1), ("ffn1_w_out", 1)],
    [("mla_w_in", 0), ("mla_w_q_up", 0), ("mla_w_kv_up", 0), ("mla_w_out", 0)],
    [("ffn2_w_in", 1), ("ffn2_w_out", 1)],
]
REPLICATED = ["ln_ffn1", "ln_mix", "ln_ffn2", "sgu_v_gain", "sgu_v_bias", "sgu_w_spatial", "sgu_b_spatial",
              "ln_final"]
NORM_SHARDS = ["mla_q_norm", "mla_kv_norm"]
WEIGHTS = ["ln_ffn1", "ffn1_w_in", "ffn1_w_out", "ln_mix", "ln_ffn2", "ffn2_w_in", "ffn2_w_out", "sgu_w_in",
           "sgu_v_gain", "sgu_v_bias", "sgu_w_spatial", "sgu_b_spatial", "sgu_w_out", "mla_w_in", "mla_q_norm",
           "mla_w_q_up", "mla_kv_norm", "mla_w_kv_up", "mla_w_out", "ln_final"]


def kernel(x, positions, ln_ffn1, ffn1_w_in, ffn1_w_out, ln_mix, ln_ffn2, ffn2_w_in, ffn2_w_out, sgu_w_in, sgu_v_gain, sgu_v_bias, sgu_w_spatial, sgu_b_spatial, sgu_w_out, mla_w_in, mla_q_norm, mla_w_q_up, mla_kv_norm, mla_w_kv_up, mla_w_out, ln_final, loss_target, m_ln_ffn1, m_ffn1_w_in, m_ffn1_w_out, m_ln_mix, m_ln_ffn2, m_ffn2_w_in, m_ffn2_w_out, m_sgu_w_in, m_sgu_v_gain, m_sgu_v_bias, m_sgu_w_spatial, m_sgu_b_spatial, m_sgu_w_out, m_mla_w_in, m_mla_q_norm, m_mla_w_q_up, m_mla_kv_norm, m_mla_w_kv_up, m_mla_w_out, m_ln_final, v_ln_ffn1, v_ffn1_w_in, v_ffn1_w_out, v_ln_mix, v_ln_ffn2, v_ffn2_w_in, v_ffn2_w_out, v_sgu_w_in, v_sgu_v_gain, v_sgu_v_bias, v_sgu_w_spatial, v_sgu_b_spatial, v_sgu_w_out, v_mla_w_in, v_mla_q_norm, v_mla_w_q_up, v_mla_kv_norm, v_mla_w_kv_up, v_mla_w_out, v_ln_final):
    given = dict(locals())
    w = {n: given[n] for n in WEIGHTS}
    mom = {n: given["m_" + n] for n in WEIGHTS}
    var = {n: given["v_" + n] for n in WEIGHTS}
    t, d = x.shape[1], x.shape[2]
    xs = x.reshape(t, d)
    target = loss_target.reshape(t, d)
    me = 2 * lax.axis_index("x") + lax.axis_index("y")

    me_idx = jnp.reshape(me, (1,)).astype(jnp.int32)
    core_idx = jnp.reshape(lax.axis_index("c"), (1,)).astype(jnp.int32)

    bufs = {(n, l): _cast_into(f"cast_{n}_{l}", me_idx, w[n], l) for n in BIG for l in range(w[n].shape[0])}
    nq = mla_q_norm.shape[1]
    norms = jnp.pad(jnp.concatenate([mla_q_norm, mla_kv_norm], axis=0), ((0, 6), (0, LANE - nq)))

    def take(stage):
        return [bufs[u] for u in stage]

    def put(stage, arrays):
        bufs.update(zip(stage, arrays))

    def full(unit):
        a = bufs[unit]
        return a.reshape(N_CHIPS, a.shape[1] * a.shape[2], a.shape[3])

    def rows(unit):
        a = bufs[unit]
        return a.reshape(-1, a.shape[-1])

    first, norms_g = _gather_weights(take(STAGES[0]), norms)
    put(STAGES[0], first)
    gq = norms_g[:, 0, :nq].reshape(1, N_CHIPS * nq)
    gkv = norms_g[:, 1, :nq].reshape(1, N_CHIPS * nq)
    tables = _rope_tables(positions.reshape(t))
    sgu_small = (sgu_v_gain, sgu_v_bias, sgu_w_spatial[0], sgu_b_spatial[0])

    a0, s_f1_0, got = _ffn_fwd("l0_ffn1", xs, ln_ffn1[0], full(("ffn1_w_in", 0)), rows(("ffn1_w_out", 0)),
                               take(STAGES[1]))
    put(STAGES[1], got)
    a1, s_sgu, got = _sgu_fwd(a0, ln_mix[0], full(("sgu_w_in", 0)), *sgu_small, rows(("sgu_w_out", 0)),
                              take(STAGES[2]))
    put(STAGES[2], got)
    a2, s_f2_0, got = _ffn_fwd("l0_ffn2", a1, ln_ffn2[0], full(("ffn2_w_in", 0)), rows(("ffn2_w_out", 0)),
                               take(STAGES[3]))
    put(STAGES[3], got)
    a3, s_f1_1, got = _ffn_fwd("l1_ffn1", a2, ln_ffn1[1], full(("ffn1_w_in", 1)), rows(("ffn1_w_out", 1)),
                               take(STAGES[4] + STAGES[5]))
    put(STAGES[4] + STAGES[5], got)
    mla_wts = _mla_weights(full(("mla_w_in", 0)), full(("mla_w_q_up", 0)), full(("mla_w_kv_up", 0)),
                           full(("mla_w_out", 0)))
    a4, s_mla = _mla_fwd(a3, ln_mix[1], mla_wts, gq, gkv, tables)
    a5, s_f2_1, _ = _ffn_fwd("l1_ffn2", a4, ln_ffn2[1], full(("ffn2_w_in", 1)), rows(("ffn2_w_out", 1)), [])

    gr, reduced = {}, {}
    n_layers = {n: w[n].shape[0] for n in BIG}

    def reduce_of(stages):
        units = [u for s in stages for u in STAGES[s]]
        return _Reduce(units, [gr[u] for u in units], n_layers, reduced, me_idx, core_idx)

    loss_part, dx, dxb, dg_final = _loss_bwd(a5, ln_final, target)
    dx, dxb, dg_f2_1, gr["ffn2_w_in", 1], gr["ffn2_w_out", 1] = _ffn_bwd(
        "l1_ffn2", s_f2_1, ln_ffn2[1], full(("ffn2_w_in", 1)), rows(("ffn2_w_out", 1)), dx, dxb, reduce_of([]))
    (dx, dxb, sm_mla, gr["mla_w_in", 0], gr["mla_w_q_up", 0], gr["mla_w_kv_up", 0],
     gr["mla_w_out", 0]) = _mla_bwd(s_mla, ln_mix[1], mla_wts, gq, gkv, tables, dx, dxb, mla_w_in.shape[2])
    dx, dxb, dg_f1_1, gr["ffn1_w_in", 1], gr["ffn1_w_out", 1] = _ffn_bwd(
        "l1_ffn1", s_f1_1, ln_ffn1[1], full(("ffn1_w_in", 1)), rows(("ffn1_w_out", 1)), dx, dxb, reduce_of([5, 4]))
    dx, dxb, dg_f2_0, gr["ffn2_w_in", 0], gr["ffn2_w_out", 0] = _ffn_bwd(
        "l0_ffn2", s_f2_0, ln_ffn2[0], full(("ffn2_w_in", 0)), rows(("ffn2_w_out", 0)), dx, dxb, reduce_of([3]))
    dx, dxb, sm_sgu, gr["sgu_w_in", 0], gr["sgu_w_out", 0] = _sgu_bwd(
        s_sgu, ln_mix[0], full(("sgu_w_in", 0)), *sgu_small, rows(("sgu_w_out", 0)), dx, dxb, reduce_of([2]))
    dx, dxb, dg_f1_0, gr["ffn1_w_in", 0], gr["ffn1_w_out", 0] = _ffn_bwd(
        "l0_ffn1", s_f1_0, ln_ffn1[0], full(("ffn1_w_in", 0)), rows(("ffn1_w_out", 0)), dx, dxb, reduce_of([1]))
    reduce_of([0]).alone()
    big_grad = {n: reduced[n].reshape(w[n].shape) for n in BIG}

    small_parts = [
        jnp.concatenate([dg_f1_0, dg_f1_1], axis=0), jnp.concatenate([sm_sgu["ln"], sm_mla["ln"]], axis=0),
        jnp.concatenate([dg_f2_0, dg_f2_1], axis=0), sm_sgu["gain"], sm_sgu["bias"], sm_sgu["w_sp"],
        sm_sgu["b_sp"], dg_final]
    rep_shapes = [w[n].shape for n in REPLICATED]
    gq_row = jnp.pad(sm_mla["gq"], ((0, 0), (0, N_CHIPS * (LANE - nq))))
    gkv_row = jnp.pad(sm_mla["gkv"], ((0, 0), (0, N_CHIPS * (LANE - nq))))
    packed = _pack(small_parts + [gq_row, gkv_row, loss_part])
    packed = jnp.pad(packed, ((0, -packed.shape[0] % 8), (0, 0)))
    everyone = _gather_all(packed)
    rows = packed.shape[0]
    zero_rows = jnp.zeros((rows - sum(math.prod(s) // LANE for s in rep_shapes), LANE), F32)
    pw = jnp.concatenate([_pack([w[n] for n in REPLICATED]), zero_rows], axis=0)
    pm = jnp.concatenate([_pack([mom[n] for n in REPLICATED]), zero_rows], axis=0)
    pv = jnp.concatenate([_pack([var[n] for n in REPLICATED]), zero_rows + 1.0], axis=0)
    g_all, d_all, m_all, v_all = _adamw_summed(pw, everyone, pm, pv)
    tail_shapes = [(1, N_CHIPS * LANE), (1, N_CHIPS * LANE), (1, LANE)]
    rep_grad = dict(zip(REPLICATED, _unpack(g_all, rep_shapes + tail_shapes)[:len(REPLICATED)]))
    rep_delta = dict(zip(REPLICATED, _unpack(d_all, rep_shapes)))
    rep_m = dict(zip(REPLICATED, _unpack(m_all, rep_shapes)))
    rep_v = dict(zip(REPLICATED, _unpack(v_all, rep_shapes)))
    tail = _unpack(g_all, rep_shapes + tail_shapes)[len(REPLICATED):]
    loss = tail[2][0, 0]
    norm_grad = {
        "mla_q_norm": lax.dynamic_slice(tail[0], (0, me * nq), (1, nq)),
        "mla_kv_norm": lax.dynamic_slice(tail[1], (0, me * nq), (1, nq)),
    }

    grad, delta, new_m, new_v = {}, {}, {}, {}
    for n in BIG:
        shp = w[n].shape
        flat = lambda a: a.reshape(-1, shp[-1])
        dl, nm, nv = _adamw("adamw_" + n, flat(w[n]), flat(big_grad[n]), flat(mom[n]), flat(var[n]))
        grad[n], delta[n], new_m[n], new_v[n] = big_grad[n], dl.reshape(shp), nm.reshape(shp), nv.reshape(shp)
    for n in REPLICATED:
        grad[n], delta[n], new_m[n], new_v[n] = rep_grad[n], rep_delta[n], rep_m[n], rep_v[n]
    stack = lambda dct: jnp.concatenate([dct[n] for n in NORM_SHARDS], axis=0)
    dl, nm, nv = _adamw("adamw_norm_shards", stack(w), stack(norm_grad), stack(mom), stack(var))
    for i, n in enumerate(NORM_SHARDS):
        grad[n], delta[n], new_m[n], new_v[n] = norm_grad[n], dl[i:i + 1], nm[i:i + 1], nv[i:i + 1]

    grad_x = dx.reshape(x.shape)
    return (loss, grad_x, *[grad[n] for n in WEIGHTS], *[delta[n] for n in WEIGHTS],
            *[new_m[n] for n in WEIGHTS], *[new_v[n] for n in WEIGHTS])
```

```python
import functools
import math

import jax
import jax.numpy as jnp
from jax import lax
from jax.experimental import pallas as pl
from jax.experimental.pallas import tpu as pltpu

F32 = jnp.float32
BF16 = jnp.bfloat16
MESH = pl.DeviceIdType.MESH

EPS = 1e-6
CHUNK = 64
SGU_BLOCK = 128
SGU_GROUPS = 8
QK_NOPE = 128
QK_ROPE = 64
V_DIM = 128
QK_DIM = QK_NOPE + QK_ROPE
HEAD_PAD = 256
ROPE_THETA = 10000.0
N_CHIPS = 4
N_DEV = 8

ADAM_LR = 0.001
ADAM_B1 = 0.9
ADAM_B2 = 0.999
ADAM_EPS = 1e-08
ADAM_WD = 0.01
ADAM_STEP = 10

LANE = 128
VMEM_LIMIT_BYTES = 56 * 1024 * 1024

_DIMS = {
    "nn": (((1,), (0,)), ((), ())),
    "nt": (((1,), (1,)), ((), ())),
    "tn": (((0,), (0,)), ((), ())),
}


def _tile(n, pref):
    t = (min(pref, n) // LANE) * LANE
    while t >= LANE:
        if n % t == 0:
            return t
        t -= LANE
    return n


def _params(sem):
    return pltpu.CompilerParams(dimension_semantics=sem, vmem_limit_bytes=VMEM_LIMIT_BYTES)


def _dot(a, b, mode):
    return lax.dot_general(a, b, _DIMS[mode], preferred_element_type=F32)


def _place():
    x, y, c = lax.axis_index("x"), lax.axis_index("y"), lax.axis_index("c")
    chips = [(1 - x, y), (x, 1 - y), (1 - x, 1 - y)]
    return x, y, c, chips


ANY = pl.BlockSpec(memory_space=pl.ANY)


def _gather_copies(phase, bufs, send, recv, landing):
    x, y, c, chips = _place()
    copies = []
    for t, buf in enumerate(bufs):
        for j, chip in enumerate(chips):
            there = 2 * chip[0] + chip[1]
            if phase == 1:
                src, lands, to = buf.at[2 * x + y, c], buf.at[there, c], (*chip, c)
            else:
                src, lands, to = buf.at[there, c], buf.at[there, 1 - c], (x, y, 1 - c)
            ref = lands if landing else src
            copies.append(pltpu.make_async_remote_copy(
                src_ref=ref, dst_ref=ref, send_sem=send.at[t, j], recv_sem=recv.at[t, j], device_id=to,
                device_id_type=MESH))
    return copies


class _Phase:
    def __init__(self, ins, fresh, sems, copies):
        self.ins, self.fresh, self.sems, self.copies = list(ins), list(fresh), sems, copies


def _gather_phase(phase, bufs):
    return _Phase(bufs, [], (len(bufs), 3),
                  lambda ins, outs, send, recv, landing: _gather_copies(phase, outs, send, recv, landing))


def _comm_io(comm):
    if comm is None:
        return [], [], [], [], [], False
    shapes = comm.fresh or [jax.ShapeDtypeStruct(b.shape, b.dtype) for b in comm.ins]
    sems = [pltpu.SemaphoreType.DMA(comm.sems), pltpu.SemaphoreType.DMA(comm.sems)]
    return comm.ins, [ANY] * len(comm.ins), [ANY] * len(shapes), shapes, sems, not comm.fresh


def _comm_run(comm, in_refs, out_refs, send, recv, first, last):
    @pl.when(first)
    def _():
        for cp in comm.copies(in_refs, out_refs, send, recv, False):
            cp.start()

    def finish():
        for cp in comm.copies(in_refs, out_refs, send, recv, True):
            cp.wait_recv()
        for cp in comm.copies(in_refs, out_refs, send, recv, False):
            cp.wait_send()

    return last, finish


def _matmul(name, mode, grid, a, a_spec, b, b_spec, extras, out_shapes, out_specs, acc_shape, epilogue, comm=None):
    nk = grid[2]
    n_ex = len(extras)
    n_out = len(out_shapes)
    c_in, c_in_specs, c_out_specs, c_shapes, c_sems, in_place = _comm_io(comm)
    n_c, n_co = len(c_in), len(c_shapes)

    def body(*refs):
        a_ref, b_ref = refs[0], refs[1]
        ex = refs[2:2 + n_ex]
        outs = refs[2 + n_ex + n_c:2 + n_ex + n_c + n_out]
        ids = (pl.program_id(0), pl.program_id(1))
        k = pl.program_id(2)
        if comm is not None:
            c_ins = refs[2 + n_ex:2 + n_ex + n_c]
            c_outs = refs[2 + n_ex + n_c + n_out:2 + n_ex + n_c + n_out + n_co]
            send, recv = refs[2 + n_ex + n_c + n_out + n_co:2 + n_ex + n_c + n_out + n_co + 2]
            first = jnp.logical_and(jnp.logical_and(ids[0] == 0, ids[1] == 0), k == 0)
            last = jnp.logical_and(jnp.logical_and(ids[0] == grid[0] - 1, ids[1] == grid[1] - 1), k == nk - 1)
            last, finish = _comm_run(comm, c_ins, c_outs, send, recv, first, last)
        part = _dot(a_ref[...], b_ref[...], mode)
        if nk == 1:
            epilogue(part, ex, outs, ids)
        else:
            acc = refs[-1]

            @pl.when(k == 0)
            def _():
                acc[...] = part

            @pl.when(k > 0)
            def _():
                acc[...] += part

            @pl.when(k == nk - 1)
            def _():
                epilogue(acc[...], ex, outs, ids)

        if comm is not None:
            pl.when(last)(finish)

    scratch = c_sems + ([pltpu.VMEM(acc_shape, F32)] if nk > 1 else [])
    sem = ("parallel", "parallel", "arbitrary") if comm is None else ("arbitrary",) * 3
    return pl.pallas_call(
        body,
        name=name,
        grid=grid,
        in_specs=[a_spec, b_spec] + [s for _, s in extras] + c_in_specs,
        out_specs=list(out_specs) + c_out_specs,
        out_shape=list(out_shapes) + c_shapes,
        input_output_aliases={2 + n_ex + t: n_out + t for t in range(n_c)} if in_place else {},
        scratch_shapes=scratch,
        compiler_params=_params(sem),
    )(a, b, *[e for e, _ in extras], *c_in)


def _store(scale, dtype):
    def epilogue(acc, ex, outs, ids):
        v = acc if scale == 1.0 else acc * scale
        outs[0][...] = v.astype(dtype)

    return epilogue


def _mm_nn_full(name, a, b, out_dtype, tm_pref=1024, tn_pref=512):
    m, kd = a.shape
    n = b.shape[1]
    tm, tn = _tile(m, tm_pref), _tile(n, tn_pref)
    return _matmul(
        name, "nn", (m // tm, n // tn, 1),
        a, pl.BlockSpec((tm, kd), lambda i, j, k: (i, 0)),
        b, pl.BlockSpec((kd, tn), lambda i, j, k: (0, j)),
        [], [jax.ShapeDtypeStruct((m, n), out_dtype)], [pl.BlockSpec((tm, tn), lambda i, j, k: (i, j))],
        None, _store(1.0, out_dtype))[0]


def _mm_nt_full(name, a, b, out_dtype, scale=1.0, tm_pref=1024, tn_pref=512):
    m, kd = a.shape
    n = b.shape[0]
    tm, tn = _tile(m, tm_pref), _tile(n, tn_pref)
    return _matmul(
        name, "nt", (m // tm, n // tn, 1),
        a, pl.BlockSpec((tm, kd), lambda i, j, k: (i, 0)),
        b, pl.BlockSpec((tn, kd), lambda i, j, k: (j, 0)),
        [], [jax.ShapeDtypeStruct((m, n), out_dtype)], [pl.BlockSpec((tm, tn), lambda i, j, k: (i, j))],
        None, _store(scale, out_dtype))[0]


def _mm_nt_k(name, a, b, out_dtype, tk_pref=1024, tm_pref=1024, tn_pref=1024):
    m, kd = a.shape
    n = b.shape[0]
    tm, tn, tk = _tile(m, tm_pref), _tile(n, tn_pref), _tile(kd, tk_pref)
    return _matmul(
        name, "nt", (m // tm, n // tn, kd // tk),
        a, pl.BlockSpec((tm, tk), lambda i, j, k: (i, k)),
        b, pl.BlockSpec((tn, tk), lambda i, j, k: (j, k)),
        [], [jax.ShapeDtypeStruct((m, n), out_dtype)], [pl.BlockSpec((tm, tn), lambda i, j, k: (i, j))],
        (tm, tn), _store(1.0, out_dtype))[0]


def _mm_tn(name, a, b, scale=1.0, tm_pref=1024, tn_pref=1024, tk_pref=1024, comm=None):
    t, m = a.shape
    n = b.shape[1]
    tm, tn, tk = _tile(m, tm_pref), _tile(n, tn_pref), _tile(t, tk_pref)
    res = _matmul(
        name, "tn", (m // tm, n // tn, t // tk),
        a, pl.BlockSpec((tk, tm), lambda i, j, k: (k, i)),
        b, pl.BlockSpec((tk, tn), lambda i, j, k: (k, j)),
        [], [jax.ShapeDtypeStruct((m, n), BF16)], [pl.BlockSpec((tm, tn), lambda i, j, k: (i, j))],
        (tm, tn), _store(scale, BF16), comm=comm)
    return res[0] if comm is None else (res[0], res[1:])


def _mm_residual(name, a, b, x, scale, tm_pref=1024, tn_pref=1024, tk_pref=1408, comm=None):
    m, kd = a.shape
    n = b.shape[1]
    tm, tn, tk = _tile(m, tm_pref), _tile(n, tn_pref), _tile(kd, tk_pref)

    def epilogue(acc, ex, outs, ids):
        outs[0][...] = ex[0][...] + scale * acc

    res = _matmul(
        name, "nn", (m // tm, n // tn, kd // tk),
        a, pl.BlockSpec((tm, tk), lambda i, j, k: (i, k)),
        b, pl.BlockSpec((tk, tn), lambda i, j, k: (k, j)),
        [(x, pl.BlockSpec((tm, tn), lambda i, j, k: (i, j)))],
        [jax.ShapeDtypeStruct((m, n), F32)], [pl.BlockSpec((tm, tn), lambda i, j, k: (i, j))],
        (tm, tn), epilogue, comm=comm)
    return res[0] if comm is None else (res[0], res[1:])


def _rms_fwd(name, x, g, with_transpose=False):
    t, d = x.shape
    tm = _tile(t, 512)

    def body(x_ref, g_ref, h_ref, *ht_ref):
        xv = x_ref[...]
        r = lax.rsqrt(jnp.mean(xv * xv, axis=-1, keepdims=True) + EPS)
        h = xv * r * g_ref[...]
        h_ref[...] = h.astype(BF16)
        if with_transpose:
            ht_ref[0][...] = h.T.astype(BF16)

    out_specs = [pl.BlockSpec((tm, d), lambda i: (i, 0))]
    out_shape = [jax.ShapeDtypeStruct((t, d), BF16)]
    if with_transpose:
        out_specs.append(pl.BlockSpec((d, tm), lambda i: (0, i)))
        out_shape.append(jax.ShapeDtypeStruct((d, t), BF16))
    res = pl.pallas_call(
        body, name=name, grid=(t // tm,),
        in_specs=[pl.BlockSpec((tm, d), lambda i: (i, 0)), pl.BlockSpec((1, d), lambda i: (0, 0))],
        out_specs=out_specs, out_shape=out_shape,
        compiler_params=_params(("parallel",)),
    )(x, g.reshape(1, d))
    return res if with_transpose else res[0]


def _rms_bwd_math(dh, xv, g):
    r = lax.rsqrt(jnp.mean(xv * xv, axis=-1, keepdims=True) + EPS)
    xhat = xv * r
    dxh = dh * g
    dx = r * (dxh - xhat * jnp.mean(dxh * xhat, axis=-1, keepdims=True))
    return dx, dh * xhat


def _rms_bwd(name, dh, x, g, dres):
    t, d = x.shape
    tm = _tile(t, 256)

    def body(dh_ref, x_ref, g_ref, dres_ref, dx_ref, dxb_ref, dg_ref):
        dx, dgt = _rms_bwd_math(dh_ref[...].astype(F32), x_ref[...], g_ref[...])
        dx = dres_ref[...] + dx
        dx_ref[...] = dx
        dxb_ref[...] = dx.astype(BF16)

        @pl.when(pl.program_id(0) == 0)
        def _():
            dg_ref[...] = jnp.zeros_like(dg_ref)

        dg_ref[...] += jnp.sum(dgt, axis=0, keepdims=True)

    row = pl.BlockSpec((tm, d), lambda i: (i, 0))
    vec = pl.BlockSpec((1, d), lambda i: (0, 0))
    return pl.pallas_call(
        body, name=name, grid=(t // tm,),
        in_specs=[row, row, vec, row],
        out_specs=[row, row, vec],
        out_shape=[jax.ShapeDtypeStruct((t, d), F32), jax.ShapeDtypeStruct((t, d), BF16),
                   jax.ShapeDtypeStruct((1, d), F32)],
        compiler_params=_params(("arbitrary",)),
    )(dh, x, g.reshape(1, d), dres)


def _loss_bwd(x, g, target):
    t, d = x.shape
    tm = _tile(t, 256)

    def body(x_ref, g_ref, tgt_ref, loss_ref, dx_ref, dxb_ref, dg_ref):
        xv = x_ref[...]
        gv = g_ref[...]
        r = lax.rsqrt(jnp.mean(xv * xv, axis=-1, keepdims=True) + EPS)
        err = xv * r * gv - tgt_ref[...]
        part = 0.5 * jnp.sum(jnp.mean(err * err, axis=-1, keepdims=True), axis=0, keepdims=True)
        dx, dgt = _rms_bwd_math(err * (1.0 / d), xv, gv)
        dx_ref[...] = dx
        dxb_ref[...] = dx.astype(BF16)

        @pl.when(pl.program_id(0) == 0)
        def _():
            dg_ref[...] = jnp.zeros_like(dg_ref)
            loss_ref[...] = jnp.zeros_like(loss_ref)

        dg_ref[...] += jnp.sum(dgt, axis=0, keepdims=True)
        loss_ref[...] += jnp.broadcast_to(part, loss_ref.shape)

    row = pl.BlockSpec((tm, d), lambda i: (i, 0))
    vec = pl.BlockSpec((1, d), lambda i: (0, 0))
    return pl.pallas_call(
        body, name="loss_bwd", grid=(t // tm,),
        in_specs=[row, vec, row],
        out_specs=[pl.BlockSpec((1, LANE), lambda i: (0, 0)), row, row, vec],
        out_shape=[jax.ShapeDtypeStruct((1, LANE), F32), jax.ShapeDtypeStruct((t, d), F32),
                   jax.ShapeDtypeStruct((t, d), BF16), jax.ShapeDtypeStruct((1, d), F32)],
        compiler_params=_params(("arbitrary",)),
    )(x, g.reshape(1, d), target)


UP_SPLIT = 2


def _sigmoid(x):
    return 0.5 * jnp.tanh(0.5 * x) + 0.5


def _ffn_up(name, h, w_in, comm=None):
    t, d = h.shape
    fs = w_in.shape[2]
    f = 2 * fs
    tm, tn = _tile(t, 2048), _tile(fs, 256)
    per = fs // tn
    grid = (t // tm, f // tn)
    c_in, c_in_specs, c_out_specs, c_shapes, c_sems, in_place = _comm_io(comm)
    n_c, n_co = len(c_in), len(c_shapes)

    def body(*refs):
        h_ref, wg_ref, wu_ref = refs[:3]
        gu_ref, z_ref = refs[3 + n_c:5 + n_c]
        if comm is not None:
            i, j = pl.program_id(0), pl.program_id(1)
            send, recv = refs[5 + n_c + n_co:]
            last, finish = _comm_run(comm, refs[3:3 + n_c], refs[5 + n_c:5 + n_c + n_co], send, recv,
                                     jnp.logical_and(i == 0, j == 0),
                                     jnp.logical_and(i == grid[0] - 1, j == grid[1] - 1))
        wg, wu = wg_ref[...], wu_ref[...]
        for r0 in range(0, tm, tm // UP_SPLIT):
            rows = slice(r0, r0 + tm // UP_SPLIT)
            hv = h_ref[rows, :]
            gate = _dot(hv, wg, "nn")
            up = _dot(hv, wu, "nn")
            sg = _sigmoid(gate)
            silu = gate * sg
            gu_ref[0, rows, :] = (sg * (1.0 + gate * (1.0 - sg)) * up).astype(BF16)
            gu_ref[1, rows, :] = silu.astype(BF16)
            z_ref[rows, :] = (silu * up).astype(BF16)
        if comm is not None:
            pl.when(last)(finish)

    res = pl.pallas_call(
        body, name=name, grid=grid,
        in_specs=[pl.BlockSpec((tm, d), lambda i, j: (i, 0)),
                  pl.BlockSpec((None, d, tn), lambda i, j: (j // per, 0, j % per)),
                  pl.BlockSpec((None, d, tn), lambda i, j: (2 + j // per, 0, j % per))] + c_in_specs,
        out_specs=[pl.BlockSpec((2, tm, tn), lambda i, j: (0, i, j)),
                   pl.BlockSpec((tm, tn), lambda i, j: (i, j))] + c_out_specs,
        out_shape=[jax.ShapeDtypeStruct((2, t, f), BF16), jax.ShapeDtypeStruct((t, f), BF16)] + c_shapes,
        input_output_aliases={3 + k: 2 + k for k in range(n_c)} if in_place else {},
        scratch_shapes=c_sems,
        compiler_params=_params(("parallel", "parallel") if comm is None else ("arbitrary", "arbitrary")),
    )(h, w_in, w_in, *c_in)
    return (res[0], res[1]) if comm is None else (res[0], res[1], res[2:])


def _ffn_dact(name, dxb, w_out, gu):
    t, d = dxb.shape
    f = w_out.shape[0]
    tm, tn = _tile(t, 1024), _tile(f, 512)

    def epilogue(acc, ex, outs, ids):
        dz = 0.5 * acc
        outs[0][0] = (dz * ex[0][0].astype(F32)).astype(BF16)
        outs[0][1] = (dz * ex[0][1].astype(F32)).astype(BF16)

    blk = pl.BlockSpec((2, tm, tn), lambda i, j, k: (0, i, j))
    return _matmul(
        name, "nt", (t // tm, f // tn, 1),
        dxb, pl.BlockSpec((tm, d), lambda i, j, k: (i, 0)),
        w_out, pl.BlockSpec((tn, d), lambda i, j, k: (j, 0)),
        [(gu, blk)], [jax.ShapeDtypeStruct((2, t, f), BF16)], [blk], None, epilogue)[0]


def _grad_colsharded(name, ht, da, comm=None):
    d, t = ht.shape
    w = da.shape[2]
    ws = w // 2
    tm, tn, tk = _tile(d, 1024), _tile(ws, 1408), _tile(t, 2048)
    per = ws // tn
    res = _matmul(
        name, "nn", (d // tm, (2 * w) // tn, t // tk),
        ht, pl.BlockSpec((tm, tk), lambda i, j, k: (i, k)),
        da, pl.BlockSpec((None, tk, tn), lambda i, j, k: (j // (2 * per), k, j % (2 * per))),
        [], [jax.ShapeDtypeStruct((N_CHIPS, d, ws), BF16)],
        [pl.BlockSpec((None, tm, tn), lambda i, j, k: (j // per, i, j % per))],
        (tm, tn), _store(1.0, BF16), comm=comm)
    return res[0] if comm is None else (res[0], res[1:])


def _back_colsharded(name, da, w_g, comm=None):
    _, t, w = da.shape
    d, ws = w_g.shape[1], w_g.shape[2]
    tm, tn, tk = _tile(t, 1024), _tile(d, 1024), _tile(ws, 2816)
    per = ws // tk
    res = _matmul(
        name, "nt", (t // tm, d // tn, (2 * w) // tk),
        da, pl.BlockSpec((None, tm, tk), lambda i, j, k: (k // (2 * per), i, k % (2 * per))),
        w_g, pl.BlockSpec((None, tn, tk), lambda i, j, k: (k // per, j, k % per)),
        [], [jax.ShapeDtypeStruct((t, d), F32)], [pl.BlockSpec((tm, tn), lambda i, j, k: (i, j))],
        (tm, tn), _store(1.0, F32), comm=comm)
    return res[0] if comm is None else (res[0], res[1:])


def _ffn_fwd(tag, x, g, w_in, w_out, prefetch):
    h, ht = _rms_fwd(tag + "_norm", x, g, with_transpose=True)
    if prefetch:
        gu, z, prefetch = _ffn_up(tag + "_up", h, w_in, comm=_gather_phase(1, prefetch))
        y, prefetch = _mm_residual(tag + "_down", z, w_out, x, 0.5, tk_pref=2816, comm=_gather_phase(2, prefetch))
    else:
        gu, z = _ffn_up(tag + "_up", h, w_in)
        y = _mm_residual(tag + "_down", z, w_out, x, 0.5, tk_pref=2816)
    return y, (x, ht, gu, z), prefetch


def _ffn_bwd(tag, saved, g, w_in, w_out, dx, dxb, red):
    x, ht, gu, z = saved
    f = z.shape[1]
    d_w_out = red.behind(0, _mm_tn, tag + "_dwout", z, dxb, scale=0.5, tm_pref=1408, tn_pref=2048)
    da = _ffn_dact(tag + "_dact", dxb, w_out, gu)
    d_w_in = red.behind(1, _grad_colsharded, tag + "_dwin", ht, da)
    dh = red.behind(2, _back_colsharded, tag + "_dh", da, w_in)
    dx, dxb, dg = _rms_bwd(tag + "_dnorm", dh, x, g, dx)
    return dx, dxb, dg, d_w_in, d_w_out.reshape(N_CHIPS, f // N_CHIPS, -1)


_GELU_K = math.sqrt(2.0 / math.pi)
_GELU_C = 0.044715


def _gelu(x):
    t = jnp.tanh(_GELU_K * (x + _GELU_C * x * x * x))
    return 0.5 * x * (1.0 + t), t


def _dgelu(x, t):
    return 0.5 * (1.0 + t) + 0.5 * x * (1.0 - t * t) * (_GELU_K * (1.0 + 3.0 * _GELU_C * x * x))


def _causal_block_mask():
    r = lax.broadcasted_iota(jnp.int32, (SGU_BLOCK, SGU_BLOCK), 0) // CHUNK
    c = lax.broadcasted_iota(jnp.int32, (SGU_BLOCK, SGU_BLOCK), 1) // CHUNK
    return r >= c


def _sgu_pre(name, h, w_in, comm=None):
    t, d = h.shape
    ws = w_in.shape[2]
    w = 2 * ws
    tm, tn = _tile(t, 1024), _tile(ws, 512)
    per = ws // tn
    res = _matmul(
        name, "nn", (t // tm, (2 * w) // tn, 1),
        h, pl.BlockSpec((tm, d), lambda i, j, k: (i, 0)),
        w_in, pl.BlockSpec((None, d, tn), lambda i, j, k: (j // per, 0, j % per)),
        [], [jax.ShapeDtypeStruct((2, t, w), BF16)],
        [pl.BlockSpec((None, tm, tn), lambda i, j, k: (j // (2 * per), i, j % (2 * per)))],
        None, _store(1.0, BF16), comm=comm)
    return res[0] if comm is None else (res[0], res[1:])


def _layernorm_stats(v):
    mu = jnp.mean(v, axis=-1, keepdims=True)
    vc = v - mu
    rstd = lax.rsqrt(jnp.mean(vc * vc, axis=-1, keepdims=True) + EPS)
    return vc * rstd, rstd


def _sgu_mid_fwd(pre, gain, bias, w_sp, b_sp_t):
    _, t, w = pre.shape
    gd = w // SGU_GROUPS

    def body(pre_ref, gain_ref, bias_ref, ws_ref, bt_ref, out_ref):
        mask = _causal_block_mask()
        u, _ = _gelu(pre_ref[0].astype(F32))
        v, _ = _gelu(pre_ref[1].astype(F32))
        vhat, _ = _layernorm_stats(v)
        vln = (vhat * gain_ref[...] + bias_ref[...]).astype(BF16)
        for gi in range(SGU_GROUPS):
            cols = slice(gi * gd, (gi + 1) * gd)
            wg = jnp.where(mask, ws_ref[gi], 0.0).astype(BF16)
            mixed = _dot(wg, vln[:, cols], "nn") + bt_ref[:, gi:gi + 1]
            out_ref[:, cols] = (u[:, cols] * mixed).astype(BF16)

    return pl.pallas_call(
        body, name="sgu_mid_fwd", grid=(t // SGU_BLOCK,),
        in_specs=[pl.BlockSpec((2, SGU_BLOCK, w), lambda n: (0, n, 0)),
                  pl.BlockSpec((1, w), lambda n: (0, 0)), pl.BlockSpec((1, w), lambda n: (0, 0)),
                  pl.BlockSpec((SGU_GROUPS, SGU_BLOCK, SGU_BLOCK), lambda n: (0, 0, 0)),
                  pl.BlockSpec((SGU_BLOCK, SGU_GROUPS), lambda n: (0, 0))],
        out_specs=pl.BlockSpec((SGU_BLOCK, w), lambda n: (n, 0)),
        out_shape=jax.ShapeDtypeStruct((t, w), BF16),
        compiler_params=_params(("parallel",)),
    )(pre, gain, bias, w_sp, b_sp_t)


def _sgu_mid_bwd(pre, dgated, gain, bias, w_sp, b_sp_t):
    _, t, w = pre.shape
    gd = w // SGU_GROUPS

    def body(pre_ref, dg_ref, gain_ref, bias_ref, ws_ref, bt_ref,
             dpre_ref, dgain_ref, dbias_ref, dws_ref, dbt_ref, dvln_s):
        @pl.when(pl.program_id(0) == 0)
        def _():
            dgain_ref[...] = jnp.zeros_like(dgain_ref)
            dbias_ref[...] = jnp.zeros_like(dbias_ref)
            dws_ref[...] = jnp.zeros_like(dws_ref)
            dbt_ref[...] = jnp.zeros_like(dbt_ref)

        mask = _causal_block_mask()
        pu = pre_ref[0].astype(F32)
        pv = pre_ref[1].astype(F32)
        u, tu = _gelu(pu)
        v, tv = _gelu(pv)
        vhat, rstd = _layernorm_stats(v)
        gain_v = gain_ref[...]
        vln = (vhat * gain_v + bias_ref[...]).astype(BF16)
        dgt = dg_ref[...].astype(F32)
        for gi in range(SGU_GROUPS):
            cols = slice(gi * gd, (gi + 1) * gd)
            wg = jnp.where(mask, ws_ref[gi], 0.0).astype(BF16)
            vg = vln[:, cols]
            mixed = _dot(wg, vg, "nn") + bt_ref[:, gi:gi + 1]
            dgg = dgt[:, cols]
            dmixed = dgg * u[:, cols]
            dmb = dmixed.astype(BF16)
            dpre_ref[0, :, cols] = (dgg * mixed * _dgelu(pu[:, cols], tu[:, cols])).astype(BF16)
            dbt_ref[:, gi:gi + 1] += jnp.sum(dmixed, axis=1, keepdims=True)
            dws_ref[gi] += jnp.where(mask, _dot(dmb, vg, "nt"), 0.0)
            dvln_s[:, cols] = _dot(wg, dmb, "tn")
        dvln = dvln_s[...]
        dgain_ref[...] += jnp.sum(dvln * vhat, axis=0, keepdims=True)
        dbias_ref[...] += jnp.sum(dvln, axis=0, keepdims=True)
        dvh = dvln * gain_v
        dv = rstd * (dvh - jnp.mean(dvh, axis=-1, keepdims=True)
                     - vhat * jnp.mean(dvh * vhat, axis=-1, keepdims=True))
        dpre_ref[1] = (dv * _dgelu(pv, tv)).astype(BF16)

    vec = pl.BlockSpec((1, w), lambda n: (0, 0))
    wsb = pl.BlockSpec((SGU_GROUPS, SGU_BLOCK, SGU_BLOCK), lambda n: (0, 0, 0))
    btb = pl.BlockSpec((SGU_BLOCK, SGU_GROUPS), lambda n: (0, 0))
    blk2 = pl.BlockSpec((2, SGU_BLOCK, w), lambda n: (0, n, 0))
    return pl.pallas_call(
        body, name="sgu_mid_bwd", grid=(t // SGU_BLOCK,),
        in_specs=[blk2, pl.BlockSpec((SGU_BLOCK, w), lambda n: (n, 0)), vec, vec, wsb, btb],
        out_specs=[blk2, vec, vec, wsb, btb],
        out_shape=[jax.ShapeDtypeStruct((2, t, w), BF16), jax.ShapeDtypeStruct((1, w), F32),
                   jax.ShapeDtypeStruct((1, w), F32),
                   jax.ShapeDtypeStruct((SGU_GROUPS, SGU_BLOCK, SGU_BLOCK), F32),
                   jax.ShapeDtypeStruct((SGU_BLOCK, SGU_GROUPS), F32)],
        scratch_shapes=[pltpu.VMEM((SGU_BLOCK, w), F32)],
        compiler_params=_params(("arbitrary",)),
    )(pre, dgated, gain, bias, w_sp, b_sp_t)


def _sgu_fwd(x, g, w_in, gain, bias, w_sp, b_sp, w_out, prefetch):
    h, ht = _rms_fwd("sgu_norm", x, g, with_transpose=True)
    pre, prefetch = _sgu_pre("sgu_pre", h, w_in, comm=_gather_phase(1, prefetch))
    gated = _sgu_mid_fwd(pre, gain, bias, w_sp, b_sp.T)
    y, prefetch = _mm_residual("sgu_out", gated, w_out, x, 1.0, tk_pref=1024, comm=_gather_phase(2, prefetch))
    return y, (x, ht, pre, gated), prefetch


def _sgu_bwd(saved, g, w_in, gain, bias, w_sp, b_sp, w_out, dx, dxb, red):
    x, ht, pre, gated = saved
    w = gated.shape[1]
    d_w_out = red.behind(0, _mm_tn, "sgu_dwout", gated, dxb)
    dgated = _mm_nt_full("sgu_dgated", dxb, w_out, BF16)
    dpre, dgain, dbias, dws, dbt = _sgu_mid_bwd(pre, dgated, gain, bias, w_sp, b_sp.T)
    d_w_in = red.behind(1, _grad_colsharded, "sgu_dwin", ht, dpre)
    dh = red.behind(2, _back_colsharded, "sgu_dh", dpre, w_in)
    dx, dxb, dg = _rms_bwd("sgu_dnorm", dh, x, g, dx)
    small = dict(ln=dg, gain=dgain, bias=dbias, w_sp=dws, b_sp=dbt.T)
    return dx, dxb, small, d_w_in, d_w_out.reshape(N_CHIPS, w // N_CHIPS, -1)


def _rope_tables(positions):
    half = QK_ROPE // 2
    inv_freq = 1.0 / (ROPE_THETA ** (jnp.arange(half, dtype=F32) / half))
    ang = positions.astype(F32)[:, None] * inv_freq
    cos, sin = jnp.cos(ang), jnp.sin(ang)
    t = positions.shape[0]
    zeros = jnp.zeros((t, half), F32)
    rest = jnp.zeros((t, LANE - QK_ROPE), F32)
    c = jnp.concatenate([cos, cos, rest + 1.0], axis=1)
    s_up = jnp.concatenate([zeros, sin, rest], axis=1)
    s_dn = jnp.concatenate([-sin, zeros, rest], axis=1)
    return c, s_up, s_dn


def _rope_apply(x, c, s_up, s_dn):
    half = QK_ROPE // 2
    return x * c + pltpu.roll(x, half, 1) * s_up + pltpu.roll(x, LANE - half, 1) * s_dn


def _rope_apply_t(dy, c, s_up, s_dn):
    half = QK_ROPE // 2
    return dy * c - pltpu.roll(dy, LANE - half, 1) * s_dn - pltpu.roll(dy, half, 1) * s_up


def _mla_norm_fwd(proj, gq, gkv):
    t, p = proj.shape
    ql, kvl = gq.shape[1], gkv.shape[1]
    tm = _tile(t, 512)

    def body(p_ref, gq_ref, gkv_ref, qn_ref, kvn_ref):
        for lo, n, g_ref, o_ref in ((0, ql, gq_ref, qn_ref), (ql, kvl, gkv_ref, kvn_ref)):
            xv = p_ref[:, lo:lo + n]
            r = lax.rsqrt(jnp.mean(xv * xv, axis=-1, keepdims=True) + EPS)
            o_ref[...] = (xv * r * g_ref[...]).astype(BF16)

    return pl.pallas_call(
        body, name="mla_norm_fwd", grid=(t // tm,),
        in_specs=[pl.BlockSpec((tm, p), lambda i: (i, 0)), pl.BlockSpec((1, ql), lambda i: (0, 0)),
                  pl.BlockSpec((1, kvl), lambda i: (0, 0))],
        out_specs=[pl.BlockSpec((tm, ql), lambda i: (i, 0)), pl.BlockSpec((tm, kvl), lambda i: (i, 0))],
        out_shape=[jax.ShapeDtypeStruct((t, ql), BF16), jax.ShapeDtypeStruct((t, kvl), BF16)],
        compiler_params=_params(("parallel",)),
    )(proj, gq, gkv)


def _mla_norm_bwd(proj, dqn, dkvn, dkr, gq, gkv):
    t, p = proj.shape
    ql, kvl = gq.shape[1], gkv.shape[1]
    tm = _tile(t, 256)

    def body(p_ref, dqn_ref, dkvn_ref, dkr_ref, gq_ref, gkv_ref, dp_ref, dgq_ref, dgkv_ref):
        @pl.when(pl.program_id(0) == 0)
        def _():
            dgq_ref[...] = jnp.zeros_like(dgq_ref)
            dgkv_ref[...] = jnp.zeros_like(dgkv_ref)

        for lo, n, g_ref, d_ref, dg_ref in ((0, ql, gq_ref, dqn_ref, dgq_ref),
                                             (ql, kvl, gkv_ref, dkvn_ref, dgkv_ref)):
            dx, dgt = _rms_bwd_math(d_ref[...], p_ref[:, lo:lo + n], g_ref[...])
            dp_ref[:, lo:lo + n] = dx.astype(BF16)
            dg_ref[...] += jnp.sum(dgt, axis=0, keepdims=True)
        dp_ref[:, ql + kvl:] = dkr_ref[...].astype(BF16)

    def row(n):
        return pl.BlockSpec((tm, n), lambda i: (i, 0))

    def vec(n):
        return pl.BlockSpec((1, n), lambda i: (0, 0))

    return pl.pallas_call(
        body, name="mla_norm_bwd", grid=(t // tm,),
        in_specs=[row(p), row(ql), row(kvl), row(LANE), vec(ql), vec(kvl)],
        out_specs=[row(p), vec(ql), vec(kvl)],
        out_shape=[jax.ShapeDtypeStruct((t, p), BF16), jax.ShapeDtypeStruct((1, ql), F32),
                   jax.ShapeDtypeStruct((1, kvl), F32)],
        compiler_params=_params(("arbitrary",)),
    )(proj, dqn, dkvn, dkr, gq, gkv)


def _mla_q_up(qn, wq, tables):
    t, ql = qn.shape
    n = wq.shape[1]
    tm = _tile(t, 2048)
    scale = QK_DIM ** -0.5

    def epilogue(acc, ex, outs, ids):
        outs[0][:, :QK_NOPE] = (scale * acc[:, :QK_NOPE]).astype(BF16)
        hi = _rope_apply(acc[:, QK_NOPE:], ex[0][...], ex[1][...], ex[2][...])
        outs[0][:, QK_NOPE:] = (scale * hi).astype(BF16)

    tab = pl.BlockSpec((tm, LANE), lambda i, j, k: (i, 0))
    return _matmul(
        "mla_q_up", "nn", (t // tm, n // HEAD_PAD, 1),
        qn, pl.BlockSpec((tm, ql), lambda i, j, k: (i, 0)),
        wq, pl.BlockSpec((ql, HEAD_PAD), lambda i, j, k: (0, j)),
        [(tb, tab) for tb in tables],
        [jax.ShapeDtypeStruct((t, n), BF16)], [pl.BlockSpec((tm, HEAD_PAD), lambda i, j, k: (i, j))],
        None, epilogue)[0]


def _mla_kv_up(kvn, wkv, proj, tables, heads):
    t, kvl = kvn.shape
    n = wkv.shape[1]
    p = proj.shape[1]
    tm = _tile(t, 2048)

    def epilogue(acc, ex, outs, ids):
        kr = _rope_apply(ex[0][...], ex[1][...], ex[2][...], ex[3][...])
        outs[0][:, :QK_NOPE] = acc[:, :QK_NOPE].astype(BF16)
        outs[0][:, QK_NOPE:] = (acc[:, QK_NOPE:] + jnp.where(ids[1] < heads, kr, 1.0)).astype(BF16)

    tab = pl.BlockSpec((tm, LANE), lambda i, j, k: (i, 0))
    kr_spec = pl.BlockSpec((tm, LANE), lambda i, j, k: (i, p // LANE - 1))
    return _matmul(
        "mla_kv_up", "nn", (t // tm, n // HEAD_PAD, 1),
        kvn, pl.BlockSpec((tm, kvl), lambda i, j, k: (i, 0)),
        wkv, pl.BlockSpec((kvl, HEAD_PAD), lambda i, j, k: (0, j)),
        [(proj, kr_spec)] + [(tb, tab) for tb in tables],
        [jax.ShapeDtypeStruct((t, n), BF16)], [pl.BlockSpec((tm, HEAD_PAD), lambda i, j, k: (i, j))],
        None, epilogue)[0]


FLASH_TQ = 1024
FLASH_TK = 1024
FLASH_SPLIT = 4
FLASH_SPLIT_BWD = 2


def _chunk_mask(tq, tk, qi, ki):
    r = (qi * tq + lax.broadcasted_iota(jnp.int32, (tq, tk), 0)) // CHUNK
    c = (ki * tk + lax.broadcasted_iota(jnp.int32, (tq, tk), 1)) // CHUNK
    return c <= r


def _block_pairs(t, tq, tk, key_major):
    def visible(qi, ki):
        return (ki * tk) // CHUNK <= (qi * tq + tq - 1) // CHUNK

    def masked(qi, ki):
        return (ki * tk + tk - 1) // CHUNK > (qi * tq) // CHUNK

    nq, nk = t // tq, t // tk
    if key_major:
        sweeps = [[(qi, ki) for qi in range(nq) if visible(qi, ki)] for ki in range(nk)]
    else:
        sweeps = [[(qi, ki) for ki in range(nk) if visible(qi, ki)] for qi in range(nq)]
    qs, ks, fs = [], [], []
    for sweep in sweeps:
        for n, (qi, ki) in enumerate(sweep):
            qs.append(qi)
            ks.append(ki)
            fs.append((1 if masked(qi, ki) else 0) + (2 if n == 0 else 0) + (4 if n == len(sweep) - 1 else 0))
    return tuple(jnp.asarray(v, jnp.int32) for v in (qs, ks, fs))


def _flash_fwd(qp, kv, heads):
    t = qp.shape[0]
    tq, tk = _tile(t, FLASH_TQ), _tile(t, FLASH_TK)
    qt, kt, ft = _block_pairs(t, tq, tk, key_major=False)

    def body(qt_ref, kt_ref, ft_ref, q_ref, k_ref, v_ref, o_ref, lse_ref, m_s, acc_s):
        p = pl.program_id(1)
        qi, ki, flags = qt_ref[p], kt_ref[p], ft_ref[p]

        @pl.when(flags & 2 != 0)
        def _():
            m_s[...] = jnp.full_like(m_s, -1e30)
            acc_s[...] = jnp.zeros_like(acc_s)

        def step(masked):
            k, v = k_ref[...], v_ref[...]
            mask = _chunk_mask(tq, tk, qi, ki) if masked else None
            tg = tq // FLASH_SPLIT
            for g in range(FLASH_SPLIT):
                rows = slice(g * tg, (g + 1) * tg)
                nc = min(tk, (g + 1) * tg) if masked and tq == tk else tk
                s = _dot(q_ref[rows, :], k[:nc], "nt")
                if masked:
                    s = jnp.where(mask[rows, :nc], s, -1e30)
                m_prev = m_s[rows, :]
                m_new = jnp.maximum(m_prev, jnp.max(s, axis=1, keepdims=True))
                alpha = jnp.exp(m_prev - m_new)
                pr = jnp.exp(s - jnp.tile(m_new, (1, nc // LANE))).astype(BF16)
                pv = _dot(pr, v[:nc], "nn")
                acc_s[rows, :V_DIM] = alpha * acc_s[rows, :V_DIM] + pv[:, :V_DIM]
                acc_s[rows, V_DIM:] = alpha * acc_s[rows, V_DIM:] + pv[:, V_DIM:]
                m_s[rows, :] = m_new

        @pl.when(flags & 1 == 0)
        def _():
            step(False)

        @pl.when(flags & 1 != 0)
        def _():
            step(True)

        @pl.when(flags & 4 != 0)
        def _():
            l = acc_s[:, V_DIM:]
            o_ref[...] = (acc_s[:, :V_DIM] / l).astype(BF16)
            lse_ref[...] = m_s[...] + jnp.log(l)

    return pl.pallas_call(
        body, name="mla_flash_fwd",
        grid_spec=pltpu.PrefetchScalarGridSpec(
            num_scalar_prefetch=3, grid=(heads, int(qt.shape[0])),
            in_specs=[pl.BlockSpec((tq, HEAD_PAD), lambda h, p, qt, kt, ft: (qt[p], h)),
                      pl.BlockSpec((tk, HEAD_PAD), lambda h, p, qt, kt, ft: (kt[p], h)),
                      pl.BlockSpec((tk, HEAD_PAD), lambda h, p, qt, kt, ft: (kt[p], heads + h))],
            out_specs=[pl.BlockSpec((tq, V_DIM), lambda h, p, qt, kt, ft: (qt[p], h)),
                       pl.BlockSpec((None, tq, LANE), lambda h, p, qt, kt, ft: (h, qt[p], 0))],
            scratch_shapes=[pltpu.VMEM((tq, LANE), F32), pltpu.VMEM((tq, HEAD_PAD), F32)]),
        out_shape=[jax.ShapeDtypeStruct((t, heads * V_DIM), BF16),
                   jax.ShapeDtypeStruct((heads, t, LANE), F32)],
        compiler_params=_params(("parallel", "arbitrary")),
    )(qt, kt, ft, qp, kv, kv)


def _flash_delta(o, do, heads):
    t = o.shape[0]
    tm = _tile(t, 256)

    def body(o_ref, do_ref, d_ref):
        for h in range(heads):
            cols = slice(h * V_DIM, (h + 1) * V_DIM)
            prod = o_ref[:, cols].astype(F32) * do_ref[:, cols].astype(F32)
            d_ref[h] = jnp.broadcast_to(jnp.sum(prod, axis=1, keepdims=True), (tm, LANE))

    row = pl.BlockSpec((tm, heads * V_DIM), lambda i: (i, 0))
    return pl.pallas_call(
        body, name="mla_flash_delta", grid=(t // tm,), in_specs=[row, row],
        out_specs=pl.BlockSpec((heads, tm, LANE), lambda i: (0, i, 0)),
        out_shape=jax.ShapeDtypeStruct((heads, t, LANE), F32),
        compiler_params=_params(("parallel",)),
    )(o, do)


def _flash_bwd(qp, kv, do, lse, delta, heads):
    t = qp.shape[0]
    tq, tk = _tile(t, FLASH_TQ), _tile(t, FLASH_TK)
    qt, kt, ft = _block_pairs(t, tq, tk, key_major=True)

    def body(qt_ref, kt_ref, ft_ref, q_ref, k_ref, v_ref, do_ref, lse_ref, dl_ref, dq_ref, dk_ref, dv_ref,
             dk_s, dv_s):
        p = pl.program_id(1)
        qi, ki, flags = qt_ref[p], kt_ref[p], ft_ref[p]

        @pl.when(p == 0)
        def _():
            dq_ref[...] = jnp.zeros_like(dq_ref)

        @pl.when(flags & 2 != 0)
        def _():
            dk_s[...] = jnp.zeros_like(dk_s)
            dv_s[...] = jnp.zeros_like(dv_s)

        def step(masked):
            k, v = k_ref[...], v_ref[...]
            mask = _chunk_mask(tq, tk, qi, ki) if masked else None
            tg = tq // FLASH_SPLIT_BWD
            for g in range(FLASH_SPLIT_BWD):
                rows = slice(g * tg, (g + 1) * tg)
                nc = min(tk, (g + 1) * tg) if masked and tq == tk else tk
                q = q_ref[rows, :]
                dov = do_ref[rows, :]
                s = _dot(q, k[:nc], "nt")
                pr = jnp.exp(s - jnp.tile(lse_ref[rows, :], (1, nc // LANE)))
                if masked:
                    pr = jnp.where(mask[rows, :nc], pr, 0.0)
                dv_s[:nc, :] += _dot(pr.astype(BF16), dov, "tn")
                dp = _dot(dov, v[:nc], "nt")
                ds = (pr * (dp - jnp.tile(dl_ref[rows, :], (1, nc // LANE)))).astype(BF16)
                dq_rows = pl.ds(pl.multiple_of(qi * tq + g * tg, tg), tg)
                dq_ref[dq_rows, :] += _dot(ds, k[:nc], "nn")
                dk_s[:nc, :] += _dot(ds, q, "tn")

        @pl.when(flags & 1 == 0)
        def _():
            step(False)

        @pl.when(flags & 1 != 0)
        def _():
            step(True)

        @pl.when(flags & 4 != 0)
        def _():
            dk_ref[...] = dk_s[...]
            dv_ref[...] = dv_s[...]

    def qrow(width):
        return pl.BlockSpec((tq, width), lambda h, p, qt, kt, ft: (qt[p], h))

    def stat():
        return pl.BlockSpec((None, tq, LANE), lambda h, p, qt, kt, ft: (h, qt[p], 0))

    return pl.pallas_call(
        body, name="mla_flash_bwd",
        grid_spec=pltpu.PrefetchScalarGridSpec(
            num_scalar_prefetch=3, grid=(heads, int(qt.shape[0])),
            in_specs=[qrow(HEAD_PAD),
                      pl.BlockSpec((tk, HEAD_PAD), lambda h, p, qt, kt, ft: (kt[p], h)),
                      pl.BlockSpec((tk, V_DIM), lambda h, p, qt, kt, ft: (kt[p], 2 * (heads + h))),
                      qrow(V_DIM), stat(), stat()],
            out_specs=[pl.BlockSpec((t, HEAD_PAD), lambda h, p, qt, kt, ft: (0, h)),
                       pl.BlockSpec((tk, HEAD_PAD), lambda h, p, qt, kt, ft: (kt[p], h)),
                       pl.BlockSpec((tk, V_DIM), lambda h, p, qt, kt, ft: (kt[p], h))],
            scratch_shapes=[pltpu.VMEM((tk, HEAD_PAD), F32), pltpu.VMEM((tk, V_DIM), F32)]),
        out_shape=[jax.ShapeDtypeStruct((t, heads * HEAD_PAD), F32),
                   jax.ShapeDtypeStruct((t, heads * HEAD_PAD), F32),
                   jax.ShapeDtypeStruct((t, heads * V_DIM), F32)],
        compiler_params=_params(("parallel", "arbitrary")),
    )(qt, kt, ft, qp, kv, kv, do, lse, delta)


def _mla_attn_post(dqp, dkp, dv, tables, heads):
    t = dqp.shape[0]
    tm = _tile(t, 256)
    scale = QK_DIM ** -0.5
    kw, vw = heads * HEAD_PAD, heads * V_DIM

    def body(dq_ref, dk_ref, dv_ref, c_ref, su_ref, sd_ref, dqb_ref, dkvb_ref, dkr_ref):
        c, su, sd = c_ref[...], su_ref[...], sd_ref[...]
        kr = jnp.zeros((tm, LANE), F32)
        for h in range(heads):
            lo = h * HEAD_PAD
            mid = lo + QK_NOPE
            dqb_ref[:, lo:mid] = (scale * dq_ref[:, lo:mid]).astype(BF16)
            dqb_ref[:, mid:mid + LANE] = (scale * _rope_apply_t(dq_ref[:, mid:mid + LANE], c, su, sd)).astype(BF16)
            kr = kr + dk_ref[:, mid:mid + LANE]
            dkvb_ref[:, kw + lo:kw + mid] = dv_ref[:, h * V_DIM:(h + 1) * V_DIM].astype(BF16)
            dkvb_ref[:, kw + mid:kw + lo + HEAD_PAD] = jnp.zeros((tm, HEAD_PAD - V_DIM), BF16)
        dkvb_ref[:, :kw] = dk_ref[...].astype(BF16)
        dkr_ref[...] = _rope_apply_t(kr, c, su, sd)

    def row(n):
        return pl.BlockSpec((tm, n), lambda i: (i, 0))

    return pl.pallas_call(
        body, name="mla_attn_post", grid=(t // tm,),
        in_specs=[row(kw), row(kw), row(vw), row(LANE), row(LANE), row(LANE)],
        out_specs=[row(kw), row(2 * kw), row(LANE)],
        out_shape=[jax.ShapeDtypeStruct((t, kw), BF16), jax.ShapeDtypeStruct((t, 2 * kw), BF16),
                   jax.ShapeDtypeStruct((t, LANE), F32)],
        compiler_params=_params(("parallel",)),
    )(dqp, dkp, dv, *tables)


def _mla_weights(w_in_g, w_q_g, w_kv_g, w_out_g):
    d = w_in_g.shape[0] * w_in_g.shape[1]
    pw = w_in_g.shape[2]
    w_in = jnp.pad(w_in_g.reshape(d, pw), ((0, 0), (0, LANE - QK_ROPE)))
    ql = w_q_g.shape[1]
    wq = jnp.transpose(w_q_g, (1, 0, 2)).reshape(ql, -1, QK_DIM)
    heads = wq.shape[1]
    wq = jnp.pad(wq, ((0, 0), (0, 0), (0, HEAD_PAD - QK_DIM))).reshape(ql, heads * HEAD_PAD)
    kvl = w_kv_g.shape[1]
    wkv = jnp.transpose(w_kv_g, (1, 0, 2)).reshape(kvl, heads, QK_NOPE + V_DIM)
    wk = jnp.pad(wkv[:, :, :QK_NOPE], ((0, 0), (0, 0), (0, HEAD_PAD - QK_NOPE))).reshape(kvl, heads * HEAD_PAD)
    wv = jnp.pad(wkv[:, :, QK_NOPE:], ((0, 0), (0, 0), (0, HEAD_PAD - V_DIM))).reshape(kvl, heads * HEAD_PAD)
    return w_in, wq, jnp.concatenate([wk, wv], axis=1), w_out_g.reshape(heads * V_DIM, -1), heads


def _mla_unpermute(d_w_in, d_wq, d_wkv, heads, pw):
    d = d_w_in.shape[0]
    g_in = d_w_in[:, :pw].reshape(N_CHIPS, d // N_CHIPS, pw)
    ql = d_wq.shape[0]
    g_q = d_wq.reshape(ql, heads, HEAD_PAD)[:, :, :QK_DIM].reshape(ql, N_CHIPS, -1)
    kvl = d_wkv.shape[0]
    g_k = d_wkv[:, :heads * HEAD_PAD].reshape(kvl, heads, HEAD_PAD)[:, :, :QK_NOPE]
    g_v = d_wkv[:, heads * HEAD_PAD:].reshape(kvl, heads, HEAD_PAD)[:, :, :V_DIM]
    g_kv = jnp.concatenate([g_k, g_v], axis=2).reshape(kvl, N_CHIPS, -1)
    return g_in, jnp.transpose(g_q, (1, 0, 2)), jnp.transpose(g_kv, (1, 0, 2))


def _mla_fwd(x, g, wts, gq, gkv, tables):
    w_in, wq, wkv, w_out, heads = wts
    h = _rms_fwd("mla_norm", x, g)
    proj = _mm_nn_full("mla_proj", h, w_in, F32, tn_pref=w_in.shape[1])
    qn, kvn = _mla_norm_fwd(proj, gq, gkv)
    qp = _mla_q_up(qn, wq, tables)
    kv = _mla_kv_up(kvn, wkv, proj, tables, heads)
    o, lse = _flash_fwd(qp, kv, heads)
    y = _mm_residual("mla_out", o, w_out, x, 1.0, tk_pref=2048)
    return y, (x, h, proj, qn, kvn, qp, kv, o, lse)


def _mla_bwd(saved, g, wts, gq, gkv, tables, dx, dxb, pw):
    w_in, wq, wkv, w_out, heads = wts
    x, h, proj, qn, kvn, qp, kv, o, lse = saved
    d_w_out = _mm_tn("mla_dwout", o, dxb)
    do = _mm_nt_full("mla_do", dxb, w_out, BF16)
    dqp, dkp, dv = _flash_bwd(qp, kv, do, lse, _flash_delta(o, do, heads), heads)
    dqb, dkvb, dkr = _mla_attn_post(dqp, dkp, dv, tables, heads)
    d_wq = _mm_tn("mla_dwq", qn, dqb)
    dqn = _mm_nt_k("mla_dqn", dqb, wq, F32)
    d_wkv = _mm_tn("mla_dwkv", kvn, dkvb)
    dkvn = _mm_nt_k("mla_dkvn", dkvb, wkv, F32)
    dproj, dgq, dgkv = _mla_norm_bwd(proj, dqn, dkvn, dkr, gq, gkv)
    d_w_in = _mm_tn("mla_dwin", h, dproj, tn_pref=dproj.shape[1])
    dh = _mm_nt_full("mla_dh", dproj, w_in, F32, tn_pref=1024)
    dx, dxb, dg = _rms_bwd("mla_dnorm", dh, x, g, dx)
    g_in, g_q, g_kv = _mla_unpermute(d_w_in, d_wq, d_wkv, heads, pw)
    small = dict(ln=dg, gq=dgq, gkv=dgkv)
    return dx, dxb, small, g_in, g_q, g_kv, d_w_out.reshape(N_CHIPS, d_w_out.shape[0] // N_CHIPS, -1)


def _gather_weights(bufs, norms):
    nt = len(bufs)

    def body(*refs):
        n_in = refs[nt]
        outs, n_out = refs[nt + 1:2 * nt + 1], refs[2 * nt + 1]
        send, recv, fsend, frecv, loc, nsend, nrecv = refs[2 * nt + 2:]
        x, y, c, chips = _place()
        me = 2 * x + y
        sib = (x, y, 1 - c)

        local = pltpu.make_async_copy(n_in, n_out.at[me], loc)
        local.start()

        def place(t, chip, half):
            return outs[t].at[2 * chip[0] + chip[1], half]

        def ici(t, j, chip):
            return pltpu.make_async_remote_copy(
                src_ref=place(t, (x, y), c), dst_ref=place(t, (x, y), c), send_sem=send.at[t, j],
                recv_sem=recv.at[t, j], device_id=(*chip, c), device_id_type=MESH)

        def fwd(t, j, chip, half):
            return pltpu.make_async_remote_copy(
                src_ref=place(t, chip, half), dst_ref=place(t, chip, half), send_sem=fsend.at[t, j],
                recv_sem=frecv.at[t, j], device_id=sib, device_id_type=MESH)

        def nrm(j, chip, owner):
            return pltpu.make_async_remote_copy(
                src_ref=n_in, dst_ref=n_out.at[2 * owner[0] + owner[1]], send_sem=nsend.at[j], recv_sem=nrecv.at[j],
                device_id=(*chip, c), device_id_type=MESH)

        firsts = [ici(t, j, chip) for t in range(nt) for j, chip in enumerate(chips)]
        firsts += [nrm(j, chip, (x, y)) for j, chip in enumerate(chips)]
        for cp in firsts:
            cp.start()
        passed = []
        for t in range(nt):
            for j, chip in enumerate(chips):
                pltpu.make_async_remote_copy(
                    src_ref=place(t, chip, c), dst_ref=place(t, chip, c), send_sem=send.at[t, j],
                    recv_sem=recv.at[t, j], device_id=(*chip, c), device_id_type=MESH).wait_recv()
                cp = fwd(t, j, chip, c)
                cp.start()
                passed.append(cp)
        for t in range(nt):
            for j, chip in enumerate(chips):
                fwd(t, j, chip, 1 - c).wait_recv()
        for j, chip in enumerate(chips):
            nrm(j, chip, chip).wait_recv()
        for cp in firsts + passed:
            cp.wait_send()
        local.wait()

    out_shape = [jax.ShapeDtypeStruct(b.shape, b.dtype) for b in bufs]
    out_shape.append(jax.ShapeDtypeStruct((N_CHIPS,) + norms.shape, norms.dtype))
    res = pl.pallas_call(
        body, name="gather_weights",
        in_specs=[ANY] * (nt + 1), out_specs=[ANY] * (nt + 1), out_shape=out_shape,
        input_output_aliases={t: t for t in range(nt)},
        scratch_shapes=[pltpu.SemaphoreType.DMA((nt, 3)), pltpu.SemaphoreType.DMA((nt, 3)),
                        pltpu.SemaphoreType.DMA((nt, 3)), pltpu.SemaphoreType.DMA((nt, 3)),
                        pltpu.SemaphoreType.DMA, pltpu.SemaphoreType.DMA((3,)),
                        pltpu.SemaphoreType.DMA((3,))],
    )(*bufs, norms)
    return res[:nt], res[nt]


def _run_phase(name, phase):
    c_in, in_specs, out_specs, shapes, sems, in_place = _comm_io(phase)
    n_c, n_co = len(c_in), len(shapes)

    def body(*refs):
        ins, outs = refs[:n_c], refs[n_c:n_c + n_co]
        send, recv = refs[n_c + n_co:]
        for cp in phase.copies(ins, outs, send, recv, False):
            cp.start()
        for cp in phase.copies(ins, outs, send, recv, True):
            cp.wait_recv()
        for cp in phase.copies(ins, outs, send, recv, False):
            cp.wait_send()

    return pl.pallas_call(
        body, name=name, in_specs=in_specs, out_specs=out_specs, out_shape=shapes,
        input_output_aliases={t: t for t in range(n_c)} if in_place else {}, scratch_shapes=sems,
    )(*c_in)


def _remote(ref_src, ref_dst, send, recv, t, j, to):
    return pltpu.make_async_remote_copy(src_ref=ref_src, dst_ref=ref_dst, send_sem=send.at[t, j],
                                        recv_sem=recv.at[t, j], device_id=to, device_id_type=MESH)


class _Reduce:
    def __init__(self, units, grads, n_layers, reduced, me_idx, core_idx):
        self.units, self.n_layers, self.reduced = list(units), n_layers, reduced
        self.me_idx, self.core_idx = me_idx, core_idx
        self.local = [g.reshape(N_CHIPS, 2, g.shape[1] // 2, g.shape[2]) for g in grads]
        self.parts = None

    def _swap(self):
        fresh = [jax.ShapeDtypeStruct((g.shape[0],) + g.shape[2:], g.dtype) for g in self.local]

        def copies(ins, outs, send, recv, landing):
            x, y, c, _ = _place()
            return [_remote(outs[t] if landing else ins[t].at[:, 1 - c], outs[t], send, recv, t, 0, (x, y, 1 - c))
                    for t in range(len(ins))]

        return _Phase(self.local, fresh, (len(self.local), 1), copies)

    def _scatter(self):
        fresh = [jax.ShapeDtypeStruct((3,) + p.shape[1:], p.dtype) for p in self.parts]

        def copies(ins, outs, send, recv, landing):
            x, y, c, chips = _place()
            return [_remote(outs[t].at[j] if landing else ins[t].at[2 * chip[0] + chip[1]], outs[t].at[j],
                            send, recv, t, j, (*chip, c))
                    for t in range(len(ins)) for j, chip in enumerate(chips)]

        return _Phase(self.parts, fresh, (len(self.parts), 3), copies)

    def _join(self):
        layers = [l for _, l in self.units]

        def copies(ins, outs, send, recv, landing):
            x, y, c, _ = _place()
            refs = [outs[t].at[l, 1 - c if landing else c] for t, l in enumerate(layers)]
            return [_remote(r, r, send, recv, t, 0, (x, y, 1 - c)) for t, r in enumerate(refs)]

        return _Phase([self.reduced[n] for n, _ in self.units], [], (len(self.units), 1), copies)

    def _after(self, step, got):
        if step == 0:
            self.parts = [_add_halves(self.core_idx, g, o) for g, o in zip(self.local, got)]
        elif step == 1:
            for (n, l), p, ld in zip(self.units, self.parts, got):
                self.reduced[n] = _sum_chips(self.me_idx, self.core_idx, p, ld, l, self.n_layers[n],
                                             self.reduced.get(n))
        else:
            for (n, _), joined in zip(self.units, got):
                self.reduced[n] = joined

    def _phase(self, step):
        return (self._swap, self._scatter, self._join)[step]()

    def behind(self, step, fn, *args, **kw):
        if not self.units:
            return fn(*args, **kw)
        out, got = fn(*args, comm=self._phase(step), **kw)
        self._after(step, got)
        return out

    def alone(self):
        for step, name in enumerate(("grad_swap_halves", "grad_scatter_chips", "grad_join_halves")):
            self._after(step, _run_phase(name, self._phase(step)))


def _gather_all(block):
    m_per, n = block.shape

    def body(x_ref, out_ref, send_sems, recv_sems, local_sem):
        x, y, c, chips = _place()
        me, sibling = (x, y, c), (x, y, 1 - c)

        def rows(px, py, pc):
            return out_ref.at[pl.ds((4 * px + 2 * py + pc) * m_per, m_per), :]

        def copy(k, block_of, to, src=None):
            return pltpu.make_async_remote_copy(
                src_ref=rows(*block_of) if src is None else src, dst_ref=rows(*block_of),
                send_sem=send_sems.at[k], recv_sem=recv_sems.at[k], device_id=to, device_id_type=MESH)

        mine = pltpu.make_async_copy(x_ref, rows(*me), local_sem)
        mine.start()
        first = [copy(0, me, sibling, src=x_ref)]
        first += [copy(1 + j, me, (*chip, c), src=x_ref) for j, chip in enumerate(chips)]
        for cp in first:
            cp.start()
        passed = [copy(4 + j, (*chip, c), sibling) for j, chip in enumerate(chips)]
        for j, chip in enumerate(chips):
            copy(1 + j, (*chip, c), me).wait_recv()
            passed[j].start()
        copy(0, sibling, me).wait_recv()
        for j, chip in enumerate(chips):
            copy(4 + j, (*chip, 1 - c), me).wait_recv()
        for cp in first + passed:
            cp.wait_send()
        mine.wait()

    return pl.pallas_call(
        body, name="gather_small_grads",
        out_shape=jax.ShapeDtypeStruct((N_DEV * m_per, n), block.dtype),
        in_specs=[pl.BlockSpec(memory_space=pltpu.VMEM)],
        out_specs=pl.BlockSpec(memory_space=pltpu.VMEM),
        scratch_shapes=[pltpu.SemaphoreType.DMA((7,)), pltpu.SemaphoreType.DMA((7,)), pltpu.SemaphoreType.DMA],
    )(block)


def _row_tile(r, c, elems=512 * 1024):
    t = max(8, min(r, (elems // c) // 8 * 8))
    while t > 8 and r % t:
        t -= 8
    return t if r % t == 0 else r


def _add_halves(idx, grad, other):
    n, _, r, w = grad.shape
    tr = _row_tile(r, w)

    def body(idx_ref, g_ref, o_ref, out_ref):
        out_ref[...] = (g_ref[...].astype(F32) + o_ref[...].astype(F32)).astype(BF16)

    return pl.pallas_call(
        body, name="grad_add_halves",
        grid_spec=pltpu.PrefetchScalarGridSpec(
            num_scalar_prefetch=1, grid=(n, r // tr),
            in_specs=[pl.BlockSpec((None, None, tr, w), lambda k, i, idx: (k, idx[0], i, 0)),
                      pl.BlockSpec((None, tr, w), lambda k, i, idx: (k, i, 0))],
            out_specs=pl.BlockSpec((None, tr, w), lambda k, i, idx: (k, i, 0))),
        out_shape=jax.ShapeDtypeStruct((n, r, w), BF16),
        compiler_params=_params(("parallel", "parallel")),
    )(idx, grad, other)


def _sum_chips(me_idx, core_idx, part, landed, layer, n_layers, prev):
    _, r, w = part.shape
    tr = _row_tile(r, w)

    def body(me_ref, c_ref, p_ref, l_ref, *rest):
        acc = p_ref[...].astype(F32)
        for j in range(3):
            acc = acc + l_ref[j].astype(F32)
        rest[-1][...] = acc

    return pl.pallas_call(
        body, name="grad_sum_chips",
        grid_spec=pltpu.PrefetchScalarGridSpec(
            num_scalar_prefetch=2, grid=(r // tr,),
            in_specs=[pl.BlockSpec((None, tr, w), lambda i, me, c: (me[0], i, 0)),
                      pl.BlockSpec((3, tr, w), lambda i, me, c: (0, i, 0))] + ([] if prev is None else [ANY]),
            out_specs=pl.BlockSpec((None, None, tr, w), lambda i, me, c: (layer, c[0], i, 0))),
        out_shape=jax.ShapeDtypeStruct((n_layers, 2, r, w), F32),
        input_output_aliases={} if prev is None else {4: 0},
        compiler_params=_params(("parallel",)),
    )(me_idx, core_idx, part, landed, *([] if prev is None else [prev]))


def _adamw_math(w, g, m, v):
    m = ADAM_B1 * m + (1.0 - ADAM_B1) * g
    v = ADAM_B2 * v + (1.0 - ADAM_B2) * (g * g)
    m_hat = m / (1.0 - ADAM_B1 ** ADAM_STEP)
    v_hat = v / (1.0 - ADAM_B2 ** ADAM_STEP)
    delta = -ADAM_LR * (m_hat / (jnp.sqrt(v_hat) + ADAM_EPS) + ADAM_WD * w)
    return delta, m, v


def _adamw(name, w, g, m, v):
    r, c = w.shape
    tr = _row_tile(r, c, 256 * 1024)

    def body(w_ref, g_ref, m_ref, v_ref, d_ref, nm_ref, nv_ref):
        d_ref[...], nm_ref[...], nv_ref[...] = _adamw_math(w_ref[...], g_ref[...], m_ref[...], v_ref[...])

    blk = pl.BlockSpec((tr, c), lambda i: (i, 0))
    return pl.pallas_call(
        body, name=name, grid=(r // tr,), in_specs=[blk] * 4, out_specs=[blk] * 3,
        out_shape=[jax.ShapeDtypeStruct((r, c), F32)] * 3,
        compiler_params=_params(("parallel",)),
    )(w, g, m, v)


def _adamw_summed(w, parts, m, v):
    r, c = w.shape

    def body(w_ref, p_ref, m_ref, v_ref, g_ref, d_ref, nm_ref, nv_ref):
        g = p_ref[0:r, :]
        for k in range(1, N_DEV):
            g = g + p_ref[k * r:(k + 1) * r, :]
        g_ref[...] = g
        d_ref[...], nm_ref[...], nv_ref[...] = _adamw_math(w_ref[...], g, m_ref[...], v_ref[...])

    return pl.pallas_call(
        body, name="adamw_replicated",
        out_shape=[jax.ShapeDtypeStruct((r, c), F32)] * 4,
        compiler_params=pltpu.CompilerParams(vmem_limit_bytes=VMEM_LIMIT_BYTES),
    )(w, parts, m, v)


def _pack(arrays):
    return jnp.concatenate([a.reshape(-1, LANE) for a in arrays], axis=0)


def _unpack(packed, shapes):
    out, row = [], 0
    for s in shapes:
        n = math.prod(s) // LANE
        out.append(packed[row:row + n].reshape(s))
        row += n
    return out


def _cast_into(name, idx, w, layer):
    _, rows, c = w.shape
    r = rows // 2
    tr = _row_tile(r, c)
    per = r // tr

    def body(idx_ref, w_ref, o_ref):
        o_ref[...] = w_ref[...].astype(BF16)

    return pl.pallas_call(
        body, name=name,
        grid_spec=pltpu.PrefetchScalarGridSpec(
            num_scalar_prefetch=1, grid=(2, per),
            in_specs=[pl.BlockSpec((None, tr, c), lambda h, i, idx: (layer, h * per + i, 0))],
            out_specs=pl.BlockSpec((None, None, tr, c), lambda h, i, idx: (idx[0], h, i, 0))),
        out_shape=jax.ShapeDtypeStruct((N_CHIPS, 2, r, c), BF16),
        compiler_params=_params(("parallel", "parallel")),
    )(idx, w)


BIG = ["ffn1_w_in", "ffn1_w_out", "ffn2_w_in", "ffn2_w_out", "sgu_w_in", "sgu_w_out",
       "mla_w_in", "mla_w_q_up", "mla_w_kv_up", "mla_w_out"]
STAGES = [
    [("ffn1_w_in", 0), ("ffn1_w_out", 0)],
    [("sgu_w_in", 0), ("sgu_w_out", 0)],
    [("ffn2_w_in", 0), ("ffn2_w_out", 0)],
    [("ffn1_w_in", 1), ("ffn1_w_out", 1)],
    [("mla_w_in", 0), ("mla_w_q_up", 0), ("mla_w_kv_up", 0), ("mla_w_out", 0)],
    [("ffn2_w_in", 1), ("ffn2_w_out", 1)],
]
REPLICATED = ["ln_ffn1", "ln_mix", "ln_ffn2", "sgu_v_gain", "sgu_v_bias", "sgu_w_spatial", "sgu_b_spatial",
              "ln_final"]
NORM_SHARDS = ["mla_q_norm", "mla_kv_norm"]
WEIGHTS = ["ln_ffn1", "ffn1_w_in", "ffn1_w_out", "ln_mix", "ln_ffn2", "ffn2_w_in", "ffn2_w_out", "sgu_w_in",
           "sgu_v_gain", "sgu_v_bias", "sgu_w_spatial", "sgu_b_spatial", "sgu_w_out", "mla_w_in", "mla_q_norm",
           "mla_w_q_up", "mla_kv_norm", "mla_w_kv_up", "mla_w_out", "ln_final"]


def kernel(x, positions, ln_ffn1, ffn1_w_in, ffn1_w_out, ln_mix, ln_ffn2, ffn2_w_in, ffn2_w_out, sgu_w_in, sgu_v_gain, sgu_v_bias, sgu_w_spatial, sgu_b_spatial, sgu_w_out, mla_w_in, mla_q_norm, mla_w_q_up, mla_kv_norm, mla_w_kv_up, mla_w_out, ln_final, loss_target, m_ln_ffn1, m_ffn1_w_in, m_ffn1_w_out, m_ln_mix, m_ln_ffn2, m_ffn2_w_in, m_ffn2_w_out, m_sgu_w_in, m_sgu_v_gain, m_sgu_v_bias, m_sgu_w_spatial, m_sgu_b_spatial, m_sgu_w_out, m_mla_w_in, m_mla_q_norm, m_mla_w_q_up, m_mla_kv_norm, m_mla_w_kv_up, m_mla_w_out, m_ln_final, v_ln_ffn1, v_ffn1_w_in, v_ffn1_w_out, v_ln_mix, v_ln_ffn2, v_ffn2_w_in, v_ffn2_w_out, v_sgu_w_in, v_sgu_v_gain, v_sgu_v_bias, v_sgu_w_spatial, v_sgu_b_spatial, v_sgu_w_out, v_mla_w_in, v_mla_q_norm, v_mla_w_q_up, v_mla_kv_norm, v_mla_w_kv_up, v_mla_w_out, v_ln_final):
    given = dict(locals())
    w = {n: given[n] for n in WEIGHTS}
    mom = {n: given["m_" + n] for n in WEIGHTS}
    var = {n: given["v_" + n] for n in WEIGHTS}
    t, d = x.shape[1], x.shape[2]
    xs = x.reshape(t, d)
    target = loss_target.reshape(t, d)
    me = 2 * lax.axis_index("x") + lax.axis_index("y")

    me_idx = jnp.reshape(me, (1,)).astype(jnp.int32)
    core_idx = jnp.reshape(lax.axis_index("c"), (1,)).astype(jnp.int32)

    bufs = {(n, l): _cast_into(f"cast_{n}_{l}", me_idx, w[n], l) for n in BIG for l in range(w[n].shape[0])}
    nq = mla_q_norm.shape[1]
    norms = jnp.pad(jnp.concatenate([mla_q_norm, mla_kv_norm], axis=0), ((0, 6), (0, LANE - nq)))

    def take(stage):
        return [bufs[u] for u in stage]

    def put(stage, arrays):
        bufs.update(zip(stage, arrays))

    def full(unit):
        a = bufs[unit]
        return a.reshape(N_CHIPS, a.shape[1] * a.shape[2], a.shape[3])

    def rows(unit):
        a = bufs[unit]
        return a.reshape(-1, a.shape[-1])

    first, norms_g = _gather_weights(take(STAGES[0]), norms)
    put(STAGES[0], first)
    gq = norms_g[:, 0, :nq].reshape(1, N_CHIPS * nq)
    gkv = norms_g[:, 1, :nq].reshape(1, N_CHIPS * nq)
    tables = _rope_tables(positions.reshape(t))
    sgu_small = (sgu_v_gain, sgu_v_bias, sgu_w_spatial[0], sgu_b_spatial[0])

    a0, s_f1_0, got = _ffn_fwd("l0_ffn1", xs, ln_ffn1[0], full(("ffn1_w_in", 0)), rows(("ffn1_w_out", 0)),
                               take(STAGES[1]))
    put(STAGES[1], got)
    a1, s_sgu, got = _sgu_fwd(a0, ln_mix[0], full(("sgu_w_in", 0)), *sgu_small, rows(("sgu_w_out", 0)),
                              take(STAGES[2]))
    put(STAGES[2], got)
    a2, s_f2_0, got = _ffn_fwd("l0_ffn2", a1, ln_ffn2[0], full(("ffn2_w_in", 0)), rows(("ffn2_w_out", 0)),
                               take(STAGES[3]))
    put(STAGES[3], got)
    a3, s_f1_1, got = _ffn_fwd("l1_ffn1", a2, ln_ffn1[1], full(("ffn1_w_in", 1)), rows(("ffn1_w_out", 1)),
                               take(STAGES[4] + STAGES[5]))
    put(STAGES[4] + STAGES[5], got)
    mla_wts = _mla_weights(full(("mla_w_in", 0)), full(("mla_w_q_up", 0)), full(("mla_w_kv_up", 0)),
                           full(("mla_w_out", 0)))
    a4, s_mla = _mla_fwd(a3, ln_mix[1], mla_wts, gq, gkv, tables)
    a5, s_f2_1, _ = _ffn_fwd("l1_ffn2", a4, ln_ffn2[1], full(("ffn2_w_in", 1)), rows(("ffn2_w_out", 1)), [])

    gr, reduced = {}, {}
    n_layers = {n: w[n].shape[0] for n in BIG}

    def reduce_of(stages):
        units = [u for s in stages for u in STAGES[s]]
        return _Reduce(units, [gr[u] for u in units], n_layers, reduced, me_idx, core_idx)

    loss_part, dx, dxb, dg_final = _loss_bwd(a5, ln_final, target)
    dx, dxb, dg_f2_1, gr["ffn2_w_in", 1], gr["ffn2_w_out", 1] = _ffn_bwd(
        "l1_ffn2", s_f2_1, ln_ffn2[1], full(("ffn2_w_in", 1)), rows(("ffn2_w_out", 1)), dx, dxb, reduce_of([]))
    (dx, dxb, sm_mla, gr["mla_w_in", 0], gr["mla_w_q_up", 0], gr["mla_w_kv_up", 0],
     gr["mla_w_out", 0]) = _mla_bwd(s_mla, ln_mix[1], mla_wts, gq, gkv, tables, dx, dxb, mla_w_in.shape[2])
    dx, dxb, dg_f1_1, gr["ffn1_w_in", 1], gr["ffn1_w_out", 1] = _ffn_bwd(
        "l1_ffn1", s_f1_1, ln_ffn1[1], full(("ffn1_w_in", 1)), rows(("ffn1_w_out", 1)), dx, dxb, reduce_of([5, 4]))
    dx, dxb, dg_f2_0, gr["ffn2_w_in", 0], gr["ffn2_w_out", 0] = _ffn_bwd(
        "l0_ffn2", s_f2_0, ln_ffn2[0], full(("ffn2_w_in", 0)), rows(("ffn2_w_out", 0)), dx, dxb, reduce_of([3]))
    dx, dxb, sm_sgu, gr["sgu_w_in", 0], gr["sgu_w_out", 0] = _sgu_bwd(
        s_sgu, ln_mix[0], full(("sgu_w_in", 0)), *sgu_small, rows(("sgu_w_out", 0)), dx, dxb, reduce_of([2]))
    dx, dxb, dg_f1_0, gr["ffn1_w_in", 0], gr["ffn1_w_out", 0] = _ffn_bwd(
        "l0_ffn1", s_f1_0, ln_ffn1[0], full(("ffn1_w_in", 0)), rows(("ffn1_w_out", 0)), dx, dxb, reduce_of([1]))
    reduce_of([0]).alone()
    big_grad = {n: reduced[n].reshape(w[n].shape) for n in BIG}

    small_parts = [
        jnp.concatenate([dg_f1_0, dg_f1_1], axis=0), jnp.concatenate([sm_sgu["ln"], sm_mla["ln"]], axis=0),
        jnp.concatenate([dg_f2_0, dg_f2_1], axis=0), sm_sgu["gain"], sm_sgu["bias"], sm_sgu["w_sp"],
        sm_sgu["b_sp"], dg_final]
    rep_shapes = [w[n].shape for n in REPLICATED]
    gq_row = jnp.pad(sm_mla["gq"], ((0, 0), (0, N_CHIPS * (LANE - nq))))
    gkv_row = jnp.pad(sm_mla["gkv"], ((0, 0), (0, N_CHIPS * (LANE - nq))))
    packed = _pack(small_parts + [gq_row, gkv_row, loss_part])
    packed = jnp.pad(packed, ((0, -packed.shape[0] % 8), (0, 0)))
    everyone = _gather_all(packed)
    rows = packed.shape[0]
    zero_rows = jnp.zeros((rows - sum(math.prod(s) // LANE for s in rep_shapes), LANE), F32)
    pw = jnp.concatenate([_pack([w[n] for n in REPLICATED]), zero_rows], axis=0)
    pm = jnp.concatenate([_pack([mom[n] for n in REPLICATED]), zero_rows], axis=0)
    pv = jnp.concatenate([_pack([var[n] for n in REPLICATED]), zero_rows + 1.0], axis=0)
    g_all, d_all, m_all, v_all = _adamw_summed(pw, everyone, pm, pv)
    tail_shapes = [(1, N_CHIPS * LANE), (1, N_CHIPS * LANE), (1, LANE)]
    rep_grad = dict(zip(REPLICATED, _unpack(g_all, rep_shapes + tail_shapes)[:len(REPLICATED)]))
    rep_delta = dict(zip(REPLICATED, _unpack(d_all, rep_shapes)))
    rep_m = dict(zip(REPLICATED, _unpack(m_all, rep_shapes)))
    rep_v = dict(zip(REPLICATED, _unpack(v_all, rep_shapes)))
    tail = _unpack(g_all, rep_shapes + tail_shapes)[len(REPLICATED):]
    loss = tail[2][0, 0]
    norm_grad = {
        "mla_q_norm": lax.dynamic_slice(tail[0], (0, me * nq), (1, nq)),
        "mla_kv_norm": lax.dynamic_slice(tail[1], (0, me * nq), (1, nq)),
    }

    grad, delta, new_m, new_v = {}, {}, {}, {}
    for n in BIG:
        shp = w[n].shape
        flat = lambda a: a.reshape(-1, shp[-1])
        dl, nm, nv = _adamw("adamw_" + n, flat(w[n]), flat(big_grad[n]), flat(mom[n]), flat(var[n]))
        grad[n], delta[n], new_m[n], new_v[n] = big_grad[n], dl.reshape(shp), nm.reshape(shp), nv.reshape(shp)
    for n in REPLICATED:
        grad[n], delta[n], new_m[n], new_v[n] = rep_grad[n], rep_delta[n], rep_m[n], rep_v[n]
    stack = lambda dct: jnp.concatenate([dct[n] for n in NORM_SHARDS], axis=0)
    dl, nm, nv = _adamw("adamw_norm_shards", stack(w), stack(norm_grad), stack(mom), stack(var))
    for i, n in enumerate(NORM_SHARDS):
        grad[n], delta[n], new_m[n], new_v[n] = norm_grad[n], dl[i:i + 1], nm[i:i + 1], nv[i:i + 1]

    grad_x = dx.reshape(x.shape)
    return (loss, grad_x, *[grad[n] for n in WEIGHTS], *[delta[n] for n in WEIGHTS],
            *[new_m[n] for n in WEIGHTS], *[new_v[n] for n in WEIGHTS])
```

```python
import functools
import math

import jax
import jax.numpy as jnp
from jax import lax
from jax.experimental import pallas as pl
from jax.experimental.pallas import tpu as pltpu

F32 = jnp.float32
BF16 = jnp.bfloat16
MESH = pl.DeviceIdType.MESH

EPS = 1e-6
CHUNK = 64
SGU_BLOCK = 128
SGU_GROUPS = 8
QK_NOPE = 128
QK_ROPE = 64
V_DIM = 128
QK_DIM = QK_NOPE + QK_ROPE
HEAD_PAD = 256
LOG2E = math.log2(math.e)
ROPE_THETA = 10000.0
N_CHIPS = 4
N_DEV = 8

ADAM_LR = 0.001
ADAM_B1 = 0.9
ADAM_B2 = 0.999
ADAM_EPS = 1e-08
ADAM_WD = 0.01
ADAM_STEP = 10

LANE = 128
VMEM_LIMIT_BYTES = 56 * 1024 * 1024

_DIMS = {
    "nn": (((1,), (0,)), ((), ())),
    "nt": (((1,), (1,)), ((), ())),
    "tn": (((0,), (0,)), ((), ())),
}


def _tile(n, pref):
    t = (min(pref, n) // LANE) * LANE
    while t >= LANE:
        if n % t == 0:
            return t
        t -= LANE
    return n


def _params(sem):
    return pltpu.CompilerParams(dimension_semantics=sem, vmem_limit_bytes=VMEM_LIMIT_BYTES)


def _dot(a, b, mode):
    return lax.dot_general(a, b, _DIMS[mode], preferred_element_type=F32)


def _place():
    x, y, c = lax.axis_index("x"), lax.axis_index("y"), lax.axis_index("c")
    chips = [(1 - x, y), (x, 1 - y), (1 - x, 1 - y)]
    return x, y, c, chips


ANY = pl.BlockSpec(memory_space=pl.ANY)


def _gather_copies(phase, bufs, send, recv, landing):
    x, y, c, chips = _place()
    copies = []
    for t, buf in enumerate(bufs):
        for j, chip in enumerate(chips):
            there = 2 * chip[0] + chip[1]
            if phase == 1:
                src, lands, to = buf.at[2 * x + y, c], buf.at[there, c], (*chip, c)
            else:
                src, lands, to = buf.at[there, c], buf.at[there, 1 - c], (x, y, 1 - c)
            ref = lands if landing else src
            copies.append(pltpu.make_async_remote_copy(
                src_ref=ref, dst_ref=ref, send_sem=send.at[t, j], recv_sem=recv.at[t, j], device_id=to,
                device_id_type=MESH))
    return copies


class _Phase:
    def __init__(self, ins, fresh, sems, copies):
        self.ins, self.fresh, self.sems, self.copies = list(ins), list(fresh), sems, copies


def _gather_phase(phase, bufs):
    return _Phase(bufs, [], (len(bufs), 3),
                  lambda ins, outs, send, recv, landing: _gather_copies(phase, outs, send, recv, landing))


def _comm_io(comm):
    if comm is None:
        return [], [], [], [], [], False
    shapes = comm.fresh or [jax.ShapeDtypeStruct(b.shape, b.dtype) for b in comm.ins]
    sems = [pltpu.SemaphoreType.DMA(comm.sems), pltpu.SemaphoreType.DMA(comm.sems)]
    return comm.ins, [ANY] * len(comm.ins), [ANY] * len(shapes), shapes, sems, not comm.fresh


def _comm_run(comm, in_refs, out_refs, send, recv, first, last):
    @pl.when(first)
    def _():
        for cp in comm.copies(in_refs, out_refs, send, recv, False):
            cp.start()

    def finish():
        for cp in comm.copies(in_refs, out_refs, send, recv, True):
            cp.wait_recv()
        for cp in comm.copies(in_refs, out_refs, send, recv, False):
            cp.wait_send()

    return last, finish


def _matmul(name, mode, grid, a, a_spec, b, b_spec, extras, out_shapes, out_specs, acc_shape, epilogue, comm=None):
    nk = grid[2]
    n_ex = len(extras)
    n_out = len(out_shapes)
    c_in, c_in_specs, c_out_specs, c_shapes, c_sems, in_place = _comm_io(comm)
    n_c, n_co = len(c_in), len(c_shapes)

    def body(*refs):
        a_ref, b_ref = refs[0], refs[1]
        ex = refs[2:2 + n_ex]
        outs = refs[2 + n_ex + n_c:2 + n_ex + n_c + n_out]
        ids = (pl.program_id(0), pl.program_id(1))
        k = pl.program_id(2)
        if comm is not None:
            c_ins = refs[2 + n_ex:2 + n_ex + n_c]
            c_outs = refs[2 + n_ex + n_c + n_out:2 + n_ex + n_c + n_out + n_co]
            send, recv = refs[2 + n_ex + n_c + n_out + n_co:2 + n_ex + n_c + n_out + n_co + 2]
            first = jnp.logical_and(jnp.logical_and(ids[0] == 0, ids[1] == 0), k == 0)
            last = jnp.logical_and(jnp.logical_and(ids[0] == grid[0] - 1, ids[1] == grid[1] - 1), k == nk - 1)
            last, finish = _comm_run(comm, c_ins, c_outs, send, recv, first, last)
        part = _dot(a_ref[...], b_ref[...], mode)
        if nk == 1:
            epilogue(part, ex, outs, ids)
        else:
            acc = refs[-1]

            @pl.when(k == 0)
            def _():
                acc[...] = part

            @pl.when(k > 0)
            def _():
                acc[...] += part

            @pl.when(k == nk - 1)
            def _():
                epilogue(acc[...], ex, outs, ids)

        if comm is not None:
            pl.when(last)(finish)

    scratch = c_sems + ([pltpu.VMEM(acc_shape, F32)] if nk > 1 else [])
    sem = ("parallel", "parallel", "arbitrary") if comm is None else ("arbitrary",) * 3
    return pl.pallas_call(
        body,
        name=name,
        grid=grid,
        in_specs=[a_spec, b_spec] + [s for _, s in extras] + c_in_specs,
        out_specs=list(out_specs) + c_out_specs,
        out_shape=list(out_shapes) + c_shapes,
        input_output_aliases={2 + n_ex + t: n_out + t for t in range(n_c)} if in_place else {},
        scratch_shapes=scratch,
        compiler_params=_params(sem),
    )(a, b, *[e for e, _ in extras], *c_in)


def _store(scale, dtype):
    def epilogue(acc, ex, outs, ids):
        v = acc if scale == 1.0 else acc * scale
        outs[0][...] = v.astype(dtype)

    return epilogue


def _mm_nn_full(name, a, b, out_dtype, tm_pref=1024, tn_pref=512):
    m, kd = a.shape
    n = b.shape[1]
    tm, tn = _tile(m, tm_pref), _tile(n, tn_pref)
    return _matmul(
        name, "nn", (m // tm, n // tn, 1),
        a, pl.BlockSpec((tm, kd), lambda i, j, k: (i, 0)),
        b, pl.BlockSpec((kd, tn), lambda i, j, k: (0, j)),
        [], [jax.ShapeDtypeStruct((m, n), out_dtype)], [pl.BlockSpec((tm, tn), lambda i, j, k: (i, j))],
        None, _store(1.0, out_dtype))[0]


def _mm_nt_full(name, a, b, out_dtype, scale=1.0, tm_pref=2048, tn_pref=512):
    m, kd = a.shape
    n = b.shape[0]
    tm, tn = _tile(m, tm_pref), _tile(n, tn_pref)
    return _matmul(
        name, "nt", (m // tm, n // tn, 1),
        a, pl.BlockSpec((tm, kd), lambda i, j, k: (i, 0)),
        b, pl.BlockSpec((tn, kd), lambda i, j, k: (j, 0)),
        [], [jax.ShapeDtypeStruct((m, n), out_dtype)], [pl.BlockSpec((tm, tn), lambda i, j, k: (i, j))],
        None, _store(scale, out_dtype))[0]


def _mm_nt_k(name, a, b, out_dtype, tk_pref=1024, tm_pref=1024, tn_pref=1024):
    m, kd = a.shape
    n = b.shape[0]
    tm, tn, tk = _tile(m, tm_pref), _tile(n, tn_pref), _tile(kd, tk_pref)
    return _matmul(
        name, "nt", (m // tm, n // tn, kd // tk),
        a, pl.BlockSpec((tm, tk), lambda i, j, k: (i, k)),
        b, pl.BlockSpec((tn, tk), lambda i, j, k: (j, k)),
        [], [jax.ShapeDtypeStruct((m, n), out_dtype)], [pl.BlockSpec((tm, tn), lambda i, j, k: (i, j))],
        (tm, tn), _store(1.0, out_dtype))[0]


def _mm_tn(name, a, b, scale=1.0, tm_pref=1024, tn_pref=1024, tk_pref=1024, comm=None):
    t, m = a.shape
    n = b.shape[1]
    tm, tn, tk = _tile(m, tm_pref), _tile(n, tn_pref), _tile(t, tk_pref)
    res = _matmul(
        name, "tn", (m // tm, n // tn, t // tk),
        a, pl.BlockSpec((tk, tm), lambda i, j, k: (k, i)),
        b, pl.BlockSpec((tk, tn), lambda i, j, k: (k, j)),
        [], [jax.ShapeDtypeStruct((m, n), BF16)], [pl.BlockSpec((tm, tn), lambda i, j, k: (i, j))],
        (tm, tn), _store(scale, BF16), comm=comm)
    return res[0] if comm is None else (res[0], res[1:])


def _mm_residual(name, a, b, x, scale, tm_pref=1024, tn_pref=1024, tk_pref=1408, comm=None):
    m, kd = a.shape
    n = b.shape[1]
    tm, tn, tk = _tile(m, tm_pref), _tile(n, tn_pref), _tile(kd, tk_pref)

    def epilogue(acc, ex, outs, ids):
        outs[0][...] = ex[0][...] + scale * acc

    res = _matmul(
        name, "nn", (m // tm, n // tn, kd // tk),
        a, pl.BlockSpec((tm, tk), lambda i, j, k: (i, k)),
        b, pl.BlockSpec((tk, tn), lambda i, j, k: (k, j)),
        [(x, pl.BlockSpec((tm, tn), lambda i, j, k: (i, j)))],
        [jax.ShapeDtypeStruct((m, n), F32)], [pl.BlockSpec((tm, tn), lambda i, j, k: (i, j))],
        (tm, tn), epilogue, comm=comm)
    return res[0] if comm is None else (res[0], res[1:])


def _rms_fwd(name, x, g, with_transpose=False):
    t, d = x.shape
    tm = _tile(t, 512)

    def body(x_ref, g_ref, h_ref, *ht_ref):
        xv = x_ref[...]
        r = lax.rsqrt(jnp.mean(xv * xv, axis=-1, keepdims=True) + EPS)
        h = xv * r * g_ref[...]
        h_ref[...] = h.astype(BF16)
        if with_transpose:
            ht_ref[0][...] = h.T.astype(BF16)

    out_specs = [pl.BlockSpec((tm, d), lambda i: (i, 0))]
    out_shape = [jax.ShapeDtypeStruct((t, d), BF16)]
    if with_transpose:
        out_specs.append(pl.BlockSpec((d, tm), lambda i: (0, i)))
        out_shape.append(jax.ShapeDtypeStruct((d, t), BF16))
    res = pl.pallas_call(
        body, name=name, grid=(t // tm,),
        in_specs=[pl.BlockSpec((tm, d), lambda i: (i, 0)), pl.BlockSpec((1, d), lambda i: (0, 0))],
        out_specs=out_specs, out_shape=out_shape,
        compiler_params=_params(("parallel",)),
    )(x, g.reshape(1, d))
    return res if with_transpose else res[0]


def _rms_bwd_math(dh, xv, g):
    r = lax.rsqrt(jnp.mean(xv * xv, axis=-1, keepdims=True) + EPS)
    xhat = xv * r
    dxh = dh * g
    dx = r * (dxh - xhat * jnp.mean(dxh * xhat, axis=-1, keepdims=True))
    return dx, dh * xhat


def _rms_bwd(name, dh, x, g, dres):
    t, d = x.shape
    tm = _tile(t, 256)

    def body(dh_ref, x_ref, g_ref, dres_ref, dx_ref, dxb_ref, dg_ref):
        dx, dgt = _rms_bwd_math(dh_ref[...].astype(F32), x_ref[...], g_ref[...])
        dx = dres_ref[...] + dx
        dx_ref[...] = dx
        dxb_ref[...] = dx.astype(BF16)

        @pl.when(pl.program_id(0) == 0)
        def _():
            dg_ref[...] = jnp.zeros_like(dg_ref)

        dg_ref[...] += jnp.sum(dgt, axis=0, keepdims=True)

    row = pl.BlockSpec((tm, d), lambda i: (i, 0))
    vec = pl.BlockSpec((1, d), lambda i: (0, 0))
    return pl.pallas_call(
        body, name=name, grid=(t // tm,),
        in_specs=[row, row, vec, row],
        out_specs=[row, row, vec],
        out_shape=[jax.ShapeDtypeStruct((t, d), F32), jax.ShapeDtypeStruct((t, d), BF16),
                   jax.ShapeDtypeStruct((1, d), F32)],
        compiler_params=_params(("arbitrary",)),
    )(dh, x, g.reshape(1, d), dres)


def _loss_bwd(x, g, target):
    t, d = x.shape
    tm = _tile(t, 256)

    def body(x_ref, g_ref, tgt_ref, loss_ref, dx_ref, dxb_ref, dg_ref):
        xv = x_ref[...]
        gv = g_ref[...]
        r = lax.rsqrt(jnp.mean(xv * xv, axis=-1, keepdims=True) + EPS)
        err = xv * r * gv - tgt_ref[...]
        part = 0.5 * jnp.sum(jnp.mean(err * err, axis=-1, keepdims=True), axis=0, keepdims=True)
        dx, dgt = _rms_bwd_math(err * (1.0 / d), xv, gv)
        dx_ref[...] = dx
        dxb_ref[...] = dx.astype(BF16)

        @pl.when(pl.program_id(0) == 0)
        def _():
            dg_ref[...] = jnp.zeros_like(dg_ref)
            loss_ref[...] = jnp.zeros_like(loss_ref)

        dg_ref[...] += jnp.sum(dgt, axis=0, keepdims=True)
        loss_ref[...] += jnp.broadcast_to(part, loss_ref.shape)

    row = pl.BlockSpec((tm, d), lambda i: (i, 0))
    vec = pl.BlockSpec((1, d), lambda i: (0, 0))
    return pl.pallas_call(
        body, name="loss_bwd", grid=(t // tm,),
        in_specs=[row, vec, row],
        out_specs=[pl.BlockSpec((1, LANE), lambda i: (0, 0)), row, row, vec],
        out_shape=[jax.ShapeDtypeStruct((1, LANE), F32), jax.ShapeDtypeStruct((t, d), F32),
                   jax.ShapeDtypeStruct((t, d), BF16), jax.ShapeDtypeStruct((1, d), F32)],
        compiler_params=_params(("arbitrary",)),
    )(x, g.reshape(1, d), target)


UP_SPLIT = 2


def _sigmoid(x):
    return 0.5 * jnp.tanh(0.5 * x) + 0.5


def _ffn_up(name, h, w_in, comm=None):
    t, d = h.shape
    fs = w_in.shape[2]
    f = 2 * fs
    tm, tn = _tile(t, 2048), _tile(fs, 256)
    per = fs // tn
    grid = (t // tm, f // tn)
    c_in, c_in_specs, c_out_specs, c_shapes, c_sems, in_place = _comm_io(comm)
    n_c, n_co = len(c_in), len(c_shapes)

    def body(*refs):
        h_ref, wg_ref, wu_ref = refs[:3]
        gu_ref, z_ref = refs[3 + n_c:5 + n_c]
        if comm is not None:
            i, j = pl.program_id(0), pl.program_id(1)
            send, recv = refs[5 + n_c + n_co:]
            last, finish = _comm_run(comm, refs[3:3 + n_c], refs[5 + n_c:5 + n_c + n_co], send, recv,
                                     jnp.logical_and(i == 0, j == 0),
                                     jnp.logical_and(i == grid[0] - 1, j == grid[1] - 1))
        wg, wu = wg_ref[...], wu_ref[...]
        for r0 in range(0, tm, tm // UP_SPLIT):
            rows = slice(r0, r0 + tm // UP_SPLIT)
            hv = h_ref[rows, :]
            gate = _dot(hv, wg, "nn")
            up = _dot(hv, wu, "nn")
            sg = _sigmoid(gate)
            silu = gate * sg
            gu_ref[0, rows, :] = (sg * (1.0 + gate * (1.0 - sg)) * up).astype(BF16)
            gu_ref[1, rows, :] = silu.astype(BF16)
            z_ref[rows, :] = (silu * up).astype(BF16)
        if comm is not None:
            pl.when(last)(finish)

    res = pl.pallas_call(
        body, name=name, grid=grid,
        in_specs=[pl.BlockSpec((tm, d), lambda i, j: (i, 0)),
                  pl.BlockSpec((None, d, tn), lambda i, j: (j // per, 0, j % per)),
                  pl.BlockSpec((None, d, tn), lambda i, j: (2 + j // per, 0, j % per))] + c_in_specs,
        out_specs=[pl.BlockSpec((2, tm, tn), lambda i, j: (0, i, j)),
                   pl.BlockSpec((tm, tn), lambda i, j: (i, j))] + c_out_specs,
        out_shape=[jax.ShapeDtypeStruct((2, t, f), BF16), jax.ShapeDtypeStruct((t, f), BF16)] + c_shapes,
        input_output_aliases={3 + k: 2 + k for k in range(n_c)} if in_place else {},
        scratch_shapes=c_sems,
        compiler_params=_params(("parallel", "parallel") if comm is None else ("arbitrary", "arbitrary")),
    )(h, w_in, w_in, *c_in)
    return (res[0], res[1]) if comm is None else (res[0], res[1], res[2:])


def _ffn_dact(name, dxb, w_out, gu):
    t, d = dxb.shape
    f = w_out.shape[0]
    tm, tn = _tile(t, 2048), _tile(f, 512)

    def epilogue(acc, ex, outs, ids):
        dz = 0.5 * acc
        outs[0][0] = (dz * ex[0][0].astype(F32)).astype(BF16)
        outs[0][1] = (dz * ex[0][1].astype(F32)).astype(BF16)

    blk = pl.BlockSpec((2, tm, tn), lambda i, j, k: (0, i, j))
    return _matmul(
        name, "nt", (t // tm, f // tn, 1),
        dxb, pl.BlockSpec((tm, d), lambda i, j, k: (i, 0)),
        w_out, pl.BlockSpec((tn, d), lambda i, j, k: (j, 0)),
        [(gu, blk)], [jax.ShapeDtypeStruct((2, t, f), BF16)], [blk], None, epilogue)[0]


def _grad_colsharded(name, ht, da, comm=None):
    d, t = ht.shape
    w = da.shape[2]
    ws = w // 2
    tm, tn, tk = _tile(d, 1024), _tile(ws, 1408), _tile(t, 2048)
    per = ws // tn
    res = _matmul(
        name, "nn", (d // tm, (2 * w) // tn, t // tk),
        ht, pl.BlockSpec((tm, tk), lambda i, j, k: (i, k)),
        da, pl.BlockSpec((None, tk, tn), lambda i, j, k: (j // (2 * per), k, j % (2 * per))),
        [], [jax.ShapeDtypeStruct((N_CHIPS, d, ws), BF16)],
        [pl.BlockSpec((None, tm, tn), lambda i, j, k: (j // per, i, j % per))],
        (tm, tn), _store(1.0, BF16), comm=comm)
    return res[0] if comm is None else (res[0], res[1:])


def _back_colsharded(name, da, w_g, comm=None):
    _, t, w = da.shape
    d, ws = w_g.shape[1], w_g.shape[2]
    tm, tn, tk = _tile(t, 1024), _tile(d, 1024), _tile(ws, 2816)
    per = ws // tk
    res = _matmul(
        name, "nt", (t // tm, d // tn, (2 * w) // tk),
        da, pl.BlockSpec((None, tm, tk), lambda i, j, k: (k // (2 * per), i, k % (2 * per))),
        w_g, pl.BlockSpec((None, tn, tk), lambda i, j, k: (k // per, j, k % per)),
        [], [jax.ShapeDtypeStruct((t, d), F32)], [pl.BlockSpec((tm, tn), lambda i, j, k: (i, j))],
        (tm, tn), _store(1.0, F32), comm=comm)
    return res[0] if comm is None else (res[0], res[1:])


def _ffn_fwd(tag, x, g, w_in, w_out, prefetch):
    h, ht = _rms_fwd(tag + "_norm", x, g, with_transpose=True)
    if prefetch:
        gu, z, prefetch = _ffn_up(tag + "_up", h, w_in, comm=_gather_phase(1, prefetch))
        y, prefetch = _mm_residual(tag + "_down", z, w_out, x, 0.5, tk_pref=2816, comm=_gather_phase(2, prefetch))
    else:
        gu, z = _ffn_up(tag + "_up", h, w_in)
        y = _mm_residual(tag + "_down", z, w_out, x, 0.5, tk_pref=2816)
    return y, (x, ht, gu, z), prefetch


def _ffn_bwd(tag, saved, g, w_in, w_out, dx, dxb, red):
    x, ht, gu, z = saved
    f = z.shape[1]
    d_w_out = red.behind(0, _mm_tn, tag + "_dwout", z, dxb, scale=0.5, tm_pref=1408, tn_pref=2048)
    da = _ffn_dact(tag + "_dact", dxb, w_out, gu)
    d_w_in = red.behind(1, _grad_colsharded, tag + "_dwin", ht, da)
    dh = red.behind(2, _back_colsharded, tag + "_dh", da, w_in)
    dx, dxb, dg = _rms_bwd(tag + "_dnorm", dh, x, g, dx)
    return dx, dxb, dg, d_w_in, d_w_out.reshape(N_CHIPS, f // N_CHIPS, -1)


_GELU_K = math.sqrt(2.0 / math.pi)
_GELU_C = 0.044715


def _gelu(x):
    t = jnp.tanh(_GELU_K * (x + _GELU_C * x * x * x))
    return 0.5 * x * (1.0 + t), t


def _dgelu(x, t):
    return 0.5 * (1.0 + t) + 0.5 * x * (1.0 - t * t) * (_GELU_K * (1.0 + 3.0 * _GELU_C * x * x))


def _causal_block_mask():
    r = lax.broadcasted_iota(jnp.int32, (SGU_BLOCK, SGU_BLOCK), 0) // CHUNK
    c = lax.broadcasted_iota(jnp.int32, (SGU_BLOCK, SGU_BLOCK), 1) // CHUNK
    return r >= c


def _sgu_pre(name, h, w_in, comm=None):
    t, d = h.shape
    ws = w_in.shape[2]
    w = 2 * ws
    tm, tn = _tile(t, 2048), _tile(ws, 512)
    per = ws // tn
    res = _matmul(
        name, "nn", (t // tm, (2 * w) // tn, 1),
        h, pl.BlockSpec((tm, d), lambda i, j, k: (i, 0)),
        w_in, pl.BlockSpec((None, d, tn), lambda i, j, k: (j // per, 0, j % per)),
        [], [jax.ShapeDtypeStruct((2, t, w), BF16)],
        [pl.BlockSpec((None, tm, tn), lambda i, j, k: (j // (2 * per), i, j % (2 * per)))],
        None, _store(1.0, BF16), comm=comm)
    return res[0] if comm is None else (res[0], res[1:])


def _layernorm_stats(v):
    mu = jnp.mean(v, axis=-1, keepdims=True)
    vc = v - mu
    rstd = lax.rsqrt(jnp.mean(vc * vc, axis=-1, keepdims=True) + EPS)
    return vc * rstd, rstd


def _sgu_mid_fwd(pre, gain, bias, w_sp, b_sp_t):
    _, t, w = pre.shape
    gd = w // SGU_GROUPS

    def body(pre_ref, gain_ref, bias_ref, ws_ref, bt_ref, out_ref):
        mask = _causal_block_mask()
        u, _ = _gelu(pre_ref[0].astype(F32))
        v, _ = _gelu(pre_ref[1].astype(F32))
        vhat, _ = _layernorm_stats(v)
        vln = (vhat * gain_ref[...] + bias_ref[...]).astype(BF16)
        for gi in range(SGU_GROUPS):
            cols = slice(gi * gd, (gi + 1) * gd)
            wg = jnp.where(mask, ws_ref[gi], 0.0).astype(BF16)
            mixed = _dot(wg, vln[:, cols], "nn") + bt_ref[:, gi:gi + 1]
            out_ref[:, cols] = (u[:, cols] * mixed).astype(BF16)

    return pl.pallas_call(
        body, name="sgu_mid_fwd", grid=(t // SGU_BLOCK,),
        in_specs=[pl.BlockSpec((2, SGU_BLOCK, w), lambda n: (0, n, 0)),
                  pl.BlockSpec((1, w), lambda n: (0, 0)), pl.BlockSpec((1, w), lambda n: (0, 0)),
                  pl.BlockSpec((SGU_GROUPS, SGU_BLOCK, SGU_BLOCK), lambda n: (0, 0, 0)),
                  pl.BlockSpec((SGU_BLOCK, SGU_GROUPS), lambda n: (0, 0))],
        out_specs=pl.BlockSpec((SGU_BLOCK, w), lambda n: (n, 0)),
        out_shape=jax.ShapeDtypeStruct((t, w), BF16),
        compiler_params=_params(("parallel",)),
    )(pre, gain, bias, w_sp, b_sp_t)


def _sgu_mid_bwd(pre, dgated, gain, bias, w_sp, b_sp_t):
    _, t, w = pre.shape
    gd = w // SGU_GROUPS

    def body(pre_ref, dg_ref, gain_ref, bias_ref, ws_ref, bt_ref,
             dpre_ref, dgain_ref, dbias_ref, dws_ref, dbt_ref, dvln_s):
        @pl.when(pl.program_id(0) == 0)
        def _():
            dgain_ref[...] = jnp.zeros_like(dgain_ref)
            dbias_ref[...] = jnp.zeros_like(dbias_ref)
            dws_ref[...] = jnp.zeros_like(dws_ref)
            dbt_ref[...] = jnp.zeros_like(dbt_ref)

        mask = _causal_block_mask()
        pu = pre_ref[0].astype(F32)
        pv = pre_ref[1].astype(F32)
        u, tu = _gelu(pu)
        v, tv = _gelu(pv)
        vhat, rstd = _layernorm_stats(v)
        gain_v = gain_ref[...]
        vln = (vhat * gain_v + bias_ref[...]).astype(BF16)
        dgt = dg_ref[...].astype(F32)
        for gi in range(SGU_GROUPS):
            cols = slice(gi * gd, (gi + 1) * gd)
            wg = jnp.where(mask, ws_ref[gi], 0.0).astype(BF16)
            vg = vln[:, cols]
            mixed = _dot(wg, vg, "nn") + bt_ref[:, gi:gi + 1]
            dgg = dgt[:, cols]
            dmixed = dgg * u[:, cols]
            dmb = dmixed.astype(BF16)
            dpre_ref[0, :, cols] = (dgg * mixed * _dgelu(pu[:, cols], tu[:, cols])).astype(BF16)
            dbt_ref[:, gi:gi + 1] += jnp.sum(dmixed, axis=1, keepdims=True)
            dws_ref[gi] += jnp.where(mask, _dot(dmb, vg, "nt"), 0.0)
            dvln_s[:, cols] = _dot(wg, dmb, "tn")
        dvln = dvln_s[...]
        dgain_ref[...] += jnp.sum(dvln * vhat, axis=0, keepdims=True)
        dbias_ref[...] += jnp.sum(dvln, axis=0, keepdims=True)
        dvh = dvln * gain_v
        dv = rstd * (dvh - jnp.mean(dvh, axis=-1, keepdims=True)
                     - vhat * jnp.mean(dvh * vhat, axis=-1, keepdims=True))
        dpre_ref[1] = (dv * _dgelu(pv, tv)).astype(BF16)

    vec = pl.BlockSpec((1, w), lambda n: (0, 0))
    wsb = pl.BlockSpec((SGU_GROUPS, SGU_BLOCK, SGU_BLOCK), lambda n: (0, 0, 0))
    btb = pl.BlockSpec((SGU_BLOCK, SGU_GROUPS), lambda n: (0, 0))
    blk2 = pl.BlockSpec((2, SGU_BLOCK, w), lambda n: (0, n, 0))
    return pl.pallas_call(
        body, name="sgu_mid_bwd", grid=(t // SGU_BLOCK,),
        in_specs=[blk2, pl.BlockSpec((SGU_BLOCK, w), lambda n: (n, 0)), vec, vec, wsb, btb],
        out_specs=[blk2, vec, vec, wsb, btb],
        out_shape=[jax.ShapeDtypeStruct((2, t, w), BF16), jax.ShapeDtypeStruct((1, w), F32),
                   jax.ShapeDtypeStruct((1, w), F32),
                   jax.ShapeDtypeStruct((SGU_GROUPS, SGU_BLOCK, SGU_BLOCK), F32),
                   jax.ShapeDtypeStruct((SGU_BLOCK, SGU_GROUPS), F32)],
        scratch_shapes=[pltpu.VMEM((SGU_BLOCK, w), F32)],
        compiler_params=_params(("arbitrary",)),
    )(pre, dgated, gain, bias, w_sp, b_sp_t)


def _sgu_fwd(x, g, w_in, gain, bias, w_sp, b_sp, w_out, prefetch):
    h, ht = _rms_fwd("sgu_norm", x, g, with_transpose=True)
    pre, prefetch = _sgu_pre("sgu_pre", h, w_in, comm=_gather_phase(1, prefetch))
    gated = _sgu_mid_fwd(pre, gain, bias, w_sp, b_sp.T)
    y, prefetch = _mm_residual("sgu_out", gated, w_out, x, 1.0, tk_pref=1024, comm=_gather_phase(2, prefetch))
    return y, (x, ht, pre, gated), prefetch


def _sgu_bwd(saved, g, w_in, gain, bias, w_sp, b_sp, w_out, dx, dxb, red):
    x, ht, pre, gated = saved
    w = gated.shape[1]
    d_w_out = red.behind(0, _mm_tn, "sgu_dwout", gated, dxb)
    dgated = _mm_nt_full("sgu_dgated", dxb, w_out, BF16)
    dpre, dgain, dbias, dws, dbt = _sgu_mid_bwd(pre, dgated, gain, bias, w_sp, b_sp.T)
    d_w_in = red.behind(1, _grad_colsharded, "sgu_dwin", ht, dpre)
    dh = red.behind(2, _back_colsharded, "sgu_dh", dpre, w_in)
    dx, dxb, dg = _rms_bwd("sgu_dnorm", dh, x, g, dx)
    small = dict(ln=dg, gain=dgain, bias=dbias, w_sp=dws, b_sp=dbt.T)
    return dx, dxb, small, d_w_in, d_w_out.reshape(N_CHIPS, w // N_CHIPS, -1)


def _rope_tables(positions):
    half = QK_ROPE // 2
    inv_freq = 1.0 / (ROPE_THETA ** (jnp.arange(half, dtype=F32) / half))
    ang = positions.astype(F32)[:, None] * inv_freq
    cos, sin = jnp.cos(ang), jnp.sin(ang)
    t = positions.shape[0]
    zeros = jnp.zeros((t, half), F32)
    rest = jnp.zeros((t, LANE - QK_ROPE), F32)
    c = jnp.concatenate([cos, cos, rest + 1.0], axis=1)
    s_up = jnp.concatenate([zeros, sin, rest], axis=1)
    s_dn = jnp.concatenate([-sin, zeros, rest], axis=1)
    return c, s_up, s_dn


def _rope_apply(x, c, s_up, s_dn):
    half = QK_ROPE // 2
    return x * c + pltpu.roll(x, half, 1) * s_up + pltpu.roll(x, LANE - half, 1) * s_dn


def _rope_apply_t(dy, c, s_up, s_dn):
    half = QK_ROPE // 2
    return dy * c - pltpu.roll(dy, LANE - half, 1) * s_dn - pltpu.roll(dy, half, 1) * s_up


def _mla_norm_fwd(proj, gq, gkv):
    t, p = proj.shape
    ql, kvl = gq.shape[1], gkv.shape[1]
    tm = _tile(t, 512)

    def body(p_ref, gq_ref, gkv_ref, qn_ref, kvn_ref):
        for lo, n, g_ref, o_ref in ((0, ql, gq_ref, qn_ref), (ql, kvl, gkv_ref, kvn_ref)):
            xv = p_ref[:, lo:lo + n]
            r = lax.rsqrt(jnp.mean(xv * xv, axis=-1, keepdims=True) + EPS)
            o_ref[...] = (xv * r * g_ref[...]).astype(BF16)

    return pl.pallas_call(
        body, name="mla_norm_fwd", grid=(t // tm,),
        in_specs=[pl.BlockSpec((tm, p), lambda i: (i, 0)), pl.BlockSpec((1, ql), lambda i: (0, 0)),
                  pl.BlockSpec((1, kvl), lambda i: (0, 0))],
        out_specs=[pl.BlockSpec((tm, ql), lambda i: (i, 0)), pl.BlockSpec((tm, kvl), lambda i: (i, 0))],
        out_shape=[jax.ShapeDtypeStruct((t, ql), BF16), jax.ShapeDtypeStruct((t, kvl), BF16)],
        compiler_params=_params(("parallel",)),
    )(proj, gq, gkv)


def _mla_norm_bwd(proj, dqn, dkvn, dkr, gq, gkv):
    t, p = proj.shape
    ql, kvl = gq.shape[1], gkv.shape[1]
    tm = _tile(t, 256)

    def body(p_ref, dqn_ref, dkvn_ref, dkr_ref, gq_ref, gkv_ref, dp_ref, dgq_ref, dgkv_ref):
        @pl.when(pl.program_id(0) == 0)
        def _():
            dgq_ref[...] = jnp.zeros_like(dgq_ref)
            dgkv_ref[...] = jnp.zeros_like(dgkv_ref)

        for lo, n, g_ref, d_ref, dg_ref in ((0, ql, gq_ref, dqn_ref, dgq_ref),
                                             (ql, kvl, gkv_ref, dkvn_ref, dgkv_ref)):
            dx, dgt = _rms_bwd_math(d_ref[...], p_ref[:, lo:lo + n], g_ref[...])
            dp_ref[:, lo:lo + n] = dx.astype(BF16)
            dg_ref[...] += jnp.sum(dgt, axis=0, keepdims=True)
        dp_ref[:, ql + kvl:] = dkr_ref[...].astype(BF16)

    def row(n):
        return pl.BlockSpec((tm, n), lambda i: (i, 0))

    def vec(n):
        return pl.BlockSpec((1, n), lambda i: (0, 0))

    return pl.pallas_call(
        body, name="mla_norm_bwd", grid=(t // tm,),
        in_specs=[row(p), row(ql), row(kvl), row(LANE), vec(ql), vec(kvl)],
        out_specs=[row(p), vec(ql), vec(kvl)],
        out_shape=[jax.ShapeDtypeStruct((t, p), BF16), jax.ShapeDtypeStruct((1, ql), F32),
                   jax.ShapeDtypeStruct((1, kvl), F32)],
        compiler_params=_params(("arbitrary",)),
    )(proj, dqn, dkvn, dkr, gq, gkv)


def _mla_q_up(qn, wq, tables):
    t, ql = qn.shape
    n = wq.shape[1]
    tm = _tile(t, 2048)
    scale = QK_DIM ** -0.5 * LOG2E

    def epilogue(acc, ex, outs, ids):
        outs[0][:, :QK_NOPE] = (scale * acc[:, :QK_NOPE]).astype(BF16)
        hi = _rope_apply(acc[:, QK_NOPE:], ex[0][...], ex[1][...], ex[2][...])
        outs[0][:, QK_NOPE:] = (scale * hi).astype(BF16)

    tab = pl.BlockSpec((tm, LANE), lambda i, j, k: (i, 0))
    return _matmul(
        "mla_q_up", "nn", (t // tm, n // HEAD_PAD, 1),
        qn, pl.BlockSpec((tm, ql), lambda i, j, k: (i, 0)),
        wq, pl.BlockSpec((ql, HEAD_PAD), lambda i, j, k: (0, j)),
        [(tb, tab) for tb in tables],
        [jax.ShapeDtypeStruct((t, n), BF16)], [pl.BlockSpec((tm, HEAD_PAD), lambda i, j, k: (i, j))],
        None, epilogue)[0]


def _mla_kv_up(kvn, wkv, proj, tables, heads):
    t, kvl = kvn.shape
    n = wkv.shape[1]
    p = proj.shape[1]
    tm = _tile(t, 2048)

    def epilogue(acc, ex, outs, ids):
        kr = _rope_apply(ex[0][...], ex[1][...], ex[2][...], ex[3][...])
        outs[0][:, :QK_NOPE] = acc[:, :QK_NOPE].astype(BF16)
        outs[0][:, QK_NOPE:] = (acc[:, QK_NOPE:] + jnp.where(ids[1] < heads, kr, 1.0)).astype(BF16)

    tab = pl.BlockSpec((tm, LANE), lambda i, j, k: (i, 0))
    kr_spec = pl.BlockSpec((tm, LANE), lambda i, j, k: (i, p // LANE - 1))
    return _matmul(
        "mla_kv_up", "nn", (t // tm, n // HEAD_PAD, 1),
        kvn, pl.BlockSpec((tm, kvl), lambda i, j, k: (i, 0)),
        wkv, pl.BlockSpec((kvl, HEAD_PAD), lambda i, j, k: (0, j)),
        [(proj, kr_spec)] + [(tb, tab) for tb in tables],
        [jax.ShapeDtypeStruct((t, n), BF16)], [pl.BlockSpec((tm, HEAD_PAD), lambda i, j, k: (i, j))],
        None, epilogue)[0]


FLASH_TQ = 1024
FLASH_TK = 1024
FLASH_SPLIT = 4
FLASH_SPLIT_BWD = 2


def _chunk_mask(tq, tk, qi, ki):
    r = (qi * tq + lax.broadcasted_iota(jnp.int32, (tq, tk), 0)) // CHUNK
    c = (ki * tk + lax.broadcasted_iota(jnp.int32, (tq, tk), 1)) // CHUNK
    return c <= r


def _block_pairs(t, tq, tk, key_major):
    def visible(qi, ki):
        return (ki * tk) // CHUNK <= (qi * tq + tq - 1) // CHUNK

    def masked(qi, ki):
        return (ki * tk + tk - 1) // CHUNK > (qi * tq) // CHUNK

    nq, nk = t // tq, t // tk
    if key_major:
        sweeps = [[(qi, ki) for qi in range(nq) if visible(qi, ki)] for ki in range(nk)]
    else:
        sweeps = [[(qi, ki) for ki in range(nk) if visible(qi, ki)] for qi in range(nq)]
    qs, ks, fs = [], [], []
    for sweep in sweeps:
        for n, (qi, ki) in enumerate(sweep):
            qs.append(qi)
            ks.append(ki)
            fs.append((1 if masked(qi, ki) else 0) + (2 if n == 0 else 0) + (4 if n == len(sweep) - 1 else 0))
    return tuple(jnp.asarray(v, jnp.int32) for v in (qs, ks, fs))


def _flash_fwd(qp, kv, heads):
    t = qp.shape[0]
    tq, tk = _tile(t, FLASH_TQ), _tile(t, FLASH_TK)
    qt, kt, ft = _block_pairs(t, tq, tk, key_major=False)

    def body(qt_ref, kt_ref, ft_ref, q_ref, k_ref, v_ref, o_ref, lse_ref, m_s, acc_s):
        p = pl.program_id(1)
        qi, ki, flags = qt_ref[p], kt_ref[p], ft_ref[p]

        @pl.when(flags & 2 != 0)
        def _():
            m_s[...] = jnp.full_like(m_s, -1e30)
            acc_s[...] = jnp.zeros_like(acc_s)

        def step(masked):
            k, v = k_ref[...], v_ref[...]
            mask = _chunk_mask(tq, tk, qi, ki) if masked else None
            tg = tq // FLASH_SPLIT
            for g in range(FLASH_SPLIT):
                rows = slice(g * tg, (g + 1) * tg)
                nc = min(tk, (g + 1) * tg) if masked and tq == tk else tk
                s = _dot(q_ref[rows, :], k[:nc], "nt")
                if masked:
                    s = jnp.where(mask[rows, :nc], s, -1e30)
                m_prev = m_s[rows, :]
                m_new = jnp.maximum(m_prev, jnp.max(s, axis=1, keepdims=True))
                alpha = jnp.exp2(m_prev - m_new)
                pr = jnp.exp2(s - jnp.tile(m_new, (1, nc // LANE))).astype(BF16)
                pv = _dot(pr, v[:nc], "nn")
                acc_s[rows, :V_DIM] = alpha * acc_s[rows, :V_DIM] + pv[:, :V_DIM]
                acc_s[rows, V_DIM:] = alpha * acc_s[rows, V_DIM:] + pv[:, V_DIM:]
                m_s[rows, :] = m_new

        @pl.when(flags & 1 == 0)
        def _():
            step(False)

        @pl.when(flags & 1 != 0)
        def _():
            step(True)

        @pl.when(flags & 4 != 0)
        def _():
            l = acc_s[:, V_DIM:]
            o_ref[...] = (acc_s[:, :V_DIM] / l).astype(BF16)
            lse_ref[...] = m_s[...] + jnp.log(l) * LOG2E

    return pl.pallas_call(
        body, name="mla_flash_fwd",
        grid_spec=pltpu.PrefetchScalarGridSpec(
            num_scalar_prefetch=3, grid=(heads, int(qt.shape[0])),
            in_specs=[pl.BlockSpec((tq, HEAD_PAD), lambda h, p, qt, kt, ft: (qt[p], h)),
                      pl.BlockSpec((tk, HEAD_PAD), lambda h, p, qt, kt, ft: (kt[p], h)),
                      pl.BlockSpec((tk, HEAD_PAD), lambda h, p, qt, kt, ft: (kt[p], heads + h))],
            out_specs=[pl.BlockSpec((tq, V_DIM), lambda h, p, qt, kt, ft: (qt[p], h)),
                       pl.BlockSpec((None, tq, LANE), lambda h, p, qt, kt, ft: (h, qt[p], 0))],
            scratch_shapes=[pltpu.VMEM((tq, LANE), F32), pltpu.VMEM((tq, HEAD_PAD), F32)]),
        out_shape=[jax.ShapeDtypeStruct((t, heads * V_DIM), BF16),
                   jax.ShapeDtypeStruct((heads, t, LANE), F32)],
        compiler_params=_params(("parallel", "arbitrary")),
    )(qt, kt, ft, qp, kv, kv)


def _flash_delta(o, do, heads):
    t = o.shape[0]
    tm = _tile(t, 256)

    def body(o_ref, do_ref, d_ref):
        for h in range(heads):
            cols = slice(h * V_DIM, (h + 1) * V_DIM)
            prod = o_ref[:, cols].astype(F32) * do_ref[:, cols].astype(F32)
            d_ref[h] = jnp.broadcast_to(jnp.sum(prod, axis=1, keepdims=True), (tm, LANE))

    row = pl.BlockSpec((tm, heads * V_DIM), lambda i: (i, 0))
    return pl.pallas_call(
        body, name="mla_flash_delta", grid=(t // tm,), in_specs=[row, row],
        out_specs=pl.BlockSpec((heads, tm, LANE), lambda i: (0, i, 0)),
        out_shape=jax.ShapeDtypeStruct((heads, t, LANE), F32),
        compiler_params=_params(("parallel",)),
    )(o, do)


def _flash_bwd(qp, kv, do, lse, delta, heads):
    t = qp.shape[0]
    tq, tk = _tile(t, FLASH_TQ), _tile(t, FLASH_TK)
    qt, kt, ft = _block_pairs(t, tq, tk, key_major=True)

    def body(qt_ref, kt_ref, ft_ref, q_ref, k_ref, v_ref, do_ref, lse_ref, dl_ref, dq_ref, dk_ref, dv_ref,
             dk_s, dv_s):
        p = pl.program_id(1)
        qi, ki, flags = qt_ref[p], kt_ref[p], ft_ref[p]

        @pl.when(p == 0)
        def _():
            dq_ref[...] = jnp.zeros_like(dq_ref)

        @pl.when(flags & 2 != 0)
        def _():
            dk_s[...] = jnp.zeros_like(dk_s)
            dv_s[...] = jnp.zeros_like(dv_s)

        def step(masked):
            k, v = k_ref[...], v_ref[...]
            mask = _chunk_mask(tq, tk, qi, ki) if masked else None
            tg = tq // FLASH_SPLIT_BWD
            for g in range(FLASH_SPLIT_BWD):
                rows = slice(g * tg, (g + 1) * tg)
                nc = min(tk, (g + 1) * tg) if masked and tq == tk else tk
                q = q_ref[rows, :]
                dov = do_ref[rows, :]
                s = _dot(q, k[:nc], "nt")
                pr = jnp.exp2(s - jnp.tile(lse_ref[rows, :], (1, nc // LANE)))
                if masked:
                    pr = jnp.where(mask[rows, :nc], pr, 0.0)
                dv_s[:nc, :] += _dot(pr.astype(BF16), dov, "tn")
                dp = _dot(dov, v[:nc], "nt")
                ds = (pr * (dp - jnp.tile(dl_ref[rows, :], (1, nc // LANE)))).astype(BF16)
                dq_rows = pl.ds(pl.multiple_of(qi * tq + g * tg, tg), tg)
                dq_ref[dq_rows, :] += _dot(ds, k[:nc], "nn")
                dk_s[:nc, :] += _dot(ds, q, "tn")

        @pl.when(flags & 1 == 0)
        def _():
            step(False)

        @pl.when(flags & 1 != 0)
        def _():
            step(True)

        @pl.when(flags & 4 != 0)
        def _():
            dk_ref[...] = dk_s[...]
            dv_ref[...] = dv_s[...]

    def qrow(width):
        return pl.BlockSpec((tq, width), lambda h, p, qt, kt, ft: (qt[p], h))

    def stat():
        return pl.BlockSpec((None, tq, LANE), lambda h, p, qt, kt, ft: (h, qt[p], 0))

    return pl.pallas_call(
        body, name="mla_flash_bwd",
        grid_spec=pltpu.PrefetchScalarGridSpec(
            num_scalar_prefetch=3, grid=(heads, int(qt.shape[0])),
            in_specs=[qrow(HEAD_PAD),
                      pl.BlockSpec((tk, HEAD_PAD), lambda h, p, qt, kt, ft: (kt[p], h)),
                      pl.BlockSpec((tk, V_DIM), lambda h, p, qt, kt, ft: (kt[p], 2 * (heads + h))),
                      qrow(V_DIM), stat(), stat()],
            out_specs=[pl.BlockSpec((t, HEAD_PAD), lambda h, p, qt, kt, ft: (0, h)),
                       pl.BlockSpec((tk, HEAD_PAD), lambda h, p, qt, kt, ft: (kt[p], h)),
                       pl.BlockSpec((tk, V_DIM), lambda h, p, qt, kt, ft: (kt[p], h))],
            scratch_shapes=[pltpu.VMEM((tk, HEAD_PAD), F32), pltpu.VMEM((tk, V_DIM), F32)]),
        out_shape=[jax.ShapeDtypeStruct((t, heads * HEAD_PAD), F32),
                   jax.ShapeDtypeStruct((t, heads * HEAD_PAD), F32),
                   jax.ShapeDtypeStruct((t, heads * V_DIM), F32)],
        compiler_params=_params(("parallel", "arbitrary")),
    )(qt, kt, ft, qp, kv, kv, do, lse, delta)


def _mla_attn_post(dqp, dkp, dv, tables, heads):
    t = dqp.shape[0]
    tm = _tile(t, 256)
    scale = QK_DIM ** -0.5
    kw, vw = heads * HEAD_PAD, heads * V_DIM

    def body(dq_ref, dk_ref, dv_ref, c_ref, su_ref, sd_ref, dqb_ref, dkvb_ref, dkr_ref):
        c, su, sd = c_ref[...], su_ref[...], sd_ref[...]
        kr = jnp.zeros((tm, LANE), F32)
        for h in range(heads):
            lo = h * HEAD_PAD
            mid = lo + QK_NOPE
            dqb_ref[:, lo:mid] = (scale * dq_ref[:, lo:mid]).astype(BF16)
            dqb_ref[:, mid:mid + LANE] = (scale * _rope_apply_t(dq_ref[:, mid:mid + LANE], c, su, sd)).astype(BF16)
            kr = kr + dk_ref[:, mid:mid + LANE]
            dkvb_ref[:, kw + lo:kw + mid] = dv_ref[:, h * V_DIM:(h + 1) * V_DIM].astype(BF16)
            dkvb_ref[:, kw + mid:kw + lo + HEAD_PAD] = jnp.zeros((tm, HEAD_PAD - V_DIM), BF16)
        dkvb_ref[:, :kw] = (dk_ref[...] * (1.0 / LOG2E)).astype(BF16)
        dkr_ref[...] = _rope_apply_t(kr * (1.0 / LOG2E), c, su, sd)

    def row(n):
        return pl.BlockSpec((tm, n), lambda i: (i, 0))

    return pl.pallas_call(
        body, name="mla_attn_post", grid=(t // tm,),
        in_specs=[row(kw), row(kw), row(vw), row(LANE), row(LANE), row(LANE)],
        out_specs=[row(kw), row(2 * kw), row(LANE)],
        out_shape=[jax.ShapeDtypeStruct((t, kw), BF16), jax.ShapeDtypeStruct((t, 2 * kw), BF16),
                   jax.ShapeDtypeStruct((t, LANE), F32)],
        compiler_params=_params(("parallel",)),
    )(dqp, dkp, dv, *tables)


def _mla_weights(w_in_g, w_q_g, w_kv_g, w_out_g):
    d = w_in_g.shape[0] * w_in_g.shape[1]
    pw = w_in_g.shape[2]
    w_in = jnp.pad(w_in_g.reshape(d, pw), ((0, 0), (0, LANE - QK_ROPE)))
    ql = w_q_g.shape[1]
    wq = jnp.transpose(w_q_g, (1, 0, 2)).reshape(ql, -1, QK_DIM)
    heads = wq.shape[1]
    wq = jnp.pad(wq, ((0, 0), (0, 0), (0, HEAD_PAD - QK_DIM))).reshape(ql, heads * HEAD_PAD)
    kvl = w_kv_g.shape[1]
    wkv = jnp.transpose(w_kv_g, (1, 0, 2)).reshape(kvl, heads, QK_NOPE + V_DIM)
    wk = jnp.pad(wkv[:, :, :QK_NOPE], ((0, 0), (0, 0), (0, HEAD_PAD - QK_NOPE))).reshape(kvl, heads * HEAD_PAD)
    wv = jnp.pad(wkv[:, :, QK_NOPE:], ((0, 0), (0, 0), (0, HEAD_PAD - V_DIM))).reshape(kvl, heads * HEAD_PAD)
    return w_in, wq, jnp.concatenate([wk, wv], axis=1), w_out_g.reshape(heads * V_DIM, -1), heads


def _mla_unpermute(d_w_in, d_wq, d_wkv, heads, pw):
    d = d_w_in.shape[0]
    g_in = d_w_in[:, :pw].reshape(N_CHIPS, d // N_CHIPS, pw)
    ql = d_wq.shape[0]
    g_q = d_wq.reshape(ql, heads, HEAD_PAD)[:, :, :QK_DIM].reshape(ql, N_CHIPS, -1)
    kvl = d_wkv.shape[0]
    g_k = d_wkv[:, :heads * HEAD_PAD].reshape(kvl, heads, HEAD_PAD)[:, :, :QK_NOPE]
    g_v = d_wkv[:, heads * HEAD_PAD:].reshape(kvl, heads, HEAD_PAD)[:, :, :V_DIM]
    g_kv = jnp.concatenate([g_k, g_v], axis=2).reshape(kvl, N_CHIPS, -1)
    return g_in, jnp.transpose(g_q, (1, 0, 2)), jnp.transpose(g_kv, (1, 0, 2))


def _mla_fwd(x, g, wts, gq, gkv, tables):
    w_in, wq, wkv, w_out, heads = wts
    h = _rms_fwd("mla_norm", x, g)
    proj = _mm_nn_full("mla_proj", h, w_in, F32, tn_pref=w_in.shape[1])
    qn, kvn = _mla_norm_fwd(proj, gq, gkv)
    qp = _mla_q_up(qn, wq, tables)
    kv = _mla_kv_up(kvn, wkv, proj, tables, heads)
    o, lse = _flash_fwd(qp, kv, heads)
    y = _mm_residual("mla_out", o, w_out, x, 1.0, tk_pref=2048)
    return y, (x, h, proj, qn, kvn, qp, kv, o, lse)


def _mla_bwd(saved, g, wts, gq, gkv, tables, dx, dxb, pw):
    w_in, wq, wkv, w_out, heads = wts
    x, h, proj, qn, kvn, qp, kv, o, lse = saved
    d_w_out = _mm_tn("mla_dwout", o, dxb)
    do = _mm_nt_full("mla_do", dxb, w_out, BF16)
    dqp, dkp, dv = _flash_bwd(qp, kv, do, lse, _flash_delta(o, do, heads), heads)
    dqb, dkvb, dkr = _mla_attn_post(dqp, dkp, dv, tables, heads)
    d_wq = _mm_tn("mla_dwq", qn, dqb)
    dqn = _mm_nt_k("mla_dqn", dqb, wq, F32)
    d_wkv = _mm_tn("mla_dwkv", kvn, dkvb)
    dkvn = _mm_nt_k("mla_dkvn", dkvb, wkv, F32)
    dproj, dgq, dgkv = _mla_norm_bwd(proj, dqn, dkvn, dkr, gq, gkv)
    d_w_in = _mm_tn("mla_dwin", h, dproj, tn_pref=dproj.shape[1])
    dh = _mm_nt_full("mla_dh", dproj, w_in, F32, tn_pref=1024)
    dx, dxb, dg = _rms_bwd("mla_dnorm", dh, x, g, dx)
    g_in, g_q, g_kv = _mla_unpermute(d_w_in, d_wq, d_wkv, heads, pw)
    small = dict(ln=dg, gq=dgq, gkv=dgkv)
    return dx, dxb, small, g_in, g_q, g_kv, d_w_out.reshape(N_CHIPS, d_w_out.shape[0] // N_CHIPS, -1)


def _gather_weights(bufs, norms):
    nt = len(bufs)

    def body(*refs):
        n_in = refs[nt]
        outs, n_out = refs[nt + 1:2 * nt + 1], refs[2 * nt + 1]
        send, recv, fsend, frecv, loc, nsend, nrecv = refs[2 * nt + 2:]
        x, y, c, chips = _place()
        me = 2 * x + y
        sib = (x, y, 1 - c)

        local = pltpu.make_async_copy(n_in, n_out.at[me], loc)
        local.start()

        def place(t, chip, half):
            return outs[t].at[2 * chip[0] + chip[1], half]

        def ici(t, j, chip):
            return pltpu.make_async_remote_copy(
                src_ref=place(t, (x, y), c), dst_ref=place(t, (x, y), c), send_sem=send.at[t, j],
                recv_sem=recv.at[t, j], device_id=(*chip, c), device_id_type=MESH)

        def fwd(t, j, chip, half):
            return pltpu.make_async_remote_copy(
                src_ref=place(t, chip, half), dst_ref=place(t, chip, half), send_sem=fsend.at[t, j],
                recv_sem=frecv.at[t, j], device_id=sib, device_id_type=MESH)

        def nrm(j, chip, owner):
            return pltpu.make_async_remote_copy(
                src_ref=n_in, dst_ref=n_out.at[2 * owner[0] + owner[1]], send_sem=nsend.at[j], recv_sem=nrecv.at[j],
                device_id=(*chip, c), device_id_type=MESH)

        firsts = [ici(t, j, chip) for t in range(nt) for j, chip in enumerate(chips)]
        firsts += [nrm(j, chip, (x, y)) for j, chip in enumerate(chips)]
        for cp in firsts:
            cp.start()
        passed = []
        for t in range(nt):
            for j, chip in enumerate(chips):
                pltpu.make_async_remote_copy(
                    src_ref=place(t, chip, c), dst_ref=place(t, chip, c), send_sem=send.at[t, j],
                    recv_sem=recv.at[t, j], device_id=(*chip, c), device_id_type=MESH).wait_recv()
                cp = fwd(t, j, chip, c)
                cp.start()
                passed.append(cp)
        for t in range(nt):
            for j, chip in enumerate(chips):
                fwd(t, j, chip, 1 - c).wait_recv()
        for j, chip in enumerate(chips):
            nrm(j, chip, chip).wait_recv()
        for cp in firsts + passed:
            cp.wait_send()
        local.wait()

    out_shape = [jax.ShapeDtypeStruct(b.shape, b.dtype) for b in bufs]
    out_shape.append(jax.ShapeDtypeStruct((N_CHIPS,) + norms.shape, norms.dtype))
    res = pl.pallas_call(
        body, name="gather_weights",
        in_specs=[ANY] * (nt + 1), out_specs=[ANY] * (nt + 1), out_shape=out_shape,
        input_output_aliases={t: t for t in range(nt)},
        scratch_shapes=[pltpu.SemaphoreType.DMA((nt, 3)), pltpu.SemaphoreType.DMA((nt, 3)),
                        pltpu.SemaphoreType.DMA((nt, 3)), pltpu.SemaphoreType.DMA((nt, 3)),
                        pltpu.SemaphoreType.DMA, pltpu.SemaphoreType.DMA((3,)),
                        pltpu.SemaphoreType.DMA((3,))],
    )(*bufs, norms)
    return res[:nt], res[nt]


def _run_phase(name, phase):
    c_in, in_specs, out_specs, shapes, sems, in_place = _comm_io(phase)
    n_c, n_co = len(c_in), len(shapes)

    def body(*refs):
        ins, outs = refs[:n_c], refs[n_c:n_c + n_co]
        send, recv = refs[n_c + n_co:]
        for cp in phase.copies(ins, outs, send, recv, False):
            cp.start()
        for cp in phase.copies(ins, outs, send, recv, True):
            cp.wait_recv()
        for cp in phase.copies(ins, outs, send, recv, False):
            cp.wait_send()

    return pl.pallas_call(
        body, name=name, in_specs=in_specs, out_specs=out_specs, out_shape=shapes,
        input_output_aliases={t: t for t in range(n_c)} if in_place else {}, scratch_shapes=sems,
    )(*c_in)


def _remote(ref_src, ref_dst, send, recv, t, j, to):
    return pltpu.make_async_remote_copy(src_ref=ref_src, dst_ref=ref_dst, send_sem=send.at[t, j],
                                        recv_sem=recv.at[t, j], device_id=to, device_id_type=MESH)


class _Reduce:
    def __init__(self, units, grads, n_layers, reduced, me_idx, core_idx):
        self.units, self.n_layers, self.reduced = list(units), n_layers, reduced
        self.me_idx, self.core_idx = me_idx, core_idx
        self.local = [g.reshape(N_CHIPS, 2, g.shape[1] // 2, g.shape[2]) for g in grads]
        self.parts = None

    def _swap(self):
        fresh = [jax.ShapeDtypeStruct((g.shape[0],) + g.shape[2:], g.dtype) for g in self.local]

        def copies(ins, outs, send, recv, landing):
            x, y, c, _ = _place()
            return [_remote(outs[t] if landing else ins[t].at[:, 1 - c], outs[t], send, recv, t, 0, (x, y, 1 - c))
                    for t in range(len(ins))]

        return _Phase(self.local, fresh, (len(self.local), 1), copies)

    def _scatter(self):
        fresh = [jax.ShapeDtypeStruct((3,) + p.shape[1:], p.dtype) for p in self.parts]

        def copies(ins, outs, send, recv, landing):
            x, y, c, chips = _place()
            return [_remote(outs[t].at[j] if landing else ins[t].at[2 * chip[0] + chip[1]], outs[t].at[j],
                            send, recv, t, j, (*chip, c))
                    for t in range(len(ins)) for j, chip in enumerate(chips)]

        return _Phase(self.parts, fresh, (len(self.parts), 3), copies)

    def _join(self):
        layers = [l for _, l in self.units]

        def copies(ins, outs, send, recv, landing):
            x, y, c, _ = _place()
            refs = [outs[t].at[l, 1 - c if landing else c] for t, l in enumerate(layers)]
            return [_remote(r, r, send, recv, t, 0, (x, y, 1 - c)) for t, r in enumerate(refs)]

        return _Phase([self.reduced[n] for n, _ in self.units], [], (len(self.units), 1), copies)

    def _after(self, step, got):
        if step == 0:
            self.parts = [_add_halves(self.core_idx, g, o) for g, o in zip(self.local, got)]
        elif step == 1:
            for (n, l), p, ld in zip(self.units, self.parts, got):
                self.reduced[n] = _sum_chips(self.me_idx, self.core_idx, p, ld, l, self.n_layers[n],
                                             self.reduced.get(n))
        else:
            for (n, _), joined in zip(self.units, got):
                self.reduced[n] = joined

    def _phase(self, step):
        return (self._swap, self._scatter, self._join)[step]()

    def behind(self, step, fn, *args, **kw):
        if not self.units:
            return fn(*args, **kw)
        out, got = fn(*args, comm=self._phase(step), **kw)
        self._after(step, got)
        return out

    def alone(self):
        for step, name in enumerate(("grad_swap_halves", "grad_scatter_chips", "grad_join_halves")):
            self._after(step, _run_phase(name, self._phase(step)))


def _gather_all(block):
    m_per, n = block.shape

    def body(x_ref, out_ref, send_sems, recv_sems, local_sem):
        x, y, c, chips = _place()
        me, sibling = (x, y, c), (x, y, 1 - c)

        def rows(px, py, pc):
            return out_ref.at[pl.ds((4 * px + 2 * py + pc) * m_per, m_per), :]

        def copy(k, block_of, to, src=None):
            return pltpu.make_async_remote_copy(
                src_ref=rows(*block_of) if src is None else src, dst_ref=rows(*block_of),
                send_sem=send_sems.at[k], recv_sem=recv_sems.at[k], device_id=to, device_id_type=MESH)

        mine = pltpu.make_async_copy(x_ref, rows(*me), local_sem)
        mine.start()
        first = [copy(0, me, sibling, src=x_ref)]
        first += [copy(1 + j, me, (*chip, c), src=x_ref) for j, chip in enumerate(chips)]
        for cp in first:
            cp.start()
        passed = [copy(4 + j, (*chip, c), sibling) for j, chip in enumerate(chips)]
        for j, chip in enumerate(chips):
            copy(1 + j, (*chip, c), me).wait_recv()
            passed[j].start()
        copy(0, sibling, me).wait_recv()
        for j, chip in enumerate(chips):
            copy(4 + j, (*chip, 1 - c), me).wait_recv()
        for cp in first + passed:
            cp.wait_send()
        mine.wait()

    return pl.pallas_call(
        body, name="gather_small_grads",
        out_shape=jax.ShapeDtypeStruct((N_DEV * m_per, n), block.dtype),
        in_specs=[pl.BlockSpec(memory_space=pltpu.VMEM)],
        out_specs=pl.BlockSpec(memory_space=pltpu.VMEM),
        scratch_shapes=[pltpu.SemaphoreType.DMA((7,)), pltpu.SemaphoreType.DMA((7,)), pltpu.SemaphoreType.DMA],
    )(block)


def _row_tile(r, c, elems=512 * 1024):
    t = max(8, min(r, (elems // c) // 8 * 8))
    while t > 8 and r % t:
        t -= 8
    return t if r % t == 0 else r


def _add_halves(idx, grad, other):
    n, _, r, w = grad.shape
    tr = _row_tile(r, w)

    def body(idx_ref, g_ref, o_ref, out_ref):
        out_ref[...] = (g_ref[...].astype(F32) + o_ref[...].astype(F32)).astype(BF16)

    return pl.pallas_call(
        body, name="grad_add_halves",
        grid_spec=pltpu.PrefetchScalarGridSpec(
            num_scalar_prefetch=1, grid=(n, r // tr),
            in_specs=[pl.BlockSpec((None, None, tr, w), lambda k, i, idx: (k, idx[0], i, 0)),
                      pl.BlockSpec((None, tr, w), lambda k, i, idx: (k, i, 0))],
            out_specs=pl.BlockSpec((None, tr, w), lambda k, i, idx: (k, i, 0))),
        out_shape=jax.ShapeDtypeStruct((n, r, w), BF16),
        compiler_params=_params(("parallel", "parallel")),
    )(idx, grad, other)


def _sum_chips(me_idx, core_idx, part, landed, layer, n_layers, prev):
    _, r, w = part.shape
    tr = _row_tile(r, w)

    def body(me_ref, c_ref, p_ref, l_ref, *rest):
        acc = p_ref[...].astype(F32)
        for j in range(3):
            acc = acc + l_ref[j].astype(F32)
        rest[-1][...] = acc

    return pl.pallas_call(
        body, name="grad_sum_chips",
        grid_spec=pltpu.PrefetchScalarGridSpec(
            num_scalar_prefetch=2, grid=(r // tr,),
            in_specs=[pl.BlockSpec((None, tr, w), lambda i, me, c: (me[0], i, 0)),
                      pl.BlockSpec((3, tr, w), lambda i, me, c: (0, i, 0))] + ([] if prev is None else [ANY]),
            out_specs=pl.BlockSpec((None, None, tr, w), lambda i, me, c: (layer, c[0], i, 0))),
        out_shape=jax.ShapeDtypeStruct((n_layers, 2, r, w), F32),
        input_output_aliases={} if prev is None else {4: 0},
        compiler_params=_params(("parallel",)),
    )(me_idx, core_idx, part, landed, *([] if prev is None else [prev]))


def _adamw_math(w, g, m, v):
    m = ADAM_B1 * m + (1.0 - ADAM_B1) * g
    v = ADAM_B2 * v + (1.0 - ADAM_B2) * (g * g)
    m_hat = m / (1.0 - ADAM_B1 ** ADAM_STEP)
    v_hat = v / (1.0 - ADAM_B2 ** ADAM_STEP)
    delta = -ADAM_LR * (m_hat / (jnp.sqrt(v_hat) + ADAM_EPS) + ADAM_WD * w)
    return delta, m, v


def _adamw(name, w, g, m, v):
    r, c = w.shape
    tr = _row_tile(r, c, 256 * 1024)

    def body(w_ref, g_ref, m_ref, v_ref, d_ref, nm_ref, nv_ref):
        d_ref[...], nm_ref[...], nv_ref[...] = _adamw_math(w_ref[...], g_ref[...], m_ref[...], v_ref[...])

    blk = pl.BlockSpec((tr, c), lambda i: (i, 0))
    return pl.pallas_call(
        body, name=name, grid=(r // tr,), in_specs=[blk] * 4, out_specs=[blk] * 3,
        out_shape=[jax.ShapeDtypeStruct((r, c), F32)] * 3,
        compiler_params=_params(("parallel",)),
    )(w, g, m, v)


def _adamw_summed(w, parts, m, v):
    r, c = w.shape

    def body(w_ref, p_ref, m_ref, v_ref, g_ref, d_ref, nm_ref, nv_ref):
        g = p_ref[0:r, :]
        for k in range(1, N_DEV):
            g = g + p_ref[k * r:(k + 1) * r, :]
        g_ref[...] = g
        d_ref[...], nm_ref[...], nv_ref[...] = _adamw_math(w_ref[...], g, m_ref[...], v_ref[...])

    return pl.pallas_call(
        body, name="adamw_replicated",
        out_shape=[jax.ShapeDtypeStruct((r, c), F32)] * 4,
        compiler_params=pltpu.CompilerParams(vmem_limit_bytes=VMEM_LIMIT_BYTES),
    )(w, parts, m, v)


def _pack(arrays):
    return jnp.concatenate([a.reshape(-1, LANE) for a in arrays], axis=0)


def _unpack(packed, shapes):
    out, row = [], 0
    for s in shapes:
        n = math.prod(s) // LANE
        out.append(packed[row:row + n].reshape(s))
        row += n
    return out


def _cast_into(name, idx, w, layer):
    _, rows, c = w.shape
    r = rows // 2
    tr = _row_tile(r, c)
    per = r // tr

    def body(idx_ref, w_ref, o_ref):
        o_ref[...] = w_ref[...].astype(BF16)

    return pl.pallas_call(
        body, name=name,
        grid_spec=pltpu.PrefetchScalarGridSpec(
            num_scalar_prefetch=1, grid=(2, per),
            in_specs=[pl.BlockSpec((None, tr, c), lambda h, i, idx: (layer, h * per + i, 0))],
            out_specs=pl.BlockSpec((None, None, tr, c), lambda h, i, idx: (idx[0], h, i, 0))),
        out_shape=jax.ShapeDtypeStruct((N_CHIPS, 2, r, c), BF16),
        compiler_params=_params(("parallel", "parallel")),
    )(idx, w)


BIG = ["ffn1_w_in", "ffn1_w_out", "ffn2_w_in", "ffn2_w_out", "sgu_w_in", "sgu_w_out",
       "mla_w_in", "mla_w_q_up", "mla_w_kv_up", "mla_w_out"]
STAGES = [
    [("ffn1_w_in", 0), ("ffn1_w_out", 0)],
    [("sgu_w_in", 0), ("sgu_w_out", 0)],
    [("ffn2_w_in", 0), ("ffn2_w_out", 0)],
    [("ffn1_w_in", 1), ("ffn1_w_out", 1)],
    [("mla_w_in", 0), ("mla_w_q_up", 0), ("mla_w_kv_up", 0), ("mla_w_out", 0)],
    [("ffn2_w_in", 1), ("ffn2_w_out", 1)],
]
REPLICATED = ["ln_ffn1", "ln_mix", "ln_ffn2", "sgu_v_gain", "sgu_v_bias", "sgu_w_spatial", "sgu_b_spatial",
              "ln_final"]
NORM_SHARDS = ["mla_q_norm", "mla_kv_norm"]
WEIGHTS = ["ln_ffn1", "ffn1_w_in", "ffn1_w_out", "ln_mix", "ln_ffn2", "ffn2_w_in", "ffn2_w_out", "sgu_w_in",
           "sgu_v_gain", "sgu_v_bias", "sgu_w_spatial", "sgu_b_spatial", "sgu_w_out", "mla_w_in", "mla_q_norm",
           "mla_w_q_up", "mla_kv_norm", "mla_w_kv_up", "mla_w_out", "ln_final"]


def kernel(x, positions, ln_ffn1, ffn1_w_in, ffn1_w_out, ln_mix, ln_ffn2, ffn2_w_in, ffn2_w_out, sgu_w_in, sgu_v_gain, sgu_v_bias, sgu_w_spatial, sgu_b_spatial, sgu_w_out, mla_w_in, mla_q_norm, mla_w_q_up, mla_kv_norm, mla_w_kv_up, mla_w_out, ln_final, loss_target, m_ln_ffn1, m_ffn1_w_in, m_ffn1_w_out, m_ln_mix, m_ln_ffn2, m_ffn2_w_in, m_ffn2_w_out, m_sgu_w_in, m_sgu_v_gain, m_sgu_v_bias, m_sgu_w_spatial, m_sgu_b_spatial, m_sgu_w_out, m_mla_w_in, m_mla_q_norm, m_mla_w_q_up, m_mla_kv_norm, m_mla_w_kv_up, m_mla_w_out, m_ln_final, v_ln_ffn1, v_ffn1_w_in, v_ffn1_w_out, v_ln_mix, v_ln_ffn2, v_ffn2_w_in, v_ffn2_w_out, v_sgu_w_in, v_sgu_v_gain, v_sgu_v_bias, v_sgu_w_spatial, v_sgu_b_spatial, v_sgu_w_out, v_mla_w_in, v_mla_q_norm, v_mla_w_q_up, v_mla_kv_norm, v_mla_w_kv_up, v_mla_w_out, v_ln_final):
    given = dict(locals())
    w = {n: given[n] for n in WEIGHTS}
    mom = {n: given["m_" + n] for n in WEIGHTS}
    var = {n: given["v_" + n] for n in WEIGHTS}
    t, d = x.shape[1], x.shape[2]
    xs = x.reshape(t, d)
    target = loss_target.reshape(t, d)
    me = 2 * lax.axis_index("x") + lax.axis_index("y")

    me_idx = jnp.reshape(me, (1,)).astype(jnp.int32)
    core_idx = jnp.reshape(lax.axis_index("c"), (1,)).astype(jnp.int32)

    bufs = {(n, l): _cast_into(f"cast_{n}_{l}", me_idx, w[n], l) for n in BIG for l in range(w[n].shape[0])}
    nq = mla_q_norm.shape[1]
    norms = jnp.pad(jnp.concatenate([mla_q_norm, mla_kv_norm], axis=0), ((0, 6), (0, LANE - nq)))

    def take(stage):
        return [bufs[u] for u in stage]

    def put(stage, arrays):
        bufs.update(zip(stage, arrays))

    def full(unit):
        a = bufs[unit]
        return a.reshape(N_CHIPS, a.shape[1] * a.shape[2], a.shape[3])

    def rows(unit):
        a = bufs[unit]
        return a.reshape(-1, a.shape[-1])

    first, norms_g = _gather_weights(take(STAGES[0]), norms)
    put(STAGES[0], first)
    gq = norms_g[:, 0, :nq].reshape(1, N_CHIPS * nq)
    gkv = norms_g[:, 1, :nq].reshape(1, N_CHIPS * nq)
    tables = _rope_tables(positions.reshape(t))
    sgu_small = (sgu_v_gain, sgu_v_bias, sgu_w_spatial[0], sgu_b_spatial[0])

    a0, s_f1_0, got = _ffn_fwd("l0_ffn1", xs, ln_ffn1[0], full(("ffn1_w_in", 0)), rows(("ffn1_w_out", 0)),
                               take(STAGES[1]))
    put(STAGES[1], got)
    a1, s_sgu, got = _sgu_fwd(a0, ln_mix[0], full(("sgu_w_in", 0)), *sgu_small, rows(("sgu_w_out", 0)),
                              take(STAGES[2]))
    put(STAGES[2], got)
    a2, s_f2_0, got = _ffn_fwd("l0_ffn2", a1, ln_ffn2[0], full(("ffn2_w_in", 0)), rows(("ffn2_w_out", 0)),
                               take(STAGES[3]))
    put(STAGES[3], got)
    a3, s_f1_1, got = _ffn_fwd("l1_ffn1", a2, ln_ffn1[1], full(("ffn1_w_in", 1)), rows(("ffn1_w_out", 1)),
                               take(STAGES[4] + STAGES[5]))
    put(STAGES[4] + STAGES[5], got)
    mla_wts = _mla_weights(full(("mla_w_in", 0)), full(("mla_w_q_up", 0)), full(("mla_w_kv_up", 0)),
                           full(("mla_w_out", 0)))
    a4, s_mla = _mla_fwd(a3, ln_mix[1], mla_wts, gq, gkv, tables)
    a5, s_f2_1, _ = _ffn_fwd("l1_ffn2", a4, ln_ffn2[1], full(("ffn2_w_in", 1)), rows(("ffn2_w_out", 1)), [])

    gr, reduced = {}, {}
    n_layers = {n: w[n].shape[0] for n in BIG}

    def reduce_of(stages):
        units = [u for s in stages for u in STAGES[s]]
        return _Reduce(units, [gr[u] for u in units], n_layers, reduced, me_idx, core_idx)

    loss_part, dx, dxb, dg_final = _loss_bwd(a5, ln_final, target)
    dx, dxb, dg_f2_1, gr["ffn2_w_in", 1], gr["ffn2_w_out", 1] = _ffn_bwd(
        "l1_ffn2", s_f2_1, ln_ffn2[1], full(("ffn2_w_in", 1)), rows(("ffn2_w_out", 1)), dx, dxb, reduce_of([]))
    (dx, dxb, sm_mla, gr["mla_w_in", 0], gr["mla_w_q_up", 0], gr["mla_w_kv_up", 0],
     gr["mla_w_out", 0]) = _mla_bwd(s_mla, ln_mix[1], mla_wts, gq, gkv, tables, dx, dxb, mla_w_in.shape[2])
    dx, dxb, dg_f1_1, gr["ffn1_w_in", 1], gr["ffn1_w_out", 1] = _ffn_bwd(
        "l1_ffn1", s_f1_1, ln_ffn1[1], full(("ffn1_w_in", 1)), rows(("ffn1_w_out", 1)), dx, dxb, reduce_of([5, 4]))
    dx, dxb, dg_f2_0, gr["ffn2_w_in", 0], gr["ffn2_w_out", 0] = _ffn_bwd(
        "l0_ffn2", s_f2_0, ln_ffn2[0], full(("ffn2_w_in", 0)), rows(("ffn2_w_out", 0)), dx, dxb, reduce_of([3]))
    dx, dxb, sm_sgu, gr["sgu_w_in", 0], gr["sgu_w_out", 0] = _sgu_bwd(
        s_sgu, ln_mix[0], full(("sgu_w_in", 0)), *sgu_small, rows(("sgu_w_out", 0)), dx, dxb, reduce_of([2]))
    dx, dxb, dg_f1_0, gr["ffn1_w_in", 0], gr["ffn1_w_out", 0] = _ffn_bwd(
        "l0_ffn1", s_f1_0, ln_ffn1[0], full(("ffn1_w_in", 0)), rows(("ffn1_w_out", 0)), dx, dxb, reduce_of([1]))
    reduce_of([0]).alone()
    big_grad = {n: reduced[n].reshape(w[n].shape) for n in BIG}

    small_parts = [
        jnp.concatenate([dg_f1_0, dg_f1_1], axis=0), jnp.concatenate([sm_sgu["ln"], sm_mla["ln"]], axis=0),
        jnp.concatenate([dg_f2_0, dg_f2_1], axis=0), sm_sgu["gain"], sm_sgu["bias"], sm_sgu["w_sp"],
        sm_sgu["b_sp"], dg_final]
    rep_shapes = [w[n].shape for n in REPLICATED]
    gq_row = jnp.pad(sm_mla["gq"], ((0, 0), (0, N_CHIPS * (LANE - nq))))
    gkv_row = jnp.pad(sm_mla["gkv"], ((0, 0), (0, N_CHIPS * (LANE - nq))))
    packed = _pack(small_parts + [gq_row, gkv_row, loss_part])
    packed = jnp.pad(packed, ((0, -packed.shape[0] % 8), (0, 0)))
    everyone = _gather_all(packed)
    rows = packed.shape[0]
    zero_rows = jnp.zeros((rows - sum(math.prod(s) // LANE for s in rep_shapes), LANE), F32)
    pw = jnp.concatenate([_pack([w[n] for n in REPLICATED]), zero_rows], axis=0)
    pm = jnp.concatenate([_pack([mom[n] for n in REPLICATED]), zero_rows], axis=0)
    pv = jnp.concatenate([_pack([var[n] for n in REPLICATED]), zero_rows + 1.0], axis=0)
    g_all, d_all, m_all, v_all = _adamw_summed(pw, everyone, pm, pv)
    tail_shapes = [(1, N_CHIPS * LANE), (1, N_CHIPS * LANE), (1, LANE)]
    rep_grad = dict(zip(REPLICATED, _unpack(g_all, rep_shapes + tail_shapes)[:len(REPLICATED)]))
    rep_delta = dict(zip(REPLICATED, _unpack(d_all, rep_shapes)))
    rep_m = dict(zip(REPLICATED, _unpack(m_all, rep_shapes)))
    rep_v = dict(zip(REPLICATED, _unpack(v_all, rep_shapes)))
    tail = _unpack(g_all, rep_shapes + tail_shapes)[len(REPLICATED):]
    loss = tail[2][0, 0]
    norm_grad = {
        "mla_q_norm": lax.dynamic_slice(tail[0], (0, me * nq), (1, nq)),
        "mla_kv_norm": lax.dynamic_slice(tail[1], (0, me * nq), (1, nq)),
    }

    grad, delta, new_m, new_v = {}, {}, {}, {}
    for n in BIG:
        shp = w[n].shape
        flat = lambda a: a.reshape(-1, shp[-1])
        dl, nm, nv = _adamw("adamw_" + n, flat(w[n]), flat(big_grad[n]), flat(mom[n]), flat(var[n]))
        grad[n], delta[n], new_m[n], new_v[n] = big_grad[n], dl.reshape(shp), nm.reshape(shp), nv.reshape(shp)
    for n in REPLICATED:
        grad[n], delta[n], new_m[n], new_v[n] = rep_grad[n], rep_delta[n], rep_m[n], rep_v[n]
    stack = lambda dct: jnp.concatenate([dct[n] for n in NORM_SHARDS], axis=0)
    dl, nm, nv = _adamw("adamw_norm_shards", stack(w), stack(norm_grad), stack(mom), stack(var))
    for i, n in enumerate(NORM_SHARDS):
        grad[n], delta[n], new_m[n], new_v[n] = norm_grad[n], dl[i:i + 1], nm[i:i + 1], nv[i:i + 1]

    grad_x = dx.reshape(x.shape)
    return (loss, grad_x, *[grad[n] for n in WEIGHTS], *[delta[n] for n in WEIGHTS],
            *[new_m[n] for n in WEIGHTS], *[new_v[n] for n in WEIGHTS])
```

```python
import functools
import math

import jax
import jax.numpy as jnp
from jax import lax
from jax.experimental import pallas as pl
from jax.experimental.pallas import tpu as pltpu

F32 = jnp.float32
BF16 = jnp.bfloat16
MESH = pl.DeviceIdType.MESH

EPS = 1e-6
CHUNK = 64
SGU_BLOCK = 128
SGU_GROUPS = 8
QK_NOPE = 128
QK_ROPE = 64
V_DIM = 128
QK_DIM = QK_NOPE + QK_ROPE
HEAD_PAD = 256
LOG2E = math.log2(math.e)
ROPE_THETA = 10000.0
N_CHIPS = 4
N_DEV = 8

ADAM_LR = 0.001
ADAM_B1 = 0.9
ADAM_B2 = 0.999
ADAM_EPS = 1e-08
ADAM_WD = 0.01
ADAM_STEP = 10

LANE = 128
VMEM_LIMIT_BYTES = 56 * 1024 * 1024

_DIMS = {
    "nn": (((1,), (0,)), ((), ())),
    "nt": (((1,), (1,)), ((), ())),
    "tn": (((0,), (0,)), ((), ())),
}


def _tile(n, pref):
    t = (min(pref, n) // LANE) * LANE
    while t >= LANE:
        if n % t == 0:
            return t
        t -= LANE
    return n


def _params(sem):
    return pltpu.CompilerParams(dimension_semantics=sem, vmem_limit_bytes=VMEM_LIMIT_BYTES)


def _dot(a, b, mode):
    return lax.dot_general(a, b, _DIMS[mode], preferred_element_type=F32)


def _place():
    x, y, c = lax.axis_index("x"), lax.axis_index("y"), lax.axis_index("c")
    chips = [(1 - x, y), (x, 1 - y), (1 - x, 1 - y)]
    return x, y, c, chips


ANY = pl.BlockSpec(memory_space=pl.ANY)


def _gather_copies(phase, bufs, send, recv, landing):
    x, y, c, chips = _place()
    copies = []
    for t, buf in enumerate(bufs):
        for j, chip in enumerate(chips):
            there = 2 * chip[0] + chip[1]
            if phase == 1:
                src, lands, to = buf.at[2 * x + y, c], buf.at[there, c], (*chip, c)
            else:
                src, lands, to = buf.at[there, c], buf.at[there, 1 - c], (x, y, 1 - c)
            ref = lands if landing else src
            copies.append(pltpu.make_async_remote_copy(
                src_ref=ref, dst_ref=ref, send_sem=send.at[t, j], recv_sem=recv.at[t, j], device_id=to,
                device_id_type=MESH))
    return copies


class _Phase:
    def __init__(self, ins, fresh, sems, copies):
        self.ins, self.fresh, self.sems, self.copies = list(ins), list(fresh), sems, copies


def _gather_phase(phase, bufs):
    return _Phase(bufs, [], (len(bufs), 3),
                  lambda ins, outs, send, recv, landing: _gather_copies(phase, outs, send, recv, landing))


def _comm_io(comm):
    if comm is None:
        return [], [], [], [], [], False
    shapes = comm.fresh or [jax.ShapeDtypeStruct(b.shape, b.dtype) for b in comm.ins]
    sems = [pltpu.SemaphoreType.DMA(comm.sems), pltpu.SemaphoreType.DMA(comm.sems)]
    return comm.ins, [ANY] * len(comm.ins), [ANY] * len(shapes), shapes, sems, not comm.fresh


def _comm_run(comm, in_refs, out_refs, send, recv, first, last):
    @pl.when(first)
    def _():
        for cp in comm.copies(in_refs, out_refs, send, recv, False):
            cp.start()

    def finish():
        for cp in comm.copies(in_refs, out_refs, send, recv, True):
            cp.wait_recv()
        for cp in comm.copies(in_refs, out_refs, send, recv, False):
            cp.wait_send()

    return last, finish


def _matmul(name, mode, grid, a, a_spec, b, b_spec, extras, out_shapes, out_specs, acc_shape, epilogue, comm=None):
    nk = grid[2]
    n_ex = len(extras)
    n_out = len(out_shapes)
    c_in, c_in_specs, c_out_specs, c_shapes, c_sems, in_place = _comm_io(comm)
    n_c, n_co = len(c_in), len(c_shapes)

    def body(*refs):
        a_ref, b_ref = refs[0], refs[1]
        ex = refs[2:2 + n_ex]
        outs = refs[2 + n_ex + n_c:2 + n_ex + n_c + n_out]
        ids = (pl.program_id(0), pl.program_id(1))
        k = pl.program_id(2)
        if comm is not None:
            c_ins = refs[2 + n_ex:2 + n_ex + n_c]
            c_outs = refs[2 + n_ex + n_c + n_out:2 + n_ex + n_c + n_out + n_co]
            send, recv = refs[2 + n_ex + n_c + n_out + n_co:2 + n_ex + n_c + n_out + n_co + 2]
            first = jnp.logical_and(jnp.logical_and(ids[0] == 0, ids[1] == 0), k == 0)
            last = jnp.logical_and(jnp.logical_and(ids[0] == grid[0] - 1, ids[1] == grid[1] - 1), k == nk - 1)
            last, finish = _comm_run(comm, c_ins, c_outs, send, recv, first, last)
        part = _dot(a_ref[...], b_ref[...], mode)
        if nk == 1:
            epilogue(part, ex, outs, ids)
        else:
            acc = refs[-1]

            @pl.when(k == 0)
            def _():
                acc[...] = part

            @pl.when(k > 0)
            def _():
                acc[...] += part

            @pl.when(k == nk - 1)
            def _():
                epilogue(acc[...], ex, outs, ids)

        if comm is not None:
            pl.when(last)(finish)

    scratch = c_sems + ([pltpu.VMEM(acc_shape, F32)] if nk > 1 else [])
    sem = ("parallel", "parallel", "arbitrary") if comm is None else ("arbitrary",) * 3
    return pl.pallas_call(
        body,
        name=name,
        grid=grid,
        in_specs=[a_spec, b_spec] + [s for _, s in extras] + c_in_specs,
        out_specs=list(out_specs) + c_out_specs,
        out_shape=list(out_shapes) + c_shapes,
        input_output_aliases={2 + n_ex + t: n_out + t for t in range(n_c)} if in_place else {},
        scratch_shapes=scratch,
        compiler_params=_params(sem),
    )(a, b, *[e for e, _ in extras], *c_in)


def _store(scale, dtype):
    def epilogue(acc, ex, outs, ids):
        v = acc if scale == 1.0 else acc * scale
        outs[0][...] = v.astype(dtype)

    return epilogue


def _mm_nn_full(name, a, b, out_dtype, tm_pref=1024, tn_pref=512):
    m, kd = a.shape
    n = b.shape[1]
    tm, tn = _tile(m, tm_pref), _tile(n, tn_pref)
    return _matmul(
        name, "nn", (m // tm, n // tn, 1),
        a, pl.BlockSpec((tm, kd), lambda i, j, k: (i, 0)),
        b, pl.BlockSpec((kd, tn), lambda i, j, k: (0, j)),
        [], [jax.ShapeDtypeStruct((m, n), out_dtype)], [pl.BlockSpec((tm, tn), lambda i, j, k: (i, j))],
        None, _store(1.0, out_dtype))[0]


def _mm_nt_full(name, a, b, out_dtype, scale=1.0, tm_pref=2048, tn_pref=512):
    m, kd = a.shape
    n = b.shape[0]
    tm, tn = _tile(m, tm_pref), _tile(n, tn_pref)
    return _matmul(
        name, "nt", (m // tm, n // tn, 1),
        a, pl.BlockSpec((tm, kd), lambda i, j, k: (i, 0)),
        b, pl.BlockSpec((tn, kd), lambda i, j, k: (j, 0)),
        [], [jax.ShapeDtypeStruct((m, n), out_dtype)], [pl.BlockSpec((tm, tn), lambda i, j, k: (i, j))],
        None, _store(scale, out_dtype))[0]


def _mm_nt_k(name, a, b, out_dtype, tk_pref=1024, tm_pref=1024, tn_pref=1024):
    m, kd = a.shape
    n = b.shape[0]
    tm, tn, tk = _tile(m, tm_pref), _tile(n, tn_pref), _tile(kd, tk_pref)
    return _matmul(
        name, "nt", (m // tm, n // tn, kd // tk),
        a, pl.BlockSpec((tm, tk), lambda i, j, k: (i, k)),
        b, pl.BlockSpec((tn, tk), lambda i, j, k: (j, k)),
        [], [jax.ShapeDtypeStruct((m, n), out_dtype)], [pl.BlockSpec((tm, tn), lambda i, j, k: (i, j))],
        (tm, tn), _store(1.0, out_dtype))[0]


def _mm_tn(name, a, b, scale=1.0, tm_pref=1024, tn_pref=1024, tk_pref=1024, comm=None):
    t, m = a.shape
    n = b.shape[1]
    tm, tn, tk = _tile(m, tm_pref), _tile(n, tn_pref), _tile(t, tk_pref)
    res = _matmul(
        name, "tn", (m // tm, n // tn, t // tk),
        a, pl.BlockSpec((tk, tm), lambda i, j, k: (k, i)),
        b, pl.BlockSpec((tk, tn), lambda i, j, k: (k, j)),
        [], [jax.ShapeDtypeStruct((m, n), BF16)], [pl.BlockSpec((tm, tn), lambda i, j, k: (i, j))],
        (tm, tn), _store(scale, BF16), comm=comm)
    return res[0] if comm is None else (res[0], res[1:])


def _mm_residual(name, a, b, x, scale, tm_pref=1024, tn_pref=1024, tk_pref=1408, comm=None):
    m, kd = a.shape
    n = b.shape[1]
    tm, tn, tk = _tile(m, tm_pref), _tile(n, tn_pref), _tile(kd, tk_pref)

    def epilogue(acc, ex, outs, ids):
        outs[0][...] = ex[0][...] + scale * acc

    res = _matmul(
        name, "nn", (m // tm, n // tn, kd // tk),
        a, pl.BlockSpec((tm, tk), lambda i, j, k: (i, k)),
        b, pl.BlockSpec((tk, tn), lambda i, j, k: (k, j)),
        [(x, pl.BlockSpec((tm, tn), lambda i, j, k: (i, j)))],
        [jax.ShapeDtypeStruct((m, n), F32)], [pl.BlockSpec((tm, tn), lambda i, j, k: (i, j))],
        (tm, tn), epilogue, comm=comm)
    return res[0] if comm is None else (res[0], res[1:])


def _rms_fwd(name, x, g, with_transpose=False):
    t, d = x.shape
    tm = _tile(t, 512)

    def body(x_ref, g_ref, h_ref, *ht_ref):
        xv = x_ref[...]
        r = lax.rsqrt(jnp.mean(xv * xv, axis=-1, keepdims=True) + EPS)
        h = xv * r * g_ref[...]
        h_ref[...] = h.astype(BF16)
        if with_transpose:
            ht_ref[0][...] = h.T.astype(BF16)

    out_specs = [pl.BlockSpec((tm, d), lambda i: (i, 0))]
    out_shape = [jax.ShapeDtypeStruct((t, d), BF16)]
    if with_transpose:
        out_specs.append(pl.BlockSpec((d, tm), lambda i: (0, i)))
        out_shape.append(jax.ShapeDtypeStruct((d, t), BF16))
    res = pl.pallas_call(
        body, name=name, grid=(t // tm,),
        in_specs=[pl.BlockSpec((tm, d), lambda i: (i, 0)), pl.BlockSpec((1, d), lambda i: (0, 0))],
        out_specs=out_specs, out_shape=out_shape,
        compiler_params=_params(("parallel",)),
    )(x, g.reshape(1, d))
    return res if with_transpose else res[0]


def _rms_bwd_math(dh, xv, g):
    r = lax.rsqrt(jnp.mean(xv * xv, axis=-1, keepdims=True) + EPS)
    xhat = xv * r
    dxh = dh * g
    dx = r * (dxh - xhat * jnp.mean(dxh * xhat, axis=-1, keepdims=True))
    return dx, dh * xhat


def _rms_bwd(name, dh, x, g, dres):
    t, d = x.shape
    tm = _tile(t, 256)

    def body(dh_ref, x_ref, g_ref, dres_ref, dx_ref, dxb_ref, dg_ref):
        dx, dgt = _rms_bwd_math(dh_ref[...].astype(F32), x_ref[...], g_ref[...])
        dx = dres_ref[...] + dx
        dx_ref[...] = dx
        dxb_ref[...] = dx.astype(BF16)

        @pl.when(pl.program_id(0) == 0)
        def _():
            dg_ref[...] = jnp.zeros_like(dg_ref)

        dg_ref[...] += jnp.sum(dgt, axis=0, keepdims=True)

    row = pl.BlockSpec((tm, d), lambda i: (i, 0))
    vec = pl.BlockSpec((1, d), lambda i: (0, 0))
    return pl.pallas_call(
        body, name=name, grid=(t // tm,),
        in_specs=[row, row, vec, row],
        out_specs=[row, row, vec],
        out_shape=[jax.ShapeDtypeStruct((t, d), F32), jax.ShapeDtypeStruct((t, d), BF16),
                   jax.ShapeDtypeStruct((1, d), F32)],
        compiler_params=_params(("arbitrary",)),
    )(dh, x, g.reshape(1, d), dres)


def _loss_bwd(x, g, target):
    t, d = x.shape
    tm = _tile(t, 256)

    def body(x_ref, g_ref, tgt_ref, loss_ref, dx_ref, dxb_ref, dg_ref):
        xv = x_ref[...]
        gv = g_ref[...]
        r = lax.rsqrt(jnp.mean(xv * xv, axis=-1, keepdims=True) + EPS)
        err = xv * r * gv - tgt_ref[...]
        part = 0.5 * jnp.sum(jnp.mean(err * err, axis=-1, keepdims=True), axis=0, keepdims=True)
        dx, dgt = _rms_bwd_math(err * (1.0 / d), xv, gv)
        dx_ref[...] = dx
        dxb_ref[...] = dx.astype(BF16)

        @pl.when(pl.program_id(0) == 0)
        def _():
            dg_ref[...] = jnp.zeros_like(dg_ref)
            loss_ref[...] = jnp.zeros_like(loss_ref)

        dg_ref[...] += jnp.sum(dgt, axis=0, keepdims=True)
        loss_ref[...] += jnp.broadcast_to(part, loss_ref.shape)

    row = pl.BlockSpec((tm, d), lambda i: (i, 0))
    vec = pl.BlockSpec((1, d), lambda i: (0, 0))
    return pl.pallas_call(
        body, name="loss_bwd", grid=(t // tm,),
        in_specs=[row, vec, row],
        out_specs=[pl.BlockSpec((1, LANE), lambda i: (0, 0)), row, row, vec],
        out_shape=[jax.ShapeDtypeStruct((1, LANE), F32), jax.ShapeDtypeStruct((t, d), F32),
                   jax.ShapeDtypeStruct((t, d), BF16), jax.ShapeDtypeStruct((1, d), F32)],
        compiler_params=_params(("arbitrary",)),
    )(x, g.reshape(1, d), target)


UP_SPLIT = 2


def _sigmoid(x):
    return 0.5 * jnp.tanh(0.5 * x) + 0.5


def _ffn_up(name, h, w_in, comm=None):
    t, d = h.shape
    fs = w_in.shape[2]
    f = 2 * fs
    tm, tn = _tile(t, 2048), _tile(fs, 256)
    per = fs // tn
    grid = (t // tm, f // tn)
    c_in, c_in_specs, c_out_specs, c_shapes, c_sems, in_place = _comm_io(comm)
    n_c, n_co = len(c_in), len(c_shapes)

    def body(*refs):
        h_ref, wg_ref, wu_ref = refs[:3]
        gu_ref, z_ref = refs[3 + n_c:5 + n_c]
        if comm is not None:
            i, j = pl.program_id(0), pl.program_id(1)
            send, recv = refs[5 + n_c + n_co:]
            last, finish = _comm_run(comm, refs[3:3 + n_c], refs[5 + n_c:5 + n_c + n_co], send, recv,
                                     jnp.logical_and(i == 0, j == 0),
                                     jnp.logical_and(i == grid[0] - 1, j == grid[1] - 1))
        wg, wu = wg_ref[...], wu_ref[...]
        for r0 in range(0, tm, tm // UP_SPLIT):
            rows = slice(r0, r0 + tm // UP_SPLIT)
            hv = h_ref[rows, :]
            gate = _dot(hv, wg, "nn")
            up = _dot(hv, wu, "nn")
            sg = _sigmoid(gate)
            silu = gate * sg
            gu_ref[0, rows, :] = (sg * (1.0 + gate * (1.0 - sg)) * up).astype(BF16)
            gu_ref[1, rows, :] = silu.astype(BF16)
            z_ref[rows, :] = (silu * up).astype(BF16)
        if comm is not None:
            pl.when(last)(finish)

    res = pl.pallas_call(
        body, name=name, grid=grid,
        in_specs=[pl.BlockSpec((tm, d), lambda i, j: (i, 0)),
                  pl.BlockSpec((None, d, tn), lambda i, j: (j // per, 0, j % per)),
                  pl.BlockSpec((None, d, tn), lambda i, j: (2 + j // per, 0, j % per))] + c_in_specs,
        out_specs=[pl.BlockSpec((2, tm, tn), lambda i, j: (0, i, j)),
                   pl.BlockSpec((tm, tn), lambda i, j: (i, j))] + c_out_specs,
        out_shape=[jax.ShapeDtypeStruct((2, t, f), BF16), jax.ShapeDtypeStruct((t, f), BF16)] + c_shapes,
        input_output_aliases={3 + k: 2 + k for k in range(n_c)} if in_place else {},
        scratch_shapes=c_sems,
        compiler_params=_params(("parallel", "parallel") if comm is None else ("arbitrary", "arbitrary")),
    )(h, w_in, w_in, *c_in)
    return (res[0], res[1]) if comm is None else (res[0], res[1], res[2:])


def _ffn_dact(name, dxb, w_out, gu):
    t, d = dxb.shape
    f = w_out.shape[0]
    tm, tn = _tile(t, 2048), _tile(f, 512)

    def epilogue(acc, ex, outs, ids):
        dz = 0.5 * acc
        outs[0][0] = (dz * ex[0][0].astype(F32)).astype(BF16)
        outs[0][1] = (dz * ex[0][1].astype(F32)).astype(BF16)

    blk = pl.BlockSpec((2, tm, tn), lambda i, j, k: (0, i, j))
    return _matmul(
        name, "nt", (t // tm, f // tn, 1),
        dxb, pl.BlockSpec((tm, d), lambda i, j, k: (i, 0)),
        w_out, pl.BlockSpec((tn, d), lambda i, j, k: (j, 0)),
        [(gu, blk)], [jax.ShapeDtypeStruct((2, t, f), BF16)], [blk], None, epilogue)[0]


def _grad_colsharded(name, ht, da, comm=None):
    d, t = ht.shape
    w = da.shape[2]
    ws = w // 2
    tm, tn, tk = _tile(d, 1024), _tile(ws, 1408), _tile(t, 2048)
    per = ws // tn
    res = _matmul(
        name, "nn", (d // tm, (2 * w) // tn, t // tk),
        ht, pl.BlockSpec((tm, tk), lambda i, j, k: (i, k)),
        da, pl.BlockSpec((None, tk, tn), lambda i, j, k: (j // (2 * per), k, j % (2 * per))),
        [], [jax.ShapeDtypeStruct((N_CHIPS, d, ws), BF16)],
        [pl.BlockSpec((None, tm, tn), lambda i, j, k: (j // per, i, j % per))],
        (tm, tn), _store(1.0, BF16), comm=comm)
    return res[0] if comm is None else (res[0], res[1:])


def _back_colsharded(name, da, w_g, comm=None):
    _, t, w = da.shape
    d, ws = w_g.shape[1], w_g.shape[2]
    tm, tn, tk = _tile(t, 1024), _tile(d, 1024), _tile(ws, 2816)
    per = ws // tk
    res = _matmul(
        name, "nt", (t // tm, d // tn, (2 * w) // tk),
        da, pl.BlockSpec((None, tm, tk), lambda i, j, k: (k // (2 * per), i, k % (2 * per))),
        w_g, pl.BlockSpec((None, tn, tk), lambda i, j, k: (k // per, j, k % per)),
        [], [jax.ShapeDtypeStruct((t, d), F32)], [pl.BlockSpec((tm, tn), lambda i, j, k: (i, j))],
        (tm, tn), _store(1.0, F32), comm=comm)
    return res[0] if comm is None else (res[0], res[1:])


def _ffn_fwd(tag, x, g, w_in, w_out, prefetch):
    h, ht = _rms_fwd(tag + "_norm", x, g, with_transpose=True)
    if prefetch:
        gu, z, prefetch = _ffn_up(tag + "_up", h, w_in, comm=_gather_phase(1, prefetch))
        y, prefetch = _mm_residual(tag + "_down", z, w_out, x, 0.5, tk_pref=2816, comm=_gather_phase(2, prefetch))
    else:
        gu, z = _ffn_up(tag + "_up", h, w_in)
        y = _mm_residual(tag + "_down", z, w_out, x, 0.5, tk_pref=2816)
    return y, (x, ht, gu, z), prefetch


def _ffn_bwd(tag, saved, g, w_in, w_out, dx, dxb, red):
    x, ht, gu, z = saved
    f = z.shape[1]
    d_w_out = red.behind(0, _mm_tn, tag + "_dwout", z, dxb, scale=0.5, tm_pref=1408, tn_pref=2048)
    da = _ffn_dact(tag + "_dact", dxb, w_out, gu)
    d_w_in = red.behind(1, _grad_colsharded, tag + "_dwin", ht, da)
    dh = red.behind(2, _back_colsharded, tag + "_dh", da, w_in)
    dx, dxb, dg = _rms_bwd(tag + "_dnorm", dh, x, g, dx)
    return dx, dxb, dg, d_w_in, d_w_out.reshape(N_CHIPS, f // N_CHIPS, -1)


_GELU_K = math.sqrt(2.0 / math.pi)
_GELU_C = 0.044715


def _gelu(x):
    t = jnp.tanh(_GELU_K * (x + _GELU_C * x * x * x))
    return 0.5 * x * (1.0 + t), t


def _dgelu(x, t):
    return 0.5 * (1.0 + t) + 0.5 * x * (1.0 - t * t) * (_GELU_K * (1.0 + 3.0 * _GELU_C * x * x))


def _causal_block_mask():
    r = lax.broadcasted_iota(jnp.int32, (SGU_BLOCK, SGU_BLOCK), 0) // CHUNK
    c = lax.broadcasted_iota(jnp.int32, (SGU_BLOCK, SGU_BLOCK), 1) // CHUNK
    return r >= c


def _sgu_pre(name, h, w_in, comm=None):
    t, d = h.shape
    ws = w_in.shape[2]
    w = 2 * ws
    tm, tn = _tile(t, 2048), _tile(ws, 512)
    per = ws // tn
    res = _matmul(
        name, "nn", (t // tm, (2 * w) // tn, 1),
        h, pl.BlockSpec((tm, d), lambda i, j, k: (i, 0)),
        w_in, pl.BlockSpec((None, d, tn), lambda i, j, k: (j // per, 0, j % per)),
        [], [jax.ShapeDtypeStruct((2, t, w), BF16)],
        [pl.BlockSpec((None, tm, tn), lambda i, j, k: (j // (2 * per), i, j % (2 * per)))],
        None, _store(1.0, BF16), comm=comm)
    return res[0] if comm is None else (res[0], res[1:])


def _layernorm_stats(v):
    mu = jnp.mean(v, axis=-1, keepdims=True)
    vc = v - mu
    rstd = lax.rsqrt(jnp.mean(vc * vc, axis=-1, keepdims=True) + EPS)
    return vc * rstd, rstd


def _sgu_mid_fwd(pre, gain, bias, w_sp, b_sp_t):
    _, t, w = pre.shape
    gd = w // SGU_GROUPS

    def body(pre_ref, gain_ref, bias_ref, ws_ref, bt_ref, out_ref):
        mask = _causal_block_mask()
        u, _ = _gelu(pre_ref[0].astype(F32))
        v, _ = _gelu(pre_ref[1].astype(F32))
        vhat, _ = _layernorm_stats(v)
        vln = (vhat * gain_ref[...] + bias_ref[...]).astype(BF16)
        for gi in range(SGU_GROUPS):
            cols = slice(gi * gd, (gi + 1) * gd)
            wg = jnp.where(mask, ws_ref[gi], 0.0).astype(BF16)
            mixed = _dot(wg, vln[:, cols], "nn") + bt_ref[:, gi:gi + 1]
            out_ref[:, cols] = (u[:, cols] * mixed).astype(BF16)

    return pl.pallas_call(
        body, name="sgu_mid_fwd", grid=(t // SGU_BLOCK,),
        in_specs=[pl.BlockSpec((2, SGU_BLOCK, w), lambda n: (0, n, 0)),
                  pl.BlockSpec((1, w), lambda n: (0, 0)), pl.BlockSpec((1, w), lambda n: (0, 0)),
                  pl.BlockSpec((SGU_GROUPS, SGU_BLOCK, SGU_BLOCK), lambda n: (0, 0, 0)),
                  pl.BlockSpec((SGU_BLOCK, SGU_GROUPS), lambda n: (0, 0))],
        out_specs=pl.BlockSpec((SGU_BLOCK, w), lambda n: (n, 0)),
        out_shape=jax.ShapeDtypeStruct((t, w), BF16),
        compiler_params=_params(("parallel",)),
    )(pre, gain, bias, w_sp, b_sp_t)


def _sgu_mid_bwd(pre, dgated, gain, bias, w_sp, b_sp_t):
    _, t, w = pre.shape
    gd = w // SGU_GROUPS

    def body(pre_ref, dg_ref, gain_ref, bias_ref, ws_ref, bt_ref,
             dpre_ref, dgain_ref, dbias_ref, dws_ref, dbt_ref, dvln_s):
        @pl.when(pl.program_id(0) == 0)
        def _():
            dgain_ref[...] = jnp.zeros_like(dgain_ref)
            dbias_ref[...] = jnp.zeros_like(dbias_ref)
            dws_ref[...] = jnp.zeros_like(dws_ref)
            dbt_ref[...] = jnp.zeros_like(dbt_ref)

        mask = _causal_block_mask()
        pu = pre_ref[0].astype(F32)
        pv = pre_ref[1].astype(F32)
        u, tu = _gelu(pu)
        v, tv = _gelu(pv)
        vhat, rstd = _layernorm_stats(v)
        gain_v = gain_ref[...]
        vln = (vhat * gain_v + bias_ref[...]).astype(BF16)
        dgt = dg_ref[...].astype(F32)
        for gi in range(SGU_GROUPS):
            cols = slice(gi * gd, (gi + 1) * gd)
            wg = jnp.where(mask, ws_ref[gi], 0.0).astype(BF16)
            vg = vln[:, cols]
            mixed = _dot(wg, vg, "nn") + bt_ref[:, gi:gi + 1]
            dgg = dgt[:, cols]
            dmixed = dgg * u[:, cols]
            dmb = dmixed.astype(BF16)
            dpre_ref[0, :, cols] = (dgg * mixed * _dgelu(pu[:, cols], tu[:, cols])).astype(BF16)
            dbt_ref[:, gi:gi + 1] += jnp.sum(dmixed, axis=1, keepdims=True)
            dws_ref[gi] += jnp.where(mask, _dot(dmb, vg, "nt"), 0.0)
            dvln_s[:, cols] = _dot(wg, dmb, "tn")
        dvln = dvln_s[...]
        dgain_ref[...] += jnp.sum(dvln * vhat, axis=0, keepdims=True)
        dbias_ref[...] += jnp.sum(dvln, axis=0, keepdims=True)
        dvh = dvln * gain_v
        dv = rstd * (dvh - jnp.mean(dvh, axis=-1, keepdims=True)
                     - vhat * jnp.mean(dvh * vhat, axis=-1, keepdims=True))
        dpre_ref[1] = (dv * _dgelu(pv, tv)).astype(BF16)

    vec = pl.BlockSpec((1, w), lambda n: (0, 0))
    wsb = pl.BlockSpec((SGU_GROUPS, SGU_BLOCK, SGU_BLOCK), lambda n: (0, 0, 0))
    btb = pl.BlockSpec((SGU_BLOCK, SGU_GROUPS), lambda n: (0, 0))
    blk2 = pl.BlockSpec((2, SGU_BLOCK, w), lambda n: (0, n, 0))
    return pl.pallas_call(
        body, name="sgu_mid_bwd", grid=(t // SGU_BLOCK,),
        in_specs=[blk2, pl.BlockSpec((SGU_BLOCK, w), lambda n: (n, 0)), vec, vec, wsb, btb],
        out_specs=[blk2, vec, vec, wsb, btb],
        out_shape=[jax.ShapeDtypeStruct((2, t, w), BF16), jax.ShapeDtypeStruct((1, w), F32),
                   jax.ShapeDtypeStruct((1, w), F32),
                   jax.ShapeDtypeStruct((SGU_GROUPS, SGU_BLOCK, SGU_BLOCK), F32),
                   jax.ShapeDtypeStruct((SGU_BLOCK, SGU_GROUPS), F32)],
        scratch_shapes=[pltpu.VMEM((SGU_BLOCK, w), F32)],
        compiler_params=_params(("arbitrary",)),
    )(pre, dgated, gain, bias, w_sp, b_sp_t)


def _sgu_fwd(x, g, w_in, gain, bias, w_sp, b_sp, w_out, prefetch):
    h, ht = _rms_fwd("sgu_norm", x, g, with_transpose=True)
    pre, prefetch = _sgu_pre("sgu_pre", h, w_in, comm=_gather_phase(1, prefetch))
    gated = _sgu_mid_fwd(pre, gain, bias, w_sp, b_sp.T)
    y, prefetch = _mm_residual("sgu_out", gated, w_out, x, 1.0, tk_pref=1024, comm=_gather_phase(2, prefetch))
    return y, (x, ht, pre, gated), prefetch


def _sgu_bwd(saved, g, w_in, gain, bias, w_sp, b_sp, w_out, dx, dxb, red):
    x, ht, pre, gated = saved
    w = gated.shape[1]
    d_w_out = red.behind(0, _mm_tn, "sgu_dwout", gated, dxb)
    dgated = _mm_nt_full("sgu_dgated", dxb, w_out, BF16)
    dpre, dgain, dbias, dws, dbt = _sgu_mid_bwd(pre, dgated, gain, bias, w_sp, b_sp.T)
    d_w_in = red.behind(1, _grad_colsharded, "sgu_dwin", ht, dpre)
    dh = red.behind(2, _back_colsharded, "sgu_dh", dpre, w_in)
    dx, dxb, dg = _rms_bwd("sgu_dnorm", dh, x, g, dx)
    small = dict(ln=dg, gain=dgain, bias=dbias, w_sp=dws, b_sp=dbt.T)
    return dx, dxb, small, d_w_in, d_w_out.reshape(N_CHIPS, w // N_CHIPS, -1)


def _rope_tables(positions):
    half = QK_ROPE // 2
    inv_freq = 1.0 / (ROPE_THETA ** (jnp.arange(half, dtype=F32) / half))
    ang = positions.astype(F32)[:, None] * inv_freq
    cos, sin = jnp.cos(ang), jnp.sin(ang)
    t = positions.shape[0]
    zeros = jnp.zeros((t, half), F32)
    rest = jnp.zeros((t, LANE - QK_ROPE), F32)
    c = jnp.concatenate([cos, cos, rest + 1.0], axis=1)
    s_up = jnp.concatenate([zeros, sin, rest], axis=1)
    s_dn = jnp.concatenate([-sin, zeros, rest], axis=1)
    return c, s_up, s_dn


def _rope_apply(x, c, s_up, s_dn):
    half = QK_ROPE // 2
    return x * c + pltpu.roll(x, half, 1) * s_up + pltpu.roll(x, LANE - half, 1) * s_dn


def _rope_apply_t(dy, c, s_up, s_dn):
    half = QK_ROPE // 2
    return dy * c - pltpu.roll(dy, LANE - half, 1) * s_dn - pltpu.roll(dy, half, 1) * s_up


def _mla_norm_fwd(proj, gq, gkv):
    t, p = proj.shape
    ql, kvl = gq.shape[1], gkv.shape[1]
    tm = _tile(t, 512)

    def body(p_ref, gq_ref, gkv_ref, qn_ref, kvn_ref):
        for lo, n, g_ref, o_ref in ((0, ql, gq_ref, qn_ref), (ql, kvl, gkv_ref, kvn_ref)):
            xv = p_ref[:, lo:lo + n]
            r = lax.rsqrt(jnp.mean(xv * xv, axis=-1, keepdims=True) + EPS)
            o_ref[...] = (xv * r * g_ref[...]).astype(BF16)

    return pl.pallas_call(
        body, name="mla_norm_fwd", grid=(t // tm,),
        in_specs=[pl.BlockSpec((tm, p), lambda i: (i, 0)), pl.BlockSpec((1, ql), lambda i: (0, 0)),
                  pl.BlockSpec((1, kvl), lambda i: (0, 0))],
        out_specs=[pl.BlockSpec((tm, ql), lambda i: (i, 0)), pl.BlockSpec((tm, kvl), lambda i: (i, 0))],
        out_shape=[jax.ShapeDtypeStruct((t, ql), BF16), jax.ShapeDtypeStruct((t, kvl), BF16)],
        compiler_params=_params(("parallel",)),
    )(proj, gq, gkv)


def _mla_norm_bwd(proj, dqn, dkvn, dkr, gq, gkv):
    t, p = proj.shape
    ql, kvl = gq.shape[1], gkv.shape[1]
    tm = _tile(t, 256)

    def body(p_ref, dqn_ref, dkvn_ref, dkr_ref, gq_ref, gkv_ref, dp_ref, dgq_ref, dgkv_ref):
        @pl.when(pl.program_id(0) == 0)
        def _():
            dgq_ref[...] = jnp.zeros_like(dgq_ref)
            dgkv_ref[...] = jnp.zeros_like(dgkv_ref)

        for lo, n, g_ref, d_ref, dg_ref in ((0, ql, gq_ref, dqn_ref, dgq_ref),
                                             (ql, kvl, gkv_ref, dkvn_ref, dgkv_ref)):
            dx, dgt = _rms_bwd_math(d_ref[...], p_ref[:, lo:lo + n], g_ref[...])
            dp_ref[:, lo:lo + n] = dx.astype(BF16)
            dg_ref[...] += jnp.sum(dgt, axis=0, keepdims=True)
        dp_ref[:, ql + kvl:] = dkr_ref[...].astype(BF16)

    def row(n):
        return pl.BlockSpec((tm, n), lambda i: (i, 0))

    def vec(n):
        return pl.BlockSpec((1, n), lambda i: (0, 0))

    return pl.pallas_call(
        body, name="mla_norm_bwd", grid=(t // tm,),
        in_specs=[row(p), row(ql), row(kvl), row(LANE), vec(ql), vec(kvl)],
        out_specs=[row(p), vec(ql), vec(kvl)],
        out_shape=[jax.ShapeDtypeStruct((t, p), BF16), jax.ShapeDtypeStruct((1, ql), F32),
                   jax.ShapeDtypeStruct((1, kvl), F32)],
        compiler_params=_params(("arbitrary",)),
    )(proj, dqn, dkvn, dkr, gq, gkv)


def _mla_q_up(qn, wq, tables):
    t, ql = qn.shape
    n = wq.shape[1]
    tm = _tile(t, 2048)
    scale = QK_DIM ** -0.5 * LOG2E

    def epilogue(acc, ex, outs, ids):
        outs[0][:, :QK_NOPE] = (scale * acc[:, :QK_NOPE]).astype(BF16)
        hi = _rope_apply(acc[:, QK_NOPE:], ex[0][...], ex[1][...], ex[2][...])
        outs[0][:, QK_NOPE:] = (scale * hi).astype(BF16)

    tab = pl.BlockSpec((tm, LANE), lambda i, j, k: (i, 0))
    return _matmul(
        "mla_q_up", "nn", (t // tm, n // HEAD_PAD, 1),
        qn, pl.BlockSpec((tm, ql), lambda i, j, k: (i, 0)),
        wq, pl.BlockSpec((ql, HEAD_PAD), lambda i, j, k: (0, j)),
        [(tb, tab) for tb in tables],
        [jax.ShapeDtypeStruct((t, n), BF16)], [pl.BlockSpec((tm, HEAD_PAD), lambda i, j, k: (i, j))],
        None, epilogue)[0]


def _mla_kv_up(kvn, wkv, proj, tables, heads):
    t, kvl = kvn.shape
    n = wkv.shape[1]
    p = proj.shape[1]
    tm = _tile(t, 2048)

    def epilogue(acc, ex, outs, ids):
        kr = _rope_apply(ex[0][...], ex[1][...], ex[2][...], ex[3][...])
        outs[0][:, :QK_NOPE] = acc[:, :QK_NOPE].astype(BF16)
        outs[0][:, QK_NOPE:] = (acc[:, QK_NOPE:] + jnp.where(ids[1] < heads, kr, 1.0)).astype(BF16)

    tab = pl.BlockSpec((tm, LANE), lambda i, j, k: (i, 0))
    kr_spec = pl.BlockSpec((tm, LANE), lambda i, j, k: (i, p // LANE - 1))
    return _matmul(
        "mla_kv_up", "nn", (t // tm, n // HEAD_PAD, 1),
        kvn, pl.BlockSpec((tm, kvl), lambda i, j, k: (i, 0)),
        wkv, pl.BlockSpec((kvl, HEAD_PAD), lambda i, j, k: (0, j)),
        [(proj, kr_spec)] + [(tb, tab) for tb in tables],
        [jax.ShapeDtypeStruct((t, n), BF16)], [pl.BlockSpec((tm, HEAD_PAD), lambda i, j, k: (i, j))],
        None, epilogue)[0]


FLASH_TQ = 1024
FLASH_TK = 1024
FLASH_SPLIT = 4
FLASH_SPLIT_BWD = 2


def _chunk_mask(tq, tk, qi, ki):
    r = (qi * tq + lax.broadcasted_iota(jnp.int32, (tq, tk), 0)) // CHUNK
    c = (ki * tk + lax.broadcasted_iota(jnp.int32, (tq, tk), 1)) // CHUNK
    return c <= r


def _block_pairs(t, tq, tk, key_major):
    def visible(qi, ki):
        return (ki * tk) // CHUNK <= (qi * tq + tq - 1) // CHUNK

    def masked(qi, ki):
        return (ki * tk + tk - 1) // CHUNK > (qi * tq) // CHUNK

    nq, nk = t // tq, t // tk
    if key_major:
        sweeps = [[(qi, ki) for qi in range(nq) if visible(qi, ki)] for ki in range(nk)]
    else:
        sweeps = [[(qi, ki) for ki in range(nk) if visible(qi, ki)] for qi in range(nq)]
    qs, ks, fs = [], [], []
    for sweep in sweeps:
        for n, (qi, ki) in enumerate(sweep):
            qs.append(qi)
            ks.append(ki)
            fs.append((1 if masked(qi, ki) else 0) + (2 if n == 0 else 0) + (4 if n == len(sweep) - 1 else 0))
    return tuple(jnp.asarray(v, jnp.int32) for v in (qs, ks, fs))


def _flash_fwd(qp, kv, heads):
    t = qp.shape[0]
    tq, tk = _tile(t, FLASH_TQ), _tile(t, FLASH_TK)
    qt, kt, ft = _block_pairs(t, tq, tk, key_major=False)

    def body(qt_ref, kt_ref, ft_ref, q_ref, k_ref, v_ref, o_ref, lse_ref, m_s, acc_s):
        p = pl.program_id(1)
        qi, ki, flags = qt_ref[p], kt_ref[p], ft_ref[p]

        @pl.when(flags & 2 != 0)
        def _():
            m_s[...] = jnp.full_like(m_s, -1e30)
            acc_s[...] = jnp.zeros_like(acc_s)

        def step(masked):
            k, v = k_ref[...], v_ref[...]
            mask = _chunk_mask(tq, tk, qi, ki) if masked else None
            tg = tq // FLASH_SPLIT
            for g in range(FLASH_SPLIT):
                rows = slice(g * tg, (g + 1) * tg)
                nc = min(tk, (g + 1) * tg) if masked and tq == tk else tk
                s = _dot(q_ref[rows, :], k[:nc], "nt")
                if masked:
                    s = jnp.where(mask[rows, :nc], s, -1e30)
                m_prev = m_s[rows, :]
                m_new = jnp.maximum(m_prev, jnp.max(s, axis=1, keepdims=True))
                alpha = jnp.exp2(m_prev - m_new)
                pr = jnp.exp2(s - jnp.tile(m_new, (1, nc // LANE))).astype(BF16)
                pv = _dot(pr, v[:nc], "nn")
                acc_s[rows, :V_DIM] = alpha * acc_s[rows, :V_DIM] + pv[:, :V_DIM]
                acc_s[rows, V_DIM:] = alpha * acc_s[rows, V_DIM:] + pv[:, V_DIM:]
                m_s[rows, :] = m_new

        @pl.when(flags & 1 == 0)
        def _():
            step(False)

        @pl.when(flags & 1 != 0)
        def _():
            step(True)

        @pl.when(flags & 4 != 0)
        def _():
            l = acc_s[:, V_DIM:]
            o_ref[...] = (acc_s[:, :V_DIM] / l).astype(BF16)
            lse_ref[...] = m_s[...] + jnp.log(l) * LOG2E

    return pl.pallas_call(
        body, name="mla_flash_fwd",
        grid_spec=pltpu.PrefetchScalarGridSpec(
            num_scalar_prefetch=3, grid=(heads, int(qt.shape[0])),
            in_specs=[pl.BlockSpec((tq, HEAD_PAD), lambda h, p, qt, kt, ft: (qt[p], h)),
                      pl.BlockSpec((tk, HEAD_PAD), lambda h, p, qt, kt, ft: (kt[p], h)),
                      pl.BlockSpec((tk, HEAD_PAD), lambda h, p, qt, kt, ft: (kt[p], heads + h))],
            out_specs=[pl.BlockSpec((tq, V_DIM), lambda h, p, qt, kt, ft: (qt[p], h)),
                       pl.BlockSpec((None, tq, LANE), lambda h, p, qt, kt, ft: (h, qt[p], 0))],
            scratch_shapes=[pltpu.VMEM((tq, LANE), F32), pltpu.VMEM((tq, HEAD_PAD), F32)]),
        out_shape=[jax.ShapeDtypeStruct((t, heads * V_DIM), BF16),
                   jax.ShapeDtypeStruct((heads, t, LANE), F32)],
        compiler_params=_params(("parallel", "arbitrary")),
    )(qt, kt, ft, qp, kv, kv)


def _flash_delta(o, do, heads):
    t = o.shape[0]
    tm = _tile(t, 256)

    def body(o_ref, do_ref, d_ref):
        for h in range(heads):
            cols = slice(h * V_DIM, (h + 1) * V_DIM)
            prod = o_ref[:, cols].astype(F32) * do_ref[:, cols].astype(F32)
            d_ref[h] = jnp.broadcast_to(jnp.sum(prod, axis=1, keepdims=True), (tm, LANE))

    row = pl.BlockSpec((tm, heads * V_DIM), lambda i: (i, 0))
    return pl.pallas_call(
        body, name="mla_flash_delta", grid=(t // tm,), in_specs=[row, row],
        out_specs=pl.BlockSpec((heads, tm, LANE), lambda i: (0, i, 0)),
        out_shape=jax.ShapeDtypeStruct((heads, t, LANE), F32),
        compiler_params=_params(("parallel",)),
    )(o, do)


def _flash_bwd(qp, kv, do, lse, delta, heads):
    t = qp.shape[0]
    tq, tk = _tile(t, FLASH_TQ), _tile(t, FLASH_TK)
    qt, kt, ft = _block_pairs(t, tq, tk, key_major=True)

    def body(qt_ref, kt_ref, ft_ref, q_ref, k_ref, v_ref, do_ref, lse_ref, dl_ref, dq_ref, dk_ref, dv_ref,
             dk_s, dv_s):
        p = pl.program_id(1)
        qi, ki, flags = qt_ref[p], kt_ref[p], ft_ref[p]

        @pl.when(p == 0)
        def _():
            dq_ref[...] = jnp.zeros_like(dq_ref)

        @pl.when(flags & 2 != 0)
        def _():
            dk_s[...] = jnp.zeros_like(dk_s)
            dv_s[...] = jnp.zeros_like(dv_s)

        def step(masked):
            k, v = k_ref[...], v_ref[...]
            mask = _chunk_mask(tq, tk, qi, ki) if masked else None
            tg = tq // FLASH_SPLIT_BWD
            for g in range(FLASH_SPLIT_BWD):
                rows = slice(g * tg, (g + 1) * tg)
                nc = min(tk, (g + 1) * tg) if masked and tq == tk else tk
                q = q_ref[rows, :]
                dov = do_ref[rows, :]
                s = _dot(q, k[:nc], "nt")
                pr = jnp.exp2(s - jnp.tile(lse_ref[rows, :], (1, nc // LANE)))
                if masked:
                    pr = jnp.where(mask[rows, :nc], pr, 0.0)
                dv_s[:nc, :] += _dot(pr.astype(BF16), dov, "tn")
                dp = _dot(dov, v[:nc], "nt")
                ds = (pr * (dp - jnp.tile(dl_ref[rows, :], (1, nc // LANE)))).astype(BF16)
                dq_rows = pl.ds(pl.multiple_of(qi * tq + g * tg, tg), tg)
                dq_ref[dq_rows, :] += _dot(ds, k[:nc], "nn")
                dk_s[:nc, :] += _dot(ds, q, "tn")

        @pl.when(flags & 1 == 0)
        def _():
            step(False)

        @pl.when(flags & 1 != 0)
        def _():
            step(True)

        @pl.when(flags & 4 != 0)
        def _():
            dk_ref[...] = dk_s[...]
            dv_ref[...] = dv_s[...]

    def qrow(width):
        return pl.BlockSpec((tq, width), lambda h, p, qt, kt, ft: (qt[p], h))

    def stat():
        return pl.BlockSpec((None, tq, LANE), lambda h, p, qt, kt, ft: (h, qt[p], 0))

    return pl.pallas_call(
        body, name="mla_flash_bwd",
        grid_spec=pltpu.PrefetchScalarGridSpec(
            num_scalar_prefetch=3, grid=(heads, int(qt.shape[0])),
            in_specs=[qrow(HEAD_PAD),
                      pl.BlockSpec((tk, HEAD_PAD), lambda h, p, qt, kt, ft: (kt[p], h)),
                      pl.BlockSpec((tk, V_DIM), lambda h, p, qt, kt, ft: (kt[p], 2 * (heads + h))),
                      qrow(V_DIM), stat(), stat()],
            out_specs=[pl.BlockSpec((t, HEAD_PAD), lambda h, p, qt, kt, ft: (0, h)),
                       pl.BlockSpec((tk, HEAD_PAD), lambda h, p, qt, kt, ft: (kt[p], h)),
                       pl.BlockSpec((tk, V_DIM), lambda h, p, qt, kt, ft: (kt[p], h))],
            scratch_shapes=[pltpu.VMEM((tk, HEAD_PAD), F32), pltpu.VMEM((tk, V_DIM), F32)]),
        out_shape=[jax.ShapeDtypeStruct((t, heads * HEAD_PAD), F32),
                   jax.ShapeDtypeStruct((t, heads * HEAD_PAD), F32),
                   jax.ShapeDtypeStruct((t, heads * V_DIM), F32)],
        compiler_params=_params(("parallel", "arbitrary")),
    )(qt, kt, ft, qp, kv, kv, do, lse, delta)


def _mla_attn_post(dqp, dkp, dv, tables, heads):
    t = dqp.shape[0]
    tm = _tile(t, 256)
    scale = QK_DIM ** -0.5
    kw, vw = heads * HEAD_PAD, heads * V_DIM

    def body(dq_ref, dk_ref, dv_ref, c_ref, su_ref, sd_ref, dqb_ref, dkvb_ref, dkr_ref):
        c, su, sd = c_ref[...], su_ref[...], sd_ref[...]
        kr = jnp.zeros((tm, LANE), F32)
        for h in range(heads):
            lo = h * HEAD_PAD
            mid = lo + QK_NOPE
            dqb_ref[:, lo:mid] = (scale * dq_ref[:, lo:mid]).astype(BF16)
            dqb_ref[:, mid:mid + LANE] = (scale * _rope_apply_t(dq_ref[:, mid:mid + LANE], c, su, sd)).astype(BF16)
            kr = kr + dk_ref[:, mid:mid + LANE]
            dkvb_ref[:, kw + lo:kw + mid] = dv_ref[:, h * V_DIM:(h + 1) * V_DIM].astype(BF16)
            dkvb_ref[:, kw + mid:kw + lo + HEAD_PAD] = jnp.zeros((tm, HEAD_PAD - V_DIM), BF16)
        dkvb_ref[:, :kw] = (dk_ref[...] * (1.0 / LOG2E)).astype(BF16)
        dkr_ref[...] = _rope_apply_t(kr * (1.0 / LOG2E), c, su, sd)

    def row(n):
        return pl.BlockSpec((tm, n), lambda i: (i, 0))

    return pl.pallas_call(
        body, name="mla_attn_post", grid=(t // tm,),
        in_specs=[row(kw), row(kw), row(vw), row(LANE), row(LANE), row(LANE)],
        out_specs=[row(kw), row(2 * kw), row(LANE)],
        out_shape=[jax.ShapeDtypeStruct((t, kw), BF16), jax.ShapeDtypeStruct((t, 2 * kw), BF16),
                   jax.ShapeDtypeStruct((t, LANE), F32)],
        compiler_params=_params(("parallel",)),
    )(dqp, dkp, dv, *tables)


def _mla_weights(w_in_g, w_q_g, w_kv_g, w_out_g):
    d = w_in_g.shape[0] * w_in_g.shape[1]
    pw = w_in_g.shape[2]
    w_in = jnp.pad(w_in_g.reshape(d, pw), ((0, 0), (0, LANE - QK_ROPE)))
    ql = w_q_g.shape[1]
    wq = jnp.transpose(w_q_g, (1, 0, 2)).reshape(ql, -1, QK_DIM)
    heads = wq.shape[1]
    wq = jnp.pad(wq, ((0, 0), (0, 0), (0, HEAD_PAD - QK_DIM))).reshape(ql, heads * HEAD_PAD)
    kvl = w_kv_g.shape[1]
    wkv = jnp.transpose(w_kv_g, (1, 0, 2)).reshape(kvl, heads, QK_NOPE + V_DIM)
    wk = jnp.pad(wkv[:, :, :QK_NOPE], ((0, 0), (0, 0), (0, HEAD_PAD - QK_NOPE))).reshape(kvl, heads * HEAD_PAD)
    wv = jnp.pad(wkv[:, :, QK_NOPE:], ((0, 0), (0, 0), (0, HEAD_PAD - V_DIM))).reshape(kvl, heads * HEAD_PAD)
    return w_in, wq, jnp.concatenate([wk, wv], axis=1), w_out_g.reshape(heads * V_DIM, -1), heads


def _mla_unpermute(d_w_in, d_wq, d_wkv, heads, pw):
    d = d_w_in.shape[0]
    g_in = d_w_in[:, :pw].reshape(N_CHIPS, d // N_CHIPS, pw)
    ql = d_wq.shape[0]
    g_q = d_wq.reshape(ql, heads, HEAD_PAD)[:, :, :QK_DIM].reshape(ql, N_CHIPS, -1)
    kvl = d_wkv.shape[0]
    g_k = d_wkv[:, :heads * HEAD_PAD].reshape(kvl, heads, HEAD_PAD)[:, :, :QK_NOPE]
    g_v = d_wkv[:, heads * HEAD_PAD:].reshape(kvl, heads, HEAD_PAD)[:, :, :V_DIM]
    g_kv = jnp.concatenate([g_k, g_v], axis=2).reshape(kvl, N_CHIPS, -1)
    return g_in, jnp.transpose(g_q, (1, 0, 2)), jnp.transpose(g_kv, (1, 0, 2))


def _mla_fwd(x, g, wts, gq, gkv, tables):
    w_in, wq, wkv, w_out, heads = wts
    h = _rms_fwd("mla_norm", x, g)
    proj = _mm_nn_full("mla_proj", h, w_in, F32, tn_pref=w_in.shape[1])
    qn, kvn = _mla_norm_fwd(proj, gq, gkv)
    qp = _mla_q_up(qn, wq, tables)
    kv = _mla_kv_up(kvn, wkv, proj, tables, heads)
    o, lse = _flash_fwd(qp, kv, heads)
    y = _mm_residual("mla_out", o, w_out, x, 1.0, tk_pref=2048)
    return y, (x, h, proj, qn, kvn, qp, kv, o, lse)


def _mla_bwd(saved, g, wts, gq, gkv, tables, dx, dxb, pw):
    w_in, wq, wkv, w_out, heads = wts
    x, h, proj, qn, kvn, qp, kv, o, lse = saved
    d_w_out = _mm_tn("mla_dwout", o, dxb)
    do = _mm_nt_full("mla_do", dxb, w_out, BF16)
    dqp, dkp, dv = _flash_bwd(qp, kv, do, lse, _flash_delta(o, do, heads), heads)
    dqb, dkvb, dkr = _mla_attn_post(dqp, dkp, dv, tables, heads)
    d_wq = _mm_tn("mla_dwq", qn, dqb)
    dqn = _mm_nt_k("mla_dqn", dqb, wq, F32)
    d_wkv = _mm_tn("mla_dwkv", kvn, dkvb)
    dkvn = _mm_nt_k("mla_dkvn", dkvb, wkv, F32)
    dproj, dgq, dgkv = _mla_norm_bwd(proj, dqn, dkvn, dkr, gq, gkv)
    d_w_in = _mm_tn("mla_dwin", h, dproj, tn_pref=dproj.shape[1])
    dh = _mm_nt_full("mla_dh", dproj, w_in, F32, tn_pref=1024)
    dx, dxb, dg = _rms_bwd("mla_dnorm", dh, x, g, dx)
    g_in, g_q, g_kv = _mla_unpermute(d_w_in, d_wq, d_wkv, heads, pw)
    small = dict(ln=dg, gq=dgq, gkv=dgkv)
    return dx, dxb, small, g_in, g_q, g_kv, d_w_out.reshape(N_CHIPS, d_w_out.shape[0] // N_CHIPS, -1)


def _gather_weights(bufs, norms):
    nt = len(bufs)

    def body(*refs):
        n_in = refs[nt]
        outs, n_out = refs[nt + 1:2 * nt + 1], refs[2 * nt + 1]
        send, recv, fsend, frecv, loc, nsend, nrecv = refs[2 * nt + 2:]
        x, y, c, chips = _place()
        me = 2 * x + y
        sib = (x, y, 1 - c)

        local = pltpu.make_async_copy(n_in, n_out.at[me], loc)
        local.start()

        def place(t, chip, half):
            return outs[t].at[2 * chip[0] + chip[1], half]

        def ici(t, j, chip):
            return pltpu.make_async_remote_copy(
                src_ref=place(t, (x, y), c), dst_ref=place(t, (x, y), c), send_sem=send.at[t, j],
                recv_sem=recv.at[t, j], device_id=(*chip, c), device_id_type=MESH)

        def fwd(t, j, chip, half):
            return pltpu.make_async_remote_copy(
                src_ref=place(t, chip, half), dst_ref=place(t, chip, half), send_sem=fsend.at[t, j],
                recv_sem=frecv.at[t, j], device_id=sib, device_id_type=MESH)

        def nrm(j, chip, owner):
            return pltpu.make_async_remote_copy(
                src_ref=n_in, dst_ref=n_out.at[2 * owner[0] + owner[1]], send_sem=nsend.at[j], recv_sem=nrecv.at[j],
                device_id=(*chip, c), device_id_type=MESH)

        firsts = [ici(t, j, chip) for t in range(nt) for j, chip in enumerate(chips)]
        firsts += [nrm(j, chip, (x, y)) for j, chip in enumerate(chips)]
        for cp in firsts:
            cp.start()
        passed = []
        for t in range(nt):
            for j, chip in enumerate(chips):
                pltpu.make_async_remote_copy(
                    src_ref=place(t, chip, c), dst_ref=place(t, chip, c), send_sem=send.at[t, j],
                    recv_sem=recv.at[t, j], device_id=(*chip, c), device_id_type=MESH).wait_recv()
                cp = fwd(t, j, chip, c)
                cp.start()
                passed.append(cp)
        for t in range(nt):
            for j, chip in enumerate(chips):
                fwd(t, j, chip, 1 - c).wait_recv()
        for j, chip in enumerate(chips):
            nrm(j, chip, chip).wait_recv()
        for cp in firsts + passed:
            cp.wait_send()
        local.wait()

    out_shape = [jax.ShapeDtypeStruct(b.shape, b.dtype) for b in bufs]
    out_shape.append(jax.ShapeDtypeStruct((N_CHIPS,) + norms.shape, norms.dtype))
    res = pl.pallas_call(
        body, name="gather_weights",
        in_specs=[ANY] * (nt + 1), out_specs=[ANY] * (nt + 1), out_shape=out_shape,
        input_output_aliases={t: t for t in range(nt)},
        scratch_shapes=[pltpu.SemaphoreType.DMA((nt, 3)), pltpu.SemaphoreType.DMA((nt, 3)),
                        pltpu.SemaphoreType.DMA((nt, 3)), pltpu.SemaphoreType.DMA((nt, 3)),
                        pltpu.SemaphoreType.DMA, pltpu.SemaphoreType.DMA((3,)),
                        pltpu.SemaphoreType.DMA((3,))],
    )(*bufs, norms)
    return res[:nt], res[nt]


def _run_phase(name, phase):
    c_in, in_specs, out_specs, shapes, sems, in_place = _comm_io(phase)
    n_c, n_co = len(c_in), len(shapes)

    def body(*refs):
        ins, outs = refs[:n_c], refs[n_c:n_c + n_co]
        send, recv = refs[n_c + n_co:]
        for cp in phase.copies(ins, outs, send, recv, False):
            cp.start()
        for cp in phase.copies(ins, outs, send, recv, True):
            cp.wait_recv()
        for cp in phase.copies(ins, outs, send, recv, False):
            cp.wait_send()

    return pl.pallas_call(
        body, name=name, in_specs=in_specs, out_specs=out_specs, out_shape=shapes,
        input_output_aliases={t: t for t in range(n_c)} if in_place else {}, scratch_shapes=sems,
    )(*c_in)


def _remote(ref_src, ref_dst, send, recv, t, j, to):
    return pltpu.make_async_remote_copy(src_ref=ref_src, dst_ref=ref_dst, send_sem=send.at[t, j],
                                        recv_sem=recv.at[t, j], device_id=to, device_id_type=MESH)


class _Reduce:
    def __init__(self, units, grads, n_layers, reduced, me_idx, core_idx):
        self.units, self.n_layers, self.reduced = list(units), n_layers, reduced
        self.me_idx, self.core_idx = me_idx, core_idx
        self.local = [g.reshape(N_CHIPS, 2, g.shape[1] // 2, g.shape[2]) for g in grads]
        self.parts = None

    def _swap(self):
        fresh = [jax.ShapeDtypeStruct((g.shape[0],) + g.shape[2:], g.dtype) for g in self.local]

        def copies(ins, outs, send, recv, landing):
            x, y, c, _ = _place()
            return [_remote(outs[t] if landing else ins[t].at[:, 1 - c], outs[t], send, recv, t, 0, (x, y, 1 - c))
                    for t in range(len(ins))]

        return _Phase(self.local, fresh, (len(self.local), 1), copies)

    def _scatter(self):
        fresh = [jax.ShapeDtypeStruct((3,) + p.shape[1:], p.dtype) for p in self.parts]

        def copies(ins, outs, send, recv, landing):
            x, y, c, chips = _place()
            return [_remote(outs[t].at[j] if landing else ins[t].at[2 * chip[0] + chip[1]], outs[t].at[j],
                            send, recv, t, j, (*chip, c))
                    for t in range(len(ins)) for j, chip in enumerate(chips)]

        return _Phase(self.parts, fresh, (len(self.parts), 3), copies)

    def _join(self):
        layers = [l for _, l in self.units]

        def copies(ins, outs, send, recv, landing):
            x, y, c, _ = _place()
            refs = [outs[t].at[l, 1 - c if landing else c] for t, l in enumerate(layers)]
            return [_remote(r, r, send, recv, t, 0, (x, y, 1 - c)) for t, r in enumerate(refs)]

        return _Phase([self.reduced[n] for n, _ in self.units], [], (len(self.units), 1), copies)

    def _after(self, step, got):
        if step == 0:
            self.parts = [_add_halves(self.core_idx, g, o) for g, o in zip(self.local, got)]
        elif step == 1:
            for (n, l), p, ld in zip(self.units, self.parts, got):
                self.reduced[n] = _sum_chips(self.me_idx, self.core_idx, p, ld, l, self.n_layers[n],
                                             self.reduced.get(n))
        else:
            for (n, _), joined in zip(self.units, got):
                self.reduced[n] = joined

    def _phase(self, step):
        return (self._swap, self._scatter, self._join)[step]()

    def behind(self, step, fn, *args, **kw):
        if not self.units:
            return fn(*args, **kw)
        out, got = fn(*args, comm=self._phase(step), **kw)
        self._after(step, got)
        return out

    def alone(self):
        for step, name in enumerate(("grad_swap_halves", "grad_scatter_chips", "grad_join_halves")):
            self._after(step, _run_phase(name, self._phase(step)))


def _gather_all(block):
    m_per, n = block.shape

    def body(x_ref, out_ref, send_sems, recv_sems, local_sem):
        x, y, c, chips = _place()
        me, sibling = (x, y, c), (x, y, 1 - c)

        def rows(px, py, pc):
            return out_ref.at[pl.ds((4 * px + 2 * py + pc) * m_per, m_per), :]

        def copy(k, block_of, to, src=None):
            return pltpu.make_async_remote_copy(
                src_ref=rows(*block_of) if src is None else src, dst_ref=rows(*block_of),
                send_sem=send_sems.at[k], recv_sem=recv_sems.at[k], device_id=to, device_id_type=MESH)

        mine = pltpu.make_async_copy(x_ref, rows(*me), local_sem)
        mine.start()
        first = [copy(0, me, sibling, src=x_ref)]
        first += [copy(1 + j, me, (*chip, c), src=x_ref) for j, chip in enumerate(chips)]
        for cp in first:
            cp.start()
        passed = [copy(4 + j, (*chip, c), sibling) for j, chip in enumerate(chips)]
        for j, chip in enumerate(chips):
            copy(1 + j, (*chip, c), me).wait_recv()
            passed[j].start()
        copy(0, sibling, me).wait_recv()
        for j, chip in enumerate(chips):
            copy(4 + j, (*chip, 1 - c), me).wait_recv()
        for cp in first + passed:
            cp.wait_send()
        mine.wait()

    return pl.pallas_call(
        body, name="gather_small_grads",
        out_shape=jax.ShapeDtypeStruct((N_DEV * m_per, n), block.dtype),
        in_specs=[pl.BlockSpec(memory_space=pltpu.VMEM)],
        out_specs=pl.BlockSpec(memory_space=pltpu.VMEM),
        scratch_shapes=[pltpu.SemaphoreType.DMA((7,)), pltpu.SemaphoreType.DMA((7,)), pltpu.SemaphoreType.DMA],
    )(block)


def _row_tile(r, c, elems=512 * 1024):
    t = max(8, min(r, (elems // c) // 8 * 8))
    while t > 8 and r % t:
        t -= 8
    return t if r % t == 0 else r


def _add_halves(idx, grad, other):
    n, _, r, w = grad.shape
    tr = _row_tile(r, w)

    def body(idx_ref, g_ref, o_ref, out_ref):
        out_ref[...] = (g_ref[...].astype(F32) + o_ref[...].astype(F32)).astype(BF16)

    return pl.pallas_call(
        body, name="grad_add_halves",
        grid_spec=pltpu.PrefetchScalarGridSpec(
            num_scalar_prefetch=1, grid=(n, r // tr),
            in_specs=[pl.BlockSpec((None, None, tr, w), lambda k, i, idx: (k, idx[0], i, 0)),
                      pl.BlockSpec((None, tr, w), lambda k, i, idx: (k, i, 0))],
            out_specs=pl.BlockSpec((None, tr, w), lambda k, i, idx: (k, i, 0))),
        out_shape=jax.ShapeDtypeStruct((n, r, w), BF16),
        compiler_params=_params(("parallel", "parallel")),
    )(idx, grad, other)


def _sum_chips(me_idx, core_idx, part, landed, layer, n_layers, prev):
    _, r, w = part.shape
    tr = _row_tile(r, w)

    def body(me_ref, c_ref, p_ref, l_ref, *rest):
        acc = p_ref[...].astype(F32)
        for j in range(3):
            acc = acc + l_ref[j].astype(F32)
        rest[-1][...] = acc

    return pl.pallas_call(
        body, name="grad_sum_chips",
        grid_spec=pltpu.PrefetchScalarGridSpec(
            num_scalar_prefetch=2, grid=(r // tr,),
            in_specs=[pl.BlockSpec((None, tr, w), lambda i, me, c: (me[0], i, 0)),
                      pl.BlockSpec((3, tr, w), lambda i, me, c: (0, i, 0))] + ([] if prev is None else [ANY]),
            out_specs=pl.BlockSpec((None, None, tr, w), lambda i, me, c: (layer, c[0], i, 0))),
        out_shape=jax.ShapeDtypeStruct((n_layers, 2, r, w), F32),
        input_output_aliases={} if prev is None else {4: 0},
        compiler_params=_params(("parallel",)),
    )(me_idx, core_idx, part, landed, *([] if prev is None else [prev]))


def _adamw_math(w, g, m, v):
    m = ADAM_B1 * m + (1.0 - ADAM_B1) * g
    v = ADAM_B2 * v + (1.0 - ADAM_B2) * (g * g)
    m_hat = m / (1.0 - ADAM_B1 ** ADAM_STEP)
    v_hat = v / (1.0 - ADAM_B2 ** ADAM_STEP)
    delta = -ADAM_LR * (m_hat / (jnp.sqrt(v_hat) + ADAM_EPS) + ADAM_WD * w)
    return delta, m, v


def _adamw(name, w, g, m, v, comm=None):
    r, c = w.shape
    tr = _row_tile(r, c, 256 * 1024)
    steps = r // tr
    c_in, c_in_specs, c_out_specs, c_shapes, c_sems, in_place = _comm_io(comm)
    n_c, n_co = len(c_in), len(c_shapes)

    def body(*refs):
        w_ref, g_ref, m_ref, v_ref = refs[:4]
        d_ref, nm_ref, nv_ref = refs[4 + n_c:7 + n_c]
        if comm is not None:
            i = pl.program_id(0)
            send, recv = refs[7 + n_c + n_co:]
            last, finish = _comm_run(comm, refs[4:4 + n_c], refs[7 + n_c:7 + n_c + n_co], send, recv,
                                     i == 0, i == steps - 1)
        d_ref[...], nm_ref[...], nv_ref[...] = _adamw_math(w_ref[...], g_ref[...], m_ref[...], v_ref[...])
        if comm is not None:
            pl.when(last)(finish)

    blk = pl.BlockSpec((tr, c), lambda i: (i, 0))
    res = pl.pallas_call(
        body, name=name, grid=(steps,), in_specs=[blk] * 4 + c_in_specs, out_specs=[blk] * 3 + c_out_specs,
        out_shape=[jax.ShapeDtypeStruct((r, c), F32)] * 3 + c_shapes,
        input_output_aliases={4 + k: 3 + k for k in range(n_c)} if in_place else {},
        scratch_shapes=c_sems,
        compiler_params=_params(("parallel",) if comm is None else ("arbitrary",)),
    )(w, g, m, v, *c_in)
    return tuple(res[:3]) if comm is None else (tuple(res[:3]), res[3:])


def _adamw_summed(w, parts, m, v):
    r, c = w.shape

    def body(w_ref, p_ref, m_ref, v_ref, g_ref, d_ref, nm_ref, nv_ref):
        g = p_ref[0:r, :]
        for k in range(1, N_DEV):
            g = g + p_ref[k * r:(k + 1) * r, :]
        g_ref[...] = g
        d_ref[...], nm_ref[...], nv_ref[...] = _adamw_math(w_ref[...], g, m_ref[...], v_ref[...])

    return pl.pallas_call(
        body, name="adamw_replicated",
        out_shape=[jax.ShapeDtypeStruct((r, c), F32)] * 4,
        compiler_params=pltpu.CompilerParams(vmem_limit_bytes=VMEM_LIMIT_BYTES),
    )(w, parts, m, v)


def _pack(arrays):
    return jnp.concatenate([a.reshape(-1, LANE) for a in arrays], axis=0)


def _unpack(packed, shapes):
    out, row = [], 0
    for s in shapes:
        n = math.prod(s) // LANE
        out.append(packed[row:row + n].reshape(s))
        row += n
    return out


def _cast_into(name, idx, w, layer):
    _, rows, c = w.shape
    r = rows // 2
    tr = _row_tile(r, c)
    per = r // tr

    def body(idx_ref, w_ref, o_ref):
        o_ref[...] = w_ref[...].astype(BF16)

    return pl.pallas_call(
        body, name=name,
        grid_spec=pltpu.PrefetchScalarGridSpec(
            num_scalar_prefetch=1, grid=(2, per),
            in_specs=[pl.BlockSpec((None, tr, c), lambda h, i, idx: (layer, h * per + i, 0))],
            out_specs=pl.BlockSpec((None, None, tr, c), lambda h, i, idx: (idx[0], h, i, 0))),
        out_shape=jax.ShapeDtypeStruct((N_CHIPS, 2, r, c), BF16),
        compiler_params=_params(("parallel", "parallel")),
    )(idx, w)


BIG = ["ffn1_w_in", "ffn1_w_out", "ffn2_w_in", "ffn2_w_out", "sgu_w_in", "sgu_w_out",
       "mla_w_in", "mla_w_q_up", "mla_w_kv_up", "mla_w_out"]
STAGES = [
    [("ffn1_w_in", 0), ("ffn1_w_out", 0)],
    [("sgu_w_in", 0), ("sgu_w_out", 0)],
    [("ffn2_w_in", 0), ("ffn2_w_out", 0)],
    [("ffn1_w_in", 1), ("ffn1_w_out", 1)],
    [("mla_w_in", 0), ("mla_w_q_up", 0), ("mla_w_kv_up", 0), ("mla_w_out", 0)],
    [("ffn2_w_in", 1), ("ffn2_w_out", 1)],
]
REPLICATED = ["ln_ffn1", "ln_mix", "ln_ffn2", "sgu_v_gain", "sgu_v_bias", "sgu_w_spatial", "sgu_b_spatial",
              "ln_final"]
NORM_SHARDS = ["mla_q_norm", "mla_kv_norm"]
WEIGHTS = ["ln_ffn1", "ffn1_w_in", "ffn1_w_out", "ln_mix", "ln_ffn2", "ffn2_w_in", "ffn2_w_out", "sgu_w_in",
           "sgu_v_gain", "sgu_v_bias", "sgu_w_spatial", "sgu_b_spatial", "sgu_w_out", "mla_w_in", "mla_q_norm",
           "mla_w_q_up", "mla_kv_norm", "mla_w_kv_up", "mla_w_out", "ln_final"]


def kernel(x, positions, ln_ffn1, ffn1_w_in, ffn1_w_out, ln_mix, ln_ffn2, ffn2_w_in, ffn2_w_out, sgu_w_in, sgu_v_gain, sgu_v_bias, sgu_w_spatial, sgu_b_spatial, sgu_w_out, mla_w_in, mla_q_norm, mla_w_q_up, mla_kv_norm, mla_w_kv_up, mla_w_out, ln_final, loss_target, m_ln_ffn1, m_ffn1_w_in, m_ffn1_w_out, m_ln_mix, m_ln_ffn2, m_ffn2_w_in, m_ffn2_w_out, m_sgu_w_in, m_sgu_v_gain, m_sgu_v_bias, m_sgu_w_spatial, m_sgu_b_spatial, m_sgu_w_out, m_mla_w_in, m_mla_q_norm, m_mla_w_q_up, m_mla_kv_norm, m_mla_w_kv_up, m_mla_w_out, m_ln_final, v_ln_ffn1, v_ffn1_w_in, v_ffn1_w_out, v_ln_mix, v_ln_ffn2, v_ffn2_w_in, v_ffn2_w_out, v_sgu_w_in, v_sgu_v_gain, v_sgu_v_bias, v_sgu_w_spatial, v_sgu_b_spatial, v_sgu_w_out, v_mla_w_in, v_mla_q_norm, v_mla_w_q_up, v_mla_kv_norm, v_mla_w_kv_up, v_mla_w_out, v_ln_final):
    given = dict(locals())
    w = {n: given[n] for n in WEIGHTS}
    mom = {n: given["m_" + n] for n in WEIGHTS}
    var = {n: given["v_" + n] for n in WEIGHTS}
    t, d = x.shape[1], x.shape[2]
    xs = x.reshape(t, d)
    target = loss_target.reshape(t, d)
    me = 2 * lax.axis_index("x") + lax.axis_index("y")

    me_idx = jnp.reshape(me, (1,)).astype(jnp.int32)
    core_idx = jnp.reshape(lax.axis_index("c"), (1,)).astype(jnp.int32)

    bufs = {(n, l): _cast_into(f"cast_{n}_{l}", me_idx, w[n], l) for n in BIG for l in range(w[n].shape[0])}
    nq = mla_q_norm.shape[1]
    norms = jnp.pad(jnp.concatenate([mla_q_norm, mla_kv_norm], axis=0), ((0, 6), (0, LANE - nq)))

    def take(stage):
        return [bufs[u] for u in stage]

    def put(stage, arrays):
        bufs.update(zip(stage, arrays))

    def full(unit):
        a = bufs[unit]
        return a.reshape(N_CHIPS, a.shape[1] * a.shape[2], a.shape[3])

    def rows(unit):
        a = bufs[unit]
        return a.reshape(-1, a.shape[-1])

    first, norms_g = _gather_weights(take(STAGES[0]), norms)
    put(STAGES[0], first)
    gq = norms_g[:, 0, :nq].reshape(1, N_CHIPS * nq)
    gkv = norms_g[:, 1, :nq].reshape(1, N_CHIPS * nq)
    tables = _rope_tables(positions.reshape(t))
    sgu_small = (sgu_v_gain, sgu_v_bias, sgu_w_spatial[0], sgu_b_spatial[0])

    a0, s_f1_0, got = _ffn_fwd("l0_ffn1", xs, ln_ffn1[0], full(("ffn1_w_in", 0)), rows(("ffn1_w_out", 0)),
                               take(STAGES[1]))
    put(STAGES[1], got)
    a1, s_sgu, got = _sgu_fwd(a0, ln_mix[0], full(("sgu_w_in", 0)), *sgu_small, rows(("sgu_w_out", 0)),
                              take(STAGES[2]))
    put(STAGES[2], got)
    a2, s_f2_0, got = _ffn_fwd("l0_ffn2", a1, ln_ffn2[0], full(("ffn2_w_in", 0)), rows(("ffn2_w_out", 0)),
                               take(STAGES[3]))
    put(STAGES[3], got)
    a3, s_f1_1, got = _ffn_fwd("l1_ffn1", a2, ln_ffn1[1], full(("ffn1_w_in", 1)), rows(("ffn1_w_out", 1)),
                               take(STAGES[4] + STAGES[5]))
    put(STAGES[4] + STAGES[5], got)
    mla_wts = _mla_weights(full(("mla_w_in", 0)), full(("mla_w_q_up", 0)), full(("mla_w_kv_up", 0)),
                           full(("mla_w_out", 0)))
    a4, s_mla = _mla_fwd(a3, ln_mix[1], mla_wts, gq, gkv, tables)
    a5, s_f2_1, _ = _ffn_fwd("l1_ffn2", a4, ln_ffn2[1], full(("ffn2_w_in", 1)), rows(("ffn2_w_out", 1)), [])

    gr, reduced = {}, {}
    n_layers = {n: w[n].shape[0] for n in BIG}

    def reduce_of(stages):
        units = [u for s in stages for u in STAGES[s]]
        return _Reduce(units, [gr[u] for u in units], n_layers, reduced, me_idx, core_idx)

    loss_part, dx, dxb, dg_final = _loss_bwd(a5, ln_final, target)
    dx, dxb, dg_f2_1, gr["ffn2_w_in", 1], gr["ffn2_w_out", 1] = _ffn_bwd(
        "l1_ffn2", s_f2_1, ln_ffn2[1], full(("ffn2_w_in", 1)), rows(("ffn2_w_out", 1)), dx, dxb, reduce_of([]))
    (dx, dxb, sm_mla, gr["mla_w_in", 0], gr["mla_w_q_up", 0], gr["mla_w_kv_up", 0],
     gr["mla_w_out", 0]) = _mla_bwd(s_mla, ln_mix[1], mla_wts, gq, gkv, tables, dx, dxb, mla_w_in.shape[2])
    dx, dxb, dg_f1_1, gr["ffn1_w_in", 1], gr["ffn1_w_out", 1] = _ffn_bwd(
        "l1_ffn1", s_f1_1, ln_ffn1[1], full(("ffn1_w_in", 1)), rows(("ffn1_w_out", 1)), dx, dxb, reduce_of([5, 4]))
    dx, dxb, dg_f2_0, gr["ffn2_w_in", 0], gr["ffn2_w_out", 0] = _ffn_bwd(
        "l0_ffn2", s_f2_0, ln_ffn2[0], full(("ffn2_w_in", 0)), rows(("ffn2_w_out", 0)), dx, dxb, reduce_of([3]))
    dx, dxb, sm_sgu, gr["sgu_w_in", 0], gr["sgu_w_out", 0] = _sgu_bwd(
        s_sgu, ln_mix[0], full(("sgu_w_in", 0)), *sgu_small, rows(("sgu_w_out", 0)), dx, dxb, reduce_of([2]))
    dx, dxb, dg_f1_0, gr["ffn1_w_in", 0], gr["ffn1_w_out", 0] = _ffn_bwd(
        "l0_ffn1", s_f1_0, ln_ffn1[0], full(("ffn1_w_in", 0)), rows(("ffn1_w_out", 0)), dx, dxb, reduce_of([1]))
    last_reduce = reduce_of([0])

    small_parts = [
        jnp.concatenate([dg_f1_0, dg_f1_1], axis=0), jnp.concatenate([sm_sgu["ln"], sm_mla["ln"]], axis=0),
        jnp.concatenate([dg_f2_0, dg_f2_1], axis=0), sm_sgu["gain"], sm_sgu["bias"], sm_sgu["w_sp"],
        sm_sgu["b_sp"], dg_final]
    rep_shapes = [w[n].shape for n in REPLICATED]
    gq_row = jnp.pad(sm_mla["gq"], ((0, 0), (0, N_CHIPS * (LANE - nq))))
    gkv_row = jnp.pad(sm_mla["gkv"], ((0, 0), (0, N_CHIPS * (LANE - nq))))
    packed = _pack(small_parts + [gq_row, gkv_row, loss_part])
    packed = jnp.pad(packed, ((0, -packed.shape[0] % 8), (0, 0)))
    everyone = _gather_all(packed)
    rows = packed.shape[0]
    zero_rows = jnp.zeros((rows - sum(math.prod(s) // LANE for s in rep_shapes), LANE), F32)
    pw = jnp.concatenate([_pack([w[n] for n in REPLICATED]), zero_rows], axis=0)
    pm = jnp.concatenate([_pack([mom[n] for n in REPLICATED]), zero_rows], axis=0)
    pv = jnp.concatenate([_pack([var[n] for n in REPLICATED]), zero_rows + 1.0], axis=0)
    g_all, d_all, m_all, v_all = _adamw_summed(pw, everyone, pm, pv)
    tail_shapes = [(1, N_CHIPS * LANE), (1, N_CHIPS * LANE), (1, LANE)]
    rep_grad = dict(zip(REPLICATED, _unpack(g_all, rep_shapes + tail_shapes)[:len(REPLICATED)]))
    rep_delta = dict(zip(REPLICATED, _unpack(d_all, rep_shapes)))
    rep_m = dict(zip(REPLICATED, _unpack(m_all, rep_shapes)))
    rep_v = dict(zip(REPLICATED, _unpack(v_all, rep_shapes)))
    tail = _unpack(g_all, rep_shapes + tail_shapes)[len(REPLICATED):]
    loss = tail[2][0, 0]
    norm_grad = {
        "mla_q_norm": lax.dynamic_slice(tail[0], (0, me * nq), (1, nq)),
        "mla_kv_norm": lax.dynamic_slice(tail[1], (0, me * nq), (1, nq)),
    }

    grad, delta, new_m, new_v = {}, {}, {}, {}

    def adamw_of(n, comm=None):
        flat = lambda a: a.reshape(-1, w[n].shape[-1])
        return _adamw("adamw_" + n, flat(w[n]), flat(reduced[n]), flat(mom[n]), flat(var[n]), comm=comm)

    carriers = ["ffn2_w_out", "ffn2_w_in", "sgu_w_in"]
    updates = {n: last_reduce.behind(step, adamw_of, n) for step, n in enumerate(carriers)}
    updates.update({n: adamw_of(n) for n in BIG if n not in carriers})
    for n in BIG:
        shp = w[n].shape
        grad[n] = reduced[n].reshape(shp)
        delta[n], new_m[n], new_v[n] = (a.reshape(shp) for a in updates[n])
    for n in REPLICATED:
        grad[n], delta[n], new_m[n], new_v[n] = rep_grad[n], rep_delta[n], rep_m[n], rep_v[n]
    stack = lambda dct: jnp.concatenate([dct[n] for n in NORM_SHARDS], axis=0)
    dl, nm, nv = _adamw("adamw_norm_shards", stack(w), stack(norm_grad), stack(mom), stack(var))
    for i, n in enumerate(NORM_SHARDS):
        grad[n], delta[n], new_m[n], new_v[n] = norm_grad[n], dl[i:i + 1], nm[i:i + 1], nv[i:i + 1]

    grad_x = dx.reshape(x.shape)
    return (loss, grad_x, *[grad[n] for n in WEIGHTS], *[delta[n] for n in WEIGHTS],
            *[new_m[n] for n in WEIGHTS], *[new_v[n] for n in WEIGHTS])
```

```python
import functools
import math

import jax
import jax.numpy as jnp
from jax import lax
from jax.experimental import pallas as pl
from jax.experimental.pallas import tpu as pltpu

F32 = jnp.float32
BF16 = jnp.bfloat16
MESH = pl.DeviceIdType.MESH

EPS = 1e-6
CHUNK = 64
SGU_BLOCK = 128
SGU_GROUPS = 8
QK_NOPE = 128
QK_ROPE = 64
V_DIM = 128
QK_DIM = QK_NOPE + QK_ROPE
HEAD_PAD = 256
LOG2E = math.log2(math.e)
ROPE_THETA = 10000.0
N_CHIPS = 4
N_DEV = 8

ADAM_LR = 0.001
ADAM_B1 = 0.9
ADAM_B2 = 0.999
ADAM_EPS = 1e-08
ADAM_WD = 0.01
ADAM_STEP = 10

LANE = 128
VMEM_LIMIT_BYTES = 56 * 1024 * 1024

_DIMS = {
    "nn": (((1,), (0,)), ((), ())),
    "nt": (((1,), (1,)), ((), ())),
    "tn": (((0,), (0,)), ((), ())),
}


def _tile(n, pref):
    t = (min(pref, n) // LANE) * LANE
    while t >= LANE:
        if n % t == 0:
            return t
        t -= LANE
    return n


def _params(sem):
    return pltpu.CompilerParams(dimension_semantics=sem, vmem_limit_bytes=VMEM_LIMIT_BYTES)


def _dot(a, b, mode):
    return lax.dot_general(a, b, _DIMS[mode], preferred_element_type=F32)


def _place():
    x, y, c = lax.axis_index("x"), lax.axis_index("y"), lax.axis_index("c")
    chips = [(1 - x, y), (x, 1 - y), (1 - x, 1 - y)]
    return x, y, c, chips


ANY = pl.BlockSpec(memory_space=pl.ANY)


def _gather_copies(phase, bufs, send, recv, landing):
    x, y, c, chips = _place()
    copies = []
    for t, buf in enumerate(bufs):
        for j, chip in enumerate(chips):
            there = 2 * chip[0] + chip[1]
            if phase == 1:
                src, lands, to = buf.at[2 * x + y, c], buf.at[there, c], (*chip, c)
            else:
                src, lands, to = buf.at[there, c], buf.at[there, 1 - c], (x, y, 1 - c)
            ref = lands if landing else src
            copies.append(pltpu.make_async_remote_copy(
                src_ref=ref, dst_ref=ref, send_sem=send.at[t, j], recv_sem=recv.at[t, j], device_id=to,
                device_id_type=MESH))
    return copies


class _Phase:
    def __init__(self, ins, fresh, sems, copies):
        self.ins, self.fresh, self.sems, self.copies = list(ins), list(fresh), sems, copies


def _gather_phase(phase, bufs):
    return _Phase(bufs, [], (len(bufs), 3),
                  lambda ins, outs, send, recv, landing: _gather_copies(phase, outs, send, recv, landing))


def _comm_io(comm):
    if comm is None:
        return [], [], [], [], [], False
    shapes = comm.fresh or [jax.ShapeDtypeStruct(b.shape, b.dtype) for b in comm.ins]
    sems = [pltpu.SemaphoreType.DMA(comm.sems), pltpu.SemaphoreType.DMA(comm.sems)]
    return comm.ins, [ANY] * len(comm.ins), [ANY] * len(shapes), shapes, sems, not comm.fresh


def _comm_run(comm, in_refs, out_refs, send, recv, first, last):
    @pl.when(first)
    def _():
        for cp in comm.copies(in_refs, out_refs, send, recv, False):
            cp.start()

    def finish():
        for cp in comm.copies(in_refs, out_refs, send, recv, True):
            cp.wait_recv()
        for cp in comm.copies(in_refs, out_refs, send, recv, False):
            cp.wait_send()

    return last, finish


def _matmul(name, mode, grid, a, a_spec, b, b_spec, extras, out_shapes, out_specs, acc_shape, epilogue, comm=None):
    nk = grid[2]
    n_ex = len(extras)
    n_out = len(out_shapes)
    c_in, c_in_specs, c_out_specs, c_shapes, c_sems, in_place = _comm_io(comm)
    n_c, n_co = len(c_in), len(c_shapes)

    def body(*refs):
        a_ref, b_ref = refs[0], refs[1]
        ex = refs[2:2 + n_ex]
        outs = refs[2 + n_ex + n_c:2 + n_ex + n_c + n_out]
        ids = (pl.program_id(0), pl.program_id(1))
        k = pl.program_id(2)
        if comm is not None:
            c_ins = refs[2 + n_ex:2 + n_ex + n_c]
            c_outs = refs[2 + n_ex + n_c + n_out:2 + n_ex + n_c + n_out + n_co]
            send, recv = refs[2 + n_ex + n_c + n_out + n_co:2 + n_ex + n_c + n_out + n_co + 2]
            first = jnp.logical_and(jnp.logical_and(ids[0] == 0, ids[1] == 0), k == 0)
            last = jnp.logical_and(jnp.logical_and(ids[0] == grid[0] - 1, ids[1] == grid[1] - 1), k == nk - 1)
            last, finish = _comm_run(comm, c_ins, c_outs, send, recv, first, last)
        part = _dot(a_ref[...], b_ref[...], mode)
        if nk == 1:
            epilogue(part, ex, outs, ids)
        else:
            acc = refs[-1]

            @pl.when(k == 0)
            def _():
                acc[...] = part

            @pl.when(k > 0)
            def _():
                acc[...] += part

            @pl.when(k == nk - 1)
            def _():
                epilogue(acc[...], ex, outs, ids)

        if comm is not None:
            pl.when(last)(finish)

    scratch = c_sems + ([pltpu.VMEM(acc_shape, F32)] if nk > 1 else [])
    sem = ("parallel", "parallel", "arbitrary") if comm is None else ("arbitrary",) * 3
    return pl.pallas_call(
        body,
        name=name,
        grid=grid,
        in_specs=[a_spec, b_spec] + [s for _, s in extras] + c_in_specs,
        out_specs=list(out_specs) + c_out_specs,
        out_shape=list(out_shapes) + c_shapes,
        input_output_aliases={2 + n_ex + t: n_out + t for t in range(n_c)} if in_place else {},
        scratch_shapes=scratch,
        compiler_params=_params(sem),
    )(a, b, *[e for e, _ in extras], *c_in)


def _store(scale, dtype):
    def epilogue(acc, ex, outs, ids):
        v = acc if scale == 1.0 else acc * scale
        outs[0][...] = v.astype(dtype)

    return epilogue


def _mm_nn_full(name, a, b, out_dtype, tm_pref=1024, tn_pref=512):
    m, kd = a.shape
    n = b.shape[1]
    tm, tn = _tile(m, tm_pref), _tile(n, tn_pref)
    return _matmul(
        name, "nn", (m // tm, n // tn, 1),
        a, pl.BlockSpec((tm, kd), lambda i, j, k: (i, 0)),
        b, pl.BlockSpec((kd, tn), lambda i, j, k: (0, j)),
        [], [jax.ShapeDtypeStruct((m, n), out_dtype)], [pl.BlockSpec((tm, tn), lambda i, j, k: (i, j))],
        None, _store(1.0, out_dtype))[0]


def _mm_nt_full(name, a, b, out_dtype, scale=1.0, tm_pref=2048, tn_pref=512):
    m, kd = a.shape
    n = b.shape[0]
    tm, tn = _tile(m, tm_pref), _tile(n, tn_pref)
    return _matmul(
        name, "nt", (m // tm, n // tn, 1),
        a, pl.BlockSpec((tm, kd), lambda i, j, k: (i, 0)),
        b, pl.BlockSpec((tn, kd), lambda i, j, k: (j, 0)),
        [], [jax.ShapeDtypeStruct((m, n), out_dtype)], [pl.BlockSpec((tm, tn), lambda i, j, k: (i, j))],
        None, _store(scale, out_dtype))[0]


def _mm_nt_k(name, a, b, out_dtype, tk_pref=1024, tm_pref=1024, tn_pref=1024):
    m, kd = a.shape
    n = b.shape[0]
    tm, tn, tk = _tile(m, tm_pref), _tile(n, tn_pref), _tile(kd, tk_pref)
    return _matmul(
        name, "nt", (m // tm, n // tn, kd // tk),
        a, pl.BlockSpec((tm, tk), lambda i, j, k: (i, k)),
        b, pl.BlockSpec((tn, tk), lambda i, j, k: (j, k)),
        [], [jax.ShapeDtypeStruct((m, n), out_dtype)], [pl.BlockSpec((tm, tn), lambda i, j, k: (i, j))],
        (tm, tn), _store(1.0, out_dtype))[0]


def _mm_tn(name, a, b, scale=1.0, tm_pref=1024, tn_pref=1024, tk_pref=1024, comm=None):
    t, m = a.shape
    n = b.shape[1]
    tm, tn, tk = _tile(m, tm_pref), _tile(n, tn_pref), _tile(t, tk_pref)
    res = _matmul(
        name, "tn", (m // tm, n // tn, t // tk),
        a, pl.BlockSpec((tk, tm), lambda i, j, k: (k, i)),
        b, pl.BlockSpec((tk, tn), lambda i, j, k: (k, j)),
        [], [jax.ShapeDtypeStruct((m, n), BF16)], [pl.BlockSpec((tm, tn), lambda i, j, k: (i, j))],
        (tm, tn), _store(scale, BF16), comm=comm)
    return res[0] if comm is None else (res[0], res[1:])


def _mm_residual(name, a, b, x, scale, tm_pref=1024, tn_pref=1024, tk_pref=1408, comm=None):
    m, kd = a.shape
    n = b.shape[1]
    tm, tn, tk = _tile(m, tm_pref), _tile(n, tn_pref), _tile(kd, tk_pref)

    def epilogue(acc, ex, outs, ids):
        outs[0][...] = ex[0][...] + scale * acc

    res = _matmul(
        name, "nn", (m // tm, n // tn, kd // tk),
        a, pl.BlockSpec((tm, tk), lambda i, j, k: (i, k)),
        b, pl.BlockSpec((tk, tn), lambda i, j, k: (k, j)),
        [(x, pl.BlockSpec((tm, tn), lambda i, j, k: (i, j)))],
        [jax.ShapeDtypeStruct((m, n), F32)], [pl.BlockSpec((tm, tn), lambda i, j, k: (i, j))],
        (tm, tn), epilogue, comm=comm)
    return res[0] if comm is None else (res[0], res[1:])


def _rms_fwd(name, x, g, with_transpose=False):
    t, d = x.shape
    tm = _tile(t, 512)

    def body(x_ref, g_ref, h_ref, *ht_ref):
        xv = x_ref[...]
        r = lax.rsqrt(jnp.mean(xv * xv, axis=-1, keepdims=True) + EPS)
        h = xv * r * g_ref[...]
        h_ref[...] = h.astype(BF16)
        if with_transpose:
            ht_ref[0][...] = h.T.astype(BF16)

    out_specs = [pl.BlockSpec((tm, d), lambda i: (i, 0))]
    out_shape = [jax.ShapeDtypeStruct((t, d), BF16)]
    if with_transpose:
        out_specs.append(pl.BlockSpec((d, tm), lambda i: (0, i)))
        out_shape.append(jax.ShapeDtypeStruct((d, t), BF16))
    res = pl.pallas_call(
        body, name=name, grid=(t // tm,),
        in_specs=[pl.BlockSpec((tm, d), lambda i: (i, 0)), pl.BlockSpec((1, d), lambda i: (0, 0))],
        out_specs=out_specs, out_shape=out_shape,
        compiler_params=_params(("parallel",)),
    )(x, g.reshape(1, d))
    return res if with_transpose else res[0]


def _rms_bwd_math(dh, xv, g):
    r = lax.rsqrt(jnp.mean(xv * xv, axis=-1, keepdims=True) + EPS)
    xhat = xv * r
    dxh = dh * g
    dx = r * (dxh - xhat * jnp.mean(dxh * xhat, axis=-1, keepdims=True))
    return dx, dh * xhat


def _rms_bwd(name, dh, x, g, dres):
    t, d = x.shape
    tm = _tile(t, 256)

    def body(dh_ref, x_ref, g_ref, dres_ref, dx_ref, dxb_ref, dg_ref):
        dx, dgt = _rms_bwd_math(dh_ref[...].astype(F32), x_ref[...], g_ref[...])
        dx = dres_ref[...] + dx
        dx_ref[...] = dx
        dxb_ref[...] = dx.astype(BF16)

        @pl.when(pl.program_id(0) == 0)
        def _():
            dg_ref[...] = jnp.zeros_like(dg_ref)

        dg_ref[...] += jnp.sum(dgt, axis=0, keepdims=True)

    row = pl.BlockSpec((tm, d), lambda i: (i, 0))
    vec = pl.BlockSpec((1, d), lambda i: (0, 0))
    return pl.pallas_call(
        body, name=name, grid=(t // tm,),
        in_specs=[row, row, vec, row],
        out_specs=[row, row, vec],
        out_shape=[jax.ShapeDtypeStruct((t, d), F32), jax.ShapeDtypeStruct((t, d), BF16),
                   jax.ShapeDtypeStruct((1, d), F32)],
        compiler_params=_params(("arbitrary",)),
    )(dh, x, g.reshape(1, d), dres)


def _loss_bwd(x, g, target):
    t, d = x.shape
    tm = _tile(t, 256)

    def body(x_ref, g_ref, tgt_ref, loss_ref, dx_ref, dxb_ref, dg_ref):
        xv = x_ref[...]
        gv = g_ref[...]
        r = lax.rsqrt(jnp.mean(xv * xv, axis=-1, keepdims=True) + EPS)
        err = xv * r * gv - tgt_ref[...]
        part = 0.5 * jnp.sum(jnp.mean(err * err, axis=-1, keepdims=True), axis=0, keepdims=True)
        dx, dgt = _rms_bwd_math(err * (1.0 / d), xv, gv)
        dx_ref[...] = dx
        dxb_ref[...] = dx.astype(BF16)

        @pl.when(pl.program_id(0) == 0)
        def _():
            dg_ref[...] = jnp.zeros_like(dg_ref)
            loss_ref[...] = jnp.zeros_like(loss_ref)

        dg_ref[...] += jnp.sum(dgt, axis=0, keepdims=True)
        loss_ref[...] += jnp.broadcast_to(part, loss_ref.shape)

    row = pl.BlockSpec((tm, d), lambda i: (i, 0))
    vec = pl.BlockSpec((1, d), lambda i: (0, 0))
    return pl.pallas_call(
        body, name="loss_bwd", grid=(t // tm,),
        in_specs=[row, vec, row],
        out_specs=[pl.BlockSpec((1, LANE), lambda i: (0, 0)), row, row, vec],
        out_shape=[jax.ShapeDtypeStruct((1, LANE), F32), jax.ShapeDtypeStruct((t, d), F32),
                   jax.ShapeDtypeStruct((t, d), BF16), jax.ShapeDtypeStruct((1, d), F32)],
        compiler_params=_params(("arbitrary",)),
    )(x, g.reshape(1, d), target)


UP_SPLIT = 2


def _sigmoid(x):
    return 0.5 * jnp.tanh(0.5 * x) + 0.5


def _ffn_up(name, h, w_in, comm=None):
    t, d = h.shape
    fs = w_in.shape[2]
    f = 2 * fs
    tm, tn = _tile(t, 2048), _tile(fs, 256)
    per = fs // tn
    grid = (t // tm, f // tn)
    c_in, c_in_specs, c_out_specs, c_shapes, c_sems, in_place = _comm_io(comm)
    n_c, n_co = len(c_in), len(c_shapes)

    def body(*refs):
        h_ref, wg_ref, wu_ref = refs[:3]
        gu_ref, z_ref = refs[3 + n_c:5 + n_c]
        if comm is not None:
            i, j = pl.program_id(0), pl.program_id(1)
            send, recv = refs[5 + n_c + n_co:]
            last, finish = _comm_run(comm, refs[3:3 + n_c], refs[5 + n_c:5 + n_c + n_co], send, recv,
                                     jnp.logical_and(i == 0, j == 0),
                                     jnp.logical_and(i == grid[0] - 1, j == grid[1] - 1))
        wg, wu = wg_ref[...], wu_ref[...]
        for r0 in range(0, tm, tm // UP_SPLIT):
            rows = slice(r0, r0 + tm // UP_SPLIT)
            hv = h_ref[rows, :]
            gate = _dot(hv, wg, "nn")
            up = _dot(hv, wu, "nn")
            sg = _sigmoid(gate)
            silu = gate * sg
            gu_ref[0, rows, :] = (sg * (1.0 + gate * (1.0 - sg)) * up).astype(BF16)
            gu_ref[1, rows, :] = silu.astype(BF16)
            z_ref[rows, :] = (silu * up).astype(BF16)
        if comm is not None:
            pl.when(last)(finish)

    res = pl.pallas_call(
        body, name=name, grid=grid,
        in_specs=[pl.BlockSpec((tm, d), lambda i, j: (i, 0)),
                  pl.BlockSpec((None, d, tn), lambda i, j: (j // per, 0, j % per)),
                  pl.BlockSpec((None, d, tn), lambda i, j: (2 + j // per, 0, j % per))] + c_in_specs,
        out_specs=[pl.BlockSpec((2, tm, tn), lambda i, j: (0, i, j)),
                   pl.BlockSpec((tm, tn), lambda i, j: (i, j))] + c_out_specs,
        out_shape=[jax.ShapeDtypeStruct((2, t, f), BF16), jax.ShapeDtypeStruct((t, f), BF16)] + c_shapes,
        input_output_aliases={3 + k: 2 + k for k in range(n_c)} if in_place else {},
        scratch_shapes=c_sems,
        compiler_params=_params(("parallel", "parallel") if comm is None else ("arbitrary", "arbitrary")),
    )(h, w_in, w_in, *c_in)
    return (res[0], res[1]) if comm is None else (res[0], res[1], res[2:])


def _ffn_dact(name, dxb, w_out, gu):
    t, d = dxb.shape
    f = w_out.shape[0]
    tm, tn = _tile(t, 2048), _tile(f, 512)

    def epilogue(acc, ex, outs, ids):
        dz = 0.5 * acc
        outs[0][0] = (dz * ex[0][0].astype(F32)).astype(BF16)
        outs[0][1] = (dz * ex[0][1].astype(F32)).astype(BF16)

    blk = pl.BlockSpec((2, tm, tn), lambda i, j, k: (0, i, j))
    return _matmul(
        name, "nt", (t // tm, f // tn, 1),
        dxb, pl.BlockSpec((tm, d), lambda i, j, k: (i, 0)),
        w_out, pl.BlockSpec((tn, d), lambda i, j, k: (j, 0)),
        [(gu, blk)], [jax.ShapeDtypeStruct((2, t, f), BF16)], [blk], None, epilogue)[0]


def _grad_colsharded(name, ht, da, comm=None):
    d, t = ht.shape
    w = da.shape[2]
    ws = w // 2
    tm, tn, tk = _tile(d, 1024), _tile(ws, 1408), _tile(t, 2048)
    per = ws // tn
    res = _matmul(
        name, "nn", (d // tm, (2 * w) // tn, t // tk),
        ht, pl.BlockSpec((tm, tk), lambda i, j, k: (i, k)),
        da, pl.BlockSpec((None, tk, tn), lambda i, j, k: (j // (2 * per), k, j % (2 * per))),
        [], [jax.ShapeDtypeStruct((N_CHIPS, d, ws), BF16)],
        [pl.BlockSpec((None, tm, tn), lambda i, j, k: (j // per, i, j % per))],
        (tm, tn), _store(1.0, BF16), comm=comm)
    return res[0] if comm is None else (res[0], res[1:])


def _back_colsharded(name, da, w_g, comm=None):
    _, t, w = da.shape
    d, ws = w_g.shape[1], w_g.shape[2]
    tm, tn, tk = _tile(t, 1024), _tile(d, 1024), _tile(ws, 2816)
    per = ws // tk
    res = _matmul(
        name, "nt", (t // tm, d // tn, (2 * w) // tk),
        da, pl.BlockSpec((None, tm, tk), lambda i, j, k: (k // (2 * per), i, k % (2 * per))),
        w_g, pl.BlockSpec((None, tn, tk), lambda i, j, k: (k // per, j, k % per)),
        [], [jax.ShapeDtypeStruct((t, d), F32)], [pl.BlockSpec((tm, tn), lambda i, j, k: (i, j))],
        (tm, tn), _store(1.0, F32), comm=comm)
    return res[0] if comm is None else (res[0], res[1:])


def _ffn_fwd(tag, x, g, w_in, w_out, prefetch, own_out=None):
    h, ht = _rms_fwd(tag + "_norm", x, g, with_transpose=True)
    if own_out is not None:
        gu, z, got = _ffn_up(tag + "_up", h, w_in, comm=_gather_phase(1, list(prefetch) + [own_out]))
        own_out = _run_phase(tag + "_handover", _gather_phase(2, [got[-1]]))[0]
        w_out = own_out.reshape(-1, own_out.shape[-1])
        y, got = _mm_residual(tag + "_down", z, w_out, x, 0.5, tk_pref=2816, comm=_gather_phase(2, got[:-1]))
        prefetch = list(got) + [own_out]
    elif prefetch:
        gu, z, prefetch = _ffn_up(tag + "_up", h, w_in, comm=_gather_phase(1, prefetch))
        y, prefetch = _mm_residual(tag + "_down", z, w_out, x, 0.5, tk_pref=2816, comm=_gather_phase(2, prefetch))
    else:
        gu, z = _ffn_up(tag + "_up", h, w_in)
        y = _mm_residual(tag + "_down", z, w_out, x, 0.5, tk_pref=2816)
    return y, (x, ht, gu, z), prefetch


def _ffn_bwd(tag, saved, g, w_in, w_out, dx, dxb, red):
    x, ht, gu, z = saved
    f = z.shape[1]
    d_w_out = red.behind(0, _mm_tn, tag + "_dwout", z, dxb, scale=0.5, tm_pref=1408, tn_pref=2048)
    da = _ffn_dact(tag + "_dact", dxb, w_out, gu)
    d_w_in = red.behind(1, _grad_colsharded, tag + "_dwin", ht, da)
    dh = red.behind(2, _back_colsharded, tag + "_dh", da, w_in)
    dx, dxb, dg = _rms_bwd(tag + "_dnorm", dh, x, g, dx)
    return dx, dxb, dg, d_w_in, d_w_out.reshape(N_CHIPS, f // N_CHIPS, -1)


_GELU_K = math.sqrt(2.0 / math.pi)
_GELU_C = 0.044715


def _gelu(x):
    t = jnp.tanh(_GELU_K * (x + _GELU_C * x * x * x))
    return 0.5 * x * (1.0 + t), t


def _dgelu(x, t):
    return 0.5 * (1.0 + t) + 0.5 * x * (1.0 - t * t) * (_GELU_K * (1.0 + 3.0 * _GELU_C * x * x))


def _causal_block_mask():
    r = lax.broadcasted_iota(jnp.int32, (SGU_BLOCK, SGU_BLOCK), 0) // CHUNK
    c = lax.broadcasted_iota(jnp.int32, (SGU_BLOCK, SGU_BLOCK), 1) // CHUNK
    return r >= c


def _sgu_pre(name, h, w_in, comm=None):
    t, d = h.shape
    ws = w_in.shape[2]
    w = 2 * ws
    tm, tn = _tile(t, 2048), _tile(ws, 512)
    per = ws // tn
    res = _matmul(
        name, "nn", (t // tm, (2 * w) // tn, 1),
        h, pl.BlockSpec((tm, d), lambda i, j, k: (i, 0)),
        w_in, pl.BlockSpec((None, d, tn), lambda i, j, k: (j // per, 0, j % per)),
        [], [jax.ShapeDtypeStruct((2, t, w), BF16)],
        [pl.BlockSpec((None, tm, tn), lambda i, j, k: (j // (2 * per), i, j % (2 * per)))],
        None, _store(1.0, BF16), comm=comm)
    return res[0] if comm is None else (res[0], res[1:])


def _layernorm_stats(v):
    mu = jnp.mean(v, axis=-1, keepdims=True)
    vc = v - mu
    rstd = lax.rsqrt(jnp.mean(vc * vc, axis=-1, keepdims=True) + EPS)
    return vc * rstd, rstd


def _sgu_mid_fwd(pre, gain, bias, w_sp, b_sp_t):
    _, t, w = pre.shape
    gd = w // SGU_GROUPS

    def body(pre_ref, gain_ref, bias_ref, ws_ref, bt_ref, out_ref):
        mask = _causal_block_mask()
        u, _ = _gelu(pre_ref[0].astype(F32))
        v, _ = _gelu(pre_ref[1].astype(F32))
        vhat, _ = _layernorm_stats(v)
        vln = (vhat * gain_ref[...] + bias_ref[...]).astype(BF16)
        for gi in range(SGU_GROUPS):
            cols = slice(gi * gd, (gi + 1) * gd)
            wg = jnp.where(mask, ws_ref[gi], 0.0).astype(BF16)
            mixed = _dot(wg, vln[:, cols], "nn") + bt_ref[:, gi:gi + 1]
            out_ref[:, cols] = (u[:, cols] * mixed).astype(BF16)

    return pl.pallas_call(
        body, name="sgu_mid_fwd", grid=(t // SGU_BLOCK,),
        in_specs=[pl.BlockSpec((2, SGU_BLOCK, w), lambda n: (0, n, 0)),
                  pl.BlockSpec((1, w), lambda n: (0, 0)), pl.BlockSpec((1, w), lambda n: (0, 0)),
                  pl.BlockSpec((SGU_GROUPS, SGU_BLOCK, SGU_BLOCK), lambda n: (0, 0, 0)),
                  pl.BlockSpec((SGU_BLOCK, SGU_GROUPS), lambda n: (0, 0))],
        out_specs=pl.BlockSpec((SGU_BLOCK, w), lambda n: (n, 0)),
        out_shape=jax.ShapeDtypeStruct((t, w), BF16),
        compiler_params=_params(("parallel",)),
    )(pre, gain, bias, w_sp, b_sp_t)


def _sgu_mid_bwd(pre, dgated, gain, bias, w_sp, b_sp_t):
    _, t, w = pre.shape
    gd = w // SGU_GROUPS

    def body(pre_ref, dg_ref, gain_ref, bias_ref, ws_ref, bt_ref,
             dpre_ref, dgain_ref, dbias_ref, dws_ref, dbt_ref, dvln_s):
        @pl.when(pl.program_id(0) == 0)
        def _():
            dgain_ref[...] = jnp.zeros_like(dgain_ref)
            dbias_ref[...] = jnp.zeros_like(dbias_ref)
            dws_ref[...] = jnp.zeros_like(dws_ref)
            dbt_ref[...] = jnp.zeros_like(dbt_ref)

        mask = _causal_block_mask()
        pu = pre_ref[0].astype(F32)
        pv = pre_ref[1].astype(F32)
        u, tu = _gelu(pu)
        v, tv = _gelu(pv)
        vhat, rstd = _layernorm_stats(v)
        gain_v = gain_ref[...]
        vln = (vhat * gain_v + bias_ref[...]).astype(BF16)
        dgt = dg_ref[...].astype(F32)
        for gi in range(SGU_GROUPS):
            cols = slice(gi * gd, (gi + 1) * gd)
            wg = jnp.where(mask, ws_ref[gi], 0.0).astype(BF16)
            vg = vln[:, cols]
            mixed = _dot(wg, vg, "nn") + bt_ref[:, gi:gi + 1]
            dgg = dgt[:, cols]
            dmixed = dgg * u[:, cols]
            dmb = dmixed.astype(BF16)
            dpre_ref[0, :, cols] = (dgg * mixed * _dgelu(pu[:, cols], tu[:, cols])).astype(BF16)
            dbt_ref[:, gi:gi + 1] += jnp.sum(dmixed, axis=1, keepdims=True)
            dws_ref[gi] += jnp.where(mask, _dot(dmb, vg, "nt"), 0.0)
            dvln_s[:, cols] = _dot(wg, dmb, "tn")
        dvln = dvln_s[...]
        dgain_ref[...] += jnp.sum(dvln * vhat, axis=0, keepdims=True)
        dbias_ref[...] += jnp.sum(dvln, axis=0, keepdims=True)
        dvh = dvln * gain_v
        dv = rstd * (dvh - jnp.mean(dvh, axis=-1, keepdims=True)
                     - vhat * jnp.mean(dvh * vhat, axis=-1, keepdims=True))
        dpre_ref[1] = (dv * _dgelu(pv, tv)).astype(BF16)

    vec = pl.BlockSpec((1, w), lambda n: (0, 0))
    wsb = pl.BlockSpec((SGU_GROUPS, SGU_BLOCK, SGU_BLOCK), lambda n: (0, 0, 0))
    btb = pl.BlockSpec((SGU_BLOCK, SGU_GROUPS), lambda n: (0, 0))
    blk2 = pl.BlockSpec((2, SGU_BLOCK, w), lambda n: (0, n, 0))
    return pl.pallas_call(
        body, name="sgu_mid_bwd", grid=(t // SGU_BLOCK,),
        in_specs=[blk2, pl.BlockSpec((SGU_BLOCK, w), lambda n: (n, 0)), vec, vec, wsb, btb],
        out_specs=[blk2, vec, vec, wsb, btb],
        out_shape=[jax.ShapeDtypeStruct((2, t, w), BF16), jax.ShapeDtypeStruct((1, w), F32),
                   jax.ShapeDtypeStruct((1, w), F32),
                   jax.ShapeDtypeStruct((SGU_GROUPS, SGU_BLOCK, SGU_BLOCK), F32),
                   jax.ShapeDtypeStruct((SGU_BLOCK, SGU_GROUPS), F32)],
        scratch_shapes=[pltpu.VMEM((SGU_BLOCK, w), F32)],
        compiler_params=_params(("arbitrary",)),
    )(pre, dgated, gain, bias, w_sp, b_sp_t)


def _sgu_fwd(x, g, w_in, gain, bias, w_sp, b_sp, w_out, prefetch):
    h, ht = _rms_fwd("sgu_norm", x, g, with_transpose=True)
    pre, prefetch = _sgu_pre("sgu_pre", h, w_in, comm=_gather_phase(1, prefetch))
    gated = _sgu_mid_fwd(pre, gain, bias, w_sp, b_sp.T)
    y, prefetch = _mm_residual("sgu_out", gated, w_out, x, 1.0, tk_pref=1024, comm=_gather_phase(2, prefetch))
    return y, (x, ht, pre, gated), prefetch


def _sgu_bwd(saved, g, w_in, gain, bias, w_sp, b_sp, w_out, dx, dxb, red):
    x, ht, pre, gated = saved
    w = gated.shape[1]
    d_w_out = red.behind(0, _mm_tn, "sgu_dwout", gated, dxb)
    dgated = _mm_nt_full("sgu_dgated", dxb, w_out, BF16)
    dpre, dgain, dbias, dws, dbt = _sgu_mid_bwd(pre, dgated, gain, bias, w_sp, b_sp.T)
    d_w_in = red.behind(1, _grad_colsharded, "sgu_dwin", ht, dpre)
    dh = red.behind(2, _back_colsharded, "sgu_dh", dpre, w_in)
    dx, dxb, dg = _rms_bwd("sgu_dnorm", dh, x, g, dx)
    small = dict(ln=dg, gain=dgain, bias=dbias, w_sp=dws, b_sp=dbt.T)
    return dx, dxb, small, d_w_in, d_w_out.reshape(N_CHIPS, w // N_CHIPS, -1)


def _rope_tables(positions):
    half = QK_ROPE // 2
    inv_freq = 1.0 / (ROPE_THETA ** (jnp.arange(half, dtype=F32) / half))
    ang = positions.astype(F32)[:, None] * inv_freq
    cos, sin = jnp.cos(ang), jnp.sin(ang)
    t = positions.shape[0]
    zeros = jnp.zeros((t, half), F32)
    rest = jnp.zeros((t, LANE - QK_ROPE), F32)
    c = jnp.concatenate([cos, cos, rest + 1.0], axis=1)
    s_up = jnp.concatenate([zeros, sin, rest], axis=1)
    s_dn = jnp.concatenate([-sin, zeros, rest], axis=1)
    return c, s_up, s_dn


def _rope_apply(x, c, s_up, s_dn):
    half = QK_ROPE // 2
    return x * c + pltpu.roll(x, half, 1) * s_up + pltpu.roll(x, LANE - half, 1) * s_dn


def _rope_apply_t(dy, c, s_up, s_dn):
    half = QK_ROPE // 2
    return dy * c - pltpu.roll(dy, LANE - half, 1) * s_dn - pltpu.roll(dy, half, 1) * s_up


def _mla_norm_fwd(proj, gq, gkv):
    t, p = proj.shape
    ql, kvl = gq.shape[1], gkv.shape[1]
    tm = _tile(t, 512)

    def body(p_ref, gq_ref, gkv_ref, qn_ref, kvn_ref):
        for lo, n, g_ref, o_ref in ((0, ql, gq_ref, qn_ref), (ql, kvl, gkv_ref, kvn_ref)):
            xv = p_ref[:, lo:lo + n]
            r = lax.rsqrt(jnp.mean(xv * xv, axis=-1, keepdims=True) + EPS)
            o_ref[...] = (xv * r * g_ref[...]).astype(BF16)

    return pl.pallas_call(
        body, name="mla_norm_fwd", grid=(t // tm,),
        in_specs=[pl.BlockSpec((tm, p), lambda i: (i, 0)), pl.BlockSpec((1, ql), lambda i: (0, 0)),
                  pl.BlockSpec((1, kvl), lambda i: (0, 0))],
        out_specs=[pl.BlockSpec((tm, ql), lambda i: (i, 0)), pl.BlockSpec((tm, kvl), lambda i: (i, 0))],
        out_shape=[jax.ShapeDtypeStruct((t, ql), BF16), jax.ShapeDtypeStruct((t, kvl), BF16)],
        compiler_params=_params(("parallel",)),
    )(proj, gq, gkv)


def _mla_norm_bwd(proj, dqn, dkvn, dkr, gq, gkv):
    t, p = proj.shape
    ql, kvl = gq.shape[1], gkv.shape[1]
    tm = _tile(t, 256)

    def body(p_ref, dqn_ref, dkvn_ref, dkr_ref, gq_ref, gkv_ref, dp_ref, dgq_ref, dgkv_ref):
        @pl.when(pl.program_id(0) == 0)
        def _():
            dgq_ref[...] = jnp.zeros_like(dgq_ref)
            dgkv_ref[...] = jnp.zeros_like(dgkv_ref)

        for lo, n, g_ref, d_ref, dg_ref in ((0, ql, gq_ref, dqn_ref, dgq_ref),
                                             (ql, kvl, gkv_ref, dkvn_ref, dgkv_ref)):
            dx, dgt = _rms_bwd_math(d_ref[...], p_ref[:, lo:lo + n], g_ref[...])
            dp_ref[:, lo:lo + n] = dx.astype(BF16)
            dg_ref[...] += jnp.sum(dgt, axis=0, keepdims=True)
        dp_ref[:, ql + kvl:] = dkr_ref[...].astype(BF16)

    def row(n):
        return pl.BlockSpec((tm, n), lambda i: (i, 0))

    def vec(n):
        return pl.BlockSpec((1, n), lambda i: (0, 0))

    return pl.pallas_call(
        body, name="mla_norm_bwd", grid=(t // tm,),
        in_specs=[row(p), row(ql), row(kvl), row(LANE), vec(ql), vec(kvl)],
        out_specs=[row(p), vec(ql), vec(kvl)],
        out_shape=[jax.ShapeDtypeStruct((t, p), BF16), jax.ShapeDtypeStruct((1, ql), F32),
                   jax.ShapeDtypeStruct((1, kvl), F32)],
        compiler_params=_params(("arbitrary",)),
    )(proj, dqn, dkvn, dkr, gq, gkv)


def _mla_q_up(qn, wq, tables):
    t, ql = qn.shape
    n = wq.shape[1]
    tm = _tile(t, 2048)
    scale = QK_DIM ** -0.5 * LOG2E

    def epilogue(acc, ex, outs, ids):
        outs[0][:, :QK_NOPE] = (scale * acc[:, :QK_NOPE]).astype(BF16)
        hi = _rope_apply(acc[:, QK_NOPE:], ex[0][...], ex[1][...], ex[2][...])
        outs[0][:, QK_NOPE:] = (scale * hi).astype(BF16)

    tab = pl.BlockSpec((tm, LANE), lambda i, j, k: (i, 0))
    return _matmul(
        "mla_q_up", "nn", (t // tm, n // HEAD_PAD, 1),
        qn, pl.BlockSpec((tm, ql), lambda i, j, k: (i, 0)),
        wq, pl.BlockSpec((ql, HEAD_PAD), lambda i, j, k: (0, j)),
        [(tb, tab) for tb in tables],
        [jax.ShapeDtypeStruct((t, n), BF16)], [pl.BlockSpec((tm, HEAD_PAD), lambda i, j, k: (i, j))],
        None, epilogue)[0]


def _mla_kv_up(kvn, wkv, proj, tables, heads):
    t, kvl = kvn.shape
    n = wkv.shape[1]
    p = proj.shape[1]
    tm = _tile(t, 2048)

    def epilogue(acc, ex, outs, ids):
        kr = _rope_apply(ex[0][...], ex[1][...], ex[2][...], ex[3][...])
        outs[0][:, :QK_NOPE] = acc[:, :QK_NOPE].astype(BF16)
        outs[0][:, QK_NOPE:] = (acc[:, QK_NOPE:] + jnp.where(ids[1] < heads, kr, 1.0)).astype(BF16)

    tab = pl.BlockSpec((tm, LANE), lambda i, j, k: (i, 0))
    kr_spec = pl.BlockSpec((tm, LANE), lambda i, j, k: (i, p // LANE - 1))
    return _matmul(
        "mla_kv_up", "nn", (t // tm, n // HEAD_PAD, 1),
        kvn, pl.BlockSpec((tm, kvl), lambda i, j, k: (i, 0)),
        wkv, pl.BlockSpec((kvl, HEAD_PAD), lambda i, j, k: (0, j)),
        [(proj, kr_spec)] + [(tb, tab) for tb in tables],
        [jax.ShapeDtypeStruct((t, n), BF16)], [pl.BlockSpec((tm, HEAD_PAD), lambda i, j, k: (i, j))],
        None, epilogue)[0]


FLASH_TQ = 1024
FLASH_TK = 1024
FLASH_SPLIT = 4
FLASH_SPLIT_BWD = 2


def _chunk_mask(tq, tk, qi, ki):
    r = (qi * tq + lax.broadcasted_iota(jnp.int32, (tq, tk), 0)) // CHUNK
    c = (ki * tk + lax.broadcasted_iota(jnp.int32, (tq, tk), 1)) // CHUNK
    return c <= r


def _block_pairs(t, tq, tk, key_major):
    def visible(qi, ki):
        return (ki * tk) // CHUNK <= (qi * tq + tq - 1) // CHUNK

    def masked(qi, ki):
        return (ki * tk + tk - 1) // CHUNK > (qi * tq) // CHUNK

    nq, nk = t // tq, t // tk
    if key_major:
        sweeps = [[(qi, ki) for qi in range(nq) if visible(qi, ki)] for ki in range(nk)]
    else:
        sweeps = [[(qi, ki) for ki in range(nk) if visible(qi, ki)] for qi in range(nq)]
    qs, ks, fs = [], [], []
    for sweep in sweeps:
        for n, (qi, ki) in enumerate(sweep):
            qs.append(qi)
            ks.append(ki)
            fs.append((1 if masked(qi, ki) else 0) + (2 if n == 0 else 0) + (4 if n == len(sweep) - 1 else 0))
    return tuple(jnp.asarray(v, jnp.int32) for v in (qs, ks, fs))


def _flash_fwd(qp, kv, heads):
    t = qp.shape[0]
    tq, tk = _tile(t, FLASH_TQ), _tile(t, FLASH_TK)
    qt, kt, ft = _block_pairs(t, tq, tk, key_major=False)

    def body(qt_ref, kt_ref, ft_ref, q_ref, k_ref, v_ref, o_ref, lse_ref, m_s, acc_s):
        p = pl.program_id(1)
        qi, ki, flags = qt_ref[p], kt_ref[p], ft_ref[p]

        @pl.when(flags & 2 != 0)
        def _():
            m_s[...] = jnp.full_like(m_s, -1e30)
            acc_s[...] = jnp.zeros_like(acc_s)

        def step(masked):
            mask = _chunk_mask(tq, tk, qi, ki) if masked else None
            tg = tq // FLASH_SPLIT
            for g in range(FLASH_SPLIT):
                rows = slice(g * tg, (g + 1) * tg)
                nc = min(tk, (g + 1) * tg) if masked and tq == tk else tk
                s = _dot(q_ref[rows, :], k_ref[:nc, :], "nt")
                if masked:
                    s = jnp.where(mask[rows, :nc], s, -1e30)
                m_prev = m_s[rows, :]
                m_new = jnp.maximum(m_prev, jnp.max(s, axis=1, keepdims=True))
                alpha = jnp.exp2(m_prev - m_new)
                pr = jnp.exp2(s - jnp.tile(m_new, (1, nc // LANE))).astype(BF16)
                pv = _dot(pr, v_ref[:nc, :], "nn")
                acc_s[rows, :V_DIM] = alpha * acc_s[rows, :V_DIM] + pv[:, :V_DIM]
                acc_s[rows, V_DIM:] = alpha * acc_s[rows, V_DIM:] + pv[:, V_DIM:]
                m_s[rows, :] = m_new

        @pl.when(flags & 1 == 0)
        def _():
            step(False)

        @pl.when(flags & 1 != 0)
        def _():
            step(True)

        @pl.when(flags & 4 != 0)
        def _():
            l = acc_s[:, V_DIM:]
            o_ref[...] = (acc_s[:, :V_DIM] / l).astype(BF16)
            lse_ref[...] = m_s[...] + jnp.log(l) * LOG2E

    return pl.pallas_call(
        body, name="mla_flash_fwd",
        grid_spec=pltpu.PrefetchScalarGridSpec(
            num_scalar_prefetch=3, grid=(heads, int(qt.shape[0])),
            in_specs=[pl.BlockSpec((tq, HEAD_PAD), lambda h, p, qt, kt, ft: (qt[p], h)),
                      pl.BlockSpec((tk, HEAD_PAD), lambda h, p, qt, kt, ft: (kt[p], h)),
                      pl.BlockSpec((tk, HEAD_PAD), lambda h, p, qt, kt, ft: (kt[p], heads + h))],
            out_specs=[pl.BlockSpec((tq, V_DIM), lambda h, p, qt, kt, ft: (qt[p], h)),
                       pl.BlockSpec((None, tq, LANE), lambda h, p, qt, kt, ft: (h, qt[p], 0))],
            scratch_shapes=[pltpu.VMEM((tq, LANE), F32), pltpu.VMEM((tq, HEAD_PAD), F32)]),
        out_shape=[jax.ShapeDtypeStruct((t, heads * V_DIM), BF16),
                   jax.ShapeDtypeStruct((heads, t, LANE), F32)],
        compiler_params=_params(("parallel", "arbitrary")),
    )(qt, kt, ft, qp, kv, kv)


def _flash_delta(o, do, heads):
    t = o.shape[0]
    tm = _tile(t, 256)

    def body(o_ref, do_ref, d_ref):
        for h in range(heads):
            cols = slice(h * V_DIM, (h + 1) * V_DIM)
            prod = o_ref[:, cols].astype(F32) * do_ref[:, cols].astype(F32)
            d_ref[h] = jnp.broadcast_to(jnp.sum(prod, axis=1, keepdims=True), (tm, LANE))

    row = pl.BlockSpec((tm, heads * V_DIM), lambda i: (i, 0))
    return pl.pallas_call(
        body, name="mla_flash_delta", grid=(t // tm,), in_specs=[row, row],
        out_specs=pl.BlockSpec((heads, tm, LANE), lambda i: (0, i, 0)),
        out_shape=jax.ShapeDtypeStruct((heads, t, LANE), F32),
        compiler_params=_params(("parallel",)),
    )(o, do)


def _flash_bwd(qp, kv, do, lse, delta, heads):
    t = qp.shape[0]
    tq, tk = _tile(t, FLASH_TQ), _tile(t, FLASH_TK)
    qt, kt, ft = _block_pairs(t, tq, tk, key_major=True)

    def body(qt_ref, kt_ref, ft_ref, q_ref, k_ref, v_ref, do_ref, lse_ref, dl_ref, dq_ref, dk_ref, dv_ref,
             dk_s, dv_s):
        p = pl.program_id(1)
        qi, ki, flags = qt_ref[p], kt_ref[p], ft_ref[p]

        @pl.when(p == 0)
        def _():
            dq_ref[...] = jnp.zeros_like(dq_ref)

        @pl.when(flags & 2 != 0)
        def _():
            dk_s[...] = jnp.zeros_like(dk_s)
            dv_s[...] = jnp.zeros_like(dv_s)

        def step(masked):
            mask = _chunk_mask(tq, tk, qi, ki) if masked else None
            tg = tq // FLASH_SPLIT_BWD
            for g in range(FLASH_SPLIT_BWD):
                rows = slice(g * tg, (g + 1) * tg)
                nc = min(tk, (g + 1) * tg) if masked and tq == tk else tk
                q = q_ref[rows, :]
                dov = do_ref[rows, :]
                s = _dot(q, k_ref[:nc, :], "nt")
                pr = jnp.exp2(s - jnp.tile(lse_ref[rows, :], (1, nc // LANE)))
                if masked:
                    pr = jnp.where(mask[rows, :nc], pr, 0.0)
                dv_s[:nc, :] += _dot(pr.astype(BF16), dov, "tn")
                dp = _dot(dov, v_ref[:nc, :], "nt")
                ds = (pr * (dp - jnp.tile(dl_ref[rows, :], (1, nc // LANE)))).astype(BF16)
                dq_rows = pl.ds(pl.multiple_of(qi * tq + g * tg, tg), tg)
                dq_ref[dq_rows, :] += _dot(ds, k_ref[:nc, :], "nn")
                dk_s[:nc, :] += _dot(ds, q, "tn")

        @pl.when(flags & 1 == 0)
        def _():
            step(False)

        @pl.when(flags & 1 != 0)
        def _():
            step(True)

        @pl.when(flags & 4 != 0)
        def _():
            dk_ref[...] = dk_s[...]
            dv_ref[...] = dv_s[...]

    def qrow(width):
        return pl.BlockSpec((tq, width), lambda h, p, qt, kt, ft: (qt[p], h))

    def stat():
        return pl.BlockSpec((None, tq, LANE), lambda h, p, qt, kt, ft: (h, qt[p], 0))

    return pl.pallas_call(
        body, name="mla_flash_bwd",
        grid_spec=pltpu.PrefetchScalarGridSpec(
            num_scalar_prefetch=3, grid=(heads, int(qt.shape[0])),
            in_specs=[qrow(HEAD_PAD),
                      pl.BlockSpec((tk, HEAD_PAD), lambda h, p, qt, kt, ft: (kt[p], h)),
                      pl.BlockSpec((tk, V_DIM), lambda h, p, qt, kt, ft: (kt[p], 2 * (heads + h))),
                      qrow(V_DIM), stat(), stat()],
            out_specs=[pl.BlockSpec((t, HEAD_PAD), lambda h, p, qt, kt, ft: (0, h)),
                       pl.BlockSpec((tk, HEAD_PAD), lambda h, p, qt, kt, ft: (kt[p], h)),
                       pl.BlockSpec((tk, V_DIM), lambda h, p, qt, kt, ft: (kt[p], h))],
            scratch_shapes=[pltpu.VMEM((tk, HEAD_PAD), F32), pltpu.VMEM((tk, V_DIM), F32)]),
        out_shape=[jax.ShapeDtypeStruct((t, heads * HEAD_PAD), F32),
                   jax.ShapeDtypeStruct((t, heads * HEAD_PAD), F32),
                   jax.ShapeDtypeStruct((t, heads * V_DIM), F32)],
        compiler_params=_params(("parallel", "arbitrary")),
    )(qt, kt, ft, qp, kv, kv, do, lse, delta)


def _mla_attn_post(dqp, dkp, dv, tables, heads):
    t = dqp.shape[0]
    tm = _tile(t, 256)
    scale = QK_DIM ** -0.5
    kw, vw = heads * HEAD_PAD, heads * V_DIM

    def body(dq_ref, dk_ref, dv_ref, c_ref, su_ref, sd_ref, dqb_ref, dkvb_ref, dkr_ref):
        c, su, sd = c_ref[...], su_ref[...], sd_ref[...]
        kr = jnp.zeros((tm, LANE), F32)
        for h in range(heads):
            lo = h * HEAD_PAD
            mid = lo + QK_NOPE
            dqb_ref[:, lo:mid] = (scale * dq_ref[:, lo:mid]).astype(BF16)
            dqb_ref[:, mid:mid + LANE] = (scale * _rope_apply_t(dq_ref[:, mid:mid + LANE], c, su, sd)).astype(BF16)
            kr = kr + dk_ref[:, mid:mid + LANE]
            dkvb_ref[:, kw + lo:kw + mid] = dv_ref[:, h * V_DIM:(h + 1) * V_DIM].astype(BF16)
            dkvb_ref[:, kw + mid:kw + lo + HEAD_PAD] = jnp.zeros((tm, HEAD_PAD - V_DIM), BF16)
        dkvb_ref[:, :kw] = (dk_ref[...] * (1.0 / LOG2E)).astype(BF16)
        dkr_ref[...] = _rope_apply_t(kr * (1.0 / LOG2E), c, su, sd)

    def row(n):
        return pl.BlockSpec((tm, n), lambda i: (i, 0))

    return pl.pallas_call(
        body, name="mla_attn_post", grid=(t // tm,),
        in_specs=[row(kw), row(kw), row(vw), row(LANE), row(LANE), row(LANE)],
        out_specs=[row(kw), row(2 * kw), row(LANE)],
        out_shape=[jax.ShapeDtypeStruct((t, kw), BF16), jax.ShapeDtypeStruct((t, 2 * kw), BF16),
                   jax.ShapeDtypeStruct((t, LANE), F32)],
        compiler_params=_params(("parallel",)),
    )(dqp, dkp, dv, *tables)


def _mla_weights(w_in_g, w_q_g, w_kv_g, w_out_g):
    d = w_in_g.shape[0] * w_in_g.shape[1]
    pw = w_in_g.shape[2]
    w_in = jnp.pad(w_in_g.reshape(d, pw), ((0, 0), (0, LANE - QK_ROPE)))
    ql = w_q_g.shape[1]
    wq = jnp.transpose(w_q_g, (1, 0, 2)).reshape(ql, -1, QK_DIM)
    heads = wq.shape[1]
    wq = jnp.pad(wq, ((0, 0), (0, 0), (0, HEAD_PAD - QK_DIM))).reshape(ql, heads * HEAD_PAD)
    kvl = w_kv_g.shape[1]
    wkv = jnp.transpose(w_kv_g, (1, 0, 2)).reshape(kvl, heads, QK_NOPE + V_DIM)
    wk = jnp.pad(wkv[:, :, :QK_NOPE], ((0, 0), (0, 0), (0, HEAD_PAD - QK_NOPE))).reshape(kvl, heads * HEAD_PAD)
    wv = jnp.pad(wkv[:, :, QK_NOPE:], ((0, 0), (0, 0), (0, HEAD_PAD - V_DIM))).reshape(kvl, heads * HEAD_PAD)
    return w_in, wq, jnp.concatenate([wk, wv], axis=1), w_out_g.reshape(heads * V_DIM, -1), heads


def _mla_unpermute(d_w_in, d_wq, d_wkv, heads, pw):
    d = d_w_in.shape[0]
    g_in = d_w_in[:, :pw].reshape(N_CHIPS, d // N_CHIPS, pw)
    ql = d_wq.shape[0]
    g_q = d_wq.reshape(ql, heads, HEAD_PAD)[:, :, :QK_DIM].reshape(ql, N_CHIPS, -1)
    kvl = d_wkv.shape[0]
    g_k = d_wkv[:, :heads * HEAD_PAD].reshape(kvl, heads, HEAD_PAD)[:, :, :QK_NOPE]
    g_v = d_wkv[:, heads * HEAD_PAD:].reshape(kvl, heads, HEAD_PAD)[:, :, :V_DIM]
    g_kv = jnp.concatenate([g_k, g_v], axis=2).reshape(kvl, N_CHIPS, -1)
    return g_in, jnp.transpose(g_q, (1, 0, 2)), jnp.transpose(g_kv, (1, 0, 2))


def _mla_fwd(x, g, wts, gq, gkv, tables):
    w_in, wq, wkv, w_out, heads = wts
    h = _rms_fwd("mla_norm", x, g)
    proj = _mm_nn_full("mla_proj", h, w_in, F32, tn_pref=w_in.shape[1])
    qn, kvn = _mla_norm_fwd(proj, gq, gkv)
    qp = _mla_q_up(qn, wq, tables)
    kv = _mla_kv_up(kvn, wkv, proj, tables, heads)
    o, lse = _flash_fwd(qp, kv, heads)
    y = _mm_residual("mla_out", o, w_out, x, 1.0, tk_pref=2048)
    return y, (x, h, proj, qn, kvn, qp, kv, o, lse)


def _mla_bwd(saved, g, wts, gq, gkv, tables, dx, dxb, pw):
    w_in, wq, wkv, w_out, heads = wts
    x, h, proj, qn, kvn, qp, kv, o, lse = saved
    d_w_out = _mm_tn("mla_dwout", o, dxb)
    do = _mm_nt_full("mla_do", dxb, w_out, BF16)
    dqp, dkp, dv = _flash_bwd(qp, kv, do, lse, _flash_delta(o, do, heads), heads)
    dqb, dkvb, dkr = _mla_attn_post(dqp, dkp, dv, tables, heads)
    d_wq = _mm_tn("mla_dwq", qn, dqb)
    dqn = _mm_nt_k("mla_dqn", dqb, wq, F32)
    d_wkv = _mm_tn("mla_dwkv", kvn, dkvb)
    dkvn = _mm_nt_k("mla_dkvn", dkvb, wkv, F32)
    dproj, dgq, dgkv = _mla_norm_bwd(proj, dqn, dkvn, dkr, gq, gkv)
    d_w_in = _mm_tn("mla_dwin", h, dproj, tn_pref=dproj.shape[1])
    dh = _mm_nt_full("mla_dh", dproj, w_in, F32, tn_pref=1024)
    dx, dxb, dg = _rms_bwd("mla_dnorm", dh, x, g, dx)
    g_in, g_q, g_kv = _mla_unpermute(d_w_in, d_wq, d_wkv, heads, pw)
    small = dict(ln=dg, gq=dgq, gkv=dgkv)
    return dx, dxb, small, g_in, g_q, g_kv, d_w_out.reshape(N_CHIPS, d_w_out.shape[0] // N_CHIPS, -1)


def _gather_weights(bufs, norms):
    nt = len(bufs)

    def body(*refs):
        n_in = refs[nt]
        outs, n_out = refs[nt + 1:2 * nt + 1], refs[2 * nt + 1]
        send, recv, fsend, frecv, loc, nsend, nrecv = refs[2 * nt + 2:]
        x, y, c, chips = _place()
        me = 2 * x + y
        sib = (x, y, 1 - c)

        local = pltpu.make_async_copy(n_in, n_out.at[me], loc)
        local.start()

        def place(t, chip, half):
            return outs[t].at[2 * chip[0] + chip[1], half]

        def ici(t, j, chip):
            return pltpu.make_async_remote_copy(
                src_ref=place(t, (x, y), c), dst_ref=place(t, (x, y), c), send_sem=send.at[t, j],
                recv_sem=recv.at[t, j], device_id=(*chip, c), device_id_type=MESH)

        def fwd(t, j, chip, half):
            return pltpu.make_async_remote_copy(
                src_ref=place(t, chip, half), dst_ref=place(t, chip, half), send_sem=fsend.at[t, j],
                recv_sem=frecv.at[t, j], device_id=sib, device_id_type=MESH)

        def nrm(j, chip, owner):
            return pltpu.make_async_remote_copy(
                src_ref=n_in, dst_ref=n_out.at[2 * owner[0] + owner[1]], send_sem=nsend.at[j], recv_sem=nrecv.at[j],
                device_id=(*chip, c), device_id_type=MESH)

        firsts = [ici(t, j, chip) for t in range(nt) for j, chip in enumerate(chips)]
        firsts += [nrm(j, chip, (x, y)) for j, chip in enumerate(chips)]
        for cp in firsts:
            cp.start()
        passed = []
        for t in range(nt):
            for j, chip in enumerate(chips):
                pltpu.make_async_remote_copy(
                    src_ref=place(t, chip, c), dst_ref=place(t, chip, c), send_sem=send.at[t, j],
                    recv_sem=recv.at[t, j], device_id=(*chip, c), device_id_type=MESH).wait_recv()
                cp = fwd(t, j, chip, c)
                cp.start()
                passed.append(cp)
        for t in range(nt):
            for j, chip in enumerate(chips):
                fwd(t, j, chip, 1 - c).wait_recv()
        for j, chip in enumerate(chips):
            nrm(j, chip, chip).wait_recv()
        for cp in firsts + passed:
            cp.wait_send()
        local.wait()

    out_shape = [jax.ShapeDtypeStruct(b.shape, b.dtype) for b in bufs]
    out_shape.append(jax.ShapeDtypeStruct((N_CHIPS,) + norms.shape, norms.dtype))
    res = pl.pallas_call(
        body, name="gather_weights",
        in_specs=[ANY] * (nt + 1), out_specs=[ANY] * (nt + 1), out_shape=out_shape,
        input_output_aliases={t: t for t in range(nt)},
        scratch_shapes=[pltpu.SemaphoreType.DMA((nt, 3)), pltpu.SemaphoreType.DMA((nt, 3)),
                        pltpu.SemaphoreType.DMA((nt, 3)), pltpu.SemaphoreType.DMA((nt, 3)),
                        pltpu.SemaphoreType.DMA, pltpu.SemaphoreType.DMA((3,)),
                        pltpu.SemaphoreType.DMA((3,))],
    )(*bufs, norms)
    return res[:nt], res[nt]


def _run_phase(name, phase):
    c_in, in_specs, out_specs, shapes, sems, in_place = _comm_io(phase)
    n_c, n_co = len(c_in), len(shapes)

    def body(*refs):
        ins, outs = refs[:n_c], refs[n_c:n_c + n_co]
        send, recv = refs[n_c + n_co:]
        for cp in phase.copies(ins, outs, send, recv, False):
            cp.start()
        for cp in phase.copies(ins, outs, send, recv, True):
            cp.wait_recv()
        for cp in phase.copies(ins, outs, send, recv, False):
            cp.wait_send()

    return pl.pallas_call(
        body, name=name, in_specs=in_specs, out_specs=out_specs, out_shape=shapes,
        input_output_aliases={t: t for t in range(n_c)} if in_place else {}, scratch_shapes=sems,
    )(*c_in)


def _remote(ref_src, ref_dst, send, recv, t, j, to):
    return pltpu.make_async_remote_copy(src_ref=ref_src, dst_ref=ref_dst, send_sem=send.at[t, j],
                                        recv_sem=recv.at[t, j], device_id=to, device_id_type=MESH)


class _Reduce:
    def __init__(self, units, grads, n_layers, reduced, me_idx, core_idx):
        self.units, self.n_layers, self.reduced = list(units), n_layers, reduced
        self.me_idx, self.core_idx = me_idx, core_idx
        self.local = [g.reshape(N_CHIPS, 2, g.shape[1] // 2, g.shape[2]) for g in grads]
        self.parts = None

    def _swap(self):
        fresh = [jax.ShapeDtypeStruct((g.shape[0],) + g.shape[2:], g.dtype) for g in self.local]

        def copies(ins, outs, send, recv, landing):
            x, y, c, _ = _place()
            return [_remote(outs[t] if landing else ins[t].at[:, 1 - c], outs[t], send, recv, t, 0, (x, y, 1 - c))
                    for t in range(len(ins))]

        return _Phase(self.local, fresh, (len(self.local), 1), copies)

    def _scatter(self):
        fresh = [jax.ShapeDtypeStruct((3,) + p.shape[1:], p.dtype) for p in self.parts]

        def copies(ins, outs, send, recv, landing):
            x, y, c, chips = _place()
            return [_remote(outs[t].at[j] if landing else ins[t].at[2 * chip[0] + chip[1]], outs[t].at[j],
                            send, recv, t, j, (*chip, c))
                    for t in range(len(ins)) for j, chip in enumerate(chips)]

        return _Phase(self.parts, fresh, (len(self.parts), 3), copies)

    def _join(self):
        layers = [l for _, l in self.units]

        def copies(ins, outs, send, recv, landing):
            x, y, c, _ = _place()
            refs = [outs[t].at[l, 1 - c if landing else c] for t, l in enumerate(layers)]
            return [_remote(r, r, send, recv, t, 0, (x, y, 1 - c)) for t, r in enumerate(refs)]

        return _Phase([self.reduced[n] for n, _ in self.units], [], (len(self.units), 1), copies)

    def _after(self, step, got):
        if step == 0:
            self.parts = [_add_halves(self.core_idx, g, o) for g, o in zip(self.local, got)]
        elif step == 1:
            for (n, l), p, ld in zip(self.units, self.parts, got):
                self.reduced[n] = _sum_chips(self.me_idx, self.core_idx, p, ld, l, self.n_layers[n],
                                             self.reduced.get(n))
        else:
            for (n, _), joined in zip(self.units, got):
                self.reduced[n] = joined

    def _phase(self, step):
        return (self._swap, self._scatter, self._join)[step]()

    def behind(self, step, fn, *args, **kw):
        if not self.units:
            return fn(*args, **kw)
        out, got = fn(*args, comm=self._phase(step), **kw)
        self._after(step, got)
        return out

    def alone(self):
        for step, name in enumerate(("grad_swap_halves", "grad_scatter_chips", "grad_join_halves")):
            self._after(step, _run_phase(name, self._phase(step)))


def _gather_all(block):
    m_per, n = block.shape

    def body(x_ref, out_ref, send_sems, recv_sems, local_sem):
        x, y, c, chips = _place()
        me, sibling = (x, y, c), (x, y, 1 - c)

        def rows(px, py, pc):
            return out_ref.at[pl.ds((4 * px + 2 * py + pc) * m_per, m_per), :]

        def copy(k, block_of, to, src=None):
            return pltpu.make_async_remote_copy(
                src_ref=rows(*block_of) if src is None else src, dst_ref=rows(*block_of),
                send_sem=send_sems.at[k], recv_sem=recv_sems.at[k], device_id=to, device_id_type=MESH)

        mine = pltpu.make_async_copy(x_ref, rows(*me), local_sem)
        mine.start()
        first = [copy(0, me, sibling, src=x_ref)]
        first += [copy(1 + j, me, (*chip, c), src=x_ref) for j, chip in enumerate(chips)]
        for cp in first:
            cp.start()
        passed = [copy(4 + j, (*chip, c), sibling) for j, chip in enumerate(chips)]
        for j, chip in enumerate(chips):
            copy(1 + j, (*chip, c), me).wait_recv()
            passed[j].start()
        copy(0, sibling, me).wait_recv()
        for j, chip in enumerate(chips):
            copy(4 + j, (*chip, 1 - c), me).wait_recv()
        for cp in first + passed:
            cp.wait_send()
        mine.wait()

    return pl.pallas_call(
        body, name="gather_small_grads",
        out_shape=jax.ShapeDtypeStruct((N_DEV * m_per, n), block.dtype),
        in_specs=[pl.BlockSpec(memory_space=pltpu.VMEM)],
        out_specs=pl.BlockSpec(memory_space=pltpu.VMEM),
        scratch_shapes=[pltpu.SemaphoreType.DMA((7,)), pltpu.SemaphoreType.DMA((7,)), pltpu.SemaphoreType.DMA],
    )(block)


def _row_tile(r, c, elems=512 * 1024):
    t = max(8, min(r, (elems // c) // 8 * 8))
    while t > 8 and r % t:
        t -= 8
    return t if r % t == 0 else r


def _add_halves(idx, grad, other):
    n, _, r, w = grad.shape
    tr = _row_tile(r, w)

    def body(idx_ref, g_ref, o_ref, out_ref):
        out_ref[...] = (g_ref[...].astype(F32) + o_ref[...].astype(F32)).astype(BF16)

    return pl.pallas_call(
        body, name="grad_add_halves",
        grid_spec=pltpu.PrefetchScalarGridSpec(
            num_scalar_prefetch=1, grid=(n, r // tr),
            in_specs=[pl.BlockSpec((None, None, tr, w), lambda k, i, idx: (k, idx[0], i, 0)),
                      pl.BlockSpec((None, tr, w), lambda k, i, idx: (k, i, 0))],
            out_specs=pl.BlockSpec((None, tr, w), lambda k, i, idx: (k, i, 0))),
        out_shape=jax.ShapeDtypeStruct((n, r, w), BF16),
        compiler_params=_params(("parallel", "parallel")),
    )(idx, grad, other)


def _sum_chips(me_idx, core_idx, part, landed, layer, n_layers, prev):
    _, r, w = part.shape
    tr = _row_tile(r, w)

    def body(me_ref, c_ref, p_ref, l_ref, *rest):
        acc = p_ref[...].astype(F32)
        for j in range(3):
            acc = acc + l_ref[j].astype(F32)
        rest[-1][...] = acc

    return pl.pallas_call(
        body, name="grad_sum_chips",
        grid_spec=pltpu.PrefetchScalarGridSpec(
            num_scalar_prefetch=2, grid=(r // tr,),
            in_specs=[pl.BlockSpec((None, tr, w), lambda i, me, c: (me[0], i, 0)),
                      pl.BlockSpec((3, tr, w), lambda i, me, c: (0, i, 0))] + ([] if prev is None else [ANY]),
            out_specs=pl.BlockSpec((None, None, tr, w), lambda i, me, c: (layer, c[0], i, 0))),
        out_shape=jax.ShapeDtypeStruct((n_layers, 2, r, w), F32),
        input_output_aliases={} if prev is None else {4: 0},
        compiler_params=_params(("parallel",)),
    )(me_idx, core_idx, part, landed, *([] if prev is None else [prev]))


def _adamw_math(w, g, m, v):
    m = ADAM_B1 * m + (1.0 - ADAM_B1) * g
    v = ADAM_B2 * v + (1.0 - ADAM_B2) * (g * g)
    m_hat = m / (1.0 - ADAM_B1 ** ADAM_STEP)
    v_hat = v / (1.0 - ADAM_B2 ** ADAM_STEP)
    delta = -ADAM_LR * (m_hat / (jnp.sqrt(v_hat) + ADAM_EPS) + ADAM_WD * w)
    return delta, m, v


def _adamw(name, w, g, m, v):
    r, c = w.shape
    tr = _row_tile(r, c, 256 * 1024)

    def body(w_ref, g_ref, m_ref, v_ref, d_ref, nm_ref, nv_ref):
        d_ref[...], nm_ref[...], nv_ref[...] = _adamw_math(w_ref[...], g_ref[...], m_ref[...], v_ref[...])

    blk = pl.BlockSpec((tr, c), lambda i: (i, 0))
    return pl.pallas_call(
        body, name=name, grid=(r // tr,), in_specs=[blk] * 4, out_specs=[blk] * 3,
        out_shape=[jax.ShapeDtypeStruct((r, c), F32)] * 3,
        compiler_params=_params(("parallel",)),
    )(w, g, m, v)


def _adamw_summed(w, parts, m, v):
    r, c = w.shape

    def body(w_ref, p_ref, m_ref, v_ref, g_ref, d_ref, nm_ref, nv_ref):
        g = p_ref[0:r, :]
        for k in range(1, N_DEV):
            g = g + p_ref[k * r:(k + 1) * r, :]
        g_ref[...] = g
        d_ref[...], nm_ref[...], nv_ref[...] = _adamw_math(w_ref[...], g, m_ref[...], v_ref[...])

    return pl.pallas_call(
        body, name="adamw_replicated",
        out_shape=[jax.ShapeDtypeStruct((r, c), F32)] * 4,
        compiler_params=pltpu.CompilerParams(vmem_limit_bytes=VMEM_LIMIT_BYTES),
    )(w, parts, m, v)


def _pack(arrays):
    return jnp.concatenate([a.reshape(-1, LANE) for a in arrays], axis=0)


def _unpack(packed, shapes):
    out, row = [], 0
    for s in shapes:
        n = math.prod(s) // LANE
        out.append(packed[row:row + n].reshape(s))
        row += n
    return out


def _cast_into(name, idx, w, layer):
    _, rows, c = w.shape
    r = rows // 2
    tr = _row_tile(r, c)
    per = r // tr

    def body(idx_ref, w_ref, o_ref):
        o_ref[...] = w_ref[...].astype(BF16)

    return pl.pallas_call(
        body, name=name,
        grid_spec=pltpu.PrefetchScalarGridSpec(
            num_scalar_prefetch=1, grid=(2, per),
            in_specs=[pl.BlockSpec((None, tr, c), lambda h, i, idx: (layer, h * per + i, 0))],
            out_specs=pl.BlockSpec((None, None, tr, c), lambda h, i, idx: (idx[0], h, i, 0))),
        out_shape=jax.ShapeDtypeStruct((N_CHIPS, 2, r, c), BF16),
        compiler_params=_params(("parallel", "parallel")),
    )(idx, w)


BIG = ["ffn1_w_in", "ffn1_w_out", "ffn2_w_in", "ffn2_w_out", "sgu_w_in", "sgu_w_out",
       "mla_w_in", "mla_w_q_up", "mla_w_kv_up", "mla_w_out"]
STAGES = [
    [("ffn1_w_in", 0), ("ffn1_w_out", 0)],
    [("sgu_w_in", 0), ("sgu_w_out", 0)],
    [("ffn2_w_in", 0), ("ffn2_w_out", 0)],
    [("ffn1_w_in", 1), ("ffn1_w_out", 1)],
    [("mla_w_in", 0), ("mla_w_q_up", 0), ("mla_w_kv_up", 0), ("mla_w_out", 0)],
    [("ffn2_w_in", 1), ("ffn2_w_out", 1)],
]
REPLICATED = ["ln_ffn1", "ln_mix", "ln_ffn2", "sgu_v_gain", "sgu_v_bias", "sgu_w_spatial", "sgu_b_spatial",
              "ln_final"]
NORM_SHARDS = ["mla_q_norm", "mla_kv_norm"]
WEIGHTS = ["ln_ffn1", "ffn1_w_in", "ffn1_w_out", "ln_mix", "ln_ffn2", "ffn2_w_in", "ffn2_w_out", "sgu_w_in",
           "sgu_v_gain", "sgu_v_bias", "sgu_w_spatial", "sgu_b_spatial", "sgu_w_out", "mla_w_in", "mla_q_norm",
           "mla_w_q_up", "mla_kv_norm", "mla_w_kv_up", "mla_w_out", "ln_final"]


def kernel(x, positions, ln_ffn1, ffn1_w_in, ffn1_w_out, ln_mix, ln_ffn2, ffn2_w_in, ffn2_w_out, sgu_w_in, sgu_v_gain, sgu_v_bias, sgu_w_spatial, sgu_b_spatial, sgu_w_out, mla_w_in, mla_q_norm, mla_w_q_up, mla_kv_norm, mla_w_kv_up, mla_w_out, ln_final, loss_target, m_ln_ffn1, m_ffn1_w_in, m_ffn1_w_out, m_ln_mix, m_ln_ffn2, m_ffn2_w_in, m_ffn2_w_out, m_sgu_w_in, m_sgu_v_gain, m_sgu_v_bias, m_sgu_w_spatial, m_sgu_b_spatial, m_sgu_w_out, m_mla_w_in, m_mla_q_norm, m_mla_w_q_up, m_mla_kv_norm, m_mla_w_kv_up, m_mla_w_out, m_ln_final, v_ln_ffn1, v_ffn1_w_in, v_ffn1_w_out, v_ln_mix, v_ln_ffn2, v_ffn2_w_in, v_ffn2_w_out, v_sgu_w_in, v_sgu_v_gain, v_sgu_v_bias, v_sgu_w_spatial, v_sgu_b_spatial, v_sgu_w_out, v_mla_w_in, v_mla_q_norm, v_mla_w_q_up, v_mla_kv_norm, v_mla_w_kv_up, v_mla_w_out, v_ln_final):
    given = dict(locals())
    w = {n: given[n] for n in WEIGHTS}
    mom = {n: given["m_" + n] for n in WEIGHTS}
    var = {n: given["v_" + n] for n in WEIGHTS}
    t, d = x.shape[1], x.shape[2]
    xs = x.reshape(t, d)
    target = loss_target.reshape(t, d)
    me = 2 * lax.axis_index("x") + lax.axis_index("y")

    me_idx = jnp.reshape(me, (1,)).astype(jnp.int32)
    core_idx = jnp.reshape(lax.axis_index("c"), (1,)).astype(jnp.int32)

    bufs = {(n, l): _cast_into(f"cast_{n}_{l}", me_idx, w[n], l) for n in BIG for l in range(w[n].shape[0])}
    nq = mla_q_norm.shape[1]
    norms = jnp.pad(jnp.concatenate([mla_q_norm, mla_kv_norm], axis=0), ((0, 6), (0, LANE - nq)))

    def take(stage):
        return [bufs[u] for u in stage]

    def put(stage, arrays):
        bufs.update(zip(stage, arrays))

    def full(unit):
        a = bufs[unit]
        return a.reshape(N_CHIPS, a.shape[1] * a.shape[2], a.shape[3])

    def rows(unit):
        a = bufs[unit]
        return a.reshape(-1, a.shape[-1])

    first, norms_g = _gather_weights(take(STAGES[0][:1]), norms)
    put(STAGES[0][:1], first)
    gq = norms_g[:, 0, :nq].reshape(1, N_CHIPS * nq)
    gkv = norms_g[:, 1, :nq].reshape(1, N_CHIPS * nq)
    tables = _rope_tables(positions.reshape(t))
    sgu_small = (sgu_v_gain, sgu_v_bias, sgu_w_spatial[0], sgu_b_spatial[0])

    a0, s_f1_0, got = _ffn_fwd("l0_ffn1", xs, ln_ffn1[0], full(("ffn1_w_in", 0)), None,
                               take(STAGES[1]), own_out=bufs[STAGES[0][1]])
    put(STAGES[1] + STAGES[0][1:], got)
    a1, s_sgu, got = _sgu_fwd(a0, ln_mix[0], full(("sgu_w_in", 0)), *sgu_small, rows(("sgu_w_out", 0)),
                              take(STAGES[2]))
    put(STAGES[2], got)
    a2, s_f2_0, got = _ffn_fwd("l0_ffn2", a1, ln_ffn2[0], full(("ffn2_w_in", 0)), rows(("ffn2_w_out", 0)),
                               take(STAGES[3]))
    put(STAGES[3], got)
    a3, s_f1_1, got = _ffn_fwd("l1_ffn1", a2, ln_ffn1[1], full(("ffn1_w_in", 1)), rows(("ffn1_w_out", 1)),
                               take(STAGES[4] + STAGES[5]))
    put(STAGES[4] + STAGES[5], got)
    mla_wts = _mla_weights(full(("mla_w_in", 0)), full(("mla_w_q_up", 0)), full(("mla_w_kv_up", 0)),
                           full(("mla_w_out", 0)))
    a4, s_mla = _mla_fwd(a3, ln_mix[1], mla_wts, gq, gkv, tables)
    a5, s_f2_1, _ = _ffn_fwd("l1_ffn2", a4, ln_ffn2[1], full(("ffn2_w_in", 1)), rows(("ffn2_w_out", 1)), [])

    gr, reduced = {}, {}
    n_layers = {n: w[n].shape[0] for n in BIG}

    def reduce_of(stages):
        units = [u for s in stages for u in STAGES[s]]
        return _Reduce(units, [gr[u] for u in units], n_layers, reduced, me_idx, core_idx)

    loss_part, dx, dxb, dg_final = _loss_bwd(a5, ln_final, target)
    dx, dxb, dg_f2_1, gr["ffn2_w_in", 1], gr["ffn2_w_out", 1] = _ffn_bwd(
        "l1_ffn2", s_f2_1, ln_ffn2[1], full(("ffn2_w_in", 1)), rows(("ffn2_w_out", 1)), dx, dxb, reduce_of([]))
    (dx, dxb, sm_mla, gr["mla_w_in", 0], gr["mla_w_q_up", 0], gr["mla_w_kv_up", 0],
     gr["mla_w_out", 0]) = _mla_bwd(s_mla, ln_mix[1], mla_wts, gq, gkv, tables, dx, dxb, mla_w_in.shape[2])
    dx, dxb, dg_f1_1, gr["ffn1_w_in", 1], gr["ffn1_w_out", 1] = _ffn_bwd(
        "l1_ffn1", s_f1_1, ln_ffn1[1], full(("ffn1_w_in", 1)), rows(("ffn1_w_out", 1)), dx, dxb, reduce_of([5, 4]))
    dx, dxb, dg_f2_0, gr["ffn2_w_in", 0], gr["ffn2_w_out", 0] = _ffn_bwd(
        "l0_ffn2", s_f2_0, ln_ffn2[0], full(("ffn2_w_in", 0)), rows(("ffn2_w_out", 0)), dx, dxb, reduce_of([3]))
    dx, dxb, sm_sgu, gr["sgu_w_in", 0], gr["sgu_w_out", 0] = _sgu_bwd(
        s_sgu, ln_mix[0], full(("sgu_w_in", 0)), *sgu_small, rows(("sgu_w_out", 0)), dx, dxb, reduce_of([2]))
    dx, dxb, dg_f1_0, gr["ffn1_w_in", 0], gr["ffn1_w_out", 0] = _ffn_bwd(
        "l0_ffn1", s_f1_0, ln_ffn1[0], full(("ffn1_w_in", 0)), rows(("ffn1_w_out", 0)), dx, dxb, reduce_of([1]))
    reduce_of([0]).alone()
    big_grad = {n: reduced[n].reshape(w[n].shape) for n in BIG}

    small_parts = [
        jnp.concatenate([dg_f1_0, dg_f1_1], axis=0), jnp.concatenate([sm_sgu["ln"], sm_mla["ln"]], axis=0),
        jnp.concatenate([dg_f2_0, dg_f2_1], axis=0), sm_sgu["gain"], sm_sgu["bias"], sm_sgu["w_sp"],
        sm_sgu["b_sp"], dg_final]
    rep_shapes = [w[n].shape for n in REPLICATED]
    gq_row = jnp.pad(sm_mla["gq"], ((0, 0), (0, N_CHIPS * (LANE - nq))))
    gkv_row = jnp.pad(sm_mla["gkv"], ((0, 0), (0, N_CHIPS * (LANE - nq))))
    packed = _pack(small_parts + [gq_row, gkv_row, loss_part])
    packed = jnp.pad(packed, ((0, -packed.shape[0] % 8), (0, 0)))
    everyone = _gather_all(packed)
    rows = packed.shape[0]
    zero_rows = jnp.zeros((rows - sum(math.prod(s) // LANE for s in rep_shapes), LANE), F32)
    pw = jnp.concatenate([_pack([w[n] for n in REPLICATED]), zero_rows], axis=0)
    pm = jnp.concatenate([_pack([mom[n] for n in REPLICATED]), zero_rows], axis=0)
    pv = jnp.concatenate([_pack([var[n] for n in REPLICATED]), zero_rows + 1.0], axis=0)
    g_all, d_all, m_all, v_all = _adamw_summed(pw, everyone, pm, pv)
    tail_shapes = [(1, N_CHIPS * LANE), (1, N_CHIPS * LANE), (1, LANE)]
    rep_grad = dict(zip(REPLICATED, _unpack(g_all, rep_shapes + tail_shapes)[:len(REPLICATED)]))
    rep_delta = dict(zip(REPLICATED, _unpack(d_all, rep_shapes)))
    rep_m = dict(zip(REPLICATED, _unpack(m_all, rep_shapes)))
    rep_v = dict(zip(REPLICATED, _unpack(v_all, rep_shapes)))
    tail = _unpack(g_all, rep_shapes + tail_shapes)[len(REPLICATED):]
    loss = tail[2][0, 0]
    norm_grad = {
        "mla_q_norm": lax.dynamic_slice(tail[0], (0, me * nq), (1, nq)),
        "mla_kv_norm": lax.dynamic_slice(tail[1], (0, me * nq), (1, nq)),
    }

    grad, delta, new_m, new_v = {}, {}, {}, {}
    for n in BIG:
        shp = w[n].shape
        flat = lambda a: a.reshape(-1, shp[-1])
        dl, nm, nv = _adamw("adamw_" + n, flat(w[n]), flat(big_grad[n]), flat(mom[n]), flat(var[n]))
        grad[n], delta[n], new_m[n], new_v[n] = big_grad[n], dl.reshape(shp), nm.reshape(shp), nv.reshape(shp)
    for n in REPLICATED:
        grad[n], delta[n], new_m[n], new_v[n] = rep_grad[n], rep_delta[n], rep_m[n], rep_v[n]
    stack = lambda dct: jnp.concatenate([dct[n] for n in NORM_SHARDS], axis=0)
    dl, nm, nv = _adamw("adamw_norm_shards", stack(w), stack(norm_grad), stack(mom), stack(var))
    for i, n in enumerate(NORM_SHARDS):
        grad[n], delta[n], new_m[n], new_v[n] = norm_grad[n], dl[i:i + 1], nm[i:i + 1], nv[i:i + 1]

    grad_x = dx.reshape(x.shape)
    return (loss, grad_x, *[grad[n] for n in WEIGHTS], *[delta[n] for n in WEIGHTS],
            *[new_m[n] for n in WEIGHTS], *[new_v[n] for n in WEIGHTS])
```
